```python
import jax, jax.numpy as jnp
from jax import lax
import numpy as np

D_MODEL = 2048
BATCH = 4
SEQ = 2048
DEPTH = 2
DEC_BATCH = 128
DEC_SEQ = 4
PAST_LEN = 16384
PAGE_SIZE = 128

PLE_DIM = 256
EPS = 1e-6
F_TINY = 1e-30
N_BRANCH = 4
BRANCH_WIDTH = D_MODEL // 4
A_GROUPS = 4
A_CHUNK = 128
A_GDIM = BRANCH_WIDTH // A_GROUPS
B_HEADS = 4
B_KDIM = 128
B_VDIM = BRANCH_WIDTH // B_HEADS
C_HEADS = 4
C_KDIM = BRANCH_WIDTH // 2 // C_HEADS
C_VDIM = BRANCH_WIDTH // C_HEADS
C_RANK = 16
C_TAU = 16.0
D_HEADS = 4
D_KDIM = BRANCH_WIDTH // D_HEADS
D_VDIM = BRANCH_WIDTH // D_HEADS
D_CONV = 4
D_QKV = 2 * D_HEADS * D_KDIM + D_HEADS * D_VDIM
GLA_CHUNK = 16
DELTA_CHUNK = 64
FFN_DIM = (8 * D_MODEL // 3 + 255) // 256 * 256
FFN_CONV = 3
IN_SIZES = (BRANCH_WIDTH, BRANCH_WIDTH,
            B_HEADS * B_KDIM, B_HEADS * B_KDIM, BRANCH_WIDTH, BRANCH_WIDTH,
            C_HEADS * C_KDIM, C_HEADS * C_KDIM, BRANCH_WIDTH, C_RANK, BRANCH_WIDTH,
            D_QKV, D_HEADS, D_HEADS, BRANCH_WIDTH,
            N_BRANCH * D_MODEL)
IN_TOTAL = sum(IN_SIZES)

kernel_name = "hybrid_gmlp_hgrn2_gla_gdn_step"


def _split_points():
    pts, acc = [], 0
    for s in IN_SIZES[:-1]:
        acc += s
        pts.append(acc)
    return pts


def _rmsnorm(x, g):
    xf = x.astype(jnp.float32)
    xf = xf * lax.rsqrt(jnp.mean(xf * xf, axis=-1, keepdims=True) + EPS)
    return xf.astype(x.dtype) * g


def _layernorm(x, g, b):
    xf = x.astype(jnp.float32)
    mu = jnp.mean(xf, axis=-1, keepdims=True)
    xc = xf - mu
    var = jnp.mean(xc * xc, axis=-1, keepdims=True)
    return (xc * lax.rsqrt(var + EPS)).astype(x.dtype) * g + b


def _l2norm(x):
    xf = x.astype(jnp.float32)
    return xf * lax.rsqrt(jnp.sum(xf * xf, axis=-1, keepdims=True) + EPS)


def _pad_time(x, mult):
    pad = (-x.shape[1]) % mult
    return jnp.pad(x, [(0, 0), (0, pad)] + [(0, 0)] * (x.ndim - 2))


def _masked_exp(mask, d):
    return jnp.where(mask, jnp.exp(jnp.where(mask, d, 0.0)), 0.0)


def _causal_dwconv(x, buf, w):
    T, W = x.shape[1], w.shape[0]
    xp = jnp.concatenate([buf.astype(x.dtype), x], axis=1)
    y = xp[:, 0:T] * w[0]
    for j in range(1, W):
        y = y + xp[:, j:j + T] * w[j]
    return y, xp[:, T:]


def _chunk_spatial_gate(u, v, w_s, b_s):
    Bsz, T, _ = v.shape
    vp = _pad_time(v, A_CHUNK)
    n = vp.shape[1] // A_CHUNK
    vp = vp.reshape(Bsz, n, A_CHUNK, A_GROUPS, A_GDIM)
    mask = jnp.tril(jnp.ones((A_CHUNK, A_CHUNK), dtype=bool))
    w = jnp.where(mask, w_s, 0)
    mixed = jnp.einsum('gts,bnsgc->bntgc', w, vp) + jnp.swapaxes(b_s, 0, 1)[None, None, :, :, None]
    mixed = mixed.reshape(Bsz, n * A_CHUNK, BRANCH_WIDTH)[:, :T]
    return u * mixed


def _gla_chunked(q, k, v, g, s0):
    dt = v.dtype
    Bsz, T, H, _ = q.shape
    V = v.shape[-1]
    q, k, v, g = [_pad_time(t.astype(jnp.float32), GLA_CHUNK) for t in (q, k, v, g)]
    n = q.shape[1] // GLA_CHUNK

    def chunks(t):
        return t.reshape(Bsz, n, GLA_CHUNK, H, t.shape[-1]).transpose(1, 0, 3, 2, 4)

    causal = jnp.tril(jnp.ones((GLA_CHUNK, GLA_CHUNK), dtype=bool))[None, None, :, :, None]

    def step(S, inp):
        qc, kc, vc, gc = inp
        G = jnp.cumsum(gc, axis=2)
        Gl = G[:, :, -1:]
        dec = _masked_exp(causal, G[:, :, :, None, :] - G[:, :, None, :, :])
        att = jnp.einsum('bhtk,bhsk,bhtsk->bhts', qc, kc, dec)
        o = (jnp.einsum('bhtk,bhkv->bhtv', qc * jnp.exp(G), S)
             + jnp.einsum('bhts,bhsv->bhtv', att, vc))
        S = (jnp.exp(Gl[:, :, 0])[..., None] * S
             + jnp.einsum('bhsk,bhsv->bhkv', kc * jnp.exp(Gl - G), vc))
        return S, o

    S, o = lax.scan(step, s0.astype(jnp.float32), (chunks(q), chunks(k), chunks(v), chunks(g)))
    o = o.transpose(1, 0, 3, 2, 4).reshape(Bsz, n * GLA_CHUNK, H, V)[:, :T]
    return o.astype(dt), S.astype(s0.dtype)


def _gated_delta_chunked(q, k, v, g, beta, s0):
    dt = v.dtype
    Bsz, T, H, K = q.shape
    V = v.shape[-1]
    C = DELTA_CHUNK
    q, k, v, g, beta = [_pad_time(t.astype(jnp.float32), C) for t in (q, k, v, g, beta)]
    n = q.shape[1] // C

    def ch4(t):
        return t.reshape(Bsz, n, C, H, t.shape[-1]).transpose(0, 3, 1, 2, 4)

    def ch3(t):
        return t.reshape(Bsz, n, C, H).transpose(0, 3, 1, 2)

    q, k, v, g, beta = ch4(q), ch4(k), ch4(v), ch3(g), ch3(beta)
    G = jnp.cumsum(g, axis=-1)
    incl = jnp.tril(jnp.ones((C, C), dtype=bool))
    strict = jnp.tril(jnp.ones((C, C), dtype=bool), k=-1)
    decay = _masked_exp(incl, G[..., :, None] - G[..., None, :])
    kk = jnp.einsum('bhntk,bhnsk->bhnts', k, k)
    a_mat = jnp.eye(C, dtype=jnp.float32) + jnp.where(strict, beta[..., :, None] * decay * kk, 0.0)
    rhs = jnp.concatenate([(beta * jnp.exp(G))[..., None] * k, beta[..., None] * v], axis=-1)
    sol = lax.linalg.triangular_solve(a_mat, rhs, left_side=True, lower=True, unit_diagonal=True)
    w_c, u_c = sol[..., :K], sol[..., K:]
    qk = jnp.einsum('bhntk,bhnsk->bhnts', q, k) * decay
    qg = q * jnp.exp(G)[..., None]
    Gl = G[..., -1:]
    kd = k * jnp.exp(Gl - G)[..., None]
    gl = jnp.exp(Gl[..., 0])

    def step(S, inp):
        wc, uc, qkc, qgc, kdc, glc = inp
        delta = uc - jnp.einsum('bhck,bhkv->bhcv', wc, S)
        o = (jnp.einsum('bhck,bhkv->bhcv', qgc, S)
             + jnp.einsum('bhts,bhsv->bhtv', qkc, delta))
        S = glc[..., None, None] * S + jnp.einsum('bhck,bhcv->bhkv', kdc, delta)
        return S, o

    xs = tuple(jnp.moveaxis(t, 2, 0) for t in (w_c, u_c, qk, qg, kd, gl))
    S, o = lax.scan(step, s0.astype(jnp.float32), xs)
    o = o.transpose(1, 0, 3, 2, 4).reshape(Bsz, n * C, H, V)[:, :T]
    return o.astype(dt), S.astype(s0.dtype)


def _trunk(x, p, s_hgrn, s_gla, s_delta, c_dconv, c_fconv, weights):
    (g_mix, w_in, a_ln_g, a_ln_b, a_w_s, a_b_s, b_lb, b_norm_g, c_w_a2, c_b_a, c_norm_g,
     d_conv_w, d_a_log, d_dt_bias, d_norm_g, w_branch, w_out, g_ffn, w_ffn_up, ffn_conv_w,
     ffn_conv_b, w_ffn_down, g_ple, w_ple_gate, w_ple_proj, g_final) = weights
    Bsz, T, _ = x.shape
    sm = jax.nn.softmax(b_lb.astype(jnp.float32), axis=0)
    lbs = jnp.cumsum(sm, axis=0) - sm[0]
    pts = _split_points()
    h = x
    out_hgrn, out_gla, out_delta, out_dconv, out_fconv, out_v = [], [], [], [], [], []
    for l in range(DEPTH):
        a = _rmsnorm(h, g_mix[l])
        proj = a @ w_in[l]
        (au, av, bq, bf, bi, bg, cq, ck, cv, clr, cr, dqkv, db, da, dz, gates) = jnp.split(proj, pts, axis=-1)

        u = jax.nn.gelu(au)
        v_a = _layernorm(jax.nn.gelu(av), a_ln_g[l], a_ln_b[l])
        o_a = _chunk_spatial_gate(u, v_a, a_w_s[l], a_b_s[l])
        out_v.append(v_a)

        lb = lbs[l].reshape(B_HEADS, B_KDIM)
        fp = bf.reshape(Bsz, T, B_HEADS, B_KDIM).astype(jnp.float32)
        f_b = lb + (1.0 - lb) * jax.nn.sigmoid(fp)
        log_f = jnp.log(jnp.maximum(f_b, F_TINY))
        k_b = (1.0 - lb) * jax.nn.sigmoid(-fp)
        o, s_b = _gla_chunked(bq.reshape(Bsz, T, B_HEADS, B_KDIM), k_b,
                              bi.reshape(Bsz, T, B_HEADS, B_VDIM), log_f, s_hgrn[l])
        o_b = _rmsnorm(o, b_norm_g[l]) * jax.nn.sigmoid(bg.reshape(Bsz, T, B_HEADS, B_VDIM))
        out_hgrn.append(s_b)

        log_a = jax.nn.log_sigmoid((clr @ c_w_a2[l] + c_b_a[l]).astype(jnp.float32)) / C_TAU
        o, s_c = _gla_chunked(cq.reshape(Bsz, T, C_HEADS, C_KDIM) * (C_KDIM ** -0.5),
                              ck.reshape(Bsz, T, C_HEADS, C_KDIM),
                              cv.reshape(Bsz, T, C_HEADS, C_VDIM),
                              log_a.reshape(Bsz, T, C_HEADS, C_KDIM), s_gla[l])
        o_c = _rmsnorm(o, c_norm_g[l]) * jax.nn.silu(cr.reshape(Bsz, T, C_HEADS, C_VDIM))
        out_gla.append(s_c)

        conv, nb_d = _causal_dwconv(dqkv, c_dconv[l], d_conv_w[l])
        conv = jax.nn.silu(conv)
        dq, dk, dv = jnp.split(conv, [D_HEADS * D_KDIM, 2 * D_HEADS * D_KDIM], axis=-1)
        dq = _l2norm(dq.reshape(Bsz, T, D_HEADS, D_KDIM)) * (D_KDIM ** -0.5)
        dk = _l2norm(dk.reshape(Bsz, T, D_HEADS, D_KDIM))
        beta = jax.nn.sigmoid(db.astype(jnp.float32))
        g_d = -jnp.exp(d_a_log[l].astype(jnp.float32)) * jax.nn.softplus((da + d_dt_bias[l]).astype(jnp.float32))
        o, s_d = _gated_delta_chunked(dq, dk, dv.reshape(Bsz, T, D_HEADS, D_VDIM), g_d, beta, s_delta[l])
        o_d = _rmsnorm(o, d_norm_g[l]) * jax.nn.silu(dz.reshape(Bsz, T, D_HEADS, D_VDIM))
        out_delta.append(s_d)
        out_dconv.append(nb_d)

        o_cat = jnp.concatenate([o_a, o_b.reshape(Bsz, T, -1), o_c.reshape(Bsz, T, -1),
                                 o_d.reshape(Bsz, T, -1)], axis=-1).reshape(Bsz, T, N_BRANCH, BRANCH_WIDTH)
        y_br = jnp.einsum('btnw,nwd->btnd', o_cat, w_branch[l])
        merged = jnp.sum(jax.nn.sigmoid(gates.reshape(Bsz, T, N_BRANCH, D_MODEL)) * y_br, axis=2)
        h = h + merged @ w_out[l]

        f = _rmsnorm(h, g_ffn[l]) @ w_ffn_up[l]
        fg, fu = jnp.split(f, 2, axis=-1)
        fgc, nb_f = _causal_dwconv(fg, c_fconv[l], ffn_conv_w[l])
        h = h + (jax.nn.gelu(fgc + ffn_conv_b[l]) * fu) @ w_ffn_down[l]
        out_fconv.append(nb_f)

        h = h + (p[l] @ w_ple_proj[l]) * jax.nn.sigmoid(_rmsnorm(h, g_ple[l]) @ w_ple_gate[l])

    y = _rmsnorm(h, g_final)
    return (y, jnp.stack(out_hgrn), jnp.stack(out_gla), jnp.stack(out_delta),
            jnp.stack(out_dconv), jnp.stack(out_fconv), out_v)


def setup_inputs(seed: int = 0) -> dict:
    key = jax.random.key(seed)
    keys = iter(jax.random.split(key, 48))
    f32 = jnp.float32

    def nrm(shape, scale):
        return jax.random.normal(next(keys), shape, f32) * scale

    d_a_log = jnp.log(jax.random.uniform(next(keys), (DEPTH, D_HEADS), f32, 1.0, 16.0))
    dt = jnp.exp(jax.random.uniform(next(keys), (DEPTH, D_HEADS), f32, jnp.log(1e-3), jnp.log(1e-1)))
    d_dt_bias = dt + jnp.log(-jnp.expm1(-dt))
    return {
        "x_prompt": nrm((BATCH, SEQ, D_MODEL), 1.0),
        "x_sample": nrm((DEC_BATCH, DEC_SEQ, D_MODEL), 1.0),
        "state_hgrn": nrm((DEPTH, DEC_BATCH, B_HEADS, B_KDIM, B_VDIM), 0.5),
        "state_gla": nrm((DEPTH, DEC_BATCH, C_HEADS, C_KDIM, C_VDIM), 0.5),
        "state_delta": nrm((DEPTH, DEC_BATCH, D_HEADS, D_KDIM, D_VDIM), 0.1),
        "state_delta_conv": nrm((DEPTH, DEC_BATCH, D_CONV - 1, D_QKV), 1.0),
        "state_ffn_conv": nrm((DEPTH, DEC_BATCH, FFN_CONV - 1, FFN_DIM), 1.0),
        "p_prompt": nrm((DEPTH, BATCH, SEQ, PLE_DIM), 1.0),
        "p_sample": nrm((DEPTH, DEC_BATCH, DEC_SEQ, PLE_DIM), 1.0),
        "g_mix": 1.0 + nrm((DEPTH, D_MODEL), 0.05),
        "w_in": nrm((DEPTH, D_MODEL, IN_TOTAL), D_MODEL ** -0.5),
        "a_ln_g": 1.0 + nrm((DEPTH, BRANCH_WIDTH), 0.05),
        "a_ln_b": nrm((DEPTH, BRANCH_WIDTH), 0.02),
        "a_w_s": nrm((DEPTH, A_GROUPS, A_CHUNK, A_CHUNK), A_CHUNK ** -0.5),
        "a_b_s": 1.0 + nrm((DEPTH, A_GROUPS, A_CHUNK), 0.1),
        "b_lb": nrm((DEPTH, B_HEADS * B_KDIM), 0.5),
        "b_norm_g": 1.0 + nrm((DEPTH, B_VDIM), 0.05),
        "c_w_a2": nrm((DEPTH, C_RANK, C_HEADS * C_KDIM), C_RANK ** -0.5),
        "c_b_a": nrm((DEPTH, C_HEADS * C_KDIM), 0.1),
        "c_norm_g": 1.0 + nrm((DEPTH, C_VDIM), 0.05),
        "d_conv_w": nrm((DEPTH, D_CONV, D_QKV), D_CONV ** -0.5),
        "d_a_log": d_a_log,
        "d_dt_bias": d_dt_bias,
        "d_norm_g": 1.0 + nrm((DEPTH, D_VDIM), 0.05),
        "w_branch": nrm((DEPTH, N_BRANCH, BRANCH_WIDTH, D_MODEL), BRANCH_WIDTH ** -0.5),
        "w_out": nrm((DEPTH, D_MODEL, D_MODEL), D_MODEL ** -0.5),
        "g_ffn": 1.0 + nrm((DEPTH, D_MODEL), 0.05),
        "w_ffn_up": nrm((DEPTH, D_MODEL, 2 * FFN_DIM), D_MODEL ** -0.5),
        "ffn_conv_w": nrm((DEPTH, FFN_CONV, FFN_DIM), FFN_CONV ** -0.5),
        "ffn_conv_b": nrm((DEPTH, FFN_DIM), 0.02),
        "w_ffn_down": nrm((DEPTH, FFN_DIM, D_MODEL), FFN_DIM ** -0.5),
        "g_ple": 1.0 + nrm((DEPTH, D_MODEL), 0.05),
        "w_ple_gate": nrm((DEPTH, D_MODEL, D_MODEL), D_MODEL ** -0.5),
        "w_ple_proj": nrm((DEPTH, PLE_DIM, D_MODEL), PLE_DIM ** -0.5),
        "g_final": 1.0 + nrm((D_MODEL,), 0.05),
    }


def reference(x_prompt, x_sample, state_hgrn, state_gla, state_delta, state_delta_conv, state_ffn_conv,
              p_prompt, p_sample, g_mix, w_in, a_ln_g, a_ln_b, a_w_s, a_b_s, b_lb, b_norm_g,
              c_w_a2, c_b_a, c_norm_g, d_conv_w, d_a_log, d_dt_bias, d_norm_g, w_branch, w_out,
              g_ffn, w_ffn_up, ffn_conv_w, ffn_conv_b, w_ffn_down, g_ple, w_ple_gate, w_ple_proj, g_final):
    weights = (g_mix, w_in, a_ln_g, a_ln_b, a_w_s, a_b_s, b_lb, b_norm_g, c_w_a2, c_b_a, c_norm_g,
               d_conv_w, d_a_log, d_dt_bias, d_norm_g, w_branch, w_out, g_ffn, w_ffn_up, ffn_conv_w,
               ffn_conv_b, w_ffn_down, g_ple, w_ple_gate, w_ple_proj, g_final)
    bp = x_prompt.shape[0]
    dtp = x_prompt.dtype
    z_hgrn = jnp.zeros((DEPTH, bp, B_HEADS, B_KDIM, B_VDIM), dtp)
    z_gla = jnp.zeros((DEPTH, bp, C_HEADS, C_KDIM, C_VDIM), dtp)
    z_delta = jnp.zeros((DEPTH, bp, D_HEADS, D_KDIM, D_VDIM), dtp)
    z_dconv = jnp.zeros((DEPTH, bp, D_CONV - 1, D_QKV), dtp)
    z_fconv = jnp.zeros((DEPTH, bp, FFN_CONV - 1, FFN_DIM), dtp)
    y_prompt, hgrn_p, gla_p, delta_p, dconv_p, fconv_p, _ = _trunk(
        x_prompt, p_prompt, z_hgrn, z_gla, z_delta, z_dconv, z_fconv, weights)
    y_sample, hgrn_s, gla_s, delta_s, dconv_s, fconv_s, v_s = _trunk(
        x_sample, p_sample, state_hgrn, state_gla, state_delta, state_delta_conv, state_ffn_conv, weights)
    chunk_v_sample = jnp.stack(v_s)
    return (y_prompt, y_sample, hgrn_p, hgrn_s, gla_p, gla_s, delta_p, delta_s,
            dconv_p, dconv_s, fconv_p, fconv_s, chunk_v_sample)
```

```python
import functools

import jax
import jax.numpy as jnp
from jax import lax
from jax.experimental import pallas as pl
from jax.experimental.pallas import tpu as pltpu

F32 = jnp.float32
BF16 = jnp.bfloat16

D_MODEL = 2048
DEPTH = 2
PLE_DIM = 256
EPS = 1e-6
F_TINY = 1e-30
N_BRANCH = 4
BRANCH_WIDTH = D_MODEL // 4
A_GROUPS = 4
A_CHUNK = 128
HEADS = 4
HEAD_DIM = 128
C_KDIM = 64
C_RANK = 16
C_TAU = 16.0
D_CONV = 4
D_QKV = 3 * BRANCH_WIDTH
FFN_DIM = 5632
FFN_CONV = 3
GLA_CHUNK = 16

OFF_D_QKV = 0
OFF_A_U = 1536
OFF_A_V = 2048
OFF_B_Q = 2560
OFF_B_F = 3072
OFF_B_I = 3584
OFF_B_G = 4096
OFF_C_V = 4608
OFF_C_R = 5120
OFF_D_Z = 5632
OFF_C_Q = 6144
OFF_C_K = 6400
OFF_C_LR = 6656
OFF_D_BETA = 6784
OFF_D_DECAY = 6912
OFF_GATES = 7168
N_SEQ_COLS = 7168
N_PROJ = OFF_GATES + N_BRANCH * D_MODEL

VMEM_LIMIT = 56 * 1024 * 1024


def _cparams(sem):
    return pltpu.CompilerParams(dimension_semantics=sem, vmem_limit_bytes=VMEM_LIMIT)


def _gelu(x):
    return 0.5 * x * (1.0 + jnp.tanh(0.7978845608028654 * (x + 0.044715 * (x * x * x))))


def _silu(x):
    return x * jax.nn.sigmoid(x)


def _softplus(x):
    return jnp.maximum(x, 0.0) + jnp.log1p(jnp.exp(-jnp.abs(x)))


def _log_sigmoid(x):
    return -_softplus(-x)


def _rms_rows(x):
    return x * lax.rsqrt(jnp.mean(x * x, axis=-1, keepdims=True) + EPS)


def _dot(a, b):
    return jnp.dot(a, b, preferred_element_type=F32)


def _dot_nt(a, b):
    return lax.dot_general(a, b, (((1,), (1,)), ((), ())), preferred_element_type=F32)


def _dot_tn(a, b):
    return lax.dot_general(a, b, (((0,), (0,)), ((), ())), preferred_element_type=F32)


def _cumsum_rows(x):
    n = x.shape[0]
    row = lax.broadcasted_iota(jnp.int32, x.shape, 0)
    sh = 1
    while sh < n:
        x = x + jnp.where(row >= sh, pltpu.roll(x, sh, axis=0), 0.0)
        sh *= 2
    return x


NORM_ROWS = 128


def _norm_to_scratch(x_ref, g_ref, a_scr):
    def slab(n, carry):
        rows = pl.ds(pl.multiple_of(n * NORM_ROWS, NORM_ROWS), NORM_ROWS)
        a_scr[rows, :] = (_rms_rows(x_ref[rows, :]) * g_ref[...]).astype(BF16)
        return carry

    lax.fori_loop(0, x_ref.shape[0] // NORM_ROWS, slab, 0)


def _norm_matmul_body(x_ref, g_ref, w_ref, o_ref, a_scr):
    @pl.when(pl.program_id(1) == 0)
    def _():
        _norm_to_scratch(x_ref, g_ref, a_scr)

    o_ref[...] = _dot(a_scr[...], w_ref[...]).astype(o_ref.dtype)


def norm_matmul(x, g, w, *, tm, tn, out_dtype=F32):
    m, k = x.shape
    n = w.shape[1]
    return pl.pallas_call(
        _norm_matmul_body,
        grid=(m // tm, n // tn),
        in_specs=[pl.BlockSpec((tm, k), lambda i, j: (i, 0)),
                  pl.BlockSpec((1, k), lambda i, j: (0, 0)),
                  pl.BlockSpec((k, tn), lambda i, j: (0, j))],
        out_specs=pl.BlockSpec((tm, tn), lambda i, j: (i, j)),
        out_shape=jax.ShapeDtypeStruct((m, n), out_dtype),
        scratch_shapes=[pltpu.VMEM((tm, k), BF16)],
        compiler_params=_cparams(("parallel", "arbitrary")),
        name="norm_matmul",
    )(x, g.reshape(1, k), w)


def _matmul_residual_body(x_ref, w_ref, r_ref, o_ref):
    o_ref[...] = r_ref[...] + _dot(x_ref[...], w_ref[...])


def matmul_residual(x, w, res, *, tm, tn):
    m, k = x.shape
    n = w.shape[1]
    return pl.pallas_call(
        _matmul_residual_body,
        grid=(m // tm, n // tn),
        in_specs=[pl.BlockSpec((tm, k), lambda i, j: (i, 0)),
                  pl.BlockSpec((k, tn), lambda i, j: (0, j)),
                  pl.BlockSpec((tm, tn), lambda i, j: (i, j))],
        out_specs=pl.BlockSpec((tm, tn), lambda i, j: (i, j)),
        out_shape=jax.ShapeDtypeStruct((m, n), F32),
        compiler_params=_cparams(("parallel", "arbitrary")),
        name="matmul_residual",
    )(x, w, res)


def _merge_body(oa_ref, ob_ref, oc_ref, od_ref, ga_ref, gb_ref, gc_ref, gd_ref, w_ref, o_ref):
    acc = None
    for b, (o_b, g_b) in enumerate(((oa_ref, ga_ref), (ob_ref, gb_ref), (oc_ref, gc_ref), (od_ref, gd_ref))):
        term = jax.nn.sigmoid(g_b[...]) * _dot(o_b[...], w_ref[b])
        acc = term if acc is None else acc + term
    o_ref[...] = acc.astype(o_ref.dtype)


def merge_branches(o_branches, proj, w_branch, *, tm, tn):
    m = proj.shape[0]
    gate_specs = [pl.BlockSpec((tm, tn), functools.partial(
        lambda i, j, b: (i, (OFF_GATES + b * D_MODEL) // tn + j), b=b)) for b in range(N_BRANCH)]
    return pl.pallas_call(
        _merge_body,
        grid=(m // tm, D_MODEL // tn),
        in_specs=[pl.BlockSpec((tm, BRANCH_WIDTH), lambda i, j: (i, 0))] * N_BRANCH + gate_specs
        + [pl.BlockSpec((N_BRANCH, BRANCH_WIDTH, tn), lambda i, j: (0, 0, j))],
        out_specs=pl.BlockSpec((tm, tn), lambda i, j: (i, j)),
        out_shape=jax.ShapeDtypeStruct((m, D_MODEL), BF16),
        compiler_params=_cparams(("parallel", "arbitrary")),
        name="merge_branches",
    )(*o_branches, proj, proj, proj, proj, w_branch)


def _ple_body(x_ref, xres_ref, p_ref, g_ref, wg_ref, wp_ref, o_ref, a_scr, p_scr):
    @pl.when(pl.program_id(1) == 0)
    def _():
        _norm_to_scratch(x_ref, g_ref, a_scr)
        p_scr[...] = p_ref[...].astype(BF16)

    gate = jax.nn.sigmoid(_dot(a_scr[...], wg_ref[...]))
    o_ref[...] = xres_ref[...] + _dot(p_scr[...], wp_ref[...]) * gate


def ple_update(x, p, g, w_gate, w_proj, *, tm, tn):
    m, k = x.shape
    return pl.pallas_call(
        _ple_body,
        grid=(m // tm, D_MODEL // tn),
        in_specs=[pl.BlockSpec((tm, k), lambda i, j: (i, 0)),
                  pl.BlockSpec((tm, tn), lambda i, j: (i, j)),
                  pl.BlockSpec((tm, PLE_DIM), lambda i, j: (i, 0)),
                  pl.BlockSpec((1, k), lambda i, j: (0, 0)),
                  pl.BlockSpec((k, tn), lambda i, j: (0, j)),
                  pl.BlockSpec((PLE_DIM, tn), lambda i, j: (0, j))],
        out_specs=pl.BlockSpec((tm, tn), lambda i, j: (i, j)),
        out_shape=jax.ShapeDtypeStruct((m, D_MODEL), F32),
        scratch_shapes=[pltpu.VMEM((tm, k), BF16), pltpu.VMEM((tm, PLE_DIM), BF16)],
        compiler_params=_cparams(("parallel", "arbitrary")),
        name="ple_update",
    )(x, x, p, g.reshape(1, k), w_gate, w_proj)


def _final_norm_body(x_ref, g_ref, o_ref):
    def slab(n, carry):
        rows = pl.ds(pl.multiple_of(n * NORM_ROWS, NORM_ROWS), NORM_ROWS)
        o_ref[rows, :] = _rms_rows(x_ref[rows, :]) * g_ref[...]
        return carry

    lax.fori_loop(0, x_ref.shape[0] // NORM_ROWS, slab, 0)


def final_norm(x, g, *, tm):
    m, k = x.shape
    return pl.pallas_call(
        _final_norm_body,
        grid=(m // tm,),
        in_specs=[pl.BlockSpec((tm, k), lambda i: (i, 0)), pl.BlockSpec((1, k), lambda i: (0, 0))],
        out_specs=pl.BlockSpec((tm, k), lambda i: (i, 0)),
        out_shape=jax.ShapeDtypeStruct((m, k), F32),
        compiler_params=_cparams(("parallel",)),
        name="final_norm",
    )(x, g.reshape(1, k))


def _ffn_gate_body(t_last_valid, fg_ref, fu_ref, st_ref, cw_ref, cb_ref, o_ref, st_out_ref, ext):
    tb = fg_ref.shape[1]
    t = pl.program_id(2)

    @pl.when(t == 0)
    def _():
        ext[6:8, :] = st_ref[0]

    ext[8:8 + tb, :] = fg_ref[0]
    conv = (ext[6:6 + tb, :] * cw_ref[0:1, :] + ext[7:7 + tb, :] * cw_ref[1:2, :]
            + ext[8:8 + tb, :] * cw_ref[2:3, :])
    o_ref[0] = (_gelu(conv + cb_ref[...]) * fu_ref[0]).astype(o_ref.dtype)

    @pl.when(t == pl.num_programs(2) - 1)
    def _():
        st_out_ref[0] = ext[8 + t_last_valid - 2:8 + t_last_valid, :]

    ext[6:8, :] = ext[6 + tb:8 + tb, :]


def ffn_gate(f, state, conv_w, conv_b, *, tb, tc):
    bsz, t, _ = f.shape
    nt = t // tb
    ncol = FFN_DIM // tc
    body = functools.partial(_ffn_gate_body, t - (nt - 1) * tb)
    return pl.pallas_call(
        body,
        grid=(bsz, ncol, nt),
        in_specs=[pl.BlockSpec((1, tb, tc), lambda b, c, s: (b, s, c)),
                  pl.BlockSpec((1, tb, tc), lambda b, c, s: (b, s, ncol + c)),
                  pl.BlockSpec((1, FFN_CONV - 1, tc), lambda b, c, s: (b, 0, c)),
                  pl.BlockSpec((FFN_CONV, tc), lambda b, c, s: (0, c)),
                  pl.BlockSpec((1, tc), lambda b, c, s: (0, c))],
        out_specs=[pl.BlockSpec((1, tb, tc), lambda b, c, s: (b, s, c)),
                   pl.BlockSpec((1, FFN_CONV - 1, tc), lambda b, c, s: (b, 0, c))],
        out_shape=[jax.ShapeDtypeStruct((bsz, t, FFN_DIM), BF16),
                   jax.ShapeDtypeStruct((bsz, FFN_CONV - 1, FFN_DIM), F32)],
        scratch_shapes=[pltpu.VMEM((tb + 8, tc), F32)],
        compiler_params=_cparams(("parallel", "parallel", "arbitrary")),
        name="ffn_gate",
    )(f, f, state, conv_w, conv_b.reshape(1, FFN_DIM))


def _mixer_a_body(u_ref, v_ref, lng_ref, lnb_ref, w_ref, bias_ref, o_ref, vout_ref):
    u = _gelu(u_ref[...])
    v = _gelu(v_ref[...])
    vc = v - jnp.mean(v, axis=-1, keepdims=True)
    var = jnp.mean(vc * vc, axis=-1, keepdims=True)
    vn = vc * lax.rsqrt(var + EPS) * lng_ref[...] + lnb_ref[...]
    vout_ref[...] = vn
    n = w_ref.shape[1]
    causal = lax.broadcasted_iota(jnp.int32, (n, n), 0) >= lax.broadcasted_iota(jnp.int32, (n, n), 1)
    gd = BRANCH_WIDTH // A_GROUPS
    for g in range(A_GROUPS):
        w = jnp.where(causal, w_ref[g], 0.0)
        mixed = _dot(w, vn[:, g * gd:(g + 1) * gd]) + bias_ref[:, g:g + 1]
        o_ref[:, g * gd:(g + 1) * gd] = (u[:, g * gd:(g + 1) * gd] * mixed).astype(o_ref.dtype)


def mixer_a(proj, ln_g, ln_b, w_mix, bias_rows):
    m = proj.shape[0]
    r = A_CHUNK
    bw = BRANCH_WIDTH
    return pl.pallas_call(
        _mixer_a_body,
        grid=(m // r,),
        in_specs=[pl.BlockSpec((r, bw), lambda i: (i, OFF_A_U // bw)),
                  pl.BlockSpec((r, bw), lambda i: (i, OFF_A_V // bw)),
                  pl.BlockSpec((1, bw), lambda i: (0, 0)),
                  pl.BlockSpec((1, bw), lambda i: (0, 0)),
                  pl.BlockSpec((A_GROUPS, r, r), lambda i: (0, 0, 0)),
                  pl.BlockSpec((r, A_GROUPS), lambda i: (0, 0))],
        out_specs=[pl.BlockSpec((r, bw), lambda i: (i, 0)), pl.BlockSpec((r, bw), lambda i: (i, 0))],
        out_shape=[jax.ShapeDtypeStruct((m, bw), BF16), jax.ShapeDtypeStruct((m, bw), F32)],
        compiler_params=_cparams(("parallel",)),
        name="mixer_a",
    )(proj, proj, ln_g.reshape(1, bw), ln_b.reshape(1, bw), w_mix, bias_rows)


def _gla_chunk(q, k, v, g, st_ref, h):
    c = q.shape[0]
    gc = _cumsum_rows(g)
    row = lax.broadcasted_iota(jnp.int32, (c, 1), 0)
    o = jnp.zeros((c, v.shape[1]), F32)
    for s in range(c):
        dec = jnp.exp(jnp.minimum(gc - gc[s:s + 1, :], 0.0))
        a = jnp.sum(q * dec * k[s:s + 1, :], axis=-1, keepdims=True)
        o = o + jnp.where(row >= s, a, 0.0) * v[s:s + 1, :]
    st = st_ref[h]
    o = o + _dot_nt(q * jnp.exp(gc), st)
    gl = gc[c - 1:c, :]
    kd = k * jnp.exp(gl - gc)
    st_ref[h] = st * jnp.exp(gl) + _dot_tn(v, kd)
    return o


def _load_state_t(s0_ref, st_ref, kdim):
    for h in range(HEADS):
        s = s0_ref[0, h]
        if kdim < HEAD_DIM:
            s = jnp.concatenate([s, jnp.zeros((HEAD_DIM - kdim, HEAD_DIM), F32)], axis=0)
        st_ref[h] = s.T


def _store_state_t(st_ref, sout_ref, kdim):
    for h in range(HEADS):
        sout_ref[0, h] = st_ref[h].T[:kdim, :]


def _valid_rows(t_valid, tb, r0, c):
    return pl.program_id(1) * tb + r0 + lax.broadcasted_iota(jnp.int32, (c, 1), 0) < t_valid


def _mixer_b_body(t_valid, t_pad, q_ref, f_ref, i_ref, og_ref, lb_ref, ng_ref, s0_ref, o_ref, sout_ref, st_ref):
    tb = q_ref.shape[1]
    c = GLA_CHUNK
    kd = HEAD_DIM

    @pl.when(pl.program_id(1) == 0)
    def _():
        _load_state_t(s0_ref, st_ref, kd)

    def chunk(n, carry):
        r0 = pl.multiple_of(n * c, c)
        rows = pl.ds(r0, c)
        for h in range(HEADS):
            cols = slice(h * kd, (h + 1) * kd)
            lb = lb_ref[:, cols]
            fp = f_ref[0, rows, cols]
            f = lb + (1.0 - lb) * jax.nn.sigmoid(fp)
            g = jnp.log(jnp.maximum(f, F_TINY))
            k = (1.0 - lb) * jax.nn.sigmoid(-fp)
            if t_valid < t_pad:
                ok = _valid_rows(t_valid, tb, r0, c)
                g = jnp.where(ok, g, 0.0)
                k = jnp.where(ok, k, 0.0)
            o = _gla_chunk(q_ref[0, rows, cols], k, i_ref[0, rows, cols], g, st_ref, h)
            o = _rms_rows(o) * ng_ref[...] * jax.nn.sigmoid(og_ref[0, rows, cols])
            o_ref[0, rows, cols] = o.astype(o_ref.dtype)
        return carry

    lax.fori_loop(0, tb // c, chunk, 0)

    @pl.when(pl.program_id(1) == pl.num_programs(1) - 1)
    def _():
        _store_state_t(st_ref, sout_ref, kd)


def mixer_b(pseq, lb, norm_g, s0, *, t_valid, tb):
    bsz, t_pad, _ = pseq.shape
    bw = BRANCH_WIDTH
    field = lambda off: pl.BlockSpec((1, tb, bw), lambda b, s: (b, s, off // bw))
    state = pl.BlockSpec((1, HEADS, HEAD_DIM, HEAD_DIM), lambda b, s: (b, 0, 0, 0))
    return pl.pallas_call(
        functools.partial(_mixer_b_body, t_valid, t_pad),
        grid=(bsz, t_pad // tb),
        in_specs=[field(OFF_B_Q), field(OFF_B_F), field(OFF_B_I), field(OFF_B_G),
                  pl.BlockSpec((1, bw), lambda b, s: (0, 0)),
                  pl.BlockSpec((1, HEAD_DIM), lambda b, s: (0, 0)),
                  state],
        out_specs=[pl.BlockSpec((1, tb, bw), lambda b, s: (b, s, 0)), state],
        out_shape=[jax.ShapeDtypeStruct((bsz, t_pad, bw), BF16),
                   jax.ShapeDtypeStruct((bsz, HEADS, HEAD_DIM, HEAD_DIM), F32)],
        scratch_shapes=[pltpu.VMEM((HEADS, HEAD_DIM, HEAD_DIM), F32)],
        compiler_params=_cparams(("parallel", "arbitrary")),
        name="mixer_b",
    )(pseq, pseq, pseq, pseq, lb.reshape(1, bw), norm_g.reshape(1, HEAD_DIM), s0)


def _mixer_c_body(t_valid, t_pad, q_ref, k_ref, v_ref, r_ref, lr_ref, w2_ref, ba_ref, ng_ref, s0_ref,
                  o_ref, sout_ref, st_ref):
    tb = q_ref.shape[1]
    c = GLA_CHUNK
    kd = C_KDIM

    @pl.when(pl.program_id(1) == 0)
    def _():
        _load_state_t(s0_ref, st_ref, kd)

    zpad = jnp.zeros((c, HEAD_DIM - kd), F32)

    def chunk(n, carry):
        r0 = pl.multiple_of(n * c, c)
        rows = pl.ds(r0, c)
        gate_in = _dot(lr_ref[0, rows, 0:C_RANK], w2_ref[...]) + ba_ref[...]
        g_all = _log_sigmoid(gate_in) / C_TAU
        if t_valid < t_pad:
            ok = _valid_rows(t_valid, tb, r0, c)
            g_all = jnp.where(ok, g_all, 0.0)
        for h in range(HEADS):
            kcols = slice(h * kd, (h + 1) * kd)
            vcols = slice(h * HEAD_DIM, (h + 1) * HEAD_DIM)
            q = jnp.concatenate([q_ref[0, rows, kcols] * (kd ** -0.5), zpad], axis=1)
            k = k_ref[0, rows, kcols]
            if t_valid < t_pad:
                k = jnp.where(ok, k, 0.0)
            k = jnp.concatenate([k, zpad], axis=1)
            g = jnp.concatenate([g_all[:, kcols], zpad], axis=1)
            o = _gla_chunk(q, k, v_ref[0, rows, vcols], g, st_ref, h)
            o = _rms_rows(o) * ng_ref[...] * _silu(r_ref[0, rows, vcols])
            o_ref[0, rows, vcols] = o.astype(o_ref.dtype)
        return carry

    lax.fori_loop(0, tb // c, chunk, 0)

    @pl.when(pl.program_id(1) == pl.num_programs(1) - 1)
    def _():
        _store_state_t(st_ref, sout_ref, kd)


def mixer_c(pseq, w_a2, b_a, norm_g, s0, *, t_valid, tb):
    bsz, t_pad, _ = pseq.shape
    bw = BRANCH_WIDTH
    kw = HEADS * C_KDIM
    field = lambda off, w: pl.BlockSpec((1, tb, w), lambda b, s: (b, s, off // w))
    state = pl.BlockSpec((1, HEADS, C_KDIM, HEAD_DIM), lambda b, s: (b, 0, 0, 0))
    return pl.pallas_call(
        functools.partial(_mixer_c_body, t_valid, t_pad),
        grid=(bsz, t_pad // tb),
        in_specs=[field(OFF_C_Q, kw), field(OFF_C_K, kw), field(OFF_C_V, bw), field(OFF_C_R, bw),
                  field(OFF_C_LR, 128),
                  pl.BlockSpec((C_RANK, kw), lambda b, s: (0, 0)),
                  pl.BlockSpec((1, kw), lambda b, s: (0, 0)),
                  pl.BlockSpec((1, HEAD_DIM), lambda b, s: (0, 0)),
                  state],
        out_specs=[pl.BlockSpec((1, tb, bw), lambda b, s: (b, s, 0)), state],
        out_shape=[jax.ShapeDtypeStruct((bsz, t_pad, bw), BF16),
                   jax.ShapeDtypeStruct((bsz, HEADS, C_KDIM, HEAD_DIM), F32)],
        scratch_shapes=[pltpu.VMEM((HEADS, HEAD_DIM, HEAD_DIM), F32)],
        compiler_params=_cparams(("parallel", "arbitrary")),
        name="mixer_c",
    )(pseq, pseq, pseq, pseq, pseq, w_a2, b_a.reshape(1, kw), norm_g.reshape(1, HEAD_DIM), s0)


def _unit_lower_inverse(a):
    c = a.shape[0]
    eye = (lax.broadcasted_iota(jnp.int32, (c, c), 0) == lax.broadcasted_iota(jnp.int32, (c, c), 1)).astype(F32)
    p = eye - a
    pw = a
    n = 2
    while n < c:
        pw = _dot(pw, pw)
        p = p + _dot(p, pw)
        n *= 2
    return p


def _mixer_d_body(chunk_len, t_valid, t_pad, x_ref, beta_ref, dec_ref, z_ref, cst_ref, s0_ref, cw_ref,
                  alog_ref, dtb_ref, ng_ref, o_ref, cst_out_ref, sout_ref, ext, conv, st_ref):
    tb = x_ref.shape[1]
    c = chunk_len
    hd = HEAD_DIM
    t = pl.program_id(1)
    nt = pl.num_programs(1)

    @pl.when(t == 0)
    def _():
        ext[5:8, :] = cst_ref[0]
        st_ref[...] = s0_ref[0]

    ext[8:8 + tb, :] = x_ref[0]
    y = ext[5:5 + tb, :] * cw_ref[0:1, :]
    for j in range(1, D_CONV):
        y = y + ext[5 + j:5 + j + tb, :] * cw_ref[j:j + 1, :]
    conv[...] = _silu(y)

    lv = t_valid - (t_pad // tb - 1) * tb

    @pl.when(t == nt - 1)
    def _():
        cst_out_ref[0] = ext[8 + lv - (D_CONV - 1):8 + lv, :]

    ext[5:8, :] = ext[5 + tb:8 + tb, :]

    ri = lax.broadcasted_iota(jnp.int32, (c, c), 0)
    ci = lax.broadcasted_iota(jnp.int32, (c, c), 1)
    incl = ri >= ci
    strict = ri > ci
    eye = ri == ci
    ones = jnp.ones((c, c), F32)
    a_neg = -jnp.exp(alog_ref[...])

    def chunk(n, carry):
        r0 = pl.multiple_of(n * c, c)
        rows = pl.ds(r0, c)
        beta_all = jax.nn.sigmoid(beta_ref[0, rows, :])
        g_all = a_neg * _softplus(dec_ref[0, rows, :] + dtb_ref[...])
        if t_valid < t_pad:
            ok = _valid_rows(t_valid, tb, r0, c)
            beta_all = jnp.where(ok, beta_all, 0.0)
            g_all = jnp.where(ok, g_all, 0.0)
        gcum_all = _cumsum_rows(g_all)
        for h in range(HEADS):
            q = conv[rows, h * hd:(h + 1) * hd]
            k = conv[rows, BRANCH_WIDTH + h * hd:BRANCH_WIDTH + (h + 1) * hd]
            v = conv[rows, 2 * BRANCH_WIDTH + h * hd:2 * BRANCH_WIDTH + (h + 1) * hd]
            q = q * lax.rsqrt(jnp.sum(q * q, axis=-1, keepdims=True) + EPS) * (hd ** -0.5)
            k = k * lax.rsqrt(jnp.sum(k * k, axis=-1, keepdims=True) + EPS)
            if t_valid < t_pad:
                q = jnp.where(ok, q, 0.0)
                k = jnp.where(ok, k, 0.0)
                v = jnp.where(ok, v, 0.0)
            beta = beta_all[:, h:h + 1]
            gc = gcum_all[:, h:h + 1]
            gdiag = jnp.where(eye, gc, 0.0)
            g_hi = gdiag.astype(BF16)
            g_r1 = gdiag - g_hi.astype(F32)
            g_mid = g_r1.astype(BF16)
            g_lo = (g_r1 - g_mid.astype(F32)).astype(BF16)
            ones_b = ones.astype(BF16)
            gr = _dot(ones_b, g_hi) + _dot(ones_b, g_mid) + _dot(ones_b, g_lo)
            decay = jnp.where(incl, jnp.exp(jnp.minimum(gc - gr, 0.0)), 0.0)
            a_mat = jnp.where(strict, beta * decay * _dot_nt(k, k), 0.0)
            t_inv = _unit_lower_inverse(a_mat)
            eg = jnp.exp(gc)
            w = _dot(t_inv, (beta * eg) * k)
            u = _dot(t_inv, beta * v)
            qk = _dot_nt(q, k) * decay
            gl = gc[c - 1:c, :]
            kdec = k * jnp.exp(gl - gc)
            st = st_ref[h]
            delta = u - _dot(w, st)
            o = _dot(q * eg, st) + _dot(qk, delta)
            st_ref[h] = jnp.exp(gl) * st + _dot_tn(kdec, delta)
            o = _rms_rows(o) * ng_ref[...] * _silu(z_ref[0, rows, h * hd:(h + 1) * hd])
            o_ref[0, rows, h * hd:(h + 1) * hd] = o.astype(o_ref.dtype)
        return carry

    lax.fori_loop(0, tb // c, chunk, 0)

    @pl.when(t == nt - 1)
    def _():
        sout_ref[0] = st_ref[...]


def mixer_d(pseq, conv_state, s0, conv_w, a_log, dt_bias, norm_g, *, t_valid, tb, chunk_len):
    bsz, t_pad, _ = pseq.shape
    bw = BRANCH_WIDTH
    field = lambda off, w: pl.BlockSpec((1, tb, w), lambda b, s: (b, s, off // w))
    state = pl.BlockSpec((1, HEADS, HEAD_DIM, HEAD_DIM), lambda b, s: (b, 0, 0, 0))
    cstate = pl.BlockSpec((1, D_CONV - 1, D_QKV), lambda b, s: (b, 0, 0))
    lane_pad = lambda x: jnp.pad(x.reshape(1, HEADS), ((0, 0), (0, 128 - HEADS)))
    return pl.pallas_call(
        functools.partial(_mixer_d_body, chunk_len, t_valid, t_pad),
        grid=(bsz, t_pad // tb),
        in_specs=[field(OFF_D_QKV, D_QKV), field(OFF_D_BETA, 128), field(OFF_D_DECAY, 128), field(OFF_D_Z, bw),
                  cstate, state,
                  pl.BlockSpec((D_CONV, D_QKV), lambda b, s: (0, 0)),
                  pl.BlockSpec((1, 128), lambda b, s: (0, 0)),
                  pl.BlockSpec((1, 128), lambda b, s: (0, 0)),
                  pl.BlockSpec((1, HEAD_DIM), lambda b, s: (0, 0))],
        out_specs=[pl.BlockSpec((1, tb, bw), lambda b, s: (b, s, 0)), cstate, state],
        out_shape=[jax.ShapeDtypeStruct((bsz, t_pad, bw), BF16),
                   jax.ShapeDtypeStruct((bsz, D_CONV - 1, D_QKV), F32),
                   jax.ShapeDtypeStruct((bsz, HEADS, HEAD_DIM, HEAD_DIM), F32)],
        scratch_shapes=[pltpu.VMEM((tb + 8, D_QKV), F32), pltpu.VMEM((tb, D_QKV), F32),
                        pltpu.VMEM((HEADS, HEAD_DIM, HEAD_DIM), F32)],
        compiler_params=_cparams(("parallel", "arbitrary")),
        name="mixer_d",
    )(pseq, pseq, pseq, pseq, conv_state, s0, conv_w, lane_pad(a_log), lane_pad(dt_bias),
      norm_g.reshape(1, HEAD_DIM))


def _pack_w_in(w):
    z = lambda n: jnp.zeros((w.shape[0], n), w.dtype)
    cols = [w[:, 4624:6160],
            w[:, 0:512], w[:, 512:1024],
            w[:, 1024:1536], w[:, 1536:2048], w[:, 2048:2560], w[:, 2560:3072],
            w[:, 3584:4096], w[:, 4112:4624],
            w[:, 6168:6680],
            w[:, 3072:3328], w[:, 3328:3584],
            w[:, 4096:4112], z(112),
            w[:, 6160:6164], z(124),
            w[:, 6164:6168], z(252),
            w[:, 6680:14872]]
    return jnp.concatenate(cols, axis=1).astype(BF16)


def _trunk(x, p, s_hgrn, s_gla, s_delta, c_dconv, c_fconv, wts, lbs, *, sample):
    bsz, t, _ = x.shape
    m = bsz * t
    if sample:
        tm, t_pad, tb_seq, d_chunk, tb_ffn = m, GLA_CHUNK, GLA_CHUNK, GLA_CHUNK, t
    else:
        tm, t_pad, tb_seq, d_chunk, tb_ffn = 1024, t, 256, 64, 512
    h = x.reshape(m, D_MODEL)
    outs = {k: [] for k in ("hgrn", "gla", "delta", "dconv", "fconv", "v")}
    for l in range(DEPTH):
        w = wts[l]
        proj = norm_matmul(h, w["g_mix"], w["w_in"], tm=tm, tn=1024)
        if sample:
            pseq = jnp.pad(proj.reshape(bsz, t, N_PROJ)[:, :, :N_SEQ_COLS], ((0, 0), (0, t_pad - t), (0, 0)))
        else:
            pseq = proj.reshape(bsz, t, N_PROJ)
        o_a, v_a = mixer_a(proj, w["a_ln_g"], w["a_ln_b"], w["a_w_mix"], w["a_bias_rows"])
        o_b, s_b = mixer_b(pseq, lbs[l], w["b_norm_g"], s_hgrn[l], t_valid=t, tb=tb_seq)
        o_c, s_c = mixer_c(pseq, w["c_w_a2"], w["c_b_a"], w["c_norm_g"], s_gla[l], t_valid=t, tb=tb_seq)
        o_d, nb_d, s_d = mixer_d(pseq, c_dconv[l], s_delta[l], w["d_conv_w"], w["d_a_log"], w["d_dt_bias"],
                                 w["d_norm_g"], t_valid=t, tb=tb_seq, chunk_len=d_chunk)
        unpad = lambda o: o[:, :t].reshape(m, BRANCH_WIDTH)
        merged = merge_branches((o_a, unpad(o_b), unpad(o_c), unpad(o_d)), proj, w["w_branch"], tm=tm, tn=512)
        h = matmul_residual(merged, w["w_out"], h, tm=tm, tn=1024)

        f = norm_matmul(h, w["g_ffn"], w["w_ffn_up"], tm=tm, tn=1024)
        act, nb_f = ffn_gate(f.reshape(bsz, t, 2 * FFN_DIM), c_fconv[l], w["ffn_conv_w"], w["ffn_conv_b"],
                             tb=tb_ffn, tc=FFN_DIM if sample else 512)
        h = matmul_residual(act.reshape(m, FFN_DIM), w["w_ffn_down"], h, tm=min(tm, 512), tn=512)

        h = ple_update(h, p[l].reshape(m, PLE_DIM), w["g_ple"], w["w_ple_gate"], w["w_ple_proj"], tm=tm, tn=512)

        outs["hgrn"].append(s_b)
        outs["gla"].append(s_c)
        outs["delta"].append(s_d)
        outs["dconv"].append(nb_d)
        outs["fconv"].append(nb_f)
        outs["v"].append(v_a.reshape(bsz, t, BRANCH_WIDTH))
    y = final_norm(h, wts[0]["g_final"], tm=min(tm, 512)).reshape(bsz, t, D_MODEL)
    return (y,) + tuple(jnp.stack(outs[k]) for k in ("hgrn", "gla", "delta", "dconv", "fconv", "v"))


def kernel(x_prompt, x_sample, state_hgrn, state_gla, state_delta, state_delta_conv, state_ffn_conv, p_prompt, p_sample, g_mix, w_in, a_ln_g, a_ln_b, a_w_s, a_b_s, b_lb, b_norm_g, c_w_a2, c_b_a, c_norm_g, d_conv_w, d_a_log, d_dt_bias, d_norm_g, w_branch, w_out, g_ffn, w_ffn_up, ffn_conv_w, ffn_conv_b, w_ffn_down, g_ple, w_ple_gate, w_ple_proj, g_final):
    bp, t_p, _ = x_prompt.shape
    bs, t_s, _ = x_sample.shape
    sm = jax.nn.softmax(b_lb.astype(F32), axis=0)
    lbs = jnp.cumsum(sm, axis=0) - sm[0]

    shared = [dict(
        g_mix=g_mix[l], w_in=_pack_w_in(w_in[l]), a_ln_g=a_ln_g[l], a_ln_b=a_ln_b[l],
        b_norm_g=b_norm_g[l], c_w_a2=c_w_a2[l], c_b_a=c_b_a[l], c_norm_g=c_norm_g[l],
        d_conv_w=d_conv_w[l], d_a_log=d_a_log[l], d_dt_bias=d_dt_bias[l], d_norm_g=d_norm_g[l],
        w_branch=w_branch[l].astype(BF16), w_out=w_out[l].astype(BF16), g_ffn=g_ffn[l],
        w_ffn_up=w_ffn_up[l].astype(BF16), ffn_conv_w=ffn_conv_w[l], ffn_conv_b=ffn_conv_b[l],
        w_ffn_down=w_ffn_down[l].astype(BF16), g_ple=g_ple[l], w_ple_gate=w_ple_gate[l].astype(BF16),
        w_ple_proj=w_ple_proj[l].astype(BF16), g_final=g_final) for l in range(DEPTH)]

    def layer_weights(l, sample):
        if sample:
            seqs = A_CHUNK // t_s
            w_mix = jnp.einsum("ab,gts->gatbs", jnp.eye(seqs, dtype=F32), a_w_s[l, :, :t_s, :t_s])
            w_mix = w_mix.reshape(A_GROUPS, A_CHUNK, A_CHUNK)
            bias_rows = jnp.tile(a_b_s[l, :, :t_s].T, (seqs, 1))
        else:
            w_mix = a_w_s[l]
            bias_rows = a_b_s[l].T
        return dict(shared[l], a_w_mix=w_mix, a_bias_rows=bias_rows)

    dt = x_prompt.dtype
    zeros = lambda *s: jnp.zeros((DEPTH, bp) + s, dt)
    out_p = _trunk(x_prompt, p_prompt, zeros(HEADS, HEAD_DIM, HEAD_DIM), zeros(HEADS, C_KDIM, HEAD_DIM),
                   zeros(HEADS, HEAD_DIM, HEAD_DIM), zeros(D_CONV - 1, D_QKV), zeros(FFN_CONV - 1, FFN_DIM),
                   [layer_weights(l, False) for l in range(DEPTH)], lbs, sample=False)
    out_s = _trunk(x_sample, p_sample, state_hgrn, state_gla, state_delta, state_delta_conv, state_ffn_conv,
                   [layer_weights(l, True) for l in range(DEPTH)], lbs, sample=True)
    y_p, hgrn_p, gla_p, delta_p, dconv_p, fconv_p, _ = out_p
    y_s, hgrn_s, gla_s, delta_s, dconv_s, fconv_s, v_s = out_s
    return (y_p, y_s, hgrn_p, hgrn_s, gla_p, gla_s, delta_p, delta_s,
            dconv_p, dconv_s, fconv_p, fconv_s, v_s)
```

```python
import functools

import jax
import jax.numpy as jnp
from jax import lax
from jax.experimental import pallas as pl
from jax.experimental.pallas import tpu as pltpu

F32 = jnp.float32
BF16 = jnp.bfloat16

D_MODEL = 2048
DEPTH = 2
PLE_DIM = 256
EPS = 1e-6
F_TINY = 1e-30
N_BRANCH = 4
BRANCH_WIDTH = D_MODEL // 4
A_GROUPS = 4
A_CHUNK = 128
HEADS = 4
HEAD_DIM = 128
C_KDIM = 64
C_RANK = 16
C_TAU = 16.0
D_CONV = 4
D_QKV = 3 * BRANCH_WIDTH
FFN_DIM = 5632
FFN_CONV = 3
GLA_CHUNK = 16

OFF_D_QKV = 0
OFF_A_U = 1536
OFF_A_V = 2048
OFF_B_Q = 2560
OFF_B_F = 3072
OFF_B_I = 3584
OFF_B_G = 4096
OFF_C_V = 4608
OFF_C_R = 5120
OFF_D_Z = 5632
OFF_C_Q = 6144
OFF_C_K = 6400
OFF_C_LR = 6656
OFF_D_BETA = 6784
OFF_D_DECAY = 6912
OFF_GATES = 7168
N_SEQ_COLS = 7168
N_PROJ = OFF_GATES + N_BRANCH * D_MODEL

VMEM_LIMIT = 56 * 1024 * 1024


def _cparams(sem):
    return pltpu.CompilerParams(dimension_semantics=sem, vmem_limit_bytes=VMEM_LIMIT)


def _gelu(x):
    return 0.5 * x * (1.0 + jnp.tanh(0.7978845608028654 * (x + 0.044715 * (x * x * x))))


def _silu(x):
    return x * jax.nn.sigmoid(x)


def _softplus(x):
    return jnp.maximum(x, 0.0) + jnp.log1p(jnp.exp(-jnp.abs(x)))


def _log_sigmoid(x):
    return -_softplus(-x)


def _rms_rows(x):
    return x * lax.rsqrt(jnp.mean(x * x, axis=-1, keepdims=True) + EPS)


def _dot(a, b):
    return jnp.dot(a, b, preferred_element_type=F32)


def _dot_nt(a, b):
    return lax.dot_general(a, b, (((1,), (1,)), ((), ())), preferred_element_type=F32)


def _dot_tn(a, b):
    return lax.dot_general(a, b, (((0,), (0,)), ((), ())), preferred_element_type=F32)


def _cumsum_rows(x):
    n = x.shape[0]
    row = lax.broadcasted_iota(jnp.int32, x.shape, 0)
    sh = 1
    while sh < n:
        x = x + jnp.where(row >= sh, pltpu.roll(x, sh, axis=0), 0.0)
        sh *= 2
    return x


NORM_ROWS = 128


def _norm_to_scratch(x_ref, g_ref, a_scr):
    def slab(n, carry):
        rows = pl.ds(pl.multiple_of(n * NORM_ROWS, NORM_ROWS), NORM_ROWS)
        a_scr[rows, :] = (_rms_rows(x_ref[rows, :]) * g_ref[...]).astype(BF16)
        return carry

    lax.fori_loop(0, x_ref.shape[0] // NORM_ROWS, slab, 0)


def _norm_matmul_body(x_ref, g_ref, w_ref, o_ref, a_scr):
    @pl.when(pl.program_id(1) == 0)
    def _():
        _norm_to_scratch(x_ref, g_ref, a_scr)

    o_ref[...] = _dot(a_scr[...], w_ref[...]).astype(o_ref.dtype)


def norm_matmul(x, g, w, *, tm, tn, out_dtype=F32):
    m, k = x.shape
    n = w.shape[1]
    return pl.pallas_call(
        _norm_matmul_body,
        grid=(m // tm, n // tn),
        in_specs=[pl.BlockSpec((tm, k), lambda i, j: (i, 0)),
                  pl.BlockSpec((1, k), lambda i, j: (0, 0)),
                  pl.BlockSpec((k, tn), lambda i, j: (0, j))],
        out_specs=pl.BlockSpec((tm, tn), lambda i, j: (i, j)),
        out_shape=jax.ShapeDtypeStruct((m, n), out_dtype),
        scratch_shapes=[pltpu.VMEM((tm, k), BF16)],
        compiler_params=_cparams(("parallel", "arbitrary")),
        name="norm_matmul",
    )(x, g.reshape(1, k), w)


def _matmul_residual_body(x_ref, w_ref, r_ref, o_ref):
    o_ref[...] = r_ref[...] + _dot(x_ref[...], w_ref[...])


def matmul_residual(x, w, res, *, tm, tn):
    m, k = x.shape
    n = w.shape[1]
    return pl.pallas_call(
        _matmul_residual_body,
        grid=(m // tm, n // tn),
        in_specs=[pl.BlockSpec((tm, k), lambda i, j: (i, 0)),
                  pl.BlockSpec((k, tn), lambda i, j: (0, j)),
                  pl.BlockSpec((tm, tn), lambda i, j: (i, j))],
        out_specs=pl.BlockSpec((tm, tn), lambda i, j: (i, j)),
        out_shape=jax.ShapeDtypeStruct((m, n), F32),
        compiler_params=_cparams(("parallel", "arbitrary")),
        name="matmul_residual",
    )(x, w, res)


def _merge_body(oa_ref, ob_ref, oc_ref, od_ref, ga_ref, gb_ref, gc_ref, gd_ref, w_ref, o_ref):
    acc = None
    for b, (o_b, g_b) in enumerate(((oa_ref, ga_ref), (ob_ref, gb_ref), (oc_ref, gc_ref), (od_ref, gd_ref))):
        term = jax.nn.sigmoid(g_b[...]) * _dot(o_b[...], w_ref[b])
        acc = term if acc is None else acc + term
    o_ref[...] = acc.astype(o_ref.dtype)


def merge_branches(o_branches, proj, w_branch, *, tm, tn):
    m = proj.shape[0]
    gate_specs = [pl.BlockSpec((tm, tn), functools.partial(
        lambda i, j, b: (i, (OFF_GATES + b * D_MODEL) // tn + j), b=b)) for b in range(N_BRANCH)]
    return pl.pallas_call(
        _merge_body,
        grid=(m // tm, D_MODEL // tn),
        in_specs=[pl.BlockSpec((tm, BRANCH_WIDTH), lambda i, j: (i, 0))] * N_BRANCH + gate_specs
        + [pl.BlockSpec((N_BRANCH, BRANCH_WIDTH, tn), lambda i, j: (0, 0, j))],
        out_specs=pl.BlockSpec((tm, tn), lambda i, j: (i, j)),
        out_shape=jax.ShapeDtypeStruct((m, D_MODEL), BF16),
        compiler_params=_cparams(("parallel", "arbitrary")),
        name="merge_branches",
    )(*o_branches, proj, proj, proj, proj, w_branch)


def _ple_body(x_ref, xres_ref, p_ref, g_ref, wg_ref, wp_ref, o_ref, a_scr, p_scr):
    @pl.when(pl.program_id(1) == 0)
    def _():
        _norm_to_scratch(x_ref, g_ref, a_scr)
        p_scr[...] = p_ref[...].astype(BF16)

    gate = jax.nn.sigmoid(_dot(a_scr[...], wg_ref[...]))
    o_ref[...] = xres_ref[...] + _dot(p_scr[...], wp_ref[...]) * gate


def ple_update(x, p, g, w_gate, w_proj, *, tm, tn):
    m, k = x.shape
    return pl.pallas_call(
        _ple_body,
        grid=(m // tm, D_MODEL // tn),
        in_specs=[pl.BlockSpec((tm, k), lambda i, j: (i, 0)),
                  pl.BlockSpec((tm, tn), lambda i, j: (i, j)),
                  pl.BlockSpec((tm, PLE_DIM), lambda i, j: (i, 0)),
                  pl.BlockSpec((1, k), lambda i, j: (0, 0)),
                  pl.BlockSpec((k, tn), lambda i, j: (0, j)),
                  pl.BlockSpec((PLE_DIM, tn), lambda i, j: (0, j))],
        out_specs=pl.BlockSpec((tm, tn), lambda i, j: (i, j)),
        out_shape=jax.ShapeDtypeStruct((m, D_MODEL), F32),
        scratch_shapes=[pltpu.VMEM((tm, k), BF16), pltpu.VMEM((tm, PLE_DIM), BF16)],
        compiler_params=_cparams(("parallel", "arbitrary")),
        name="ple_update",
    )(x, x, p, g.reshape(1, k), w_gate, w_proj)


def _final_norm_body(x_ref, g_ref, o_ref):
    def slab(n, carry):
        rows = pl.ds(pl.multiple_of(n * NORM_ROWS, NORM_ROWS), NORM_ROWS)
        o_ref[rows, :] = _rms_rows(x_ref[rows, :]) * g_ref[...]
        return carry

    lax.fori_loop(0, x_ref.shape[0] // NORM_ROWS, slab, 0)


def final_norm(x, g, *, tm):
    m, k = x.shape
    return pl.pallas_call(
        _final_norm_body,
        grid=(m // tm,),
        in_specs=[pl.BlockSpec((tm, k), lambda i: (i, 0)), pl.BlockSpec((1, k), lambda i: (0, 0))],
        out_specs=pl.BlockSpec((tm, k), lambda i: (i, 0)),
        out_shape=jax.ShapeDtypeStruct((m, k), F32),
        compiler_params=_cparams(("parallel",)),
        name="final_norm",
    )(x, g.reshape(1, k))


def _ffn_gate_body(t_last_valid, fg_ref, fu_ref, st_ref, cw_ref, cb_ref, o_ref, st_out_ref, ext):
    tb = fg_ref.shape[1]
    t = pl.program_id(2)

    @pl.when(t == 0)
    def _():
        ext[6:8, :] = st_ref[0]

    ext[8:8 + tb, :] = fg_ref[0]
    conv = (ext[6:6 + tb, :] * cw_ref[0:1, :] + ext[7:7 + tb, :] * cw_ref[1:2, :]
            + ext[8:8 + tb, :] * cw_ref[2:3, :])
    o_ref[0] = (_gelu(conv + cb_ref[...]) * fu_ref[0]).astype(o_ref.dtype)

    @pl.when(t == pl.num_programs(2) - 1)
    def _():
        st_out_ref[0] = ext[8 + t_last_valid - 2:8 + t_last_valid, :]

    ext[6:8, :] = ext[6 + tb:8 + tb, :]


def ffn_gate(f, state, conv_w, conv_b, *, tb, tc):
    bsz, t, _ = f.shape
    nt = t // tb
    ncol = FFN_DIM // tc
    body = functools.partial(_ffn_gate_body, t - (nt - 1) * tb)
    return pl.pallas_call(
        body,
        grid=(bsz, ncol, nt),
        in_specs=[pl.BlockSpec((1, tb, tc), lambda b, c, s: (b, s, c)),
                  pl.BlockSpec((1, tb, tc), lambda b, c, s: (b, s, ncol + c)),
                  pl.BlockSpec((1, FFN_CONV - 1, tc), lambda b, c, s: (b, 0, c)),
                  pl.BlockSpec((FFN_CONV, tc), lambda b, c, s: (0, c)),
                  pl.BlockSpec((1, tc), lambda b, c, s: (0, c))],
        out_specs=[pl.BlockSpec((1, tb, tc), lambda b, c, s: (b, s, c)),
                   pl.BlockSpec((1, FFN_CONV - 1, tc), lambda b, c, s: (b, 0, c))],
        out_shape=[jax.ShapeDtypeStruct((bsz, t, FFN_DIM), BF16),
                   jax.ShapeDtypeStruct((bsz, FFN_CONV - 1, FFN_DIM), F32)],
        scratch_shapes=[pltpu.VMEM((tb + 8, tc), F32)],
        compiler_params=_cparams(("parallel", "parallel", "arbitrary")),
        name="ffn_gate",
    )(f, f, state, conv_w, conv_b.reshape(1, FFN_DIM))


def _mixer_a_body(u_ref, v_ref, lng_ref, lnb_ref, w_ref, bias_ref, o_ref, vout_ref):
    u = _gelu(u_ref[...])
    v = _gelu(v_ref[...])
    vc = v - jnp.mean(v, axis=-1, keepdims=True)
    var = jnp.mean(vc * vc, axis=-1, keepdims=True)
    vn = vc * lax.rsqrt(var + EPS) * lng_ref[...] + lnb_ref[...]
    vout_ref[...] = vn
    n = w_ref.shape[1]
    causal = lax.broadcasted_iota(jnp.int32, (n, n), 0) >= lax.broadcasted_iota(jnp.int32, (n, n), 1)
    gd = BRANCH_WIDTH // A_GROUPS
    for g in range(A_GROUPS):
        w = jnp.where(causal, w_ref[g], 0.0)
        mixed = _dot(w, vn[:, g * gd:(g + 1) * gd]) + bias_ref[:, g:g + 1]
        o_ref[:, g * gd:(g + 1) * gd] = (u[:, g * gd:(g + 1) * gd] * mixed).astype(o_ref.dtype)


def mixer_a(proj, ln_g, ln_b, w_mix, bias_rows):
    m = proj.shape[0]
    r = A_CHUNK
    bw = BRANCH_WIDTH
    return pl.pallas_call(
        _mixer_a_body,
        grid=(m // r,),
        in_specs=[pl.BlockSpec((r, bw), lambda i: (i, OFF_A_U // bw)),
                  pl.BlockSpec((r, bw), lambda i: (i, OFF_A_V // bw)),
                  pl.BlockSpec((1, bw), lambda i: (0, 0)),
                  pl.BlockSpec((1, bw), lambda i: (0, 0)),
                  pl.BlockSpec((A_GROUPS, r, r), lambda i: (0, 0, 0)),
                  pl.BlockSpec((r, A_GROUPS), lambda i: (0, 0))],
        out_specs=[pl.BlockSpec((r, bw), lambda i: (i, 0)), pl.BlockSpec((r, bw), lambda i: (i, 0))],
        out_shape=[jax.ShapeDtypeStruct((m, bw), BF16), jax.ShapeDtypeStruct((m, bw), F32)],
        compiler_params=_cparams(("parallel",)),
        name="mixer_a",
    )(proj, proj, ln_g.reshape(1, bw), ln_b.reshape(1, bw), w_mix, bias_rows)


def _gla_chunk(q, k, v, g, st_ref, si, n_src):
    c = q.shape[0]
    gc = _cumsum_rows(g)
    row = lax.broadcasted_iota(jnp.int32, (c, 1), 0)
    o = jnp.zeros((c, v.shape[1]), F32)
    for s in range(n_src):
        dec = jnp.exp(jnp.minimum(gc - gc[s:s + 1, :], 0.0))
        a = jnp.sum(q * dec * k[s:s + 1, :], axis=-1, keepdims=True)
        o = o + jnp.where(row >= s, a, 0.0) * v[s:s + 1, :]
    st = st_ref[si]
    o = o + _dot_nt(q * jnp.exp(gc), st)
    gl = gc[c - 1:c, :]
    kd = k * jnp.exp(gl - gc)
    st_ref[si] = st * jnp.exp(gl) + _dot_tn(v, kd)
    return o


def _load_state_t(s0_ref, st_ref, kdim):
    def per_seq(bi, carry):
        for h in range(HEADS):
            s = s0_ref[bi, h]
            if kdim < HEAD_DIM:
                s = jnp.concatenate([s, jnp.zeros((HEAD_DIM - kdim, HEAD_DIM), F32)], axis=0)
            st_ref[bi * HEADS + h] = s.T
        return carry

    lax.fori_loop(0, s0_ref.shape[0], per_seq, 0)


def _store_state_t(st_ref, sout_ref, kdim):
    def per_seq(bi, carry):
        for h in range(HEADS):
            sout_ref[bi, h] = st_ref[bi * HEADS + h].T[:kdim, :]
        return carry

    lax.fori_loop(0, sout_ref.shape[0], per_seq, 0)


def _seq_chunk_loop(nb, n_chunks, unroll, fn):
    def step(it, carry):
        n = 0 if n_chunks == 1 else it % n_chunks
        bj = 0 if nb == unroll else it // n_chunks
        for u in range(unroll):
            fn(bj * unroll + u, n)
        return carry

    lax.fori_loop(0, (nb // unroll) * n_chunks, step, 0)


def _chunk_rows(n, c):
    if isinstance(n, int):
        return n * c, pl.ds(n * c, c)
    r0 = pl.multiple_of(n * c, c)
    return r0, pl.ds(r0, c)


def _valid_rows(t_valid, tb, r0, c):
    return pl.program_id(1) * tb + r0 + lax.broadcasted_iota(jnp.int32, (c, 1), 0) < t_valid


def _mixer_b_body(t_valid, t_pad, unroll, q_ref, f_ref, i_ref, og_ref, lb_ref, ng_ref, s0_ref, o_ref, sout_ref,
                  st_ref):
    nb, tb = q_ref.shape[0], q_ref.shape[1]
    c = min(GLA_CHUNK, tb)
    kd = HEAD_DIM
    padded = t_valid < t_pad
    n_src = t_valid if padded else c

    @pl.when(pl.program_id(1) == 0)
    def _():
        _load_state_t(s0_ref, st_ref, kd)

    def chunk(bi, n):
        r0, rows = _chunk_rows(n, c)
        for h in range(HEADS):
            cols = slice(h * kd, (h + 1) * kd)
            lb = lb_ref[:, cols]
            fp = f_ref[bi, rows, cols]
            f = lb + (1.0 - lb) * jax.nn.sigmoid(fp)
            g = jnp.log(jnp.maximum(f, F_TINY))
            k = (1.0 - lb) * jax.nn.sigmoid(-fp)
            if padded:
                ok = _valid_rows(t_valid, tb, r0, c)
                g = jnp.where(ok, g, 0.0)
                k = jnp.where(ok, k, 0.0)
            o = _gla_chunk(q_ref[bi, rows, cols], k, i_ref[bi, rows, cols], g, st_ref, bi * HEADS + h, n_src)
            o = _rms_rows(o) * ng_ref[...] * jax.nn.sigmoid(og_ref[bi, rows, cols])
            o_ref[bi, rows, cols] = o.astype(o_ref.dtype)

    _seq_chunk_loop(nb, tb // c, unroll, chunk)

    @pl.when(pl.program_id(1) == pl.num_programs(1) - 1)
    def _():
        _store_state_t(st_ref, sout_ref, kd)


def mixer_b(pseq, lb, norm_g, s0, *, t_valid, tb, nb, unroll):
    bsz, t_pad, _ = pseq.shape
    assert t_valid == t_pad or t_pad == tb <= GLA_CHUNK
    bw = BRANCH_WIDTH
    field = lambda off: pl.BlockSpec((nb, tb, bw), lambda b, s: (b, s, off // bw))
    state = pl.BlockSpec((nb, HEADS, HEAD_DIM, HEAD_DIM), lambda b, s: (b, 0, 0, 0))
    return pl.pallas_call(
        functools.partial(_mixer_b_body, t_valid, t_pad, unroll),
        grid=(bsz // nb, t_pad // tb),
        in_specs=[field(OFF_B_Q), field(OFF_B_F), field(OFF_B_I), field(OFF_B_G),
                  pl.BlockSpec((1, bw), lambda b, s: (0, 0)),
                  pl.BlockSpec((1, HEAD_DIM), lambda b, s: (0, 0)),
                  state],
        out_specs=[pl.BlockSpec((nb, tb, bw), lambda b, s: (b, s, 0)), state],
        out_shape=[jax.ShapeDtypeStruct((bsz, t_pad, bw), BF16),
                   jax.ShapeDtypeStruct((bsz, HEADS, HEAD_DIM, HEAD_DIM), F32)],
        scratch_shapes=[pltpu.VMEM((nb * HEADS, HEAD_DIM, HEAD_DIM), F32)],
        compiler_params=_cparams(("parallel", "arbitrary")),
        name="mixer_b",
    )(pseq, pseq, pseq, pseq, lb.reshape(1, bw), norm_g.reshape(1, HEAD_DIM), s0)


def _mixer_c_body(t_valid, t_pad, unroll, q_ref, k_ref, v_ref, r_ref, lr_ref, w2_ref, ba_ref, ng_ref, s0_ref,
                  o_ref, sout_ref, st_ref):
    nb, tb = q_ref.shape[0], q_ref.shape[1]
    c = min(GLA_CHUNK, tb)
    kd = C_KDIM
    padded = t_valid < t_pad
    n_src = t_valid if padded else c

    @pl.when(pl.program_id(1) == 0)
    def _():
        _load_state_t(s0_ref, st_ref, kd)

    zpad = jnp.zeros((c, HEAD_DIM - kd), F32)

    def chunk(bi, n):
        r0, rows = _chunk_rows(n, c)
        gate_in = _dot(lr_ref[bi, rows, 0:C_RANK], w2_ref[...]) + ba_ref[...]
        g_all = _log_sigmoid(gate_in) / C_TAU
        if padded:
            ok = _valid_rows(t_valid, tb, r0, c)
            g_all = jnp.where(ok, g_all, 0.0)
        for h in range(HEADS):
            kcols = slice(h * kd, (h + 1) * kd)
            vcols = slice(h * HEAD_DIM, (h + 1) * HEAD_DIM)
            q = jnp.concatenate([q_ref[bi, rows, kcols] * (kd ** -0.5), zpad], axis=1)
            k = k_ref[bi, rows, kcols]
            if padded:
                k = jnp.where(ok, k, 0.0)
            k = jnp.concatenate([k, zpad], axis=1)
            g = jnp.concatenate([g_all[:, kcols], zpad], axis=1)
            o = _gla_chunk(q, k, v_ref[bi, rows, vcols], g, st_ref, bi * HEADS + h, n_src)
            o = _rms_rows(o) * ng_ref[...] * _silu(r_ref[bi, rows, vcols])
            o_ref[bi, rows, vcols] = o.astype(o_ref.dtype)

    _seq_chunk_loop(nb, tb // c, unroll, chunk)

    @pl.when(pl.program_id(1) == pl.num_programs(1) - 1)
    def _():
        _store_state_t(st_ref, sout_ref, kd)


def mixer_c(pseq, w_a2, b_a, norm_g, s0, *, t_valid, tb, nb, unroll):
    bsz, t_pad, _ = pseq.shape
    assert t_valid == t_pad or t_pad == tb <= GLA_CHUNK
    bw = BRANCH_WIDTH
    kw = HEADS * C_KDIM
    field = lambda off, w: pl.BlockSpec((nb, tb, w), lambda b, s: (b, s, off // w))
    state = pl.BlockSpec((nb, HEADS, C_KDIM, HEAD_DIM), lambda b, s: (b, 0, 0, 0))
    return pl.pallas_call(
        functools.partial(_mixer_c_body, t_valid, t_pad, unroll),
        grid=(bsz // nb, t_pad // tb),
        in_specs=[field(OFF_C_Q, kw), field(OFF_C_K, kw), field(OFF_C_V, bw), field(OFF_C_R, bw),
                  field(OFF_C_LR, 128),
                  pl.BlockSpec((C_RANK, kw), lambda b, s: (0, 0)),
                  pl.BlockSpec((1, kw), lambda b, s: (0, 0)),
                  pl.BlockSpec((1, HEAD_DIM), lambda b, s: (0, 0)),
                  state],
        out_specs=[pl.BlockSpec((nb, tb, bw), lambda b, s: (b, s, 0)), state],
        out_shape=[jax.ShapeDtypeStruct((bsz, t_pad, bw), BF16),
                   jax.ShapeDtypeStruct((bsz, HEADS, C_KDIM, HEAD_DIM), F32)],
        scratch_shapes=[pltpu.VMEM((nb * HEADS, HEAD_DIM, HEAD_DIM), F32)],
        compiler_params=_cparams(("parallel", "arbitrary")),
        name="mixer_c",
    )(pseq, pseq, pseq, pseq, pseq, w_a2, b_a.reshape(1, kw), norm_g.reshape(1, HEAD_DIM), s0)


def _unit_lower_inverse(a, order):
    c = a.shape[0]
    eye = (lax.broadcasted_iota(jnp.int32, (c, c), 0) == lax.broadcasted_iota(jnp.int32, (c, c), 1)).astype(F32)
    p = eye - a
    pw = a
    n = 2
    while n < order:
        pw = _dot(pw, pw)
        p = p + _dot(p, pw)
        n *= 2
    return p


D_ROWS = 64


def _stack_heads(x):
    return jnp.concatenate([x[:, h * HEAD_DIM:(h + 1) * HEAD_DIM] for h in range(HEADS)], axis=0)


def _stack_head_lanes(x):
    return jnp.concatenate([x[:, h:h + 1] for h in range(HEADS)], axis=0)


def _lane_pad_heads(x):
    return jnp.pad(x.reshape(1, HEADS), ((0, 0), (0, 128 - HEADS)))


def _segment_cumsum(x, seg):
    tpos = lax.broadcasted_iota(jnp.int32, x.shape, 0) & (seg - 1)
    sh = 1
    while sh < seg:
        x = x + jnp.where(tpos >= sh, pltpu.roll(x, sh, axis=0), 0.0)
        sh *= 2
    return x


def _delta_chunk_operands(seg, qkv, beta_all, g_all):
    hd = HEAD_DIM
    bw = BRANCH_WIDTH
    r = HEADS * qkv.shape[0]
    q = _stack_heads(qkv[:, 0:bw])
    k = _stack_heads(qkv[:, bw:2 * bw])
    v = _stack_heads(qkv[:, 2 * bw:3 * bw])
    q = q * lax.rsqrt(jnp.sum(q * q, axis=-1, keepdims=True) + EPS) * (hd ** -0.5)
    k = k * lax.rsqrt(jnp.sum(k * k, axis=-1, keepdims=True) + EPS)
    beta = _stack_head_lanes(beta_all)
    gc = _stack_head_lanes(_segment_cumsum(g_all, seg))
    ri = lax.broadcasted_iota(jnp.int32, (r, r), 0)
    ci = lax.broadcasted_iota(jnp.int32, (r, r), 1)
    shift = seg.bit_length() - 1
    same = (ri >> shift) == (ci >> shift)
    gr = jnp.sum(jnp.where(ri == ci, gc, 0.0), axis=0, keepdims=True)
    decay = jnp.where(same, jnp.where(ri >= ci, jnp.exp(jnp.minimum(gc - gr, 0.0)), 0.0), 0.0)
    a_mat = jnp.where(ri > ci, beta * decay * _dot_nt(k, k), 0.0)
    t_inv = _unit_lower_inverse(a_mat, seg)
    eg = jnp.exp(gc)
    sol = _dot(t_inv, jnp.concatenate([(beta * eg) * k, beta * v], axis=1))
    qk = _dot_nt(q, k) * decay
    is_last = (ci & (seg - 1)) == seg - 1
    gl = jnp.sum(jnp.where(same, jnp.where(is_last, gr, 0.0), 0.0), axis=1, keepdims=True)
    return sol[:, :hd], sol[:, hd:], qk, q * eg, k * jnp.exp(gl - gc), jnp.exp(gl)


def _mixer_d_body(x_ref, beta_ref, dec_ref, z_ref, cst_ref, s0_ref, cw_ref, alog_ref, dtb_ref, ng_ref,
                  o_ref, cst_out_ref, sout_ref, ext, conv, st_ref):
    tb = x_ref.shape[1]
    c = D_ROWS
    hd = HEAD_DIM
    t = pl.program_id(1)
    nt = pl.num_programs(1)

    @pl.when(t == 0)
    def _():
        ext[5:8, :] = cst_ref[0]
        st_ref[...] = s0_ref[0]

    ext[8:8 + tb, :] = x_ref[0]
    y = ext[5:5 + tb, :] * cw_ref[0:1, :]
    for j in range(1, D_CONV):
        y = y + ext[5 + j:5 + j + tb, :] * cw_ref[j:j + 1, :]
    conv[...] = _silu(y)

    @pl.when(t == nt - 1)
    def _():
        cst_out_ref[0] = ext[8 + tb - (D_CONV - 1):8 + tb, :]

    ext[5:8, :] = ext[5 + tb:8 + tb, :]

    a_neg = -jnp.exp(alog_ref[...])

    def chunk(n, carry):
        rows = pl.ds(pl.multiple_of(n * c, c), c)
        beta_all = jax.nn.sigmoid(beta_ref[0, rows, :])
        g_all = a_neg * _softplus(dec_ref[0, rows, :] + dtb_ref[...])
        w, u, qk, qg, kdec, egl = _delta_chunk_operands(c, conv[rows, :], beta_all, g_all)
        deltas, oqs = [], []
        for h in range(HEADS):
            hs = slice(h * c, (h + 1) * c)
            st = st_ref[h]
            x = _dot(jnp.concatenate([w[hs], qg[hs]], axis=0), st)
            delta = u[hs] - x[:c]
            deltas.append(delta)
            oqs.append(x[c:])
            st_ref[h] = egl[(h + 1) * c - 1:(h + 1) * c, :] * st + _dot_tn(kdec[hs], delta)
        o = jnp.concatenate(oqs, axis=0) + _dot(qk, jnp.concatenate(deltas, axis=0))
        o = _rms_rows(o) * ng_ref[...] * _silu(_stack_heads(z_ref[0, rows, :]))
        for h in range(HEADS):
            o_ref[0, rows, h * hd:(h + 1) * hd] = o[h * c:(h + 1) * c].astype(o_ref.dtype)
        return carry

    lax.fori_loop(0, tb // c, chunk, 0)

    @pl.when(t == nt - 1)
    def _():
        sout_ref[0] = st_ref[...]


def _mixer_d_group_body(seg, x_ref, stp_ref, beta_ref, dec_ref, z_ref, s0_ref, cw_ref, alog_ref, dtb_ref, ng_ref,
                        o_ref, cst_out_ref, sout_ref, conv, w_s, u_s, qg_s, kd_s, egl_s, delta_s, oq_s):
    tb = x_ref.shape[1]
    c = D_ROWS
    hd = HEAD_DIM
    pairs = c // 8

    x = x_ref[0]
    stp = stp_ref[0]
    tpos = lax.broadcasted_iota(jnp.int32, (tb, 1), 0) & (seg - 1)
    y = x * cw_ref[3:4, :]
    for d in range(1, D_CONV):
        hist = stp if d == 3 else pltpu.roll(stp, tb - (3 - d), axis=0)
        y = y + jnp.where(tpos < d, hist, pltpu.roll(x, d, axis=0)) * cw_ref[3 - d:4 - d, :]
    conv[...] = _silu(y)
    cst_out_ref[0] = pltpu.roll(x, tb - (seg - 3), axis=0)

    a_neg = -jnp.exp(alog_ref[...])
    low = lax.broadcasted_iota(jnp.int32, (8, 1), 0) < seg

    for n in range(tb // c):
        rows = slice(n * c, (n + 1) * c)
        beta_all = jax.nn.sigmoid(beta_ref[0, rows, :])
        g_all = a_neg * _softplus(dec_ref[0, rows, :] + dtb_ref[...])
        w, u, qk, qg, kdec, egl = _delta_chunk_operands(seg, conv[rows, :], beta_all, g_all)
        w_s[...] = w
        u_s[...] = u
        qg_s[...] = qg
        kd_s[...] = kdec
        egl_s[...] = jnp.broadcast_to(egl, (HEADS * c, hd))

        def pair(p, carry):
            rp = pl.ds(pl.multiple_of(p * 8, 8), 8)
            h = p // pairs
            ja = n * (c // seg) + (p % pairs) * 2
            sa = s0_ref[ja, h]
            sb = s0_ref[ja + 1, h]
            lhs = jnp.concatenate([w_s[rp, :], qg_s[rp, :]], axis=0)
            xa = _dot(lhs, sa)
            xb = _dot(lhs, sb)
            delta = u_s[rp, :] - jnp.where(low, xa[:8], xb[:8])
            delta_s[rp, :] = delta
            oq_s[rp, :] = jnp.where(low, xa[8:], xb[8:])
            kd = kd_s[rp, :]
            e = egl_s[rp, :]
            sout_ref[ja, h] = e[seg - 1:seg, :] * sa + _dot_tn(jnp.where(low, kd, 0.0), delta)
            sout_ref[ja + 1, h] = e[2 * seg - 1:2 * seg, :] * sb + _dot_tn(jnp.where(low, 0.0, kd), delta)
            return carry

        lax.fori_loop(0, HEADS * pairs, pair, 0)
        o = oq_s[...] + _dot(qk, delta_s[...])
        o = _rms_rows(o) * ng_ref[...] * _silu(_stack_heads(z_ref[0, rows, :]))
        for h in range(HEADS):
            o_ref[0, rows, h * hd:(h + 1) * hd] = o[h * c:(h + 1) * c].astype(o_ref.dtype)


def mixer_d(pseq, conv_state, s0, conv_w, a_log, dt_bias, norm_g, *, tb):
    bsz, t, _ = pseq.shape
    bw = BRANCH_WIDTH
    field = lambda off, w: pl.BlockSpec((1, tb, w), lambda b, s: (b, s, off // w))
    state = pl.BlockSpec((1, HEADS, HEAD_DIM, HEAD_DIM), lambda b, s: (b, 0, 0, 0))
    cstate = pl.BlockSpec((1, D_CONV - 1, D_QKV), lambda b, s: (b, 0, 0))
    return pl.pallas_call(
        _mixer_d_body,
        grid=(bsz, t // tb),
        in_specs=[field(OFF_D_QKV, D_QKV), field(OFF_D_BETA, 128), field(OFF_D_DECAY, 128), field(OFF_D_Z, bw),
                  cstate, state,
                  pl.BlockSpec((D_CONV, D_QKV), lambda b, s: (0, 0)),
                  pl.BlockSpec((1, 128), lambda b, s: (0, 0)),
                  pl.BlockSpec((1, 128), lambda b, s: (0, 0)),
                  pl.BlockSpec((1, HEAD_DIM), lambda b, s: (0, 0))],
        out_specs=[pl.BlockSpec((1, tb, bw), lambda b, s: (b, s, 0)), cstate, state],
        out_shape=[jax.ShapeDtypeStruct((bsz, t, bw), BF16),
                   jax.ShapeDtypeStruct((bsz, D_CONV - 1, D_QKV), F32),
                   jax.ShapeDtypeStruct((bsz, HEADS, HEAD_DIM, HEAD_DIM), F32)],
        scratch_shapes=[pltpu.VMEM((tb + 8, D_QKV), F32), pltpu.VMEM((tb, D_QKV), F32),
                        pltpu.VMEM((HEADS, HEAD_DIM, HEAD_DIM), F32)],
        compiler_params=_cparams(("parallel", "arbitrary")),
        name="mixer_d",
    )(pseq, pseq, pseq, pseq, conv_state, s0, conv_w, _lane_pad_heads(a_log), _lane_pad_heads(dt_bias),
      norm_g.reshape(1, HEAD_DIM))


def mixer_d_grouped(prows, conv_state, s0, conv_w, a_log, dt_bias, norm_g, *, seg, tb):
    m = prows.shape[0]
    bsz = m // seg
    bw = BRANCH_WIDTH
    stp = jnp.pad(conv_state, ((0, 0), (0, seg - (D_CONV - 1)), (0, 0))).reshape(1, m, D_QKV)
    field = lambda off, w: pl.BlockSpec((1, tb, w), lambda s: (0, s, off // w))
    state = pl.BlockSpec((tb // seg, HEADS, HEAD_DIM, HEAD_DIM), lambda s: (s, 0, 0, 0))
    rows128 = pltpu.VMEM((HEADS * D_ROWS, HEAD_DIM), F32)
    o, cst, s_out = pl.pallas_call(
        functools.partial(_mixer_d_group_body, seg),
        grid=(m // tb,),
        in_specs=[field(OFF_D_QKV, D_QKV), pl.BlockSpec((1, tb, D_QKV), lambda s: (0, s, 0)),
                  field(OFF_D_BETA, 128), field(OFF_D_DECAY, 128), field(OFF_D_Z, bw),
                  state,
                  pl.BlockSpec((D_CONV, D_QKV), lambda s: (0, 0)),
                  pl.BlockSpec((1, 128), lambda s: (0, 0)),
                  pl.BlockSpec((1, 128), lambda s: (0, 0)),
                  pl.BlockSpec((1, HEAD_DIM), lambda s: (0, 0))],
        out_specs=[pl.BlockSpec((1, tb, bw), lambda s: (0, s, 0)),
                   pl.BlockSpec((1, tb, D_QKV), lambda s: (0, s, 0)), state],
        out_shape=[jax.ShapeDtypeStruct((1, m, bw), BF16),
                   jax.ShapeDtypeStruct((1, m, D_QKV), F32),
                   jax.ShapeDtypeStruct((bsz, HEADS, HEAD_DIM, HEAD_DIM), F32)],
        scratch_shapes=[pltpu.VMEM((tb, D_QKV), F32)] + [rows128] * 7,
        compiler_params=_cparams(("parallel",)),
        name="mixer_d_grouped",
    )(prows.reshape(1, m, -1), stp, prows.reshape(1, m, -1), prows.reshape(1, m, -1), prows.reshape(1, m, -1),
      s0, conv_w, _lane_pad_heads(a_log), _lane_pad_heads(dt_bias), norm_g.reshape(1, HEAD_DIM))
    return o.reshape(m, bw), cst.reshape(bsz, seg, D_QKV)[:, :D_CONV - 1], s_out


def _pack_w_in(w):
    z = lambda n: jnp.zeros((w.shape[0], n), w.dtype)
    cols = [w[:, 4624:6160],
            w[:, 0:512], w[:, 512:1024],
            w[:, 1024:1536], w[:, 1536:2048], w[:, 2048:2560], w[:, 2560:3072],
            w[:, 3584:4096], w[:, 4112:4624],
            w[:, 6168:6680],
            w[:, 3072:3328], w[:, 3328:3584],
            w[:, 4096:4112], z(112),
            w[:, 6160:6164], z(124),
            w[:, 6164:6168], z(252),
            w[:, 6680:14872]]
    return jnp.concatenate(cols, axis=1).astype(BF16)


def _trunk(x, p, s_hgrn, s_gla, s_delta, c_dconv, c_fconv, wts, lbs, *, sample):
    bsz, t, _ = x.shape
    m = bsz * t
    if sample:
        tm, t_pad, tb_ffn = m, 8, t
        cfg_gla = dict(tb=8, nb=8, unroll=2)
    else:
        tm, t_pad, tb_ffn = 1024, t, 512
        cfg_gla = dict(tb=256, nb=1, unroll=1)
    h = x.reshape(m, D_MODEL)
    outs = {k: [] for k in ("hgrn", "gla", "delta", "dconv", "fconv", "v")}
    for l in range(DEPTH):
        w = wts[l]
        proj = norm_matmul(h, w["g_mix"], w["w_in"], tm=tm, tn=1024)
        if sample:
            pseq = jnp.pad(proj.reshape(bsz, t, N_PROJ)[:, :, :N_SEQ_COLS], ((0, 0), (0, t_pad - t), (0, 0)))
        else:
            pseq = proj.reshape(bsz, t, N_PROJ)
        o_a, v_a = mixer_a(proj, w["a_ln_g"], w["a_ln_b"], w["a_w_mix"], w["a_bias_rows"])
        o_b, s_b = mixer_b(pseq, lbs[l], w["b_norm_g"], s_hgrn[l], t_valid=t, **cfg_gla)
        o_c, s_c = mixer_c(pseq, w["c_w_a2"], w["c_b_a"], w["c_norm_g"], s_gla[l], t_valid=t, **cfg_gla)
        d_args = (c_dconv[l], s_delta[l], w["d_conv_w"], w["d_a_log"], w["d_dt_bias"], w["d_norm_g"])
        if sample:
            o_d, nb_d, s_d = mixer_d_grouped(proj, *d_args, seg=t, tb=D_ROWS)
        else:
            o_d, nb_d, s_d = mixer_d(pseq, *d_args, tb=256)
            o_d = o_d.reshape(m, BRANCH_WIDTH)
        unpad = lambda o: o[:, :t].reshape(m, BRANCH_WIDTH)
        merged = merge_branches((o_a, unpad(o_b), unpad(o_c), o_d), proj, w["w_branch"], tm=tm, tn=512)
        h = matmul_residual(merged, w["w_out"], h, tm=tm, tn=1024)

        f = norm_matmul(h, w["g_ffn"], w["w_ffn_up"], tm=tm, tn=1024)
        act, nb_f = ffn_gate(f.reshape(bsz, t, 2 * FFN_DIM), c_fconv[l], w["ffn_conv_w"], w["ffn_conv_b"],
                             tb=tb_ffn, tc=FFN_DIM if sample else 512)
        h = matmul_residual(act.reshape(m, FFN_DIM), w["w_ffn_down"], h, tm=min(tm, 512), tn=512)

        h = ple_update(h, p[l].reshape(m, PLE_DIM), w["g_ple"], w["w_ple_gate"], w["w_ple_proj"], tm=tm, tn=512)

        outs["hgrn"].append(s_b)
        outs["gla"].append(s_c)
        outs["delta"].append(s_d)
        outs["dconv"].append(nb_d)
        outs["fconv"].append(nb_f)
        outs["v"].append(v_a.reshape(bsz, t, BRANCH_WIDTH))
    y = final_norm(h, wts[0]["g_final"], tm=min(tm, 512)).reshape(bsz, t, D_MODEL)
    return (y,) + tuple(jnp.stack(outs[k]) for k in ("hgrn", "gla", "delta", "dconv", "fconv", "v"))


def kernel(x_prompt, x_sample, state_hgrn, state_gla, state_delta, state_delta_conv, state_ffn_conv, p_prompt, p_sample, g_mix, w_in, a_ln_g, a_ln_b, a_w_s, a_b_s, b_lb, b_norm_g, c_w_a2, c_b_a, c_norm_g, d_conv_w, d_a_log, d_dt_bias, d_norm_g, w_branch, w_out, g_ffn, w_ffn_up, ffn_conv_w, ffn_conv_b, w_ffn_down, g_ple, w_ple_gate, w_ple_proj, g_final):
    bp, t_p, _ = x_prompt.shape
    bs, t_s, _ = x_sample.shape
    sm = jax.nn.softmax(b_lb.astype(F32), axis=0)
    lbs = jnp.cumsum(sm, axis=0) - sm[0]

    shared = [dict(
        g_mix=g_mix[l], w_in=_pack_w_in(w_in[l]), a_ln_g=a_ln_g[l], a_ln_b=a_ln_b[l],
        b_norm_g=b_norm_g[l], c_w_a2=c_w_a2[l], c_b_a=c_b_a[l], c_norm_g=c_norm_g[l],
        d_conv_w=d_conv_w[l], d_a_log=d_a_log[l], d_dt_bias=d_dt_bias[l], d_norm_g=d_norm_g[l],
        w_branch=w_branch[l].astype(BF16), w_out=w_out[l].astype(BF16), g_ffn=g_ffn[l],
        w_ffn_up=w_ffn_up[l].astype(BF16), ffn_conv_w=ffn_conv_w[l], ffn_conv_b=ffn_conv_b[l],
        w_ffn_down=w_ffn_down[l].astype(BF16), g_ple=g_ple[l], w_ple_gate=w_ple_gate[l].astype(BF16),
        w_ple_proj=w_ple_proj[l].astype(BF16), g_final=g_final) for l in range(DEPTH)]

    def layer_weights(l, sample):
        if sample:
            seqs = A_CHUNK // t_s
            w_mix = jnp.einsum("ab,gts->gatbs", jnp.eye(seqs, dtype=F32), a_w_s[l, :, :t_s, :t_s])
            w_mix = w_mix.reshape(A_GROUPS, A_CHUNK, A_CHUNK)
            bias_rows = jnp.tile(a_b_s[l, :, :t_s].T, (seqs, 1))
        else:
            w_mix = a_w_s[l]
            bias_rows = a_b_s[l].T
        return dict(shared[l], a_w_mix=w_mix, a_bias_rows=bias_rows)

    dt = x_prompt.dtype
    zeros = lambda *s: jnp.zeros((DEPTH, bp) + s, dt)
    out_p = _trunk(x_prompt, p_prompt, zeros(HEADS, HEAD_DIM, HEAD_DIM), zeros(HEADS, C_KDIM, HEAD_DIM),
                   zeros(HEADS, HEAD_DIM, HEAD_DIM), zeros(D_CONV - 1, D_QKV), zeros(FFN_CONV - 1, FFN_DIM),
                   [layer_weights(l, False) for l in range(DEPTH)], lbs, sample=False)
    out_s = _trunk(x_sample, p_sample, state_hgrn, state_gla, state_delta, state_delta_conv, state_ffn_conv,
                   [layer_weights(l, True) for l in range(DEPTH)], lbs, sample=True)
    y_p, hgrn_p, gla_p, delta_p, dconv_p, fconv_p, _ = out_p
    y_s, hgrn_s, gla_s, delta_s, dconv_s, fconv_s, v_s = out_s
    return (y_p, y_s, hgrn_p, hgrn_s, gla_p, gla_s, delta_p, delta_s,
            dconv_p, dconv_s, fconv_p, fconv_s, v_s)
```

```python
import functools

import jax
import jax.numpy as jnp
from jax import lax
from jax.experimental import pallas as pl
from jax.experimental.pallas import tpu as pltpu

F32 = jnp.float32
BF16 = jnp.bfloat16

D_MODEL = 2048
DEPTH = 2
PLE_DIM = 256
EPS = 1e-6
F_TINY = 1e-30
N_BRANCH = 4
BRANCH_WIDTH = D_MODEL // 4
A_GROUPS = 4
A_CHUNK = 128
HEADS = 4
HEAD_DIM = 128
C_KDIM = 64
C_RANK = 16
C_TAU = 16.0
D_CONV = 4
D_QKV = 3 * BRANCH_WIDTH
FFN_DIM = 5632
FFN_CONV = 3
GLA_CHUNK = 16
LOG2E = 1.4426950408889634

OFF_D_QKV = 0
OFF_A_U = 1536
OFF_A_V = 2048
OFF_B_Q = 2560
OFF_B_F = 3072
OFF_B_I = 3584
OFF_B_G = 4096
OFF_C_V = 4608
OFF_C_R = 5120
OFF_D_Z = 5632
OFF_C_Q = 6144
OFF_C_K = 6400
OFF_C_LR = 6656
OFF_D_BETA = 6784
OFF_D_DECAY = 6912
OFF_GATES = 7168
N_SEQ_COLS = 7168
N_PROJ = OFF_GATES + N_BRANCH * D_MODEL

VMEM_LIMIT = 56 * 1024 * 1024


def _cparams(sem):
    return pltpu.CompilerParams(dimension_semantics=sem, vmem_limit_bytes=VMEM_LIMIT)


def _gelu(x):
    return 0.5 * x * (1.0 + jnp.tanh(0.7978845608028654 * (x + 0.044715 * (x * x * x))))


def _silu(x):
    return x * jax.nn.sigmoid(x)


def _softplus(x):
    return jnp.maximum(x, 0.0) + jnp.log1p(jnp.exp(-jnp.abs(x)))


def _log_sigmoid(x):
    return -_softplus(-x)


def _rms_rows(x):
    return x * lax.rsqrt(jnp.mean(x * x, axis=-1, keepdims=True) + EPS)


def _dot(a, b):
    return jnp.dot(a, b, preferred_element_type=F32)


def _dot_nt(a, b):
    return lax.dot_general(a, b, (((1,), (1,)), ((), ())), preferred_element_type=F32)


def _dot_tn(a, b):
    return lax.dot_general(a, b, (((0,), (0,)), ((), ())), preferred_element_type=F32)


def _cumsum_rows(x):
    n = x.shape[0]
    row = lax.broadcasted_iota(jnp.int32, x.shape, 0)
    sh = 1
    while sh < n:
        x = x + jnp.where(row >= sh, pltpu.roll(x, sh, axis=0), 0.0)
        sh *= 2
    return x


NORM_ROWS = 128


def _norm_to_scratch(x_ref, g_ref, a_scr):
    def slab(n, carry):
        rows = pl.ds(pl.multiple_of(n * NORM_ROWS, NORM_ROWS), NORM_ROWS)
        a_scr[rows, :] = (_rms_rows(x_ref[rows, :]) * g_ref[...]).astype(BF16)
        return carry

    lax.fori_loop(0, x_ref.shape[0] // NORM_ROWS, slab, 0)


def _norm_matmul_body(x_ref, g_ref, w_ref, o_ref, a_scr):
    @pl.when(pl.program_id(1) == 0)
    def _():
        _norm_to_scratch(x_ref, g_ref, a_scr)

    o_ref[...] = _dot(a_scr[...], w_ref[...]).astype(o_ref.dtype)


def norm_matmul(x, g, w, *, tm, tn, out_dtype=F32):
    m, k = x.shape
    n = w.shape[1]
    return pl.pallas_call(
        _norm_matmul_body,
        grid=(m // tm, n // tn),
        in_specs=[pl.BlockSpec((tm, k), lambda i, j: (i, 0)),
                  pl.BlockSpec((1, k), lambda i, j: (0, 0)),
                  pl.BlockSpec((k, tn), lambda i, j: (0, j))],
        out_specs=pl.BlockSpec((tm, tn), lambda i, j: (i, j)),
        out_shape=jax.ShapeDtypeStruct((m, n), out_dtype),
        scratch_shapes=[pltpu.VMEM((tm, k), BF16)],
        compiler_params=_cparams(("parallel", "arbitrary")),
        name="norm_matmul",
    )(x, g.reshape(1, k), w)


def _matmul_residual_body(x_ref, w_ref, r_ref, o_ref):
    o_ref[...] = r_ref[...] + _dot(x_ref[...], w_ref[...])


def matmul_residual(x, w, res, *, tm, tn):
    m, k = x.shape
    n = w.shape[1]
    return pl.pallas_call(
        _matmul_residual_body,
        grid=(m // tm, n // tn),
        in_specs=[pl.BlockSpec((tm, k), lambda i, j: (i, 0)),
                  pl.BlockSpec((k, tn), lambda i, j: (0, j)),
                  pl.BlockSpec((tm, tn), lambda i, j: (i, j))],
        out_specs=pl.BlockSpec((tm, tn), lambda i, j: (i, j)),
        out_shape=jax.ShapeDtypeStruct((m, n), F32),
        compiler_params=_cparams(("parallel", "arbitrary")),
        name="matmul_residual",
    )(x, w, res)


def _merge_body(oa_ref, ob_ref, oc_ref, od_ref, ga_ref, gb_ref, gc_ref, gd_ref, w_ref, o_ref):
    acc = None
    for b, (o_b, g_b) in enumerate(((oa_ref, ga_ref), (ob_ref, gb_ref), (oc_ref, gc_ref), (od_ref, gd_ref))):
        term = jax.nn.sigmoid(g_b[...]) * _dot(o_b[...], w_ref[b])
        acc = term if acc is None else acc + term
    o_ref[...] = acc.astype(o_ref.dtype)


def merge_branches(o_branches, proj, w_branch, *, tm, tn):
    m = proj.shape[0]
    gate_specs = [pl.BlockSpec((tm, tn), functools.partial(
        lambda i, j, b: (i, (OFF_GATES + b * D_MODEL) // tn + j), b=b)) for b in range(N_BRANCH)]
    return pl.pallas_call(
        _merge_body,
        grid=(m // tm, D_MODEL // tn),
        in_specs=[pl.BlockSpec((tm, BRANCH_WIDTH), lambda i, j: (i, 0))] * N_BRANCH + gate_specs
        + [pl.BlockSpec((N_BRANCH, BRANCH_WIDTH, tn), lambda i, j: (0, 0, j))],
        out_specs=pl.BlockSpec((tm, tn), lambda i, j: (i, j)),
        out_shape=jax.ShapeDtypeStruct((m, D_MODEL), BF16),
        compiler_params=_cparams(("parallel", "arbitrary")),
        name="merge_branches",
    )(*o_branches, proj, proj, proj, proj, w_branch)


def _ple_body(x_ref, xres_ref, p_ref, g_ref, wg_ref, wp_ref, o_ref, a_scr, p_scr):
    @pl.when(pl.program_id(1) == 0)
    def _():
        _norm_to_scratch(x_ref, g_ref, a_scr)
        p_scr[...] = p_ref[...].astype(BF16)

    gate = jax.nn.sigmoid(_dot(a_scr[...], wg_ref[...]))
    o_ref[...] = xres_ref[...] + _dot(p_scr[...], wp_ref[...]) * gate


def ple_update(x, p, g, w_gate, w_proj, *, tm, tn):
    m, k = x.shape
    return pl.pallas_call(
        _ple_body,
        grid=(m // tm, D_MODEL // tn),
        in_specs=[pl.BlockSpec((tm, k), lambda i, j: (i, 0)),
                  pl.BlockSpec((tm, tn), lambda i, j: (i, j)),
                  pl.BlockSpec((tm, PLE_DIM), lambda i, j: (i, 0)),
                  pl.BlockSpec((1, k), lambda i, j: (0, 0)),
                  pl.BlockSpec((k, tn), lambda i, j: (0, j)),
                  pl.BlockSpec((PLE_DIM, tn), lambda i, j: (0, j))],
        out_specs=pl.BlockSpec((tm, tn), lambda i, j: (i, j)),
        out_shape=jax.ShapeDtypeStruct((m, D_MODEL), F32),
        scratch_shapes=[pltpu.VMEM((tm, k), BF16), pltpu.VMEM((tm, PLE_DIM), BF16)],
        compiler_params=_cparams(("parallel", "arbitrary")),
        name="ple_update",
    )(x, x, p, g.reshape(1, k), w_gate, w_proj)


def _final_norm_body(x_ref, g_ref, o_ref):
    def slab(n, carry):
        rows = pl.ds(pl.multiple_of(n * NORM_ROWS, NORM_ROWS), NORM_ROWS)
        o_ref[rows, :] = _rms_rows(x_ref[rows, :]) * g_ref[...]
        return carry

    lax.fori_loop(0, x_ref.shape[0] // NORM_ROWS, slab, 0)


def final_norm(x, g, *, tm):
    m, k = x.shape
    return pl.pallas_call(
        _final_norm_body,
        grid=(m // tm,),
        in_specs=[pl.BlockSpec((tm, k), lambda i: (i, 0)), pl.BlockSpec((1, k), lambda i: (0, 0))],
        out_specs=pl.BlockSpec((tm, k), lambda i: (i, 0)),
        out_shape=jax.ShapeDtypeStruct((m, k), F32),
        compiler_params=_cparams(("parallel",)),
        name="final_norm",
    )(x, g.reshape(1, k))


FFN_SUB = 256


def _ffn_up_long_body(tiles_per_seq, x_ref, g_ref, wg_ref, wu_ref, st_ref, cw_ref, cb_ref, o_ref, st_out_ref,
                      a_scr, ext, carry):
    i, j = pl.program_id(0), pl.program_id(1)
    tm = x_ref.shape[0]

    @pl.when(j == 0)
    def _():
        _norm_to_scratch(x_ref, g_ref, a_scr)

    @pl.when(i % tiles_per_seq == 0)
    def _():
        ext[6:8, :] = st_ref[0]

    @pl.when(i % tiles_per_seq != 0)
    def _():
        ext[6:8, :] = carry[j, 0:2, :]

    for r in range(0, tm, FFN_SUB):
        a = a_scr[r:r + FFN_SUB, :]
        ext[8 + r:8 + r + FFN_SUB, :] = _dot(a, wg_ref[...])
        conv = (ext[6 + r:6 + r + FFN_SUB, :] * cw_ref[0:1, :] + ext[7 + r:7 + r + FFN_SUB, :] * cw_ref[1:2, :]
                + ext[8 + r:8 + r + FFN_SUB, :] * cw_ref[2:3, :])
        o_ref[r:r + FFN_SUB, :] = (_gelu(conv + cb_ref[...]) * _dot(a, wu_ref[...])).astype(o_ref.dtype)

    last = ext[6 + tm:8 + tm, :]
    carry[j, 0:2, :] = last
    st_out_ref[0] = last


def _ffn_up_group_body(seg, x_ref, g_ref, wg_ref, wu_ref, stp_ref, cw_ref, cb_ref, o_ref, st_out_ref, a_scr):
    tm = x_ref.shape[0]

    @pl.when(pl.program_id(1) == 0)
    def _():
        _norm_to_scratch(x_ref, g_ref, a_scr)

    a = a_scr[...]
    fg = _dot(a, wg_ref[...])
    stp = stp_ref[...]
    tpos = lax.broadcasted_iota(jnp.int32, (tm, 1), 0) & (seg - 1)
    lag1 = jnp.where(tpos < 1, pltpu.roll(stp, tm - 1, axis=0), pltpu.roll(fg, 1, axis=0))
    lag2 = jnp.where(tpos < 2, stp, pltpu.roll(fg, 2, axis=0))
    conv = lag2 * cw_ref[0:1, :] + lag1 * cw_ref[1:2, :] + fg * cw_ref[2:3, :]
    o_ref[...] = (_gelu(conv + cb_ref[...]) * _dot(a, wu_ref[...])).astype(o_ref.dtype)
    st_out_ref[...] = pltpu.roll(fg, tm - (seg - 2), axis=0)


def ffn_up_gate(x, g, w_up, state, conv_w, conv_b, *, seq_len, tm, tn):
    m, k = x.shape
    bsz = m // seq_len
    ncol = FFN_DIM // tn
    common_in = [pl.BlockSpec((tm, k), lambda i, j: (i, 0)),
                 pl.BlockSpec((1, k), lambda i, j: (0, 0)),
                 pl.BlockSpec((k, tn), lambda i, j: (0, j)),
                 pl.BlockSpec((k, tn), lambda i, j: (0, ncol + j))]
    conv_in = [pl.BlockSpec((FFN_CONV, tn), lambda i, j: (0, j)), pl.BlockSpec((1, tn), lambda i, j: (0, j))]
    act_spec = pl.BlockSpec((tm, tn), lambda i, j: (i, j))
    act_shape = jax.ShapeDtypeStruct((m, FFN_DIM), BF16)
    a_scr = pltpu.VMEM((tm, k), BF16)
    if seq_len >= tm:
        tps = seq_len // tm
        st_spec = pl.BlockSpec((1, FFN_CONV - 1, tn), lambda i, j: (i // tps, 0, j))
        act, tile_last = pl.pallas_call(
            functools.partial(_ffn_up_long_body, tps),
            grid=(m // tm, ncol),
            in_specs=common_in + [st_spec] + conv_in,
            out_specs=[act_spec, pl.BlockSpec((1, FFN_CONV - 1, tn), lambda i, j: (i, 0, j))],
            out_shape=[act_shape, jax.ShapeDtypeStruct((m // tm, FFN_CONV - 1, FFN_DIM), F32)],
            scratch_shapes=[a_scr, pltpu.VMEM((tm + 8, tn), F32), pltpu.VMEM((ncol, 8, tn), F32)],
            compiler_params=_cparams(("arbitrary", "arbitrary")),
            name="ffn_up_gate",
        )(x, g.reshape(1, k), w_up, w_up, state, conv_w, conv_b.reshape(1, FFN_DIM))
        return act, tile_last[tps - 1::tps]
    stp = jnp.pad(state, ((0, 0), (0, seq_len - (FFN_CONV - 1)), (0, 0))).reshape(m, FFN_DIM)
    act, st_rows = pl.pallas_call(
        functools.partial(_ffn_up_group_body, seq_len),
        grid=(m // tm, ncol),
        in_specs=common_in + [act_spec] + conv_in,
        out_specs=[act_spec, act_spec],
        out_shape=[act_shape, jax.ShapeDtypeStruct((m, FFN_DIM), F32)],
        scratch_shapes=[a_scr],
        compiler_params=_cparams(("parallel", "arbitrary")),
        name="ffn_up_gate_grouped",
    )(x, g.reshape(1, k), w_up, w_up, stp, conv_w, conv_b.reshape(1, FFN_DIM))
    return act, st_rows.reshape(bsz, seq_len, FFN_DIM)[:, :FFN_CONV - 1]


def _mixer_a_body(u_ref, v_ref, lng_ref, lnb_ref, w_ref, bias_ref, o_ref, vout_ref):
    u = _gelu(u_ref[...])
    v = _gelu(v_ref[...])
    vc = v - jnp.mean(v, axis=-1, keepdims=True)
    var = jnp.mean(vc * vc, axis=-1, keepdims=True)
    vn = vc * lax.rsqrt(var + EPS) * lng_ref[...] + lnb_ref[...]
    vout_ref[...] = vn
    n = w_ref.shape[1]
    causal = lax.broadcasted_iota(jnp.int32, (n, n), 0) >= lax.broadcasted_iota(jnp.int32, (n, n), 1)
    gd = BRANCH_WIDTH // A_GROUPS
    for g in range(A_GROUPS):
        w = jnp.where(causal, w_ref[g], 0.0)
        mixed = _dot(w, vn[:, g * gd:(g + 1) * gd]) + bias_ref[:, g:g + 1]
        o_ref[:, g * gd:(g + 1) * gd] = (u[:, g * gd:(g + 1) * gd] * mixed).astype(o_ref.dtype)


def mixer_a(proj, ln_g, ln_b, w_mix, bias_rows):
    m = proj.shape[0]
    r = A_CHUNK
    bw = BRANCH_WIDTH
    return pl.pallas_call(
        _mixer_a_body,
        grid=(m // r,),
        in_specs=[pl.BlockSpec((r, bw), lambda i: (i, OFF_A_U // bw)),
                  pl.BlockSpec((r, bw), lambda i: (i, OFF_A_V // bw)),
                  pl.BlockSpec((1, bw), lambda i: (0, 0)),
                  pl.BlockSpec((1, bw), lambda i: (0, 0)),
                  pl.BlockSpec((A_GROUPS, r, r), lambda i: (0, 0, 0)),
                  pl.BlockSpec((r, A_GROUPS), lambda i: (0, 0))],
        out_specs=[pl.BlockSpec((r, bw), lambda i: (i, 0)), pl.BlockSpec((r, bw), lambda i: (i, 0))],
        out_shape=[jax.ShapeDtypeStruct((m, bw), BF16), jax.ShapeDtypeStruct((m, bw), F32)],
        compiler_params=_cparams(("parallel",)),
        name="mixer_a",
    )(proj, proj, ln_g.reshape(1, bw), ln_b.reshape(1, bw), w_mix, bias_rows)


def _gla_head_chunk(q, k, v, g, n_src):
    c = q.shape[0]
    g2 = _cumsum_rows(g) * LOG2E
    n_tiles = c // 8
    row8 = lax.broadcasted_iota(jnp.int32, (8, 1), 0)
    q_t = [q[i * 8:(i + 1) * 8] for i in range(n_tiles)]
    g_t = [g2[i * 8:(i + 1) * 8] for i in range(n_tiles)]
    o_t = [jnp.zeros((8, v.shape[1]), F32) for _ in range(n_tiles)]
    for s in range(n_src):
        k_s = k[s:s + 1, :]
        v_s = v[s:s + 1, :]
        g_s = g2[s:s + 1, :]
        for i in range(s // 8, n_tiles):
            a = jnp.sum(q_t[i] * jnp.exp2(g_t[i] - g_s) * k_s, axis=-1, keepdims=True)
            if i == s // 8:
                a = jnp.where(row8 >= s - 8 * i, a, 0.0)
            o_t[i] = o_t[i] + a * v_s
    o = jnp.concatenate(o_t, axis=0) if n_tiles > 1 else o_t[0]
    gl = g2[c - 1:c, :]
    return o, q * jnp.exp2(g2), k * jnp.exp2(gl - g2), jnp.exp2(gl)


def _head_block_diag(xs):
    z = jnp.zeros_like(xs[0])
    return jnp.concatenate(
        [jnp.concatenate([x if j == h else z for j in range(HEADS)], axis=1) for h, x in enumerate(xs)], axis=0)


def _gla_state_dots(qgs, kds, vs, es, st_ref, bi):
    st = st_ref[bi]
    o_inter = _dot_nt(_head_block_diag(qgs), st)
    st_ref[bi] = st * jnp.concatenate(es, axis=1) + _dot_tn(jnp.concatenate(vs, axis=0), _head_block_diag(kds))
    return o_inter


def _gla_state_step(parts, vs, st_ref, bi):
    c = vs[0].shape[0]
    o_inter = _gla_state_dots([p[1] for p in parts], [p[2] for p in parts], vs, [p[3] for p in parts], st_ref, bi)
    return [p[0] + o_inter[h * c:(h + 1) * c] for h, p in enumerate(parts)]


def _gla_pair_tile(q, k, v, g, seg):
    g2 = _segment_cumsum(g, seg) * LOG2E
    tpos = lax.broadcasted_iota(jnp.int32, (8, 1), 0) & (seg - 1)
    o = jnp.sum(q * k, axis=-1, keepdims=True) * v
    for d in range(1, seg):
        a = jnp.sum(q * jnp.exp2(g2 - pltpu.roll(g2, d, axis=0)) * pltpu.roll(k, d, axis=0), axis=-1, keepdims=True)
        o = o + jnp.where(tpos >= d, a, 0.0) * pltpu.roll(v, d, axis=0)
    first = lax.broadcasted_iota(jnp.int32, (8, 1), 0) < seg
    gl_a, gl_b = g2[seg - 1:seg, :], g2[2 * seg - 1:2 * seg, :]
    kd = k * jnp.exp2(jnp.where(first, gl_a, gl_b) - g2)
    return o, q * jnp.exp2(g2), kd, jnp.exp2(gl_a), jnp.exp2(gl_b)


def _gla_pair_step(tiles, vs, st_ref, p, seg):
    first = lax.broadcasted_iota(jnp.int32, (8, 1), 0) < seg
    qgs = [t[1] for t in tiles]
    oi_a = _gla_state_dots(qgs, [jnp.where(first, t[2], 0.0) for t in tiles], vs, [t[3] for t in tiles],
                           st_ref, 2 * p)
    oi_b = _gla_state_dots(qgs, [jnp.where(first, 0.0, t[2]) for t in tiles], vs, [t[4] for t in tiles],
                           st_ref, 2 * p + 1)
    return [t[0] + jnp.where(first, oi_a[h * 8:(h + 1) * 8], oi_b[h * 8:(h + 1) * 8]) for h, t in enumerate(tiles)]


def _load_state_t(s0_ref, st_ref, kdim):
    def per_seq(bi, carry):
        for h in range(HEADS):
            s = s0_ref[bi, h]
            if kdim < HEAD_DIM:
                s = jnp.concatenate([s, jnp.zeros((HEAD_DIM - kdim, HEAD_DIM), F32)], axis=0)
            st_ref[bi, :, h * HEAD_DIM:(h + 1) * HEAD_DIM] = s.T
        return carry

    lax.fori_loop(0, s0_ref.shape[0], per_seq, 0)


def _store_state_t(st_ref, sout_ref, kdim):
    def per_seq(bi, carry):
        for h in range(HEADS):
            sout_ref[bi, h] = st_ref[bi, :, h * HEAD_DIM:(h + 1) * HEAD_DIM].T[:kdim, :]
        return carry

    lax.fori_loop(0, sout_ref.shape[0], per_seq, 0)


def _seq_chunk_loop(nb, n_chunks, unroll, fn):
    def step(it, carry):
        n = 0 if n_chunks == 1 else it % n_chunks
        bj = 0 if nb == unroll else it // n_chunks
        for u in range(unroll):
            fn(bj * unroll + u, n)
        return carry

    lax.fori_loop(0, (nb // unroll) * n_chunks, step, 0)


def _chunk_rows(n, c):
    if isinstance(n, int):
        return n * c, pl.ds(n * c, c)
    r0 = pl.multiple_of(n * c, c)
    return r0, pl.ds(r0, c)


def _valid_rows(t_valid, tb, r0, c):
    return pl.program_id(1) * tb + r0 + lax.broadcasted_iota(jnp.int32, (c, 1), 0) < t_valid


def _mixer_b_body(t_valid, t_pad, unroll, q_ref, f_ref, i_ref, og_ref, lb_ref, ng_ref, s0_ref, o_ref, sout_ref,
                  st_ref):
    nb, tb = q_ref.shape[0], q_ref.shape[1]
    c = min(GLA_CHUNK, tb)
    kd = HEAD_DIM
    padded = t_valid < t_pad
    n_src = t_valid if padded else c

    @pl.when(pl.program_id(1) == 0)
    def _():
        _load_state_t(s0_ref, st_ref, kd)

    def chunk(bi, n):
        r0, rows = _chunk_rows(n, c)
        parts, vs = [], []
        for h in range(HEADS):
            cols = slice(h * kd, (h + 1) * kd)
            lb = lb_ref[:, cols]
            fp = f_ref[bi, rows, cols]
            f = lb + (1.0 - lb) * jax.nn.sigmoid(fp)
            g = jnp.log(jnp.maximum(f, F_TINY))
            k = (1.0 - lb) * jax.nn.sigmoid(-fp)
            if padded:
                ok = _valid_rows(t_valid, tb, r0, c)
                g = jnp.where(ok, g, 0.0)
                k = jnp.where(ok, k, 0.0)
            vs.append(i_ref[bi, rows, cols])
            parts.append(_gla_head_chunk(q_ref[bi, rows, cols], k, vs[h], g, n_src))
        for h, o in enumerate(_gla_state_step(parts, vs, st_ref, bi)):
            cols = slice(h * kd, (h + 1) * kd)
            o = _rms_rows(o) * ng_ref[...] * jax.nn.sigmoid(og_ref[bi, rows, cols])
            o_ref[bi, rows, cols] = o.astype(o_ref.dtype)

    _seq_chunk_loop(nb, tb // c, unroll, chunk)

    @pl.when(pl.program_id(1) == pl.num_programs(1) - 1)
    def _():
        _store_state_t(st_ref, sout_ref, kd)


def mixer_b(pseq, lb, norm_g, s0, *, t_valid, tb, nb, unroll):
    bsz, t_pad, _ = pseq.shape
    assert t_valid == t_pad or t_pad == tb <= GLA_CHUNK
    bw = BRANCH_WIDTH
    field = lambda off: pl.BlockSpec((nb, tb, bw), lambda b, s: (b, s, off // bw))
    state = pl.BlockSpec((nb, HEADS, HEAD_DIM, HEAD_DIM), lambda b, s: (b, 0, 0, 0))
    return pl.pallas_call(
        functools.partial(_mixer_b_body, t_valid, t_pad, unroll),
        grid=(bsz // nb, t_pad // tb),
        in_specs=[field(OFF_B_Q), field(OFF_B_F), field(OFF_B_I), field(OFF_B_G),
                  pl.BlockSpec((1, bw), lambda b, s: (0, 0)),
                  pl.BlockSpec((1, HEAD_DIM), lambda b, s: (0, 0)),
                  state],
        out_specs=[pl.BlockSpec((nb, tb, bw), lambda b, s: (b, s, 0)), state],
        out_shape=[jax.ShapeDtypeStruct((bsz, t_pad, bw), BF16),
                   jax.ShapeDtypeStruct((bsz, HEADS, HEAD_DIM, HEAD_DIM), F32)],
        scratch_shapes=[pltpu.VMEM((nb, HEAD_DIM, HEADS * HEAD_DIM), F32)],
        compiler_params=_cparams(("parallel", "arbitrary")),
        name="mixer_b",
    )(pseq, pseq, pseq, pseq, lb.reshape(1, bw), norm_g.reshape(1, HEAD_DIM), s0)


def _mixer_c_body(t_valid, t_pad, unroll, q_ref, k_ref, v_ref, r_ref, lr_ref, w2_ref, ba_ref, ng_ref, s0_ref,
                  o_ref, sout_ref, st_ref):
    nb, tb = q_ref.shape[0], q_ref.shape[1]
    c = min(GLA_CHUNK, tb)
    kd = C_KDIM
    padded = t_valid < t_pad
    n_src = t_valid if padded else c

    @pl.when(pl.program_id(1) == 0)
    def _():
        _load_state_t(s0_ref, st_ref, kd)

    zpad = jnp.zeros((c, HEAD_DIM - kd), F32)

    def chunk(bi, n):
        r0, rows = _chunk_rows(n, c)
        gate_in = _dot(lr_ref[bi, rows, 0:C_RANK], w2_ref[...]) + ba_ref[...]
        g_all = _log_sigmoid(gate_in) / C_TAU
        if padded:
            ok = _valid_rows(t_valid, tb, r0, c)
            g_all = jnp.where(ok, g_all, 0.0)
        parts, vs = [], []
        for h in range(HEADS):
            kcols = slice(h * kd, (h + 1) * kd)
            q = jnp.concatenate([q_ref[bi, rows, kcols] * (kd ** -0.5), zpad], axis=1)
            k = k_ref[bi, rows, kcols]
            if padded:
                k = jnp.where(ok, k, 0.0)
            k = jnp.concatenate([k, zpad], axis=1)
            g = jnp.concatenate([g_all[:, kcols], zpad], axis=1)
            vs.append(v_ref[bi, rows, h * HEAD_DIM:(h + 1) * HEAD_DIM])
            parts.append(_gla_head_chunk(q, k, vs[h], g, n_src))
        for h, o in enumerate(_gla_state_step(parts, vs, st_ref, bi)):
            vcols = slice(h * HEAD_DIM, (h + 1) * HEAD_DIM)
            o = _rms_rows(o) * ng_ref[...] * _silu(r_ref[bi, rows, vcols])
            o_ref[bi, rows, vcols] = o.astype(o_ref.dtype)

    _seq_chunk_loop(nb, tb // c, unroll, chunk)

    @pl.when(pl.program_id(1) == pl.num_programs(1) - 1)
    def _():
        _store_state_t(st_ref, sout_ref, kd)


def mixer_c(pseq, w_a2, b_a, norm_g, s0, *, t_valid, tb, nb, unroll):
    bsz, t_pad, _ = pseq.shape
    assert t_valid == t_pad or t_pad == tb <= GLA_CHUNK
    bw = BRANCH_WIDTH
    kw = HEADS * C_KDIM
    field = lambda off, w: pl.BlockSpec((nb, tb, w), lambda b, s: (b, s, off // w))
    state = pl.BlockSpec((nb, HEADS, C_KDIM, HEAD_DIM), lambda b, s: (b, 0, 0, 0))
    return pl.pallas_call(
        functools.partial(_mixer_c_body, t_valid, t_pad, unroll),
        grid=(bsz // nb, t_pad // tb),
        in_specs=[field(OFF_C_Q, kw), field(OFF_C_K, kw), field(OFF_C_V, bw), field(OFF_C_R, bw),
                  field(OFF_C_LR, 128),
                  pl.BlockSpec((C_RANK, kw), lambda b, s: (0, 0)),
                  pl.BlockSpec((1, kw), lambda b, s: (0, 0)),
                  pl.BlockSpec((1, HEAD_DIM), lambda b, s: (0, 0)),
                  state],
        out_specs=[pl.BlockSpec((nb, tb, bw), lambda b, s: (b, s, 0)), state],
        out_shape=[jax.ShapeDtypeStruct((bsz, t_pad, bw), BF16),
                   jax.ShapeDtypeStruct((bsz, HEADS, C_KDIM, HEAD_DIM), F32)],
        scratch_shapes=[pltpu.VMEM((nb, HEAD_DIM, HEADS * HEAD_DIM), F32)],
        compiler_params=_cparams(("parallel", "arbitrary")),
        name="mixer_c",
    )(pseq, pseq, pseq, pseq, pseq, w_a2, b_a.reshape(1, kw), norm_g.reshape(1, HEAD_DIM), s0)


def _grouped_pairs_loop(n_rows, seg, st_ref, head_inputs, head_outputs):
    def step(it, carry):
        rows = pl.ds(pl.multiple_of(it * 16, 16), 16)
        ins = [head_inputs(rows, h) for h in range(HEADS)]
        halves = []
        for half in range(2):
            sl = slice(half * 8, (half + 1) * 8)
            vs = [x[2][sl] for x in ins]
            tiles = [_gla_pair_tile(x[0][sl], x[1][sl], x[2][sl], x[3][sl], seg) for x in ins]
            halves.append(_gla_pair_step(tiles, vs, st_ref, 2 * it + half, seg))
        for h in range(HEADS):
            head_outputs(rows, h, jnp.concatenate([halves[0][h], halves[1][h]], axis=0))
        return carry

    lax.fori_loop(0, n_rows // 16, step, 0)


def _mixer_b_group_body(seg, q_ref, f_ref, i_ref, og_ref, lb_ref, ng_ref, s0_ref, o_ref, sout_ref, st_ref):
    kd = HEAD_DIM
    _load_state_t(s0_ref, st_ref, kd)

    def head_inputs(rows, h):
        cols = slice(h * kd, (h + 1) * kd)
        lb = lb_ref[:, cols]
        fp = f_ref[0, rows, cols]
        f = lb + (1.0 - lb) * jax.nn.sigmoid(fp)
        return (q_ref[0, rows, cols], (1.0 - lb) * jax.nn.sigmoid(-fp), i_ref[0, rows, cols],
                jnp.log(jnp.maximum(f, F_TINY)))

    def head_outputs(rows, h, o):
        cols = slice(h * kd, (h + 1) * kd)
        o = _rms_rows(o) * ng_ref[...] * jax.nn.sigmoid(og_ref[0, rows, cols])
        o_ref[0, rows, cols] = o.astype(o_ref.dtype)

    _grouped_pairs_loop(q_ref.shape[1], seg, st_ref, head_inputs, head_outputs)
    _store_state_t(st_ref, sout_ref, kd)


def _mixer_c_group_body(seg, q_ref, k_ref, v_ref, r_ref, lr_ref, w2_ref, ba_ref, ng_ref, s0_ref, o_ref, sout_ref,
                        st_ref, g_scr):
    kd = C_KDIM
    _load_state_t(s0_ref, st_ref, kd)
    g_scr[...] = _log_sigmoid(_dot(lr_ref[0, :, 0:C_RANK], w2_ref[...]) + ba_ref[...]) / C_TAU
    zpad = jnp.zeros((16, HEAD_DIM - kd), F32)

    def head_inputs(rows, h):
        kcols = slice(h * kd, (h + 1) * kd)
        pad = lambda x: jnp.concatenate([x, zpad], axis=1)
        return (pad(q_ref[0, rows, kcols] * (kd ** -0.5)), pad(k_ref[0, rows, kcols]),
                v_ref[0, rows, h * HEAD_DIM:(h + 1) * HEAD_DIM], pad(g_scr[rows, kcols]))

    def head_outputs(rows, h, o):
        vcols = slice(h * HEAD_DIM, (h + 1) * HEAD_DIM)
        o = _rms_rows(o) * ng_ref[...] * _silu(r_ref[0, rows, vcols])
        o_ref[0, rows, vcols] = o.astype(o_ref.dtype)

    _grouped_pairs_loop(q_ref.shape[1], seg, st_ref, head_inputs, head_outputs)
    _store_state_t(st_ref, sout_ref, kd)


def _grouped_gla_call(body, prows, fields, params, s0, kdim, *, seg, tb, extra_scratch=()):
    m = prows.shape[0]
    bw = BRANCH_WIDTH
    p3 = prows.reshape(1, m, -1)
    state = pl.BlockSpec((tb // seg, HEADS, kdim, HEAD_DIM), lambda s: (s, 0, 0, 0))
    o, s_out = pl.pallas_call(
        functools.partial(body, seg),
        grid=(m // tb,),
        in_specs=[pl.BlockSpec((1, tb, w), functools.partial(lambda s, c: (0, s, c), c=off // w)) for off, w in fields]
        + [pl.BlockSpec(x.shape, functools.partial(lambda s, n: (0,) * n, n=x.ndim)) for x in params] + [state],
        out_specs=[pl.BlockSpec((1, tb, bw), lambda s: (0, s, 0)), state],
        out_shape=[jax.ShapeDtypeStruct((1, m, bw), BF16), jax.ShapeDtypeStruct(s0.shape, F32)],
        scratch_shapes=[pltpu.VMEM((tb // seg, HEAD_DIM, HEADS * HEAD_DIM), F32), *extra_scratch],
        compiler_params=_cparams(("parallel",)),
        name=body.__name__.strip("_"),
    )(*([p3] * len(fields)), *params, s0)
    return o.reshape(m, bw), s_out


def mixer_b_grouped(prows, lb, norm_g, s0, *, seg, tb):
    bw = BRANCH_WIDTH
    return _grouped_gla_call(
        _mixer_b_group_body, prows, [(OFF_B_Q, bw), (OFF_B_F, bw), (OFF_B_I, bw), (OFF_B_G, bw)],
        [lb.reshape(1, bw), norm_g.reshape(1, HEAD_DIM)], s0, HEAD_DIM, seg=seg, tb=tb)


def mixer_c_grouped(prows, w_a2, b_a, norm_g, s0, *, seg, tb):
    bw = BRANCH_WIDTH
    kw = HEADS * C_KDIM
    return _grouped_gla_call(
        _mixer_c_group_body, prows, [(OFF_C_Q, kw), (OFF_C_K, kw), (OFF_C_V, bw), (OFF_C_R, bw), (OFF_C_LR, 128)],
        [w_a2, b_a.reshape(1, kw), norm_g.reshape(1, HEAD_DIM)], s0, C_KDIM, seg=seg, tb=tb,
        extra_scratch=(pltpu.VMEM((tb, kw), F32),))


def _unit_lower_inverse(a, order):
    c = a.shape[0]
    eye = (lax.broadcasted_iota(jnp.int32, (c, c), 0) == lax.broadcasted_iota(jnp.int32, (c, c), 1)).astype(F32)
    p = eye - a
    pw = a
    n = 2
    while n < order:
        pw = _dot(pw, pw)
        p = p + _dot(p, pw)
        n *= 2
    return p


D_ROWS = 64


def _stack_heads(x):
    return jnp.concatenate([x[:, h * HEAD_DIM:(h + 1) * HEAD_DIM] for h in range(HEADS)], axis=0)


def _stack_head_lanes(x):
    return jnp.concatenate([x[:, h:h + 1] for h in range(HEADS)], axis=0)


def _lane_pad_heads(x):
    return jnp.pad(x.reshape(1, HEADS), ((0, 0), (0, 128 - HEADS)))


def _segment_cumsum(x, seg):
    tpos = lax.broadcasted_iota(jnp.int32, x.shape, 0) & (seg - 1)
    sh = 1
    while sh < seg:
        x = x + jnp.where(tpos >= sh, pltpu.roll(x, sh, axis=0), 0.0)
        sh *= 2
    return x


def _delta_chunk_operands(seg, qkv, beta_all, g_all):
    hd = HEAD_DIM
    bw = BRANCH_WIDTH
    r = HEADS * qkv.shape[0]
    q = _stack_heads(qkv[:, 0:bw])
    k = _stack_heads(qkv[:, bw:2 * bw])
    v = _stack_heads(qkv[:, 2 * bw:3 * bw])
    q = q * lax.rsqrt(jnp.sum(q * q, axis=-1, keepdims=True) + EPS) * (hd ** -0.5)
    k = k * lax.rsqrt(jnp.sum(k * k, axis=-1, keepdims=True) + EPS)
    beta = _stack_head_lanes(beta_all)
    gc = _stack_head_lanes(_segment_cumsum(g_all, seg))
    ri = lax.broadcasted_iota(jnp.int32, (r, r), 0)
    ci = lax.broadcasted_iota(jnp.int32, (r, r), 1)
    shift = seg.bit_length() - 1
    same = (ri >> shift) == (ci >> shift)
    gr = jnp.sum(jnp.where(ri == ci, gc, 0.0), axis=0, keepdims=True)
    decay = jnp.where(same, jnp.where(ri >= ci, jnp.exp(jnp.minimum(gc - gr, 0.0)), 0.0), 0.0)
    a_mat = jnp.where(ri > ci, beta * decay * _dot_nt(k, k), 0.0)
    t_inv = _unit_lower_inverse(a_mat, seg)
    eg = jnp.exp(gc)
    sol = _dot(t_inv, jnp.concatenate([(beta * eg) * k, beta * v], axis=1))
    qk = _dot_nt(q, k) * decay
    is_last = (ci & (seg - 1)) == seg - 1
    gl = jnp.sum(jnp.where(same, jnp.where(is_last, gr, 0.0), 0.0), axis=1, keepdims=True)
    return sol[:, :hd], sol[:, hd:], qk, q * eg, k * jnp.exp(gl - gc), jnp.exp(gl)


def _mixer_d_body(x_ref, beta_ref, dec_ref, z_ref, cst_ref, s0_ref, cw_ref, alog_ref, dtb_ref, ng_ref,
                  o_ref, cst_out_ref, sout_ref, ext, conv, st_ref):
    nb, tb = x_ref.shape[0], x_ref.shape[1]
    c = D_ROWS
    hd = HEAD_DIM
    t = pl.program_id(1)
    nt = pl.num_programs(1)

    @pl.when(t == 0)
    def _():
        ext[:, 5:8, :] = cst_ref[...]
        st_ref[...] = s0_ref[...].reshape(nb * HEADS, hd, hd)

    for bi in range(nb):
        ext[bi, 8:8 + tb, :] = x_ref[bi]
        y = ext[bi, 5:5 + tb, :] * cw_ref[0:1, :]
        for j in range(1, D_CONV):
            y = y + ext[bi, 5 + j:5 + j + tb, :] * cw_ref[j:j + 1, :]
        conv[bi] = _silu(y)

    @pl.when(t == nt - 1)
    def _():
        cst_out_ref[...] = ext[:, 8 + tb - (D_CONV - 1):8 + tb, :]

    ext[:, 5:8, :] = ext[:, 5 + tb:8 + tb, :]

    a_neg = -jnp.exp(alog_ref[...])

    def chunk(n, carry):
        rows = pl.ds(pl.multiple_of(n * c, c), c)
        for bi in range(nb):
            beta_all = jax.nn.sigmoid(beta_ref[bi, rows, :])
            g_all = a_neg * _softplus(dec_ref[bi, rows, :] + dtb_ref[...])
            w, u, qk, qg, kdec, egl = _delta_chunk_operands(c, conv[bi, rows, :], beta_all, g_all)
            deltas, oqs = [], []
            for h in range(HEADS):
                hs = slice(h * c, (h + 1) * c)
                st = st_ref[bi * HEADS + h]
                x = _dot(jnp.concatenate([w[hs], qg[hs]], axis=0), st)
                delta = u[hs] - x[:c]
                deltas.append(delta)
                oqs.append(x[c:])
                st_ref[bi * HEADS + h] = egl[(h + 1) * c - 1:(h + 1) * c, :] * st + _dot_tn(kdec[hs], delta)
            o = jnp.concatenate(oqs, axis=0) + _dot(qk, jnp.concatenate(deltas, axis=0))
            o = _rms_rows(o) * ng_ref[...] * _silu(_stack_heads(z_ref[bi, rows, :]))
            for h in range(HEADS):
                o_ref[bi, rows, h * hd:(h + 1) * hd] = o[h * c:(h + 1) * c].astype(o_ref.dtype)
        return carry

    lax.fori_loop(0, tb // c, chunk, 0)

    @pl.when(t == nt - 1)
    def _():
        sout_ref[...] = st_ref[...].reshape(nb, HEADS, hd, hd)


def _mixer_d_group_body(seg, x_ref, stp_ref, beta_ref, dec_ref, z_ref, s0_ref, cw_ref, alog_ref, dtb_ref, ng_ref,
                        o_ref, cst_out_ref, sout_ref, conv, w_s, u_s, qg_s, kd_s, egl_s, delta_s, oq_s):
    tb = x_ref.shape[1]
    c = D_ROWS
    hd = HEAD_DIM
    pairs = c // 8

    x = x_ref[0]
    stp = stp_ref[0]
    tpos = lax.broadcasted_iota(jnp.int32, (tb, 1), 0) & (seg - 1)
    y = x * cw_ref[3:4, :]
    for d in range(1, D_CONV):
        hist = stp if d == 3 else pltpu.roll(stp, tb - (3 - d), axis=0)
        y = y + jnp.where(tpos < d, hist, pltpu.roll(x, d, axis=0)) * cw_ref[3 - d:4 - d, :]
    conv[...] = _silu(y)
    cst_out_ref[0] = pltpu.roll(x, tb - (seg - 3), axis=0)

    a_neg = -jnp.exp(alog_ref[...])
    low = lax.broadcasted_iota(jnp.int32, (8, 1), 0) < seg

    for n in range(tb // c):
        rows = slice(n * c, (n + 1) * c)
        beta_all = jax.nn.sigmoid(beta_ref[0, rows, :])
        g_all = a_neg * _softplus(dec_ref[0, rows, :] + dtb_ref[...])
        w, u, qk, qg, kdec, egl = _delta_chunk_operands(seg, conv[rows, :], beta_all, g_all)
        w_s[...] = w
        u_s[...] = u
        qg_s[...] = qg
        kd_s[...] = kdec
        egl_s[...] = jnp.broadcast_to(egl, (HEADS * c, hd))

        def pair(p, carry):
            rp = pl.ds(pl.multiple_of(p * 8, 8), 8)
            h = p // pairs
            ja = n * (c // seg) + (p % pairs) * 2
            sa = s0_ref[ja, h]
            sb = s0_ref[ja + 1, h]
            lhs = jnp.concatenate([w_s[rp, :], qg_s[rp, :]], axis=0)
            xa = _dot(lhs, sa)
            xb = _dot(lhs, sb)
            delta = u_s[rp, :] - jnp.where(low, xa[:8], xb[:8])
            delta_s[rp, :] = delta
            oq_s[rp, :] = jnp.where(low, xa[8:], xb[8:])
            kd = kd_s[rp, :]
            e = egl_s[rp, :]
            sout_ref[ja, h] = e[seg - 1:seg, :] * sa + _dot_tn(jnp.where(low, kd, 0.0), delta)
            sout_ref[ja + 1, h] = e[2 * seg - 1:2 * seg, :] * sb + _dot_tn(jnp.where(low, 0.0, kd), delta)
            return carry

        lax.fori_loop(0, HEADS * pairs, pair, 0)
        o = oq_s[...] + _dot(qk, delta_s[...])
        o = _rms_rows(o) * ng_ref[...] * _silu(_stack_heads(z_ref[0, rows, :]))
        for h in range(HEADS):
            o_ref[0, rows, h * hd:(h + 1) * hd] = o[h * c:(h + 1) * c].astype(o_ref.dtype)


def mixer_d(pseq, conv_state, s0, conv_w, a_log, dt_bias, norm_g, *, tb, nb):
    bsz, t, _ = pseq.shape
    bw = BRANCH_WIDTH
    field = lambda off, w: pl.BlockSpec((nb, tb, w), lambda b, s: (b, s, off // w))
    state = pl.BlockSpec((nb, HEADS, HEAD_DIM, HEAD_DIM), lambda b, s: (b, 0, 0, 0))
    cstate = pl.BlockSpec((nb, D_CONV - 1, D_QKV), lambda b, s: (b, 0, 0))
    return pl.pallas_call(
        _mixer_d_body,
        grid=(bsz // nb, t // tb),
        in_specs=[field(OFF_D_QKV, D_QKV), field(OFF_D_BETA, 128), field(OFF_D_DECAY, 128), field(OFF_D_Z, bw),
                  cstate, state,
                  pl.BlockSpec((D_CONV, D_QKV), lambda b, s: (0, 0)),
                  pl.BlockSpec((1, 128), lambda b, s: (0, 0)),
                  pl.BlockSpec((1, 128), lambda b, s: (0, 0)),
                  pl.BlockSpec((1, HEAD_DIM), lambda b, s: (0, 0))],
        out_specs=[pl.BlockSpec((nb, tb, bw), lambda b, s: (b, s, 0)), cstate, state],
        out_shape=[jax.ShapeDtypeStruct((bsz, t, bw), BF16),
                   jax.ShapeDtypeStruct((bsz, D_CONV - 1, D_QKV), F32),
                   jax.ShapeDtypeStruct((bsz, HEADS, HEAD_DIM, HEAD_DIM), F32)],
        scratch_shapes=[pltpu.VMEM((nb, tb + 8, D_QKV), F32), pltpu.VMEM((nb, tb, D_QKV), F32),
                        pltpu.VMEM((nb * HEADS, HEAD_DIM, HEAD_DIM), F32)],
        compiler_params=_cparams(("parallel", "arbitrary")),
        name="mixer_d",
    )(pseq, pseq, pseq, pseq, conv_state, s0, conv_w, _lane_pad_heads(a_log), _lane_pad_heads(dt_bias),
      norm_g.reshape(1, HEAD_DIM))


def mixer_d_grouped(prows, conv_state, s0, conv_w, a_log, dt_bias, norm_g, *, seg, tb):
    m = prows.shape[0]
    bsz = m // seg
    bw = BRANCH_WIDTH
    stp = jnp.pad(conv_state, ((0, 0), (0, seg - (D_CONV - 1)), (0, 0))).reshape(1, m, D_QKV)
    field = lambda off, w: pl.BlockSpec((1, tb, w), lambda s: (0, s, off // w))
    state = pl.BlockSpec((tb // seg, HEADS, HEAD_DIM, HEAD_DIM), lambda s: (s, 0, 0, 0))
    rows128 = pltpu.VMEM((HEADS * D_ROWS, HEAD_DIM), F32)
    o, cst, s_out = pl.pallas_call(
        functools.partial(_mixer_d_group_body, seg),
        grid=(m // tb,),
        in_specs=[field(OFF_D_QKV, D_QKV), pl.BlockSpec((1, tb, D_QKV), lambda s: (0, s, 0)),
                  field(OFF_D_BETA, 128), field(OFF_D_DECAY, 128), field(OFF_D_Z, bw),
                  state,
                  pl.BlockSpec((D_CONV, D_QKV), lambda s: (0, 0)),
                  pl.BlockSpec((1, 128), lambda s: (0, 0)),
                  pl.BlockSpec((1, 128), lambda s: (0, 0)),
                  pl.BlockSpec((1, HEAD_DIM), lambda s: (0, 0))],
        out_specs=[pl.BlockSpec((1, tb, bw), lambda s: (0, s, 0)),
                   pl.BlockSpec((1, tb, D_QKV), lambda s: (0, s, 0)), state],
        out_shape=[jax.ShapeDtypeStruct((1, m, bw), BF16),
                   jax.ShapeDtypeStruct((1, m, D_QKV), F32),
                   jax.ShapeDtypeStruct((bsz, HEADS, HEAD_DIM, HEAD_DIM), F32)],
        scratch_shapes=[pltpu.VMEM((tb, D_QKV), F32)] + [rows128] * 7,
        compiler_params=_cparams(("parallel",)),
        name="mixer_d_grouped",
    )(prows.reshape(1, m, -1), stp, prows.reshape(1, m, -1), prows.reshape(1, m, -1), prows.reshape(1, m, -1),
      s0, conv_w, _lane_pad_heads(a_log), _lane_pad_heads(dt_bias), norm_g.reshape(1, HEAD_DIM))
    return o.reshape(m, bw), cst.reshape(bsz, seg, D_QKV)[:, :D_CONV - 1], s_out


def _pack_w_in(w):
    z = lambda n: jnp.zeros((w.shape[0], n), w.dtype)
    cols = [w[:, 4624:6160],
            w[:, 0:512], w[:, 512:1024],
            w[:, 1024:1536], w[:, 1536:2048], w[:, 2048:2560], w[:, 2560:3072],
            w[:, 3584:4096], w[:, 4112:4624],
            w[:, 6168:6680],
            w[:, 3072:3328], w[:, 3328:3584],
            w[:, 4096:4112], z(112),
            w[:, 6160:6164], z(124),
            w[:, 6164:6168], z(252),
            w[:, 6680:14872]]
    return jnp.concatenate(cols, axis=1).astype(BF16)


def _trunk(x, p, s_hgrn, s_gla, s_delta, c_dconv, c_fconv, wts, lbs, *, sample):
    bsz, t, _ = x.shape
    m = bsz * t
    tm = m if sample else 1024
    h = x.reshape(m, D_MODEL)
    outs = {k: [] for k in ("hgrn", "gla", "delta", "dconv", "fconv", "v")}
    for l in range(DEPTH):
        w = wts[l]
        proj = norm_matmul(h, w["g_mix"], w["w_in"], tm=tm, tn=1024)
        o_a, v_a = mixer_a(proj, w["a_ln_g"], w["a_ln_b"], w["a_w_mix"], w["a_bias_rows"])
        b_args = (lbs[l], w["b_norm_g"], s_hgrn[l])
        c_args = (w["c_w_a2"], w["c_b_a"], w["c_norm_g"], s_gla[l])
        d_args = (c_dconv[l], s_delta[l], w["d_conv_w"], w["d_a_log"], w["d_dt_bias"], w["d_norm_g"])
        if sample:
            o_b, s_b = mixer_b_grouped(proj, *b_args, seg=t, tb=D_ROWS)
            o_c, s_c = mixer_c_grouped(proj, *c_args, seg=t, tb=D_ROWS)
            o_d, nb_d, s_d = mixer_d_grouped(proj, *d_args, seg=t, tb=D_ROWS)
        else:
            pseq = proj.reshape(bsz, t, N_PROJ)
            flat = lambda o: o.reshape(m, BRANCH_WIDTH)
            o_b, s_b = mixer_b(pseq, *b_args, t_valid=t, tb=256, nb=1, unroll=1)
            o_c, s_c = mixer_c(pseq, *c_args, t_valid=t, tb=256, nb=1, unroll=1)
            o_d, nb_d, s_d = mixer_d(pseq, *d_args, tb=256, nb=2 if bsz % 2 == 0 else 1)
            o_b, o_c, o_d = flat(o_b), flat(o_c), flat(o_d)
        merged = merge_branches((o_a, o_b, o_c, o_d), proj, w["w_branch"], tm=tm, tn=512)
        h = matmul_residual(merged, w["w_out"], h, tm=tm, tn=1024)

        act, nb_f = ffn_up_gate(h, w["g_ffn"], w["w_ffn_up"], c_fconv[l], w["ffn_conv_w"], w["ffn_conv_b"],
                                seq_len=t, tm=tm, tn=512)
        h = matmul_residual(act, w["w_ffn_down"], h, tm=min(tm, 512), tn=512)

        h = ple_update(h, p[l].reshape(m, PLE_DIM), w["g_ple"], w["w_ple_gate"], w["w_ple_proj"], tm=tm, tn=512)

        outs["hgrn"].append(s_b)
        outs["gla"].append(s_c)
        outs["delta"].append(s_d)
        outs["dconv"].append(nb_d)
        outs["fconv"].append(nb_f)
        outs["v"].append(v_a.reshape(bsz, t, BRANCH_WIDTH))
    y = final_norm(h, wts[0]["g_final"], tm=min(tm, 512)).reshape(bsz, t, D_MODEL)
    return (y,) + tuple(jnp.stack(outs[k]) for k in ("hgrn", "gla", "delta", "dconv", "fconv", "v"))


def kernel(x_prompt, x_sample, state_hgrn, state_gla, state_delta, state_delta_conv, state_ffn_conv, p_prompt, p_sample, g_mix, w_in, a_ln_g, a_ln_b, a_w_s, a_b_s, b_lb, b_norm_g, c_w_a2, c_b_a, c_norm_g, d_conv_w, d_a_log, d_dt_bias, d_norm_g, w_branch, w_out, g_ffn, w_ffn_up, ffn_conv_w, ffn_conv_b, w_ffn_down, g_ple, w_ple_gate, w_ple_proj, g_final):
    bp, t_p, _ = x_prompt.shape
    bs, t_s, _ = x_sample.shape
    sm = jax.nn.softmax(b_lb.astype(F32), axis=0)
    lbs = jnp.cumsum(sm, axis=0) - sm[0]

    shared = [dict(
        g_mix=g_mix[l], w_in=_pack_w_in(w_in[l]), a_ln_g=a_ln_g[l], a_ln_b=a_ln_b[l],
        b_norm_g=b_norm_g[l], c_w_a2=c_w_a2[l], c_b_a=c_b_a[l], c_norm_g=c_norm_g[l],
        d_conv_w=d_conv_w[l], d_a_log=d_a_log[l], d_dt_bias=d_dt_bias[l], d_norm_g=d_norm_g[l],
        w_branch=w_branch[l].astype(BF16), w_out=w_out[l].astype(BF16), g_ffn=g_ffn[l],
        w_ffn_up=w_ffn_up[l].astype(BF16), ffn_conv_w=ffn_conv_w[l], ffn_conv_b=ffn_conv_b[l],
        w_ffn_down=w_ffn_down[l].astype(BF16), g_ple=g_ple[l], w_ple_gate=w_ple_gate[l].astype(BF16),
        w_ple_proj=w_ple_proj[l].astype(BF16), g_final=g_final) for l in range(DEPTH)]

    def layer_weights(l, sample):
        if sample:
            seqs = A_CHUNK // t_s
            w_mix = jnp.einsum("ab,gts->gatbs", jnp.eye(seqs, dtype=F32), a_w_s[l, :, :t_s, :t_s])
            w_mix = w_mix.reshape(A_GROUPS, A_CHUNK, A_CHUNK)
            bias_rows = jnp.tile(a_b_s[l, :, :t_s].T, (seqs, 1))
        else:
            w_mix = a_w_s[l]
            bias_rows = a_b_s[l].T
        return dict(shared[l], a_w_mix=w_mix, a_bias_rows=bias_rows)

    dt = x_prompt.dtype
    zeros = lambda *s: jnp.zeros((DEPTH, bp) + s, dt)
    out_p = _trunk(x_prompt, p_prompt, zeros(HEADS, HEAD_DIM, HEAD_DIM), zeros(HEADS, C_KDIM, HEAD_DIM),
                   zeros(HEADS, HEAD_DIM, HEAD_DIM), zeros(D_CONV - 1, D_QKV), zeros(FFN_CONV - 1, FFN_DIM),
                   [layer_weights(l, False) for l in range(DEPTH)], lbs, sample=False)
    out_s = _trunk(x_sample, p_sample, state_hgrn, state_gla, state_delta, state_delta_conv, state_ffn_conv,
                   [layer_weights(l, True) for l in range(DEPTH)], lbs, sample=True)
    y_p, hgrn_p, gla_p, delta_p, dconv_p, fconv_p, _ = out_p
    y_s, hgrn_s, gla_s, delta_s, dconv_s, fconv_s, v_s = out_s
    return (y_p, y_s, hgrn_p, hgrn_s, gla_p, gla_s, delta_p, delta_s,
            dconv_p, dconv_s, fconv_p, fconv_s, v_s)
```

```python
import functools

import jax
import jax.numpy as jnp
from jax import lax
from jax.experimental import pallas as pl
from jax.experimental.pallas import tpu as pltpu

F32 = jnp.float32
BF16 = jnp.bfloat16

D_MODEL = 2048
DEPTH = 2
PLE_DIM = 256
EPS = 1e-6
F_TINY = 1e-30
N_BRANCH = 4
BRANCH_WIDTH = D_MODEL // 4
A_GROUPS = 4
A_CHUNK = 128
HEADS = 4
HEAD_DIM = 128
C_KDIM = 64
C_RANK = 16
C_TAU = 16.0
D_CONV = 4
D_QKV = 3 * BRANCH_WIDTH
FFN_DIM = 5632
FFN_CONV = 3
GLA_CHUNK = 16
LOG2E = 1.4426950408889634

OFF_D_QKV = 0
OFF_A_U = 1536
OFF_A_V = 2048
OFF_B_Q = 2560
OFF_B_F = 3072
OFF_B_I = 3584
OFF_B_G = 4096
OFF_C_V = 4608
OFF_C_R = 5120
OFF_D_Z = 5632
OFF_C_Q = 6144
OFF_C_K = 6400
OFF_C_LR = 6656
OFF_D_BETA = 6784
OFF_D_DECAY = 6912
OFF_GATES = 7168
N_SEQ_COLS = 7168
N_PROJ = OFF_GATES + N_BRANCH * D_MODEL

VMEM_LIMIT = 56 * 1024 * 1024


def _cparams(sem):
    return pltpu.CompilerParams(dimension_semantics=sem, vmem_limit_bytes=VMEM_LIMIT)


def _gelu(x):
    return 0.5 * x * (1.0 + jnp.tanh(0.7978845608028654 * (x + 0.044715 * (x * x * x))))


def _silu(x):
    return x * jax.nn.sigmoid(x)


def _softplus(x):
    return jnp.maximum(x, 0.0) + jnp.log1p(jnp.exp(-jnp.abs(x)))


def _log_sigmoid(x):
    return -_softplus(-x)


def _rms_rows(x):
    return x * lax.rsqrt(jnp.mean(x * x, axis=-1, keepdims=True) + EPS)


def _dot(a, b):
    return jnp.dot(a, b, preferred_element_type=F32)


def _dot_nt(a, b):
    return lax.dot_general(a, b, (((1,), (1,)), ((), ())), preferred_element_type=F32)


def _dot_tn(a, b):
    return lax.dot_general(a, b, (((0,), (0,)), ((), ())), preferred_element_type=F32)


def _cumsum_rows(x):
    n = x.shape[0]
    row = lax.broadcasted_iota(jnp.int32, x.shape, 0)
    sh = 1
    while sh < n:
        x = x + jnp.where(row >= sh, pltpu.roll(x, sh, axis=0), 0.0)
        sh *= 2
    return x


NORM_ROWS = 128


def _norm_to_scratch(x_ref, g_ref, a_scr):
    def slab(n, carry):
        rows = pl.ds(pl.multiple_of(n * NORM_ROWS, NORM_ROWS), NORM_ROWS)
        a_scr[rows, :] = (_rms_rows(x_ref[rows, :]) * g_ref[...]).astype(BF16)
        return carry

    lax.fori_loop(0, x_ref.shape[0] // NORM_ROWS, slab, 0)


def _matmul_residual_body(x_ref, w_ref, r_ref, o_ref, w_scr):
    @pl.when(pl.program_id(1) == 0)
    def _():
        def slab(n, carry):
            rows = pl.ds(pl.multiple_of(n * NORM_ROWS, NORM_ROWS), NORM_ROWS)
            w_scr[rows, :] = w_ref[rows, :].astype(BF16)
            return carry

        lax.fori_loop(0, w_ref.shape[0] // NORM_ROWS, slab, 0)

    o_ref[...] = r_ref[...] + _dot(x_ref[...], w_scr[...])


def matmul_residual(x, w_layers, layer, res, *, tm, tn):
    m, k = x.shape
    n = w_layers.shape[2]
    return pl.pallas_call(
        _matmul_residual_body,
        grid=(n // tn, m // tm),
        in_specs=[pl.BlockSpec((tm, k), lambda j, i: (i, 0)),
                  pl.BlockSpec((None, k, tn), lambda j, i: (layer, 0, j)),
                  pl.BlockSpec((tm, tn), lambda j, i: (i, j))],
        out_specs=pl.BlockSpec((tm, tn), lambda j, i: (i, j)),
        out_shape=jax.ShapeDtypeStruct((m, n), F32),
        scratch_shapes=[pltpu.VMEM((k, tn), BF16)],
        compiler_params=_cparams(("parallel", "arbitrary")),
        name="matmul_residual",
    )(x, w_layers, res)


def _in_proj_body(n_seq_tiles, x_ref, g_ref, w_ref, seq_ref, gate_ref, a_scr):
    j = pl.program_id(1)

    @pl.when(j == 0)
    def _():
        _norm_to_scratch(x_ref, g_ref, a_scr)

    acc = _dot(a_scr[...], w_ref[...])

    @pl.when(j < n_seq_tiles)
    def _():
        seq_ref[...] = acc

    @pl.when(j >= n_seq_tiles)
    def _():
        gate_ref[...] = acc.astype(gate_ref.dtype)


def in_proj(x, g, w, *, tm, tn):
    m, k = x.shape
    n_seq = N_SEQ_COLS // tn
    n_gate = N_BRANCH * D_MODEL // tn
    return pl.pallas_call(
        functools.partial(_in_proj_body, n_seq),
        grid=(m // tm, n_seq + n_gate),
        in_specs=[pl.BlockSpec((tm, k), lambda i, j: (i, 0)),
                  pl.BlockSpec((1, k), lambda i, j: (0, 0)),
                  pl.BlockSpec((k, tn), lambda i, j: (0, j))],
        out_specs=[pl.BlockSpec((tm, tn), lambda i, j: (i, jnp.minimum(j, n_seq - 1))),
                   pl.BlockSpec((tm, tn), lambda i, j: (i, jnp.maximum(j - n_seq, 0)))],
        out_shape=[jax.ShapeDtypeStruct((m, N_SEQ_COLS), F32),
                   jax.ShapeDtypeStruct((m, N_BRANCH * D_MODEL), BF16)],
        scratch_shapes=[pltpu.VMEM((tm, k), BF16)],
        compiler_params=_cparams(("parallel", "arbitrary")),
        name="in_proj",
    )(x, g.reshape(1, k), w)


def _merge_body(oa_ref, ob_ref, oc_ref, od_ref, ga_ref, gb_ref, gc_ref, gd_ref, w_ref, o_ref):
    acc = None
    for b, (o_b, g_b) in enumerate(((oa_ref, ga_ref), (ob_ref, gb_ref), (oc_ref, gc_ref), (od_ref, gd_ref))):
        term = jax.nn.sigmoid(g_b[...].astype(F32)) * _dot(o_b[...], w_ref[b])
        acc = term if acc is None else acc + term
    o_ref[...] = acc.astype(o_ref.dtype)


def merge_branches(o_branches, gates, w_branch, *, tm, tn):
    m = gates.shape[0]
    gate_specs = [pl.BlockSpec((tm, tn), functools.partial(
        lambda i, j, b: (i, b * (D_MODEL // tn) + j), b=b)) for b in range(N_BRANCH)]
    return pl.pallas_call(
        _merge_body,
        grid=(m // tm, D_MODEL // tn),
        in_specs=[pl.BlockSpec((tm, BRANCH_WIDTH), lambda i, j: (i, 0))] * N_BRANCH + gate_specs
        + [pl.BlockSpec((N_BRANCH, BRANCH_WIDTH, tn), lambda i, j: (0, 0, j))],
        out_specs=pl.BlockSpec((tm, tn), lambda i, j: (i, j)),
        out_shape=jax.ShapeDtypeStruct((m, D_MODEL), BF16),
        compiler_params=_cparams(("parallel", "arbitrary")),
        name="merge_branches",
    )(*o_branches, gates, gates, gates, gates, w_branch)


def _ple_body(x_ref, xres_ref, p_ref, g_ref, wg_ref, wp_ref, o_ref, a_scr, p_scr):
    @pl.when(pl.program_id(1) == 0)
    def _():
        _norm_to_scratch(x_ref, g_ref, a_scr)
        p_scr[...] = p_ref[...].astype(BF16)

    gate = jax.nn.sigmoid(_dot(a_scr[...], wg_ref[...]))
    o_ref[...] = xres_ref[...] + _dot(p_scr[...], wp_ref[...]) * gate


def ple_update(x, p, g, w_gate, w_proj, *, tm, tn):
    m, k = x.shape
    return pl.pallas_call(
        _ple_body,
        grid=(m // tm, D_MODEL // tn),
        in_specs=[pl.BlockSpec((tm, k), lambda i, j: (i, 0)),
                  pl.BlockSpec((tm, tn), lambda i, j: (i, j)),
                  pl.BlockSpec((tm, PLE_DIM), lambda i, j: (i, 0)),
                  pl.BlockSpec((1, k), lambda i, j: (0, 0)),
                  pl.BlockSpec((k, tn), lambda i, j: (0, j)),
                  pl.BlockSpec((PLE_DIM, tn), lambda i, j: (0, j))],
        out_specs=pl.BlockSpec((tm, tn), lambda i, j: (i, j)),
        out_shape=jax.ShapeDtypeStruct((m, D_MODEL), F32),
        scratch_shapes=[pltpu.VMEM((tm, k), BF16), pltpu.VMEM((tm, PLE_DIM), BF16)],
        compiler_params=_cparams(("parallel", "arbitrary")),
        name="ple_update",
    )(x, x, p, g.reshape(1, k), w_gate, w_proj)


def _final_norm_body(x_ref, g_ref, o_ref):
    def slab(n, carry):
        rows = pl.ds(pl.multiple_of(n * NORM_ROWS, NORM_ROWS), NORM_ROWS)
        o_ref[rows, :] = _rms_rows(x_ref[rows, :]) * g_ref[...]
        return carry

    lax.fori_loop(0, x_ref.shape[0] // NORM_ROWS, slab, 0)


def final_norm(x, g, *, tm):
    m, k = x.shape
    return pl.pallas_call(
        _final_norm_body,
        grid=(m // tm,),
        in_specs=[pl.BlockSpec((tm, k), lambda i: (i, 0)), pl.BlockSpec((1, k), lambda i: (0, 0))],
        out_specs=pl.BlockSpec((tm, k), lambda i: (i, 0)),
        out_shape=jax.ShapeDtypeStruct((m, k), F32),
        compiler_params=_cparams(("parallel",)),
        name="final_norm",
    )(x, g.reshape(1, k))


FFN_SUB = 256


def _cast_rows_to_bf16(src_ref, dst_ref):
    def slab(n, carry):
        rows = pl.ds(pl.multiple_of(n * NORM_ROWS, NORM_ROWS), NORM_ROWS)
        dst_ref[rows, :] = src_ref[rows, :].astype(BF16)
        return carry

    lax.fori_loop(0, src_ref.shape[0] // NORM_ROWS, slab, 0)


def _ffn_up_long_body(tiles_per_seq, a_ref, wg_ref, wu_ref, st_ref, cw_ref, cb_ref, o_ref, st_out_ref,
                      wg_s, wu_s, ext):
    i = pl.program_id(1)
    tm = a_ref.shape[0]

    @pl.when(i == 0)
    def _():
        _cast_rows_to_bf16(wg_ref, wg_s)
        _cast_rows_to_bf16(wu_ref, wu_s)

    @pl.when(i % tiles_per_seq == 0)
    def _():
        ext[6:8, :] = st_ref[0]

    for r in range(0, tm, FFN_SUB):
        a = a_ref[r:r + FFN_SUB, :]
        ext[8 + r:8 + r + FFN_SUB, :] = _dot(a, wg_s[...])
        conv = (ext[6 + r:6 + r + FFN_SUB, :] * cw_ref[0:1, :] + ext[7 + r:7 + r + FFN_SUB, :] * cw_ref[1:2, :]
                + ext[8 + r:8 + r + FFN_SUB, :] * cw_ref[2:3, :])
        o_ref[r:r + FFN_SUB, :] = (_gelu(conv + cb_ref[...]) * _dot(a, wu_s[...])).astype(o_ref.dtype)

    last = ext[6 + tm:8 + tm, :]
    ext[6:8, :] = last
    st_out_ref[0] = last


def _ffn_up_group_body(seg, a_ref, wg_ref, wu_ref, stp_ref, cw_ref, cb_ref, o_ref, st_out_ref, wg_s, wu_s):
    tm = a_ref.shape[0]

    @pl.when(pl.program_id(1) == 0)
    def _():
        _cast_rows_to_bf16(wg_ref, wg_s)
        _cast_rows_to_bf16(wu_ref, wu_s)

    a = a_ref[...]
    fg = _dot(a, wg_s[...])
    stp = stp_ref[...]
    tpos = lax.broadcasted_iota(jnp.int32, (tm, 1), 0) & (seg - 1)
    lag1 = jnp.where(tpos < 1, pltpu.roll(stp, tm - 1, axis=0), pltpu.roll(fg, 1, axis=0))
    lag2 = jnp.where(tpos < 2, stp, pltpu.roll(fg, 2, axis=0))
    conv = lag2 * cw_ref[0:1, :] + lag1 * cw_ref[1:2, :] + fg * cw_ref[2:3, :]
    o_ref[...] = (_gelu(conv + cb_ref[...]) * _dot(a, wu_s[...])).astype(o_ref.dtype)
    st_out_ref[...] = pltpu.roll(fg, tm - (seg - 2), axis=0)


def _rms_bf16_body(x_ref, g_ref, o_ref):
    _norm_to_scratch(x_ref, g_ref, o_ref)


def rms_bf16(x, g, *, tm):
    m, k = x.shape
    return pl.pallas_call(
        _rms_bf16_body,
        grid=(m // tm,),
        in_specs=[pl.BlockSpec((tm, k), lambda i: (i, 0)), pl.BlockSpec((1, k), lambda i: (0, 0))],
        out_specs=pl.BlockSpec((tm, k), lambda i: (i, 0)),
        out_shape=jax.ShapeDtypeStruct((m, k), BF16),
        compiler_params=_cparams(("parallel",)),
        name="rms_bf16",
    )(x, g.reshape(1, k))


def ffn_up_gate(x, g, w_up_layers, layer, state, conv_w, conv_b, *, seq_len, tm, tn):
    m, k = x.shape
    bsz = m // seq_len
    ncol = FFN_DIM // tn
    a = rms_bf16(x, g, tm=min(tm, 512))
    common_in = [pl.BlockSpec((tm, k), lambda j, i: (i, 0)),
                 pl.BlockSpec((None, k, tn), lambda j, i: (layer, 0, j)),
                 pl.BlockSpec((None, k, tn), lambda j, i: (layer, 0, ncol + j))]
    conv_in = [pl.BlockSpec((FFN_CONV, tn), lambda j, i: (0, j)), pl.BlockSpec((1, tn), lambda j, i: (0, j))]
    act_spec = pl.BlockSpec((tm, tn), lambda j, i: (i, j))
    act_shape = jax.ShapeDtypeStruct((m, FFN_DIM), BF16)
    w_scr = [pltpu.VMEM((k, tn), BF16)] * 2
    if seq_len >= tm:
        tps = seq_len // tm
        act, tile_last = pl.pallas_call(
            functools.partial(_ffn_up_long_body, tps),
            grid=(ncol, m // tm),
            in_specs=common_in + [pl.BlockSpec((1, FFN_CONV - 1, tn), lambda j, i: (i // tps, 0, j))] + conv_in,
            out_specs=[act_spec, pl.BlockSpec((1, FFN_CONV - 1, tn), lambda j, i: (i, 0, j))],
            out_shape=[act_shape, jax.ShapeDtypeStruct((m // tm, FFN_CONV - 1, FFN_DIM), F32)],
            scratch_shapes=w_scr + [pltpu.VMEM((tm + 8, tn), F32)],
            compiler_params=_cparams(("parallel", "arbitrary")),
            name="ffn_up_gate",
        )(a, w_up_layers, w_up_layers, state, conv_w, conv_b.reshape(1, FFN_DIM))
        return act, tile_last[tps - 1::tps]
    stp = jnp.pad(state, ((0, 0), (0, seq_len - (FFN_CONV - 1)), (0, 0))).reshape(m, FFN_DIM)
    act, st_rows = pl.pallas_call(
        functools.partial(_ffn_up_group_body, seq_len),
        grid=(ncol, m // tm),
        in_specs=common_in + [act_spec] + conv_in,
        out_specs=[act_spec, act_spec],
        out_shape=[act_shape, jax.ShapeDtypeStruct((m, FFN_DIM), F32)],
        scratch_shapes=w_scr,
        compiler_params=_cparams(("parallel", "arbitrary")),
        name="ffn_up_gate_grouped",
    )(a, w_up_layers, w_up_layers, stp, conv_w, conv_b.reshape(1, FFN_DIM))
    return act, st_rows.reshape(bsz, seq_len, FFN_DIM)[:, :FFN_CONV - 1]


def _mixer_a_body(u_ref, v_ref, lng_ref, lnb_ref, w_ref, bias_ref, o_ref, vout_ref):
    u = _gelu(u_ref[...])
    v = _gelu(v_ref[...])
    vc = v - jnp.mean(v, axis=-1, keepdims=True)
    var = jnp.mean(vc * vc, axis=-1, keepdims=True)
    vn = vc * lax.rsqrt(var + EPS) * lng_ref[...] + lnb_ref[...]
    vout_ref[...] = vn
    n = w_ref.shape[1]
    causal = lax.broadcasted_iota(jnp.int32, (n, n), 0) >= lax.broadcasted_iota(jnp.int32, (n, n), 1)
    gd = BRANCH_WIDTH // A_GROUPS
    for g in range(A_GROUPS):
        w = jnp.where(causal, w_ref[g], 0.0)
        mixed = _dot(w, vn[:, g * gd:(g + 1) * gd]) + bias_ref[:, g:g + 1]
        o_ref[:, g * gd:(g + 1) * gd] = (u[:, g * gd:(g + 1) * gd] * mixed).astype(o_ref.dtype)


def mixer_a(proj, ln_g, ln_b, w_mix, bias_rows):
    m = proj.shape[0]
    r = A_CHUNK
    bw = BRANCH_WIDTH
    return pl.pallas_call(
        _mixer_a_body,
        grid=(m // r,),
        in_specs=[pl.BlockSpec((r, bw), lambda i: (i, OFF_A_U // bw)),
                  pl.BlockSpec((r, bw), lambda i: (i, OFF_A_V // bw)),
                  pl.BlockSpec((1, bw), lambda i: (0, 0)),
                  pl.BlockSpec((1, bw), lambda i: (0, 0)),
                  pl.BlockSpec((A_GROUPS, r, r), lambda i: (0, 0, 0)),
                  pl.BlockSpec((r, A_GROUPS), lambda i: (0, 0))],
        out_specs=[pl.BlockSpec((r, bw), lambda i: (i, 0)), pl.BlockSpec((r, bw), lambda i: (i, 0))],
        out_shape=[jax.ShapeDtypeStruct((m, bw), BF16), jax.ShapeDtypeStruct((m, bw), F32)],
        compiler_params=_cparams(("parallel",)),
        name="mixer_a",
    )(proj, proj, ln_g.reshape(1, bw), ln_b.reshape(1, bw), w_mix, bias_rows)


def _gla_head_chunk(q, k, v, g, n_src):
    c = q.shape[0]
    g2 = _cumsum_rows(g) * LOG2E
    n_tiles = c // 8
    row8 = lax.broadcasted_iota(jnp.int32, (8, 1), 0)
    q_t = [q[i * 8:(i + 1) * 8] for i in range(n_tiles)]
    g_t = [g2[i * 8:(i + 1) * 8] for i in range(n_tiles)]
    o_t = [jnp.zeros((8, v.shape[1]), F32) for _ in range(n_tiles)]
    for s in range(n_src):
        for i in range(s // 8, n_tiles):
            a = jnp.sum(q_t[i] * jnp.exp2(g_t[i] - g2[s:s + 1, :]) * k[s:s + 1, :], axis=-1, keepdims=True)
            if i == s // 8:
                a = jnp.where(row8 >= s - 8 * i, a, 0.0)
            o_t[i] = o_t[i] + a * v[s:s + 1, :]
    o = jnp.concatenate(o_t, axis=0) if n_tiles > 1 else o_t[0]
    gl = g2[c - 1:c, :]
    return o, q * jnp.exp2(g2), k * jnp.exp2(gl - g2), jnp.exp2(gl)


def _head_block_diag(xs):
    z = jnp.zeros_like(xs[0])
    return jnp.concatenate(
        [jnp.concatenate([x if j == h else z for j in range(HEADS)], axis=1) for h, x in enumerate(xs)], axis=0)


def _gla_state_dots(qgs, kds, vs, es, st_ref, bi):
    st = st_ref[bi]
    o_inter = _dot_nt(_head_block_diag(qgs), st)
    st_ref[bi] = st * jnp.concatenate(es, axis=1) + _dot_tn(jnp.concatenate(vs, axis=0), _head_block_diag(kds))
    return o_inter


def _gla_state_step(parts, vs, st_ref, bi):
    c = vs[0].shape[0]
    o_inter = _gla_state_dots([p[1] for p in parts], [p[2] for p in parts], vs, [p[3] for p in parts], st_ref, bi)
    return [p[0] + o_inter[h * c:(h + 1) * c] for h, p in enumerate(parts)]


def _gla_pair_tile(q, k, v, g, seg):
    g2 = _segment_cumsum(g, seg) * LOG2E
    tpos = lax.broadcasted_iota(jnp.int32, (8, 1), 0) & (seg - 1)
    o = jnp.sum(q * k, axis=-1, keepdims=True) * v
    for d in range(1, seg):
        a = jnp.sum(q * jnp.exp2(g2 - pltpu.roll(g2, d, axis=0)) * pltpu.roll(k, d, axis=0), axis=-1, keepdims=True)
        o = o + jnp.where(tpos >= d, a, 0.0) * pltpu.roll(v, d, axis=0)
    first = lax.broadcasted_iota(jnp.int32, (8, 1), 0) < seg
    gl_a, gl_b = g2[seg - 1:seg, :], g2[2 * seg - 1:2 * seg, :]
    kd = k * jnp.exp2(jnp.where(first, gl_a, gl_b) - g2)
    return o, q * jnp.exp2(g2), kd, jnp.exp2(gl_a), jnp.exp2(gl_b)


def _gla_pair_step(tiles, vs, st_ref, p, seg):
    first = lax.broadcasted_iota(jnp.int32, (8, 1), 0) < seg
    qgs = [t[1] for t in tiles]
    oi_a = _gla_state_dots(qgs, [jnp.where(first, t[2], 0.0) for t in tiles], vs, [t[3] for t in tiles],
                           st_ref, 2 * p)
    oi_b = _gla_state_dots(qgs, [jnp.where(first, 0.0, t[2]) for t in tiles], vs, [t[4] for t in tiles],
                           st_ref, 2 * p + 1)
    return [t[0] + jnp.where(first, oi_a[h * 8:(h + 1) * 8], oi_b[h * 8:(h + 1) * 8]) for h, t in enumerate(tiles)]


def _load_state_t(s0_ref, st_ref, kdim):
    def per_seq(bi, carry):
        for h in range(HEADS):
            s = s0_ref[bi, h]
            if kdim < HEAD_DIM:
                s = jnp.concatenate([s, jnp.zeros((HEAD_DIM - kdim, HEAD_DIM), F32)], axis=0)
            st_ref[bi, :, h * HEAD_DIM:(h + 1) * HEAD_DIM] = s.T
        return carry

    lax.fori_loop(0, s0_ref.shape[0], per_seq, 0)


def _store_state_t(st_ref, sout_ref, kdim):
    def per_seq(bi, carry):
        for h in range(HEADS):
            sout_ref[bi, h] = st_ref[bi, :, h * HEAD_DIM:(h + 1) * HEAD_DIM].T[:kdim, :]
        return carry

    lax.fori_loop(0, sout_ref.shape[0], per_seq, 0)


def _seq_chunk_loop(nb, n_chunks, unroll, fn):
    def step(it, carry):
        n = 0 if n_chunks == 1 else it % n_chunks
        bj = 0 if nb == unroll else it // n_chunks
        for u in range(unroll):
            fn(bj * unroll + u, n)
        return carry

    lax.fori_loop(0, (nb // unroll) * n_chunks, step, 0, unroll=4)


def _chunk_rows(n, c):
    if isinstance(n, int):
        return n * c, pl.ds(n * c, c)
    r0 = pl.multiple_of(n * c, c)
    return r0, pl.ds(r0, c)


def _valid_rows(t_valid, tb, r0, c):
    return pl.program_id(1) * tb + r0 + lax.broadcasted_iota(jnp.int32, (c, 1), 0) < t_valid


def _mixer_b_body(t_valid, t_pad, unroll, q_ref, f_ref, i_ref, og_ref, lb_ref, ng_ref, s0_ref, o_ref, sout_ref,
                  st_ref):
    nb, tb = q_ref.shape[0], q_ref.shape[1]
    c = min(GLA_CHUNK, tb)
    kd = HEAD_DIM
    padded = t_valid < t_pad
    n_src = t_valid if padded else c

    @pl.when(pl.program_id(1) == 0)
    def _():
        _load_state_t(s0_ref, st_ref, kd)

    def chunk(bi, n):
        r0, rows = _chunk_rows(n, c)
        parts, vs = [], []
        for h in range(HEADS):
            cols = slice(h * kd, (h + 1) * kd)
            lb = lb_ref[:, cols]
            fp = f_ref[bi, rows, cols]
            f = lb + (1.0 - lb) * jax.nn.sigmoid(fp)
            g = jnp.log(jnp.maximum(f, F_TINY))
            k = (1.0 - lb) * jax.nn.sigmoid(-fp)
            if padded:
                ok = _valid_rows(t_valid, tb, r0, c)
                g = jnp.where(ok, g, 0.0)
                k = jnp.where(ok, k, 0.0)
            vs.append(i_ref[bi, rows, cols])
            parts.append(_gla_head_chunk(q_ref[bi, rows, cols], k, vs[h], g, n_src))
        for h, o in enumerate(_gla_state_step(parts, vs, st_ref, bi)):
            cols = slice(h * kd, (h + 1) * kd)
            o = _rms_rows(o) * ng_ref[...] * jax.nn.sigmoid(og_ref[bi, rows, cols])
            o_ref[bi, rows, cols] = o.astype(o_ref.dtype)

    _seq_chunk_loop(nb, tb // c, unroll, chunk)

    @pl.when(pl.program_id(1) == pl.num_programs(1) - 1)
    def _():
        _store_state_t(st_ref, sout_ref, kd)


def mixer_b(pseq, lb, norm_g, s0, *, t_valid, tb, nb, unroll):
    bsz, t_pad, _ = pseq.shape
    assert t_valid == t_pad or t_pad == tb <= GLA_CHUNK
    bw = BRANCH_WIDTH
    field = lambda off: pl.BlockSpec((nb, tb, bw), lambda b, s: (b, s, off // bw))
    state = pl.BlockSpec((nb, HEADS, HEAD_DIM, HEAD_DIM), lambda b, s: (b, 0, 0, 0))
    return pl.pallas_call(
        functools.partial(_mixer_b_body, t_valid, t_pad, unroll),
        grid=(bsz // nb, t_pad // tb),
        in_specs=[field(OFF_B_Q), field(OFF_B_F), field(OFF_B_I), field(OFF_B_G),
                  pl.BlockSpec((1, bw), lambda b, s: (0, 0)),
                  pl.BlockSpec((1, HEAD_DIM), lambda b, s: (0, 0)),
                  state],
        out_specs=[pl.BlockSpec((nb, tb, bw), lambda b, s: (b, s, 0)), state],
        out_shape=[jax.ShapeDtypeStruct((bsz, t_pad, bw), BF16),
                   jax.ShapeDtypeStruct((bsz, HEADS, HEAD_DIM, HEAD_DIM), F32)],
        scratch_shapes=[pltpu.VMEM((nb, HEAD_DIM, HEADS * HEAD_DIM), F32)],
        compiler_params=_cparams(("parallel", "arbitrary")),
        name="mixer_b",
    )(pseq, pseq, pseq, pseq, lb.reshape(1, bw), norm_g.reshape(1, HEAD_DIM), s0)


def _mixer_c_body(t_valid, t_pad, unroll, q_ref, k_ref, v_ref, r_ref, lr_ref, w2_ref, ba_ref, ng_ref, s0_ref,
                  o_ref, sout_ref, st_ref):
    nb, tb = q_ref.shape[0], q_ref.shape[1]
    c = min(GLA_CHUNK, tb)
    kd = C_KDIM
    padded = t_valid < t_pad
    n_src = t_valid if padded else c

    @pl.when(pl.program_id(1) == 0)
    def _():
        _load_state_t(s0_ref, st_ref, kd)

    zpad = jnp.zeros((c, HEAD_DIM - kd), F32)

    def chunk(bi, n):
        r0, rows = _chunk_rows(n, c)
        gate_in = _dot(lr_ref[bi, rows, 0:C_RANK], w2_ref[...]) + ba_ref[...]
        g_all = _log_sigmoid(gate_in) / C_TAU
        if padded:
            ok = _valid_rows(t_valid, tb, r0, c)
            g_all = jnp.where(ok, g_all, 0.0)
        parts, vs = [], []
        for h in range(HEADS):
            kcols = slice(h * kd, (h + 1) * kd)
            q = jnp.concatenate([q_ref[bi, rows, kcols] * (kd ** -0.5), zpad], axis=1)
            k = k_ref[bi, rows, kcols]
            if padded:
                k = jnp.where(ok, k, 0.0)
            k = jnp.concatenate([k, zpad], axis=1)
            g = jnp.concatenate([g_all[:, kcols], zpad], axis=1)
            vs.append(v_ref[bi, rows, h * HEAD_DIM:(h + 1) * HEAD_DIM])
            parts.append(_gla_head_chunk(q, k, vs[h], g, n_src))
        for h, o in enumerate(_gla_state_step(parts, vs, st_ref, bi)):
            vcols = slice(h * HEAD_DIM, (h + 1) * HEAD_DIM)
            o = _rms_rows(o) * ng_ref[...] * _silu(r_ref[bi, rows, vcols])
            o_ref[bi, rows, vcols] = o.astype(o_ref.dtype)

    _seq_chunk_loop(nb, tb // c, unroll, chunk)

    @pl.when(pl.program_id(1) == pl.num_programs(1) - 1)
    def _():
        _store_state_t(st_ref, sout_ref, kd)


def mixer_c(pseq, w_a2, b_a, norm_g, s0, *, t_valid, tb, nb, unroll):
    bsz, t_pad, _ = pseq.shape
    assert t_valid == t_pad or t_pad == tb <= GLA_CHUNK
    bw = BRANCH_WIDTH
    kw = HEADS * C_KDIM
    field = lambda off, w: pl.BlockSpec((nb, tb, w), lambda b, s: (b, s, off // w))
    state = pl.BlockSpec((nb, HEADS, C_KDIM, HEAD_DIM), lambda b, s: (b, 0, 0, 0))
    return pl.pallas_call(
        functools.partial(_mixer_c_body, t_valid, t_pad, unroll),
        grid=(bsz // nb, t_pad // tb),
        in_specs=[field(OFF_C_Q, kw), field(OFF_C_K, kw), field(OFF_C_V, bw), field(OFF_C_R, bw),
                  field(OFF_C_LR, 128),
                  pl.BlockSpec((C_RANK, kw), lambda b, s: (0, 0)),
                  pl.BlockSpec((1, kw), lambda b, s: (0, 0)),
                  pl.BlockSpec((1, HEAD_DIM), lambda b, s: (0, 0)),
                  state],
        out_specs=[pl.BlockSpec((nb, tb, bw), lambda b, s: (b, s, 0)), state],
        out_shape=[jax.ShapeDtypeStruct((bsz, t_pad, bw), BF16),
                   jax.ShapeDtypeStruct((bsz, HEADS, C_KDIM, HEAD_DIM), F32)],
        scratch_shapes=[pltpu.VMEM((nb, HEAD_DIM, HEADS * HEAD_DIM), F32)],
        compiler_params=_cparams(("parallel", "arbitrary")),
        name="mixer_c",
    )(pseq, pseq, pseq, pseq, pseq, w_a2, b_a.reshape(1, kw), norm_g.reshape(1, HEAD_DIM), s0)


def _grouped_pairs_loop(n_rows, seg, st_ref, head_inputs, head_outputs):
    def step(it, carry):
        rows = pl.ds(pl.multiple_of(it * 16, 16), 16)
        ins = [head_inputs(rows, h) for h in range(HEADS)]
        halves = []
        for half in range(2):
            sl = slice(half * 8, (half + 1) * 8)
            vs = [x[2][sl] for x in ins]
            tiles = [_gla_pair_tile(x[0][sl], x[1][sl], x[2][sl], x[3][sl], seg) for x in ins]
            halves.append(_gla_pair_step(tiles, vs, st_ref, 2 * it + half, seg))
        for h in range(HEADS):
            head_outputs(rows, h, jnp.concatenate([halves[0][h], halves[1][h]], axis=0))
        return carry

    lax.fori_loop(0, n_rows // 16, step, 0)


def _mixer_b_group_body(seg, q_ref, f_ref, i_ref, og_ref, lb_ref, ng_ref, s0_ref, o_ref, sout_ref, st_ref):
    kd = HEAD_DIM
    _load_state_t(s0_ref, st_ref, kd)

    def head_inputs(rows, h):
        cols = slice(h * kd, (h + 1) * kd)
        lb = lb_ref[:, cols]
        fp = f_ref[0, rows, cols]
        f = lb + (1.0 - lb) * jax.nn.sigmoid(fp)
        return (q_ref[0, rows, cols], (1.0 - lb) * jax.nn.sigmoid(-fp), i_ref[0, rows, cols],
                jnp.log(jnp.maximum(f, F_TINY)))

    def head_outputs(rows, h, o):
        cols = slice(h * kd, (h + 1) * kd)
        o = _rms_rows(o) * ng_ref[...] * jax.nn.sigmoid(og_ref[0, rows, cols])
        o_ref[0, rows, cols] = o.astype(o_ref.dtype)

    _grouped_pairs_loop(q_ref.shape[1], seg, st_ref, head_inputs, head_outputs)
    _store_state_t(st_ref, sout_ref, kd)


def _mixer_c_group_body(seg, q_ref, k_ref, v_ref, r_ref, lr_ref, w2_ref, ba_ref, ng_ref, s0_ref, o_ref, sout_ref,
                        st_ref, g_scr):
    kd = C_KDIM
    _load_state_t(s0_ref, st_ref, kd)
    g_scr[...] = _log_sigmoid(_dot(lr_ref[0, :, 0:C_RANK], w2_ref[...]) + ba_ref[...]) / C_TAU
    zpad = jnp.zeros((16, HEAD_DIM - kd), F32)

    def head_inputs(rows, h):
        kcols = slice(h * kd, (h + 1) * kd)
        pad = lambda x: jnp.concatenate([x, zpad], axis=1)
        return (pad(q_ref[0, rows, kcols] * (kd ** -0.5)), pad(k_ref[0, rows, kcols]),
                v_ref[0, rows, h * HEAD_DIM:(h + 1) * HEAD_DIM], pad(g_scr[rows, kcols]))

    def head_outputs(rows, h, o):
        vcols = slice(h * HEAD_DIM, (h + 1) * HEAD_DIM)
        o = _rms_rows(o) * ng_ref[...] * _silu(r_ref[0, rows, vcols])
        o_ref[0, rows, vcols] = o.astype(o_ref.dtype)

    _grouped_pairs_loop(q_ref.shape[1], seg, st_ref, head_inputs, head_outputs)
    _store_state_t(st_ref, sout_ref, kd)


def _grouped_gla_call(body, prows, fields, params, s0, kdim, *, seg, tb, extra_scratch=()):
    m = prows.shape[0]
    bw = BRANCH_WIDTH
    p3 = prows.reshape(1, m, -1)
    state = pl.BlockSpec((tb // seg, HEADS, kdim, HEAD_DIM), lambda s: (s, 0, 0, 0))
    o, s_out = pl.pallas_call(
        functools.partial(body, seg),
        grid=(m // tb,),
        in_specs=[pl.BlockSpec((1, tb, w), functools.partial(lambda s, c: (0, s, c), c=off // w)) for off, w in fields]
        + [pl.BlockSpec(x.shape, functools.partial(lambda s, n: (0,) * n, n=x.ndim)) for x in params] + [state],
        out_specs=[pl.BlockSpec((1, tb, bw), lambda s: (0, s, 0)), state],
        out_shape=[jax.ShapeDtypeStruct((1, m, bw), BF16), jax.ShapeDtypeStruct(s0.shape, F32)],
        scratch_shapes=[pltpu.VMEM((tb // seg, HEAD_DIM, HEADS * HEAD_DIM), F32), *extra_scratch],
        compiler_params=_cparams(("parallel",)),
        name=body.__name__.strip("_"),
    )(*([p3] * len(fields)), *params, s0)
    return o.reshape(m, bw), s_out


def mixer_b_grouped(prows, lb, norm_g, s0, *, seg, tb):
    bw = BRANCH_WIDTH
    return _grouped_gla_call(
        _mixer_b_group_body, prows, [(OFF_B_Q, bw), (OFF_B_F, bw), (OFF_B_I, bw), (OFF_B_G, bw)],
        [lb.reshape(1, bw), norm_g.reshape(1, HEAD_DIM)], s0, HEAD_DIM, seg=seg, tb=tb)


def mixer_c_grouped(prows, w_a2, b_a, norm_g, s0, *, seg, tb):
    bw = BRANCH_WIDTH
    kw = HEADS * C_KDIM
    return _grouped_gla_call(
        _mixer_c_group_body, prows, [(OFF_C_Q, kw), (OFF_C_K, kw), (OFF_C_V, bw), (OFF_C_R, bw), (OFF_C_LR, 128)],
        [w_a2, b_a.reshape(1, kw), norm_g.reshape(1, HEAD_DIM)], s0, C_KDIM, seg=seg, tb=tb,
        extra_scratch=(pltpu.VMEM((tb, kw), F32),))


def _unit_lower_inverse(a, order):
    c = a.shape[0]
    eye = (lax.broadcasted_iota(jnp.int32, (c, c), 0) == lax.broadcasted_iota(jnp.int32, (c, c), 1)).astype(F32)
    p = eye - a
    pw = a
    n = 2
    while n < order:
        pw = _dot(pw, pw)
        p = p + _dot(p, pw)
        n *= 2
    return p


D_ROWS = 64


def _stack_heads(x):
    return jnp.concatenate([x[:, h * HEAD_DIM:(h + 1) * HEAD_DIM] for h in range(HEADS)], axis=0)


def _stack_head_lanes(x):
    return jnp.concatenate([x[:, h:h + 1] for h in range(HEADS)], axis=0)


def _lane_pad_heads(x):
    return jnp.pad(x.reshape(1, HEADS), ((0, 0), (0, 128 - HEADS)))


def _segment_cumsum(x, seg):
    tpos = lax.broadcasted_iota(jnp.int32, x.shape, 0) & (seg - 1)
    sh = 1
    while sh < seg:
        x = x + jnp.where(tpos >= sh, pltpu.roll(x, sh, axis=0), 0.0)
        sh *= 2
    return x


def _delta_chunk_operands(seg, qkv, beta_all, g_all):
    hd = HEAD_DIM
    bw = BRANCH_WIDTH
    r = HEADS * qkv.shape[0]
    q = _stack_heads(qkv[:, 0:bw])
    k = _stack_heads(qkv[:, bw:2 * bw])
    v = _stack_heads(qkv[:, 2 * bw:3 * bw])
    q = q * lax.rsqrt(jnp.sum(q * q, axis=-1, keepdims=True) + EPS) * (hd ** -0.5)
    k = k * lax.rsqrt(jnp.sum(k * k, axis=-1, keepdims=True) + EPS)
    beta = _stack_head_lanes(beta_all)
    gc = _stack_head_lanes(_segment_cumsum(g_all, seg))
    ri = lax.broadcasted_iota(jnp.int32, (r, r), 0)
    ci = lax.broadcasted_iota(jnp.int32, (r, r), 1)
    shift = seg.bit_length() - 1
    same = (ri >> shift) == (ci >> shift)
    gr = jnp.sum(jnp.where(ri == ci, gc, 0.0), axis=0, keepdims=True)
    decay = jnp.where(same, jnp.where(ri >= ci, jnp.exp(jnp.minimum(gc - gr, 0.0)), 0.0), 0.0)
    a_mat = jnp.where(ri > ci, beta * decay * _dot_nt(k, k), 0.0)
    t_inv = _unit_lower_inverse(a_mat, seg)
    eg = jnp.exp(gc)
    sol = _dot(t_inv, jnp.concatenate([(beta * eg) * k, beta * v], axis=1))
    qk = _dot_nt(q, k) * decay
    is_last = (ci & (seg - 1)) == seg - 1
    gl = jnp.sum(jnp.where(same, jnp.where(is_last, gr, 0.0), 0.0), axis=1, keepdims=True)
    return sol[:, :hd], sol[:, hd:], qk, q * eg, k * jnp.exp(gl - gc), jnp.exp(gl)


def _mixer_d_body(x_ref, beta_ref, dec_ref, z_ref, cst_ref, s0_ref, cw_ref, alog_ref, dtb_ref, ng_ref,
                  o_ref, cst_out_ref, sout_ref, ext, conv, st_ref):
    nb, tb = x_ref.shape[0], x_ref.shape[1]
    c = D_ROWS
    hd = HEAD_DIM
    t = pl.program_id(1)
    nt = pl.num_programs(1)

    @pl.when(t == 0)
    def _():
        ext[:, 5:8, :] = cst_ref[...]
        st_ref[...] = s0_ref[...].reshape(nb * HEADS, hd, hd)

    for bi in range(nb):
        ext[bi, 8:8 + tb, :] = x_ref[bi]
        y = ext[bi, 5:5 + tb, :] * cw_ref[0:1, :]
        for j in range(1, D_CONV):
            y = y + ext[bi, 5 + j:5 + j + tb, :] * cw_ref[j:j + 1, :]
        conv[bi] = _silu(y)

    @pl.when(t == nt - 1)
    def _():
        cst_out_ref[...] = ext[:, 8 + tb - (D_CONV - 1):8 + tb, :]

    ext[:, 5:8, :] = ext[:, 5 + tb:8 + tb, :]

    a_neg = -jnp.exp(alog_ref[...])

    def chunk(n, carry):
        rows = pl.ds(pl.multiple_of(n * c, c), c)
        for bi in range(nb):
            beta_all = jax.nn.sigmoid(beta_ref[bi, rows, :])
            g_all = a_neg * _softplus(dec_ref[bi, rows, :] + dtb_ref[...])
            w, u, qk, qg, kdec, egl = _delta_chunk_operands(c, conv[bi, rows, :], beta_all, g_all)
            deltas, oqs = [], []
            for h in range(HEADS):
                hs = slice(h * c, (h + 1) * c)
                st = st_ref[bi * HEADS + h]
                x = _dot(jnp.concatenate([w[hs], qg[hs]], axis=0), st)
                delta = u[hs] - x[:c]
                deltas.append(delta)
                oqs.append(x[c:])
                st_ref[bi * HEADS + h] = egl[(h + 1) * c - 1:(h + 1) * c, :] * st + _dot_tn(kdec[hs], delta)
            o = jnp.concatenate(oqs, axis=0) + _dot(qk, jnp.concatenate(deltas, axis=0))
            o = _rms_rows(o) * ng_ref[...] * _silu(_stack_heads(z_ref[bi, rows, :]))
            for h in range(HEADS):
                o_ref[bi, rows, h * hd:(h + 1) * hd] = o[h * c:(h + 1) * c].astype(o_ref.dtype)
        return carry

    lax.fori_loop(0, tb // c, chunk, 0)

    @pl.when(t == nt - 1)
    def _():
        sout_ref[...] = st_ref[...].reshape(nb, HEADS, hd, hd)


def _mixer_d_group_body(seg, x_ref, stp_ref, beta_ref, dec_ref, z_ref, s0_ref, cw_ref, alog_ref, dtb_ref, ng_ref,
                        o_ref, cst_out_ref, sout_ref, conv, w_s, u_s, qg_s, kd_s, egl_s, delta_s, oq_s):
    tb = x_ref.shape[1]
    c = D_ROWS
    hd = HEAD_DIM
    pairs = c // 8

    x = x_ref[0]
    stp = stp_ref[0]
    tpos = lax.broadcasted_iota(jnp.int32, (tb, 1), 0) & (seg - 1)
    y = x * cw_ref[3:4, :]
    for d in range(1, D_CONV):
        hist = stp if d == 3 else pltpu.roll(stp, tb - (3 - d), axis=0)
        y = y + jnp.where(tpos < d, hist, pltpu.roll(x, d, axis=0)) * cw_ref[3 - d:4 - d, :]
    conv[...] = _silu(y)
    cst_out_ref[0] = pltpu.roll(x, tb - (seg - 3), axis=0)

    a_neg = -jnp.exp(alog_ref[...])
    low = lax.broadcasted_iota(jnp.int32, (8, 1), 0) < seg

    for n in range(tb // c):
        rows = slice(n * c, (n + 1) * c)
        beta_all = jax.nn.sigmoid(beta_ref[0, rows, :])
        g_all = a_neg * _softplus(dec_ref[0, rows, :] + dtb_ref[...])
        w, u, qk, qg, kdec, egl = _delta_chunk_operands(seg, conv[rows, :], beta_all, g_all)
        w_s[...] = w
        u_s[...] = u
        qg_s[...] = qg
        kd_s[...] = kdec
        egl_s[...] = jnp.broadcast_to(egl, (HEADS * c, hd))

        def pair(p, carry):
            rp = pl.ds(pl.multiple_of(p * 8, 8), 8)
            h = p // pairs
            ja = n * (c // seg) + (p % pairs) * 2
            sa = s0_ref[ja, h]
            sb = s0_ref[ja + 1, h]
            lhs = jnp.concatenate([w_s[rp, :], qg_s[rp, :]], axis=0)
            xa = _dot(lhs, sa)
            xb = _dot(lhs, sb)
            delta = u_s[rp, :] - jnp.where(low, xa[:8], xb[:8])
            delta_s[rp, :] = delta
            oq_s[rp, :] = jnp.where(low, xa[8:], xb[8:])
            kd = kd_s[rp, :]
            e = egl_s[rp, :]
            sout_ref[ja, h] = e[seg - 1:seg, :] * sa + _dot_tn(jnp.where(low, kd, 0.0), delta)
            sout_ref[ja + 1, h] = e[2 * seg - 1:2 * seg, :] * sb + _dot_tn(jnp.where(low, 0.0, kd), delta)
            return carry

        lax.fori_loop(0, HEADS * pairs, pair, 0)
        o = oq_s[...] + _dot(qk, delta_s[...])
        o = _rms_rows(o) * ng_ref[...] * _silu(_stack_heads(z_ref[0, rows, :]))
        for h in range(HEADS):
            o_ref[0, rows, h * hd:(h + 1) * hd] = o[h * c:(h + 1) * c].astype(o_ref.dtype)


def mixer_d(pseq, conv_state, s0, conv_w, a_log, dt_bias, norm_g, *, tb, nb):
    bsz, t, _ = pseq.shape
    bw = BRANCH_WIDTH
    field = lambda off, w: pl.BlockSpec((nb, tb, w), lambda b, s: (b, s, off // w))
    state = pl.BlockSpec((nb, HEADS, HEAD_DIM, HEAD_DIM), lambda b, s: (b, 0, 0, 0))
    cstate = pl.BlockSpec((nb, D_CONV - 1, D_QKV), lambda b, s: (b, 0, 0))
    return pl.pallas_call(
        _mixer_d_body,
        grid=(bsz // nb, t // tb),
        in_specs=[field(OFF_D_QKV, D_QKV), field(OFF_D_BETA, 128), field(OFF_D_DECAY, 128), field(OFF_D_Z, bw),
                  cstate, state,
                  pl.BlockSpec((D_CONV, D_QKV), lambda b, s: (0, 0)),
                  pl.BlockSpec((1, 128), lambda b, s: (0, 0)),
                  pl.BlockSpec((1, 128), lambda b, s: (0, 0)),
                  pl.BlockSpec((1, HEAD_DIM), lambda b, s: (0, 0))],
        out_specs=[pl.BlockSpec((nb, tb, bw), lambda b, s: (b, s, 0)), cstate, state],
        out_shape=[jax.ShapeDtypeStruct((bsz, t, bw), BF16),
                   jax.ShapeDtypeStruct((bsz, D_CONV - 1, D_QKV), F32),
                   jax.ShapeDtypeStruct((bsz, HEADS, HEAD_DIM, HEAD_DIM), F32)],
        scratch_shapes=[pltpu.VMEM((nb, tb + 8, D_QKV), F32), pltpu.VMEM((nb, tb, D_QKV), F32),
                        pltpu.VMEM((nb * HEADS, HEAD_DIM, HEAD_DIM), F32)],
        compiler_params=_cparams(("parallel", "arbitrary")),
        name="mixer_d",
    )(pseq, pseq, pseq, pseq, conv_state, s0, conv_w, _lane_pad_heads(a_log), _lane_pad_heads(dt_bias),
      norm_g.reshape(1, HEAD_DIM))


def mixer_d_grouped(prows, conv_state, s0, conv_w, a_log, dt_bias, norm_g, *, seg, tb):
    m = prows.shape[0]
    bsz = m // seg
    bw = BRANCH_WIDTH
    stp = jnp.pad(conv_state, ((0, 0), (0, seg - (D_CONV - 1)), (0, 0))).reshape(1, m, D_QKV)
    field = lambda off, w: pl.BlockSpec((1, tb, w), lambda s: (0, s, off // w))
    state = pl.BlockSpec((tb // seg, HEADS, HEAD_DIM, HEAD_DIM), lambda s: (s, 0, 0, 0))
    rows128 = pltpu.VMEM((HEADS * D_ROWS, HEAD_DIM), F32)
    o, cst, s_out = pl.pallas_call(
        functools.partial(_mixer_d_group_body, seg),
        grid=(m // tb,),
        in_specs=[field(OFF_D_QKV, D_QKV), pl.BlockSpec((1, tb, D_QKV), lambda s: (0, s, 0)),
                  field(OFF_D_BETA, 128), field(OFF_D_DECAY, 128), field(OFF_D_Z, bw),
                  state,
                  pl.BlockSpec((D_CONV, D_QKV), lambda s: (0, 0)),
                  pl.BlockSpec((1, 128), lambda s: (0, 0)),
                  pl.BlockSpec((1, 128), lambda s: (0, 0)),
                  pl.BlockSpec((1, HEAD_DIM), lambda s: (0, 0))],
        out_specs=[pl.BlockSpec((1, tb, bw), lambda s: (0, s, 0)),
                   pl.BlockSpec((1, tb, D_QKV), lambda s: (0, s, 0)), state],
        out_shape=[jax.ShapeDtypeStruct((1, m, bw), BF16),
                   jax.ShapeDtypeStruct((1, m, D_QKV), F32),
                   jax.ShapeDtypeStruct((bsz, HEADS, HEAD_DIM, HEAD_DIM), F32)],
        scratch_shapes=[pltpu.VMEM((tb, D_QKV), F32)] + [rows128] * 7,
        compiler_params=_cparams(("parallel",)),
        name="mixer_d_grouped",
    )(prows.reshape(1, m, -1), stp, prows.reshape(1, m, -1), prows.reshape(1, m, -1), prows.reshape(1, m, -1),
      s0, conv_w, _lane_pad_heads(a_log), _lane_pad_heads(dt_bias), norm_g.reshape(1, HEAD_DIM))
    return o.reshape(m, bw), cst.reshape(bsz, seg, D_QKV)[:, :D_CONV - 1], s_out


def _pack_w_in(w):
    z = lambda n: jnp.zeros((w.shape[0], n), w.dtype)
    cols = [w[:, 4624:6160],
            w[:, 0:512], w[:, 512:1024],
            w[:, 1024:1536], w[:, 1536:2048], w[:, 2048:2560], w[:, 2560:3072],
            w[:, 3584:4096], w[:, 4112:4624],
            w[:, 6168:6680],
            w[:, 3072:3328], w[:, 3328:3584],
            w[:, 4096:4112], z(112),
            w[:, 6160:6164], z(124),
            w[:, 6164:6168], z(252),
            w[:, 6680:14872]]
    return jnp.concatenate(cols, axis=1).astype(BF16)


def _trunk(x, p, s_hgrn, s_gla, s_delta, c_dconv, c_fconv, wts, lbs, *, sample):
    bsz, t, _ = x.shape
    m = bsz * t
    tm = m if sample else 1024
    h = x.reshape(m, D_MODEL)
    outs = {k: [] for k in ("hgrn", "gla", "delta", "dconv", "fconv", "v")}
    for l in range(DEPTH):
        w = wts[l]
        proj, gates = in_proj(h, w["g_mix"], w["w_in"], tm=tm, tn=1024)
        o_a, v_a = mixer_a(proj, w["a_ln_g"], w["a_ln_b"], w["a_w_mix"], w["a_bias_rows"])
        b_args = (lbs[l], w["b_norm_g"], s_hgrn[l])
        c_args = (w["c_w_a2"], w["c_b_a"], w["c_norm_g"], s_gla[l])
        d_args = (c_dconv[l], s_delta[l], w["d_conv_w"], w["d_a_log"], w["d_dt_bias"], w["d_norm_g"])
        if sample:
            o_b, s_b = mixer_b_grouped(proj, *b_args, seg=t, tb=D_ROWS)
            o_c, s_c = mixer_c_grouped(proj, *c_args, seg=t, tb=D_ROWS)
            o_d, nb_d, s_d = mixer_d_grouped(proj, *d_args, seg=t, tb=D_ROWS)
        else:
            pseq = proj.reshape(bsz, t, N_SEQ_COLS)
            flat = lambda o: o.reshape(m, BRANCH_WIDTH)
            o_b, s_b = mixer_b(pseq, *b_args, t_valid=t, tb=256, nb=1, unroll=1)
            o_c, s_c = mixer_c(pseq, *c_args, t_valid=t, tb=256, nb=1, unroll=1)
            o_d, nb_d, s_d = mixer_d(pseq, *d_args, tb=256, nb=2 if bsz % 2 == 0 else 1)
            o_b, o_c, o_d = flat(o_b), flat(o_c), flat(o_d)
        merged = merge_branches((o_a, o_b, o_c, o_d), gates, w["w_branch"], tm=tm, tn=512)
        h = matmul_residual(merged, w["w_out_layers"], l, h, tm=tm, tn=1024)

        act, nb_f = ffn_up_gate(h, w["g_ffn"], w["w_ffn_up_layers"], l, c_fconv[l], w["ffn_conv_w"],
                                w["ffn_conv_b"], seq_len=t, tm=tm, tn=512)
        h = matmul_residual(act, w["w_ffn_down_layers"], l, h, tm=min(tm, 512), tn=512)

        h = ple_update(h, p[l].reshape(m, PLE_DIM), w["g_ple"], w["w_ple_gate"], w["w_ple_proj"], tm=tm, tn=512)

        outs["hgrn"].append(s_b)
        outs["gla"].append(s_c)
        outs["delta"].append(s_d)
        outs["dconv"].append(nb_d)
        outs["fconv"].append(nb_f)
        outs["v"].append(v_a.reshape(bsz, t, BRANCH_WIDTH))
    y = final_norm(h, wts[0]["g_final"], tm=min(tm, 512)).reshape(bsz, t, D_MODEL)
    return (y,) + tuple(jnp.stack(outs[k]) for k in ("hgrn", "gla", "delta", "dconv", "fconv", "v"))


def kernel(x_prompt, x_sample, state_hgrn, state_gla, state_delta, state_delta_conv, state_ffn_conv, p_prompt, p_sample, g_mix, w_in, a_ln_g, a_ln_b, a_w_s, a_b_s, b_lb, b_norm_g, c_w_a2, c_b_a, c_norm_g, d_conv_w, d_a_log, d_dt_bias, d_norm_g, w_branch, w_out, g_ffn, w_ffn_up, ffn_conv_w, ffn_conv_b, w_ffn_down, g_ple, w_ple_gate, w_ple_proj, g_final):
    bp, t_p, _ = x_prompt.shape
    bs, t_s, _ = x_sample.shape
    sm = jax.nn.softmax(b_lb.astype(F32), axis=0)
    lbs = jnp.cumsum(sm, axis=0) - sm[0]

    shared = [dict(
        g_mix=g_mix[l], w_in=_pack_w_in(w_in[l]), a_ln_g=a_ln_g[l], a_ln_b=a_ln_b[l],
        b_norm_g=b_norm_g[l], c_w_a2=c_w_a2[l], c_b_a=c_b_a[l], c_norm_g=c_norm_g[l],
        d_conv_w=d_conv_w[l], d_a_log=d_a_log[l], d_dt_bias=d_dt_bias[l], d_norm_g=d_norm_g[l],
        w_branch=w_branch[l].astype(BF16), w_out_layers=w_out, g_ffn=g_ffn[l],
        w_ffn_up_layers=w_ffn_up, ffn_conv_w=ffn_conv_w[l], ffn_conv_b=ffn_conv_b[l],
        w_ffn_down_layers=w_ffn_down, g_ple=g_ple[l], w_ple_gate=w_ple_gate[l].astype(BF16),
        w_ple_proj=w_ple_proj[l].astype(BF16), g_final=g_final) for l in range(DEPTH)]

    def layer_weights(l, sample):
        if sample:
            seqs = A_CHUNK // t_s
            w_mix = jnp.einsum("ab,gts->gatbs", jnp.eye(seqs, dtype=F32), a_w_s[l, :, :t_s, :t_s])
            w_mix = w_mix.reshape(A_GROUPS, A_CHUNK, A_CHUNK)
            bias_rows = jnp.tile(a_b_s[l, :, :t_s].T, (seqs, 1))
        else:
            w_mix = a_w_s[l]
            bias_rows = a_b_s[l].T
        return dict(shared[l], a_w_mix=w_mix, a_bias_rows=bias_rows)

    dt = x_prompt.dtype
    zeros = lambda *s: jnp.zeros((DEPTH, bp) + s, dt)
    out_p = _trunk(x_prompt, p_prompt, zeros(HEADS, HEAD_DIM, HEAD_DIM), zeros(HEADS, C_KDIM, HEAD_DIM),
                   zeros(HEADS, HEAD_DIM, HEAD_DIM), zeros(D_CONV - 1, D_QKV), zeros(FFN_CONV - 1, FFN_DIM),
                   [layer_weights(l, False) for l in range(DEPTH)], lbs, sample=False)
    out_s = _trunk(x_sample, p_sample, state_hgrn, state_gla, state_delta, state_delta_conv, state_ffn_conv,
                   [layer_weights(l, True) for l in range(DEPTH)], lbs, sample=True)
    y_p, hgrn_p, gla_p, delta_p, dconv_p, fconv_p, _ = out_p
    y_s, hgrn_s, gla_s, delta_s, dconv_s, fconv_s, v_s = out_s
    return (y_p, y_s, hgrn_p, hgrn_s, gla_p, gla_s, delta_p, delta_s,
            dconv_p, dconv_s, fconv_p, fconv_s, v_s)
```

```python
import functools

import jax
import jax.numpy as jnp
from jax import lax
from jax.experimental import pallas as pl
from jax.experimental.pallas import tpu as pltpu

F32 = jnp.float32
BF16 = jnp.bfloat16

D_MODEL = 2048
DEPTH = 2
PLE_DIM = 256
EPS = 1e-6
F_TINY = 1e-30
N_BRANCH = 4
BRANCH_WIDTH = D_MODEL // 4
A_GROUPS = 4
A_CHUNK = 128
HEADS = 4
HEAD_DIM = 128
C_KDIM = 64
C_RANK = 16
C_TAU = 16.0
D_CONV = 4
D_QKV = 3 * BRANCH_WIDTH
FFN_DIM = 5632
FFN_CONV = 3
GLA_CHUNK = 16
LOG2E = 1.4426950408889634

OFF_D_QKV = 0
OFF_A_U = 1536
OFF_A_V = 2048
OFF_B_Q = 2560
OFF_B_F = 3072
OFF_B_I = 3584
OFF_B_G = 4096
OFF_C_V = 4608
OFF_C_R = 5120
OFF_D_Z = 5632
OFF_C_Q = 6144
OFF_C_K = 6400
OFF_C_LR = 6656
OFF_D_BETA = 6784
OFF_D_DECAY = 6912
OFF_GATES = 7168
N_SEQ_COLS = 7168
N_PROJ = OFF_GATES + N_BRANCH * D_MODEL

VMEM_LIMIT = 56 * 1024 * 1024


def _cparams(sem):
    return pltpu.CompilerParams(dimension_semantics=sem, vmem_limit_bytes=VMEM_LIMIT)


def _gelu(x):
    return 0.5 * x * (1.0 + jnp.tanh(0.7978845608028654 * (x + 0.044715 * (x * x * x))))


def _silu(x):
    return x * jax.nn.sigmoid(x)


def _softplus(x):
    return jnp.maximum(x, 0.0) + jnp.log1p(jnp.exp(-jnp.abs(x)))


def _log_sigmoid(x):
    return -_softplus(-x)


def _rms_rows(x):
    return x * lax.rsqrt(jnp.mean(x * x, axis=-1, keepdims=True) + EPS)


def _dot(a, b):
    return jnp.dot(a, b, preferred_element_type=F32)


def _dot_nt(a, b):
    return lax.dot_general(a, b, (((1,), (1,)), ((), ())), preferred_element_type=F32)


def _dot_tn(a, b):
    return lax.dot_general(a, b, (((0,), (0,)), ((), ())), preferred_element_type=F32)


def _cumsum_rows(x):
    n = x.shape[0]
    row = lax.broadcasted_iota(jnp.int32, x.shape, 0)
    sh = 1
    while sh < n:
        x = x + jnp.where(row >= sh, pltpu.roll(x, sh, axis=0), 0.0)
        sh *= 2
    return x


NORM_ROWS = 128


def _norm_to_scratch(x_ref, g_ref, a_scr):
    def slab(n, carry):
        rows = pl.ds(pl.multiple_of(n * NORM_ROWS, NORM_ROWS), NORM_ROWS)
        a_scr[rows, :] = (_rms_rows(x_ref[rows, :]) * g_ref[...]).astype(BF16)
        return carry

    lax.fori_loop(0, x_ref.shape[0] // NORM_ROWS, slab, 0)


def _matmul_residual_body(x_ref, w_ref, r_ref, o_ref, w_scr):
    @pl.when(pl.program_id(1) == 0)
    def _():
        def slab(n, carry):
            rows = pl.ds(pl.multiple_of(n * NORM_ROWS, NORM_ROWS), NORM_ROWS)
            w_scr[rows, :] = w_ref[rows, :].astype(BF16)
            return carry

        lax.fori_loop(0, w_ref.shape[0] // NORM_ROWS, slab, 0)

    o_ref[...] = r_ref[...] + _dot(x_ref[...], w_scr[...])


def matmul_residual(x, w_layers, layer, res, *, tm, tn):
    m, k = x.shape
    n = w_layers.shape[2]
    return pl.pallas_call(
        _matmul_residual_body,
        grid=(n // tn, m // tm),
        in_specs=[pl.BlockSpec((tm, k), lambda j, i: (i, 0)),
                  pl.BlockSpec((None, k, tn), lambda j, i: (layer, 0, j)),
                  pl.BlockSpec((tm, tn), lambda j, i: (i, j))],
        out_specs=pl.BlockSpec((tm, tn), lambda j, i: (i, j)),
        out_shape=jax.ShapeDtypeStruct((m, n), F32),
        scratch_shapes=[pltpu.VMEM((k, tn), BF16)],
        compiler_params=_cparams(("parallel", "arbitrary")),
        name="matmul_residual",
    )(x, w_layers, res)


def _in_proj_body(n_seq_tiles, x_ref, g_ref, w_ref, seq_ref, gate_ref, a_scr):
    j = pl.program_id(1)

    @pl.when(j == 0)
    def _():
        _norm_to_scratch(x_ref, g_ref, a_scr)

    acc = _dot(a_scr[...], w_ref[...])

    @pl.when(j < n_seq_tiles)
    def _():
        seq_ref[...] = acc

    @pl.when(j >= n_seq_tiles)
    def _():
        gate_ref[...] = acc.astype(gate_ref.dtype)


def in_proj(x, g, w, *, tm, tn):
    m, k = x.shape
    n_seq = N_SEQ_COLS // tn
    n_gate = N_BRANCH * D_MODEL // tn
    return pl.pallas_call(
        functools.partial(_in_proj_body, n_seq),
        grid=(m // tm, n_seq + n_gate),
        in_specs=[pl.BlockSpec((tm, k), lambda i, j: (i, 0)),
                  pl.BlockSpec((1, k), lambda i, j: (0, 0)),
                  pl.BlockSpec((k, tn), lambda i, j: (0, j))],
        out_specs=[pl.BlockSpec((tm, tn), lambda i, j: (i, jnp.minimum(j, n_seq - 1))),
                   pl.BlockSpec((tm, tn), lambda i, j: (i, jnp.maximum(j - n_seq, 0)))],
        out_shape=[jax.ShapeDtypeStruct((m, N_SEQ_COLS), F32),
                   jax.ShapeDtypeStruct((m, N_BRANCH * D_MODEL), BF16)],
        scratch_shapes=[pltpu.VMEM((tm, k), BF16)],
        compiler_params=_cparams(("parallel", "arbitrary")),
        name="in_proj",
    )(x, g.reshape(1, k), w)


def _merge_body(oa_ref, ob_ref, oc_ref, od_ref, ga_ref, gb_ref, gc_ref, gd_ref, w_ref, o_ref):
    acc = None
    for b, (o_b, g_b) in enumerate(((oa_ref, ga_ref), (ob_ref, gb_ref), (oc_ref, gc_ref), (od_ref, gd_ref))):
        term = jax.nn.sigmoid(g_b[...].astype(F32)) * _dot(o_b[...], w_ref[b])
        acc = term if acc is None else acc + term
    o_ref[...] = acc.astype(o_ref.dtype)


def merge_branches(o_branches, gates, w_branch, *, tm, tn):
    m = gates.shape[0]
    gate_specs = [pl.BlockSpec((tm, tn), functools.partial(
        lambda i, j, b: (i, b * (D_MODEL // tn) + j), b=b)) for b in range(N_BRANCH)]
    return pl.pallas_call(
        _merge_body,
        grid=(m // tm, D_MODEL // tn),
        in_specs=[pl.BlockSpec((tm, BRANCH_WIDTH), lambda i, j: (i, 0))] * N_BRANCH + gate_specs
        + [pl.BlockSpec((N_BRANCH, BRANCH_WIDTH, tn), lambda i, j: (0, 0, j))],
        out_specs=pl.BlockSpec((tm, tn), lambda i, j: (i, j)),
        out_shape=jax.ShapeDtypeStruct((m, D_MODEL), BF16),
        compiler_params=_cparams(("parallel", "arbitrary")),
        name="merge_branches",
    )(*o_branches, gates, gates, gates, gates, w_branch)


def _ple_body(x_ref, xres_ref, p_ref, g_ref, wg_ref, wp_ref, o_ref, a_scr, p_scr):
    @pl.when(pl.program_id(1) == 0)
    def _():
        _norm_to_scratch(x_ref, g_ref, a_scr)
        p_scr[...] = p_ref[...].astype(BF16)

    gate = jax.nn.sigmoid(_dot(a_scr[...], wg_ref[...]))
    o_ref[...] = xres_ref[...] + _dot(p_scr[...], wp_ref[...]) * gate


def ple_update(x, p, g, w_gate, w_proj, *, tm, tn):
    m, k = x.shape
    return pl.pallas_call(
        _ple_body,
        grid=(m // tm, D_MODEL // tn),
        in_specs=[pl.BlockSpec((tm, k), lambda i, j: (i, 0)),
                  pl.BlockSpec((tm, tn), lambda i, j: (i, j)),
                  pl.BlockSpec((tm, PLE_DIM), lambda i, j: (i, 0)),
                  pl.BlockSpec((1, k), lambda i, j: (0, 0)),
                  pl.BlockSpec((k, tn), lambda i, j: (0, j)),
                  pl.BlockSpec((PLE_DIM, tn), lambda i, j: (0, j))],
        out_specs=pl.BlockSpec((tm, tn), lambda i, j: (i, j)),
        out_shape=jax.ShapeDtypeStruct((m, D_MODEL), F32),
        scratch_shapes=[pltpu.VMEM((tm, k), BF16), pltpu.VMEM((tm, PLE_DIM), BF16)],
        compiler_params=_cparams(("parallel", "arbitrary")),
        name="ple_update",
    )(x, x, p, g.reshape(1, k), w_gate, w_proj)


def _final_norm_body(x_ref, g_ref, o_ref):
    def slab(n, carry):
        rows = pl.ds(pl.multiple_of(n * NORM_ROWS, NORM_ROWS), NORM_ROWS)
        o_ref[rows, :] = _rms_rows(x_ref[rows, :]) * g_ref[...]
        return carry

    lax.fori_loop(0, x_ref.shape[0] // NORM_ROWS, slab, 0)


def final_norm(x, g, *, tm):
    m, k = x.shape
    return pl.pallas_call(
        _final_norm_body,
        grid=(m // tm,),
        in_specs=[pl.BlockSpec((tm, k), lambda i: (i, 0)), pl.BlockSpec((1, k), lambda i: (0, 0))],
        out_specs=pl.BlockSpec((tm, k), lambda i: (i, 0)),
        out_shape=jax.ShapeDtypeStruct((m, k), F32),
        compiler_params=_cparams(("parallel",)),
        name="final_norm",
    )(x, g.reshape(1, k))


FFN_SUB = 256


def _cast_rows_to_bf16(src_ref, dst_ref):
    def slab(n, carry):
        rows = pl.ds(pl.multiple_of(n * NORM_ROWS, NORM_ROWS), NORM_ROWS)
        dst_ref[rows, :] = src_ref[rows, :].astype(BF16)
        return carry

    lax.fori_loop(0, src_ref.shape[0] // NORM_ROWS, slab, 0)


def _ffn_up_long_body(tiles_per_seq, a_ref, wg_ref, wu_ref, st_ref, cw_ref, cb_ref, o_ref, st_out_ref,
                      wg_s, wu_s, ext):
    i = pl.program_id(1)
    tm = a_ref.shape[0]

    @pl.when(i == 0)
    def _():
        _cast_rows_to_bf16(wg_ref, wg_s)
        _cast_rows_to_bf16(wu_ref, wu_s)

    @pl.when(i % tiles_per_seq == 0)
    def _():
        ext[6:8, :] = st_ref[0]

    for r in range(0, tm, FFN_SUB):
        a = a_ref[r:r + FFN_SUB, :]
        ext[8 + r:8 + r + FFN_SUB, :] = _dot(a, wg_s[...])
        conv = (ext[6 + r:6 + r + FFN_SUB, :] * cw_ref[0:1, :] + ext[7 + r:7 + r + FFN_SUB, :] * cw_ref[1:2, :]
                + ext[8 + r:8 + r + FFN_SUB, :] * cw_ref[2:3, :])
        o_ref[r:r + FFN_SUB, :] = (_gelu(conv + cb_ref[...]) * _dot(a, wu_s[...])).astype(o_ref.dtype)

    last = ext[6 + tm:8 + tm, :]
    ext[6:8, :] = last
    st_out_ref[0] = last


def _ffn_up_group_body(seg, a_ref, wg_ref, wu_ref, st_ref, cw_ref, cb_ref, o_ref, st_out_ref, wg_s, wu_s, stp, fg_s):
    tm = a_ref.shape[0]
    nseq = tm // seg
    hist = FFN_CONV - 1

    @pl.when(pl.program_id(1) == 0)
    def _():
        _cast_rows_to_bf16(wg_ref, wg_s)
        _cast_rows_to_bf16(wu_ref, wu_s)

    lane_blocks = stp.shape[0]
    stp[...] = jnp.zeros_like(stp)
    for j in range(hist):
        for c in range(lane_blocks):
            stp[c, pl.ds(j, nseq, stride=seg), :] = st_ref[:, j, c * 128:(c + 1) * 128]

    a = a_ref[...]
    fg = _dot(a, wg_s[...])
    for c in range(lane_blocks):
        fg_s[c] = fg[:, c * 128:(c + 1) * 128]
    h = jnp.concatenate([stp[c] for c in range(lane_blocks)], axis=1)
    tpos = lax.broadcasted_iota(jnp.int32, (tm, 1), 0) & (seg - 1)
    lag1 = jnp.where(tpos < 1, pltpu.roll(h, tm - 1, axis=0), pltpu.roll(fg, 1, axis=0))
    lag2 = jnp.where(tpos < 2, h, pltpu.roll(fg, 2, axis=0))
    conv = lag2 * cw_ref[0:1, :] + lag1 * cw_ref[1:2, :] + fg * cw_ref[2:3, :]
    o_ref[...] = (_gelu(conv + cb_ref[...]) * _dot(a, wu_s[...])).astype(o_ref.dtype)
    for j in range(hist):
        for c in range(lane_blocks):
            st_out_ref[:, j, c * 128:(c + 1) * 128] = fg_s[c, pl.ds(seg - hist + j, nseq, stride=seg), :]


def _rms_bf16_body(x_ref, g_ref, o_ref):
    _norm_to_scratch(x_ref, g_ref, o_ref)


def rms_bf16(x, g, *, tm):
    m, k = x.shape
    return pl.pallas_call(
        _rms_bf16_body,
        grid=(m // tm,),
        in_specs=[pl.BlockSpec((tm, k), lambda i: (i, 0)), pl.BlockSpec((1, k), lambda i: (0, 0))],
        out_specs=pl.BlockSpec((tm, k), lambda i: (i, 0)),
        out_shape=jax.ShapeDtypeStruct((m, k), BF16),
        compiler_params=_cparams(("parallel",)),
        name="rms_bf16",
    )(x, g.reshape(1, k))


def ffn_up_gate(x, g, w_up_layers, layer, state, conv_w, conv_b, *, seq_len, tm, tn):
    m, k = x.shape
    bsz = m // seq_len
    ncol = FFN_DIM // tn
    a = rms_bf16(x, g, tm=min(tm, 512))
    common_in = [pl.BlockSpec((tm, k), lambda j, i: (i, 0)),
                 pl.BlockSpec((None, k, tn), lambda j, i: (layer, 0, j)),
                 pl.BlockSpec((None, k, tn), lambda j, i: (layer, 0, ncol + j))]
    conv_in = [pl.BlockSpec((FFN_CONV, tn), lambda j, i: (0, j)), pl.BlockSpec((1, tn), lambda j, i: (0, j))]
    act_spec = pl.BlockSpec((tm, tn), lambda j, i: (i, j))
    act_shape = jax.ShapeDtypeStruct((m, FFN_DIM), BF16)
    w_scr = [pltpu.VMEM((k, tn), BF16)] * 2
    if seq_len >= tm:
        tps = seq_len // tm
        act, tile_last = pl.pallas_call(
            functools.partial(_ffn_up_long_body, tps),
            grid=(ncol, m // tm),
            in_specs=common_in + [pl.BlockSpec((1, FFN_CONV - 1, tn), lambda j, i: (i // tps, 0, j))] + conv_in,
            out_specs=[act_spec, pl.BlockSpec((1, FFN_CONV - 1, tn), lambda j, i: (i, 0, j))],
            out_shape=[act_shape, jax.ShapeDtypeStruct((m // tm, FFN_CONV - 1, FFN_DIM), F32)],
            scratch_shapes=w_scr + [pltpu.VMEM((tm + 8, tn), F32)],
            compiler_params=_cparams(("parallel", "arbitrary")),
            name="ffn_up_gate",
        )(a, w_up_layers, w_up_layers, state, conv_w, conv_b.reshape(1, FFN_DIM))
        return act, tile_last[tps - 1::tps]
    st_spec = pl.BlockSpec((tm // seq_len, FFN_CONV - 1, tn), lambda j, i: (i, 0, j))
    return pl.pallas_call(
        functools.partial(_ffn_up_group_body, seq_len),
        grid=(ncol, m // tm),
        in_specs=common_in + [st_spec] + conv_in,
        out_specs=[act_spec, st_spec],
        out_shape=[act_shape, jax.ShapeDtypeStruct((bsz, FFN_CONV - 1, FFN_DIM), F32)],
        scratch_shapes=w_scr + [pltpu.VMEM((tn // 128, tm, 128), F32)] * 2,
        compiler_params=_cparams(("parallel", "arbitrary")),
        name="ffn_up_gate_grouped",
    )(a, w_up_layers, w_up_layers, state, conv_w, conv_b.reshape(1, FFN_DIM))


def _mixer_a_body(u_ref, v_ref, lng_ref, lnb_ref, w_ref, bias_ref, o_ref, vout_ref):
    u = _gelu(u_ref[...])
    v = _gelu(v_ref[...])
    vc = v - jnp.mean(v, axis=-1, keepdims=True)
    var = jnp.mean(vc * vc, axis=-1, keepdims=True)
    vn = vc * lax.rsqrt(var + EPS) * lng_ref[...] + lnb_ref[...]
    vout_ref[...] = vn
    n = w_ref.shape[1]
    causal = lax.broadcasted_iota(jnp.int32, (n, n), 0) >= lax.broadcasted_iota(jnp.int32, (n, n), 1)
    gd = BRANCH_WIDTH // A_GROUPS
    for g in range(A_GROUPS):
        w = jnp.where(causal, w_ref[g], 0.0)
        mixed = _dot(w, vn[:, g * gd:(g + 1) * gd]) + bias_ref[:, g:g + 1]
        o_ref[:, g * gd:(g + 1) * gd] = (u[:, g * gd:(g + 1) * gd] * mixed).astype(o_ref.dtype)


def mixer_a(proj, ln_g, ln_b, w_mix, bias_rows):
    m = proj.shape[0]
    r = A_CHUNK
    bw = BRANCH_WIDTH
    return pl.pallas_call(
        _mixer_a_body,
        grid=(m // r,),
        in_specs=[pl.BlockSpec((r, bw), lambda i: (i, OFF_A_U // bw)),
                  pl.BlockSpec((r, bw), lambda i: (i, OFF_A_V // bw)),
                  pl.BlockSpec((1, bw), lambda i: (0, 0)),
                  pl.BlockSpec((1, bw), lambda i: (0, 0)),
                  pl.BlockSpec((A_GROUPS, r, r), lambda i: (0, 0, 0)),
                  pl.BlockSpec((r, A_GROUPS), lambda i: (0, 0))],
        out_specs=[pl.BlockSpec((r, bw), lambda i: (i, 0)), pl.BlockSpec((r, bw), lambda i: (i, 0))],
        out_shape=[jax.ShapeDtypeStruct((m, bw), BF16), jax.ShapeDtypeStruct((m, bw), F32)],
        compiler_params=_cparams(("parallel",)),
        name="mixer_a",
    )(proj, proj, ln_g.reshape(1, bw), ln_b.reshape(1, bw), w_mix, bias_rows)


def _gla_head_chunk(q, k, v, g, n_src):
    c = q.shape[0]
    g2 = _cumsum_rows(g) * LOG2E
    n_tiles = c // 8
    row8 = lax.broadcasted_iota(jnp.int32, (8, 1), 0)
    q_t = [q[i * 8:(i + 1) * 8] for i in range(n_tiles)]
    g_t = [g2[i * 8:(i + 1) * 8] for i in range(n_tiles)]
    o_t = [jnp.zeros((8, v.shape[1]), F32) for _ in range(n_tiles)]
    for s in range(n_src):
        for i in range(s // 8, n_tiles):
            a = jnp.sum(q_t[i] * jnp.exp2(g_t[i] - g2[s:s + 1, :]) * k[s:s + 1, :], axis=-1, keepdims=True)
            if i == s // 8:
                a = jnp.where(row8 >= s - 8 * i, a, 0.0)
            o_t[i] = o_t[i] + a * v[s:s + 1, :]
    o = jnp.concatenate(o_t, axis=0) if n_tiles > 1 else o_t[0]
    gl = g2[c - 1:c, :]
    return o, q * jnp.exp2(g2), k * jnp.exp2(gl - g2), jnp.exp2(gl)


def _head_block_diag(xs):
    z = jnp.zeros_like(xs[0])
    return jnp.concatenate(
        [jnp.concatenate([x if j == h else z for j in range(HEADS)], axis=1) for h, x in enumerate(xs)], axis=0)


def _gla_state_dots(qgs, kds, vs, es, st_ref, bi):
    st = st_ref[bi]
    o_inter = _dot_nt(_head_block_diag(qgs), st)
    st_ref[bi] = st * jnp.concatenate(es, axis=1) + _dot_tn(jnp.concatenate(vs, axis=0), _head_block_diag(kds))
    return o_inter


def _gla_state_step(parts, vs, st_ref, bi):
    c = vs[0].shape[0]
    o_inter = _gla_state_dots([p[1] for p in parts], [p[2] for p in parts], vs, [p[3] for p in parts], st_ref, bi)
    return [p[0] + o_inter[h * c:(h + 1) * c] for h, p in enumerate(parts)]


def _gla_pair_tile(q, k, v, g, seg):
    g2 = _segment_cumsum(g, seg) * LOG2E
    tpos = lax.broadcasted_iota(jnp.int32, (8, 1), 0) & (seg - 1)
    o = jnp.sum(q * k, axis=-1, keepdims=True) * v
    for d in range(1, seg):
        a = jnp.sum(q * jnp.exp2(g2 - pltpu.roll(g2, d, axis=0)) * pltpu.roll(k, d, axis=0), axis=-1, keepdims=True)
        o = o + jnp.where(tpos >= d, a, 0.0) * pltpu.roll(v, d, axis=0)
    first = lax.broadcasted_iota(jnp.int32, (8, 1), 0) < seg
    gl_a, gl_b = g2[seg - 1:seg, :], g2[2 * seg - 1:2 * seg, :]
    kd = k * jnp.exp2(jnp.where(first, gl_a, gl_b) - g2)
    return o, q * jnp.exp2(g2), kd, jnp.exp2(gl_a), jnp.exp2(gl_b)


def _gla_pair_step(tiles, vs, st_ref, p, seg):
    first = lax.broadcasted_iota(jnp.int32, (8, 1), 0) < seg
    qgs = [t[1] for t in tiles]
    oi_a = _gla_state_dots(qgs, [jnp.where(first, t[2], 0.0) for t in tiles], vs, [t[3] for t in tiles],
                           st_ref, 2 * p)
    oi_b = _gla_state_dots(qgs, [jnp.where(first, 0.0, t[2]) for t in tiles], vs, [t[4] for t in tiles],
                           st_ref, 2 * p + 1)
    return [t[0] + jnp.where(first, oi_a[h * 8:(h + 1) * 8], oi_b[h * 8:(h + 1) * 8]) for h, t in enumerate(tiles)]


def _load_state_t(s0_ref, st_ref, kdim):
    def per_seq(bi, carry):
        for h in range(HEADS):
            s = s0_ref[bi, h]
            if kdim < HEAD_DIM:
                s = jnp.concatenate([s, jnp.zeros((HEAD_DIM - kdim, HEAD_DIM), F32)], axis=0)
            st_ref[bi, :, h * HEAD_DIM:(h + 1) * HEAD_DIM] = s.T
        return carry

    lax.fori_loop(0, s0_ref.shape[0], per_seq, 0)


def _store_state_t(st_ref, sout_ref, kdim):
    def per_seq(bi, carry):
        for h in range(HEADS):
            sout_ref[bi, h] = st_ref[bi, :, h * HEAD_DIM:(h + 1) * HEAD_DIM].T[:kdim, :]
        return carry

    lax.fori_loop(0, sout_ref.shape[0], per_seq, 0)


def _seq_chunk_loop(nb, n_chunks, unroll, fn):
    def step(it, carry):
        n = 0 if n_chunks == 1 else it % n_chunks
        bj = 0 if nb == unroll else it // n_chunks
        for u in range(unroll):
            fn(bj * unroll + u, n)
        return carry

    lax.fori_loop(0, (nb // unroll) * n_chunks, step, 0, unroll=4)


def _chunk_rows(n, c):
    if isinstance(n, int):
        return n * c, pl.ds(n * c, c)
    r0 = pl.multiple_of(n * c, c)
    return r0, pl.ds(r0, c)


def _valid_rows(t_valid, tb, r0, c):
    return pl.program_id(1) * tb + r0 + lax.broadcasted_iota(jnp.int32, (c, 1), 0) < t_valid


def _mixer_b_body(t_valid, t_pad, unroll, q_ref, f_ref, i_ref, og_ref, lb_ref, ng_ref, s0_ref, o_ref, sout_ref,
                  st_ref):
    nb, tb = q_ref.shape[0], q_ref.shape[1]
    c = min(GLA_CHUNK, tb)
    kd = HEAD_DIM
    padded = t_valid < t_pad
    n_src = t_valid if padded else c

    @pl.when(pl.program_id(1) == 0)
    def _():
        _load_state_t(s0_ref, st_ref, kd)

    def chunk(bi, n):
        r0, rows = _chunk_rows(n, c)
        parts, vs = [], []
        for h in range(HEADS):
            cols = slice(h * kd, (h + 1) * kd)
            lb = lb_ref[:, cols]
            fp = f_ref[bi, rows, cols]
            f = lb + (1.0 - lb) * jax.nn.sigmoid(fp)
            g = jnp.log(jnp.maximum(f, F_TINY))
            k = (1.0 - lb) * jax.nn.sigmoid(-fp)
            if padded:
                ok = _valid_rows(t_valid, tb, r0, c)
                g = jnp.where(ok, g, 0.0)
                k = jnp.where(ok, k, 0.0)
            vs.append(i_ref[bi, rows, cols])
            parts.append(_gla_head_chunk(q_ref[bi, rows, cols], k, vs[h], g, n_src))
        for h, o in enumerate(_gla_state_step(parts, vs, st_ref, bi)):
            cols = slice(h * kd, (h + 1) * kd)
            o = _rms_rows(o) * ng_ref[...] * jax.nn.sigmoid(og_ref[bi, rows, cols])
            o_ref[bi, rows, cols] = o.astype(o_ref.dtype)

    _seq_chunk_loop(nb, tb // c, unroll, chunk)

    @pl.when(pl.program_id(1) == pl.num_programs(1) - 1)
    def _():
        _store_state_t(st_ref, sout_ref, kd)


def mixer_b(pseq, lb, norm_g, s0, *, t_valid, tb, nb, unroll):
    bsz, t_pad, _ = pseq.shape
    assert t_valid == t_pad or t_pad == tb <= GLA_CHUNK
    bw = BRANCH_WIDTH
    field = lambda off: pl.BlockSpec((nb, tb, bw), lambda b, s: (b, s, off // bw))
    state = pl.BlockSpec((nb, HEADS, HEAD_DIM, HEAD_DIM), lambda b, s: (b, 0, 0, 0))
    return pl.pallas_call(
        functools.partial(_mixer_b_body, t_valid, t_pad, unroll),
        grid=(bsz // nb, t_pad // tb),
        in_specs=[field(OFF_B_Q), field(OFF_B_F), field(OFF_B_I), field(OFF_B_G),
                  pl.BlockSpec((1, bw), lambda b, s: (0, 0)),
                  pl.BlockSpec((1, HEAD_DIM), lambda b, s: (0, 0)),
                  state],
        out_specs=[pl.BlockSpec((nb, tb, bw), lambda b, s: (b, s, 0)), state],
        out_shape=[jax.ShapeDtypeStruct((bsz, t_pad, bw), BF16),
                   jax.ShapeDtypeStruct((bsz, HEADS, HEAD_DIM, HEAD_DIM), F32)],
        scratch_shapes=[pltpu.VMEM((nb, HEAD_DIM, HEADS * HEAD_DIM), F32)],
        compiler_params=_cparams(("parallel", "arbitrary")),
        name="mixer_b",
    )(pseq, pseq, pseq, pseq, lb.reshape(1, bw), norm_g.reshape(1, HEAD_DIM), s0)


def _mixer_c_body(t_valid, t_pad, unroll, q_ref, k_ref, v_ref, r_ref, lr_ref, w2_ref, ba_ref, ng_ref, s0_ref,
                  o_ref, sout_ref, st_ref):
    nb, tb = q_ref.shape[0], q_ref.shape[1]
    c = min(GLA_CHUNK, tb)
    kd = C_KDIM
    padded = t_valid < t_pad
    n_src = t_valid if padded else c

    @pl.when(pl.program_id(1) == 0)
    def _():
        _load_state_t(s0_ref, st_ref, kd)

    zpad = jnp.zeros((c, HEAD_DIM - kd), F32)

    def chunk(bi, n):
        r0, rows = _chunk_rows(n, c)
        gate_in = _dot(lr_ref[bi, rows, 0:C_RANK], w2_ref[...]) + ba_ref[...]
        g_all = _log_sigmoid(gate_in) / C_TAU
        if padded:
            ok = _valid_rows(t_valid, tb, r0, c)
            g_all = jnp.where(ok, g_all, 0.0)
        parts, vs = [], []
        for h in range(HEADS):
            kcols = slice(h * kd, (h + 1) * kd)
            q = jnp.concatenate([q_ref[bi, rows, kcols] * (kd ** -0.5), zpad], axis=1)
            k = k_ref[bi, rows, kcols]
            if padded:
                k = jnp.where(ok, k, 0.0)
            k = jnp.concatenate([k, zpad], axis=1)
            g = jnp.concatenate([g_all[:, kcols], zpad], axis=1)
            vs.append(v_ref[bi, rows, h * HEAD_DIM:(h + 1) * HEAD_DIM])
            parts.append(_gla_head_chunk(q, k, vs[h], g, n_src))
        for h, o in enumerate(_gla_state_step(parts, vs, st_ref, bi)):
            vcols = slice(h * HEAD_DIM, (h + 1) * HEAD_DIM)
            o = _rms_rows(o) * ng_ref[...] * _silu(r_ref[bi, rows, vcols])
            o_ref[bi, rows, vcols] = o.astype(o_ref.dtype)

    _seq_chunk_loop(nb, tb // c, unroll, chunk)

    @pl.when(pl.program_id(1) == pl.num_programs(1) - 1)
    def _():
        _store_state_t(st_ref, sout_ref, kd)


def mixer_c(pseq, w_a2, b_a, norm_g, s0, *, t_valid, tb, nb, unroll):
    bsz, t_pad, _ = pseq.shape
    assert t_valid == t_pad or t_pad == tb <= GLA_CHUNK
    bw = BRANCH_WIDTH
    kw = HEADS * C_KDIM
    field = lambda off, w: pl.BlockSpec((nb, tb, w), lambda b, s: (b, s, off // w))
    state = pl.BlockSpec((nb, HEADS, C_KDIM, HEAD_DIM), lambda b, s: (b, 0, 0, 0))
    return pl.pallas_call(
        functools.partial(_mixer_c_body, t_valid, t_pad, unroll),
        grid=(bsz // nb, t_pad // tb),
        in_specs=[field(OFF_C_Q, kw), field(OFF_C_K, kw), field(OFF_C_V, bw), field(OFF_C_R, bw),
                  field(OFF_C_LR, 128),
                  pl.BlockSpec((C_RANK, kw), lambda b, s: (0, 0)),
                  pl.BlockSpec((1, kw), lambda b, s: (0, 0)),
                  pl.BlockSpec((1, HEAD_DIM), lambda b, s: (0, 0)),
                  state],
        out_specs=[pl.BlockSpec((nb, tb, bw), lambda b, s: (b, s, 0)), state],
        out_shape=[jax.ShapeDtypeStruct((bsz, t_pad, bw), BF16),
                   jax.ShapeDtypeStruct((bsz, HEADS, C_KDIM, HEAD_DIM), F32)],
        scratch_shapes=[pltpu.VMEM((nb, HEAD_DIM, HEADS * HEAD_DIM), F32)],
        compiler_params=_cparams(("parallel", "arbitrary")),
        name="mixer_c",
    )(pseq, pseq, pseq, pseq, pseq, w_a2, b_a.reshape(1, kw), norm_g.reshape(1, HEAD_DIM), s0)


def _grouped_pairs_loop(n_rows, seg, st_ref, head_inputs, head_outputs):
    def step(it, carry):
        rows = pl.ds(pl.multiple_of(it * 16, 16), 16)
        ins = [head_inputs(rows, h) for h in range(HEADS)]
        halves = []
        for half in range(2):
            sl = slice(half * 8, (half + 1) * 8)
            vs = [x[2][sl] for x in ins]
            tiles = [_gla_pair_tile(x[0][sl], x[1][sl], x[2][sl], x[3][sl], seg) for x in ins]
            halves.append(_gla_pair_step(tiles, vs, st_ref, 2 * it + half, seg))
        for h in range(HEADS):
            head_outputs(rows, h, jnp.concatenate([halves[0][h], halves[1][h]], axis=0))
        return carry

    lax.fori_loop(0, n_rows // 16, step, 0)


def _mixer_b_group_body(seg, q_ref, f_ref, i_ref, og_ref, lb_ref, ng_ref, s0_ref, o_ref, sout_ref, st_ref):
    kd = HEAD_DIM
    _load_state_t(s0_ref, st_ref, kd)

    def head_inputs(rows, h):
        cols = slice(h * kd, (h + 1) * kd)
        lb = lb_ref[:, cols]
        fp = f_ref[0, rows, cols]
        f = lb + (1.0 - lb) * jax.nn.sigmoid(fp)
        return (q_ref[0, rows, cols], (1.0 - lb) * jax.nn.sigmoid(-fp), i_ref[0, rows, cols],
                jnp.log(jnp.maximum(f, F_TINY)))

    def head_outputs(rows, h, o):
        cols = slice(h * kd, (h + 1) * kd)
        o = _rms_rows(o) * ng_ref[...] * jax.nn.sigmoid(og_ref[0, rows, cols])
        o_ref[0, rows, cols] = o.astype(o_ref.dtype)

    _grouped_pairs_loop(q_ref.shape[1], seg, st_ref, head_inputs, head_outputs)
    _store_state_t(st_ref, sout_ref, kd)


def _mixer_c_group_body(seg, q_ref, k_ref, v_ref, r_ref, lr_ref, w2_ref, ba_ref, ng_ref, s0_ref, o_ref, sout_ref,
                        st_ref, g_scr):
    kd = C_KDIM
    _load_state_t(s0_ref, st_ref, kd)
    g_scr[...] = _log_sigmoid(_dot(lr_ref[0, :, 0:C_RANK], w2_ref[...]) + ba_ref[...]) / C_TAU
    zpad = jnp.zeros((16, HEAD_DIM - kd), F32)

    def head_inputs(rows, h):
        kcols = slice(h * kd, (h + 1) * kd)
        pad = lambda x: jnp.concatenate([x, zpad], axis=1)
        return (pad(q_ref[0, rows, kcols] * (kd ** -0.5)), pad(k_ref[0, rows, kcols]),
                v_ref[0, rows, h * HEAD_DIM:(h + 1) * HEAD_DIM], pad(g_scr[rows, kcols]))

    def head_outputs(rows, h, o):
        vcols = slice(h * HEAD_DIM, (h + 1) * HEAD_DIM)
        o = _rms_rows(o) * ng_ref[...] * _silu(r_ref[0, rows, vcols])
        o_ref[0, rows, vcols] = o.astype(o_ref.dtype)

    _grouped_pairs_loop(q_ref.shape[1], seg, st_ref, head_inputs, head_outputs)
    _store_state_t(st_ref, sout_ref, kd)


def _layer_stacked_state(body, prev_states, state_ref_pos, seqs_per_step):
    n_prev = 0 if prev_states is None else prev_states.shape[0]

    def wrapped(*refs):
        refs = list(refs)
        if n_prev:
            prev_ref = refs.pop(0)
            refs[state_ref_pos][0:n_prev] = prev_ref[...]
        refs[state_ref_pos] = refs[state_ref_pos].at[n_prev]
        return body(*refs)

    def spec(shape_tail, layers):
        return pl.BlockSpec((layers, seqs_per_step) + shape_tail, lambda s: (0, s) + (0,) * len(shape_tail))

    def out_shape(s0):
        return jax.ShapeDtypeStruct((n_prev + 1,) + s0.shape, F32)

    prev_specs = lambda s0: [spec(s0.shape[1:], n_prev)] if n_prev else []
    prev_inputs = [prev_states] if n_prev else []
    return wrapped, prev_specs, prev_inputs, lambda s0: spec(s0.shape[1:], n_prev + 1), out_shape


def _grouped_gla_call(body, prows, fields, params, s0, prev_states, kdim, *, seg, tb, extra_scratch=()):
    m = prows.shape[0]
    bw = BRANCH_WIDTH
    p3 = prows.reshape(1, m, -1)
    state = pl.BlockSpec((tb // seg, HEADS, kdim, HEAD_DIM), lambda s: (s, 0, 0, 0))
    n_in = len(fields) + len(params) + 1
    wrapped, prev_specs, prev_inputs, state_out_spec, state_out_shape = _layer_stacked_state(
        functools.partial(body, seg), prev_states, n_in + 1, tb // seg)
    o, s_out = pl.pallas_call(
        wrapped,
        grid=(m // tb,),
        in_specs=prev_specs(s0)
        + [pl.BlockSpec((1, tb, w), functools.partial(lambda s, c: (0, s, c), c=off // w)) for off, w in fields]
        + [pl.BlockSpec(x.shape, functools.partial(lambda s, n: (0,) * n, n=x.ndim)) for x in params] + [state],
        out_specs=[pl.BlockSpec((1, tb, bw), lambda s: (0, s, 0)), state_out_spec(s0)],
        out_shape=[jax.ShapeDtypeStruct((1, m, bw), BF16), state_out_shape(s0)],
        scratch_shapes=[pltpu.VMEM((tb // seg, HEAD_DIM, HEADS * HEAD_DIM), F32), *extra_scratch],
        compiler_params=_cparams(("parallel",)),
        name=body.__name__.strip("_"),
    )(*prev_inputs, *([p3] * len(fields)), *params, s0)
    return o.reshape(m, bw), s_out


def mixer_b_grouped(prows, lb, norm_g, s0, prev_states, *, seg, tb):
    bw = BRANCH_WIDTH
    return _grouped_gla_call(
        _mixer_b_group_body, prows, [(OFF_B_Q, bw), (OFF_B_F, bw), (OFF_B_I, bw), (OFF_B_G, bw)],
        [lb.reshape(1, bw), norm_g.reshape(1, HEAD_DIM)], s0, prev_states, HEAD_DIM, seg=seg, tb=tb)


def mixer_c_grouped(prows, w_a2, b_a, norm_g, s0, prev_states, *, seg, tb):
    bw = BRANCH_WIDTH
    kw = HEADS * C_KDIM
    return _grouped_gla_call(
        _mixer_c_group_body, prows, [(OFF_C_Q, kw), (OFF_C_K, kw), (OFF_C_V, bw), (OFF_C_R, bw), (OFF_C_LR, 128)],
        [w_a2, b_a.reshape(1, kw), norm_g.reshape(1, HEAD_DIM)], s0, prev_states, C_KDIM, seg=seg, tb=tb,
        extra_scratch=(pltpu.VMEM((tb, kw), F32),))


def _unit_lower_inverse(a, order):
    c = a.shape[0]
    eye = (lax.broadcasted_iota(jnp.int32, (c, c), 0) == lax.broadcasted_iota(jnp.int32, (c, c), 1)).astype(F32)
    p = eye - a
    pw = a
    n = 2
    while n < order:
        pw = _dot(pw, pw)
        p = p + _dot(p, pw)
        n *= 2
    return p


D_ROWS = 64


def _stack_heads(x):
    return jnp.concatenate([x[:, h * HEAD_DIM:(h + 1) * HEAD_DIM] for h in range(HEADS)], axis=0)


def _stack_head_lanes(x):
    return jnp.concatenate([x[:, h:h + 1] for h in range(HEADS)], axis=0)


def _lane_pad_heads(x):
    return jnp.pad(x.reshape(1, HEADS), ((0, 0), (0, 128 - HEADS)))


def _segment_cumsum(x, seg):
    tpos = lax.broadcasted_iota(jnp.int32, x.shape, 0) & (seg - 1)
    sh = 1
    while sh < seg:
        x = x + jnp.where(tpos >= sh, pltpu.roll(x, sh, axis=0), 0.0)
        sh *= 2
    return x


def _delta_chunk_operands(seg, qkv, beta_all, g_all):
    hd = HEAD_DIM
    bw = BRANCH_WIDTH
    r = HEADS * qkv.shape[0]
    q = _stack_heads(qkv[:, 0:bw])
    k = _stack_heads(qkv[:, bw:2 * bw])
    v = _stack_heads(qkv[:, 2 * bw:3 * bw])
    q = q * lax.rsqrt(jnp.sum(q * q, axis=-1, keepdims=True) + EPS) * (hd ** -0.5)
    k = k * lax.rsqrt(jnp.sum(k * k, axis=-1, keepdims=True) + EPS)
    beta = _stack_head_lanes(beta_all)
    gc = _stack_head_lanes(_segment_cumsum(g_all, seg))
    ri = lax.broadcasted_iota(jnp.int32, (r, r), 0)
    ci = lax.broadcasted_iota(jnp.int32, (r, r), 1)
    shift = seg.bit_length() - 1
    same = (ri >> shift) == (ci >> shift)
    gr = jnp.sum(jnp.where(ri == ci, gc, 0.0), axis=0, keepdims=True)
    decay = jnp.where(same, jnp.where(ri >= ci, jnp.exp(jnp.minimum(gc - gr, 0.0)), 0.0), 0.0)
    a_mat = jnp.where(ri > ci, beta * decay * _dot_nt(k, k), 0.0)
    t_inv = _unit_lower_inverse(a_mat, seg)
    eg = jnp.exp(gc)
    sol = _dot(t_inv, jnp.concatenate([(beta * eg) * k, beta * v], axis=1))
    qk = _dot_nt(q, k) * decay
    is_last = (ci & (seg - 1)) == seg - 1
    gl = jnp.sum(jnp.where(same, jnp.where(is_last, gr, 0.0), 0.0), axis=1, keepdims=True)
    return sol[:, :hd], sol[:, hd:], qk, q * eg, k * jnp.exp(gl - gc), jnp.exp(gl)


def _mixer_d_body(x_ref, beta_ref, dec_ref, z_ref, cst_ref, s0_ref, cw_ref, alog_ref, dtb_ref, ng_ref,
                  o_ref, cst_out_ref, sout_ref, ext, conv, st_ref):
    nb, tb = x_ref.shape[0], x_ref.shape[1]
    c = D_ROWS
    hd = HEAD_DIM
    t = pl.program_id(1)
    nt = pl.num_programs(1)

    @pl.when(t == 0)
    def _():
        ext[:, 5:8, :] = cst_ref[...]
        st_ref[...] = s0_ref[...].reshape(nb * HEADS, hd, hd)

    for bi in range(nb):
        ext[bi, 8:8 + tb, :] = x_ref[bi]
        y = ext[bi, 5:5 + tb, :] * cw_ref[0:1, :]
        for j in range(1, D_CONV):
            y = y + ext[bi, 5 + j:5 + j + tb, :] * cw_ref[j:j + 1, :]
        conv[bi] = _silu(y)

    @pl.when(t == nt - 1)
    def _():
        cst_out_ref[...] = ext[:, 8 + tb - (D_CONV - 1):8 + tb, :]

    ext[:, 5:8, :] = ext[:, 5 + tb:8 + tb, :]

    a_neg = -jnp.exp(alog_ref[...])

    def chunk(n, carry):
        rows = pl.ds(pl.multiple_of(n * c, c), c)
        for bi in range(nb):
            beta_all = jax.nn.sigmoid(beta_ref[bi, rows, :])
            g_all = a_neg * _softplus(dec_ref[bi, rows, :] + dtb_ref[...])
            w, u, qk, qg, kdec, egl = _delta_chunk_operands(c, conv[bi, rows, :], beta_all, g_all)
            deltas, oqs = [], []
            for h in range(HEADS):
                hs = slice(h * c, (h + 1) * c)
                st = st_ref[bi * HEADS + h]
                x = _dot(jnp.concatenate([w[hs], qg[hs]], axis=0), st)
                delta = u[hs] - x[:c]
                deltas.append(delta)
                oqs.append(x[c:])
                st_ref[bi * HEADS + h] = egl[(h + 1) * c - 1:(h + 1) * c, :] * st + _dot_tn(kdec[hs], delta)
            o = jnp.concatenate(oqs, axis=0) + _dot(qk, jnp.concatenate(deltas, axis=0))
            o = _rms_rows(o) * ng_ref[...] * _silu(_stack_heads(z_ref[bi, rows, :]))
            for h in range(HEADS):
                o_ref[bi, rows, h * hd:(h + 1) * hd] = o[h * c:(h + 1) * c].astype(o_ref.dtype)
        return carry

    lax.fori_loop(0, tb // c, chunk, 0, unroll=2)

    @pl.when(t == nt - 1)
    def _():
        sout_ref[...] = st_ref[...].reshape(nb, HEADS, hd, hd)


def _mixer_d_group_body(seg, x_ref, cst_ref, beta_ref, dec_ref, z_ref, s0_ref, cw_ref, alog_ref, dtb_ref, ng_ref,
                        o_ref, cst_out_ref, sout_ref, conv, w_s, u_s, qg_s, kd_s, egl_s, delta_s, oq_s, stp_s, x_s):
    tb = x_ref.shape[1]
    nseq = tb // seg
    c = D_ROWS
    hd = HEAD_DIM
    pairs = c // 8
    hist_rows = D_CONV - 1

    lane_blocks = stp_s.shape[0]
    x = x_ref[0]
    stp_s[...] = jnp.zeros_like(stp_s)
    for cb in range(lane_blocks):
        lanes = slice(cb * 128, (cb + 1) * 128)
        x_s[cb] = x[:, lanes]
        for j in range(hist_rows):
            stp_s[cb, pl.ds(j, nseq, stride=seg), :] = cst_ref[:, j, lanes]
    stp = jnp.concatenate([stp_s[cb] for cb in range(lane_blocks)], axis=1)

    tpos = lax.broadcasted_iota(jnp.int32, (tb, 1), 0) & (seg - 1)
    y = x * cw_ref[3:4, :]
    for d in range(1, D_CONV):
        hist = stp if d == 3 else pltpu.roll(stp, tb - (3 - d), axis=0)
        y = y + jnp.where(tpos < d, hist, pltpu.roll(x, d, axis=0)) * cw_ref[3 - d:4 - d, :]
    conv[...] = _silu(y)
    for cb in range(lane_blocks):
        for j in range(hist_rows):
            cst_out_ref[:, j, cb * 128:(cb + 1) * 128] = x_s[cb, pl.ds(seg - hist_rows + j, nseq, stride=seg), :]

    a_neg = -jnp.exp(alog_ref[...])
    low = lax.broadcasted_iota(jnp.int32, (8, 1), 0) < seg

    for n in range(tb // c):
        rows = slice(n * c, (n + 1) * c)
        beta_all = jax.nn.sigmoid(beta_ref[0, rows, :])
        g_all = a_neg * _softplus(dec_ref[0, rows, :] + dtb_ref[...])
        w, u, qk, qg, kdec, egl = _delta_chunk_operands(seg, conv[rows, :], beta_all, g_all)
        w_s[...] = w
        u_s[...] = u
        qg_s[...] = qg
        kd_s[...] = kdec
        egl_s[...] = jnp.broadcast_to(egl, (HEADS * c, hd))

        def pair(p, carry):
            rp = pl.ds(pl.multiple_of(p * 8, 8), 8)
            h = p // pairs
            ja = n * (c // seg) + (p % pairs) * 2
            sa = s0_ref[ja, h]
            sb = s0_ref[ja + 1, h]
            lhs = jnp.concatenate([w_s[rp, :], qg_s[rp, :]], axis=0)
            xa = _dot(lhs, sa)
            xb = _dot(lhs, sb)
            delta = u_s[rp, :] - jnp.where(low, xa[:8], xb[:8])
            delta_s[rp, :] = delta
            oq_s[rp, :] = jnp.where(low, xa[8:], xb[8:])
            kd = kd_s[rp, :]
            e = egl_s[rp, :]
            sout_ref[ja, h] = e[seg - 1:seg, :] * sa + _dot_tn(jnp.where(low, kd, 0.0), delta)
            sout_ref[ja + 1, h] = e[2 * seg - 1:2 * seg, :] * sb + _dot_tn(jnp.where(low, 0.0, kd), delta)
            return carry

        lax.fori_loop(0, HEADS * pairs, pair, 0)
        o = oq_s[...] + _dot(qk, delta_s[...])
        o = _rms_rows(o) * ng_ref[...] * _silu(_stack_heads(z_ref[0, rows, :]))
        for h in range(HEADS):
            o_ref[0, rows, h * hd:(h + 1) * hd] = o[h * c:(h + 1) * c].astype(o_ref.dtype)


def mixer_d(pseq, conv_state, s0, conv_w, a_log, dt_bias, norm_g, *, tb, nb):
    bsz, t, _ = pseq.shape
    bw = BRANCH_WIDTH
    field = lambda off, w: pl.BlockSpec((nb, tb, w), lambda b, s: (b, s, off // w))
    state = pl.BlockSpec((nb, HEADS, HEAD_DIM, HEAD_DIM), lambda b, s: (b, 0, 0, 0))
    cstate = pl.BlockSpec((nb, D_CONV - 1, D_QKV), lambda b, s: (b, 0, 0))
    return pl.pallas_call(
        _mixer_d_body,
        grid=(bsz // nb, t // tb),
        in_specs=[field(OFF_D_QKV, D_QKV), field(OFF_D_BETA, 128), field(OFF_D_DECAY, 128), field(OFF_D_Z, bw),
                  cstate, state,
                  pl.BlockSpec((D_CONV, D_QKV), lambda b, s: (0, 0)),
                  pl.BlockSpec((1, 128), lambda b, s: (0, 0)),
                  pl.BlockSpec((1, 128), lambda b, s: (0, 0)),
                  pl.BlockSpec((1, HEAD_DIM), lambda b, s: (0, 0))],
        out_specs=[pl.BlockSpec((nb, tb, bw), lambda b, s: (b, s, 0)), cstate, state],
        out_shape=[jax.ShapeDtypeStruct((bsz, t, bw), BF16),
                   jax.ShapeDtypeStruct((bsz, D_CONV - 1, D_QKV), F32),
                   jax.ShapeDtypeStruct((bsz, HEADS, HEAD_DIM, HEAD_DIM), F32)],
        scratch_shapes=[pltpu.VMEM((nb, tb + 8, D_QKV), F32), pltpu.VMEM((nb, tb, D_QKV), F32),
                        pltpu.VMEM((nb * HEADS, HEAD_DIM, HEAD_DIM), F32)],
        compiler_params=_cparams(("parallel", "arbitrary")),
        name="mixer_d",
    )(pseq, pseq, pseq, pseq, conv_state, s0, conv_w, _lane_pad_heads(a_log), _lane_pad_heads(dt_bias),
      norm_g.reshape(1, HEAD_DIM))


def mixer_d_grouped(prows, conv_state, s0, conv_w, a_log, dt_bias, norm_g, prev_states, *, seg, tb):
    m = prows.shape[0]
    bsz = m // seg
    bw = BRANCH_WIDTH
    field = lambda off, w: pl.BlockSpec((1, tb, w), lambda s: (0, s, off // w))
    state = pl.BlockSpec((tb // seg, HEADS, HEAD_DIM, HEAD_DIM), lambda s: (s, 0, 0, 0))
    cstate = pl.BlockSpec((tb // seg, D_CONV - 1, D_QKV), lambda s: (s, 0, 0))
    rows128 = pltpu.VMEM((HEADS * D_ROWS, HEAD_DIM), F32)
    lane_rows = pltpu.VMEM((D_QKV // 128, tb, 128), F32)
    n_in, state_out_pos = 10, 2
    wrapped, prev_specs, prev_inputs, state_out_spec, state_out_shape = _layer_stacked_state(
        functools.partial(_mixer_d_group_body, seg), prev_states, n_in + state_out_pos, tb // seg)
    o, cst, s_out = pl.pallas_call(
        wrapped,
        grid=(m // tb,),
        in_specs=prev_specs(s0) + [
                  field(OFF_D_QKV, D_QKV), cstate,
                  field(OFF_D_BETA, 128), field(OFF_D_DECAY, 128), field(OFF_D_Z, bw),
                  state,
                  pl.BlockSpec((D_CONV, D_QKV), lambda s: (0, 0)),
                  pl.BlockSpec((1, 128), lambda s: (0, 0)),
                  pl.BlockSpec((1, 128), lambda s: (0, 0)),
                  pl.BlockSpec((1, HEAD_DIM), lambda s: (0, 0))],
        out_specs=[pl.BlockSpec((1, tb, bw), lambda s: (0, s, 0)), cstate, state_out_spec(s0)],
        out_shape=[jax.ShapeDtypeStruct((1, m, bw), BF16),
                   jax.ShapeDtypeStruct((bsz, D_CONV - 1, D_QKV), F32),
                   state_out_shape(s0)],
        scratch_shapes=[pltpu.VMEM((tb, D_QKV), F32)] + [rows128] * 7 + [lane_rows] * 2,
        compiler_params=_cparams(("parallel",)),
        name="mixer_d_grouped",
    )(*prev_inputs, prows.reshape(1, m, -1), conv_state, prows.reshape(1, m, -1), prows.reshape(1, m, -1),
      prows.reshape(1, m, -1), s0, conv_w, _lane_pad_heads(a_log), _lane_pad_heads(dt_bias),
      norm_g.reshape(1, HEAD_DIM))
    return o.reshape(m, bw), cst, s_out


def _pack_w_in(w):
    z = lambda n: jnp.zeros((w.shape[0], n), w.dtype)
    cols = [w[:, 4624:6160],
            w[:, 0:512], w[:, 512:1024],
            w[:, 1024:1536], w[:, 1536:2048], w[:, 2048:2560], w[:, 2560:3072],
            w[:, 3584:4096], w[:, 4112:4624],
            w[:, 6168:6680],
            w[:, 3072:3328], w[:, 3328:3584],
            w[:, 4096:4112], z(112),
            w[:, 6160:6164], z(124),
            w[:, 6164:6168], z(252),
            w[:, 6680:14872]]
    return jnp.concatenate(cols, axis=1).astype(BF16)


def _trunk(x, p, s_hgrn, s_gla, s_delta, c_dconv, c_fconv, wts, lbs, *, sample):
    bsz, t, _ = x.shape
    m = bsz * t
    tm = m if sample else 1024
    h = x.reshape(m, D_MODEL)
    outs = {k: [] for k in ("hgrn", "gla", "delta", "dconv", "fconv", "v")}
    s_b = s_c = s_d = None
    for l in range(DEPTH):
        w = wts[l]
        proj, gates = in_proj(h, w["g_mix"], w["w_in"], tm=tm, tn=1024)
        o_a, v_a = mixer_a(proj, w["a_ln_g"], w["a_ln_b"], w["a_w_mix"], w["a_bias_rows"])
        b_args = (lbs[l], w["b_norm_g"], s_hgrn[l])
        c_args = (w["c_w_a2"], w["c_b_a"], w["c_norm_g"], s_gla[l])
        d_args = (c_dconv[l], s_delta[l], w["d_conv_w"], w["d_a_log"], w["d_dt_bias"], w["d_norm_g"])
        if sample:
            o_b, s_b = mixer_b_grouped(proj, *b_args, s_b, seg=t, tb=D_ROWS)
            o_c, s_c = mixer_c_grouped(proj, *c_args, s_c, seg=t, tb=D_ROWS)
            o_d, nb_d, s_d = mixer_d_grouped(proj, *d_args, s_d, seg=t, tb=D_ROWS)
        else:
            pseq = proj.reshape(bsz, t, N_SEQ_COLS)
            flat = lambda o: o.reshape(m, BRANCH_WIDTH)
            o_b, s_b = mixer_b(pseq, *b_args, t_valid=t, tb=256, nb=1, unroll=1)
            o_c, s_c = mixer_c(pseq, *c_args, t_valid=t, tb=256, nb=1, unroll=1)
            o_d, nb_d, s_d = mixer_d(pseq, *d_args, tb=256, nb=2 if bsz % 2 == 0 else 1)
            o_b, o_c, o_d = flat(o_b), flat(o_c), flat(o_d)
        merged = merge_branches((o_a, o_b, o_c, o_d), gates, w["w_branch"], tm=tm, tn=512)
        h = matmul_residual(merged, w["w_out_layers"], l, h, tm=tm, tn=1024)

        act, nb_f = ffn_up_gate(h, w["g_ffn"], w["w_ffn_up_layers"], l, c_fconv[l], w["ffn_conv_w"],
                                w["ffn_conv_b"], seq_len=t, tm=tm, tn=512)
        h = matmul_residual(act, w["w_ffn_down_layers"], l, h, tm=min(tm, 512), tn=512)

        h = ple_update(h, p[l].reshape(m, PLE_DIM), w["g_ple"], w["w_ple_gate"], w["w_ple_proj"], tm=tm, tn=512)

        if not sample:
            outs["hgrn"].append(s_b)
            outs["gla"].append(s_c)
            outs["delta"].append(s_d)
        outs["dconv"].append(nb_d)
        outs["fconv"].append(nb_f)
        outs["v"].append(v_a.reshape(bsz, t, BRANCH_WIDTH))
    y = final_norm(h, wts[0]["g_final"], tm=min(tm, 512)).reshape(bsz, t, D_MODEL)
    states = (s_b, s_c, s_d) if sample else tuple(jnp.stack(outs[k]) for k in ("hgrn", "gla", "delta"))
    return (y,) + states + tuple(jnp.stack(outs[k]) for k in ("dconv", "fconv", "v"))


def kernel(x_prompt, x_sample, state_hgrn, state_gla, state_delta, state_delta_conv, state_ffn_conv, p_prompt, p_sample, g_mix, w_in, a_ln_g, a_ln_b, a_w_s, a_b_s, b_lb, b_norm_g, c_w_a2, c_b_a, c_norm_g, d_conv_w, d_a_log, d_dt_bias, d_norm_g, w_branch, w_out, g_ffn, w_ffn_up, ffn_conv_w, ffn_conv_b, w_ffn_down, g_ple, w_ple_gate, w_ple_proj, g_final):
    bp, t_p, _ = x_prompt.shape
    bs, t_s, _ = x_sample.shape
    sm = jax.nn.softmax(b_lb.astype(F32), axis=0)
    lbs = jnp.cumsum(sm, axis=0) - sm[0]

    shared = [dict(
        g_mix=g_mix[l], w_in=_pack_w_in(w_in[l]), a_ln_g=a_ln_g[l], a_ln_b=a_ln_b[l],
        b_norm_g=b_norm_g[l], c_w_a2=c_w_a2[l], c_b_a=c_b_a[l], c_norm_g=c_norm_g[l],
        d_conv_w=d_conv_w[l], d_a_log=d_a_log[l], d_dt_bias=d_dt_bias[l], d_norm_g=d_norm_g[l],
        w_branch=w_branch[l].astype(BF16), w_out_layers=w_out, g_ffn=g_ffn[l],
        w_ffn_up_layers=w_ffn_up, ffn_conv_w=ffn_conv_w[l], ffn_conv_b=ffn_conv_b[l],
        w_ffn_down_layers=w_ffn_down, g_ple=g_ple[l], w_ple_gate=w_ple_gate[l].astype(BF16),
        w_ple_proj=w_ple_proj[l].astype(BF16), g_final=g_final) for l in range(DEPTH)]

    def layer_weights(l, sample):
        if sample:
            seqs = A_CHUNK // t_s
            idx = jnp.arange(A_CHUNK) // t_s
            same_seq = idx[:, None] == idx[None, :]
            w_mix = jnp.where(same_seq, jnp.tile(a_w_s[l, :, :t_s, :t_s], (1, seqs, seqs)), 0.0)
            bias_rows = jnp.tile(a_b_s[l, :, :t_s].T, (seqs, 1))
        else:
            w_mix = a_w_s[l]
            bias_rows = a_b_s[l].T
        return dict(shared[l], a_w_mix=w_mix, a_bias_rows=bias_rows)

    dt = x_prompt.dtype
    zeros = lambda *s: jnp.zeros((DEPTH, bp) + s, dt)
    out_p = _trunk(x_prompt, p_prompt, zeros(HEADS, HEAD_DIM, HEAD_DIM), zeros(HEADS, C_KDIM, HEAD_DIM),
                   zeros(HEADS, HEAD_DIM, HEAD_DIM), zeros(D_CONV - 1, D_QKV), zeros(FFN_CONV - 1, FFN_DIM),
                   [layer_weights(l, False) for l in range(DEPTH)], lbs, sample=False)
    out_s = _trunk(x_sample, p_sample, state_hgrn, state_gla, state_delta, state_delta_conv, state_ffn_conv,
                   [layer_weights(l, True) for l in range(DEPTH)], lbs, sample=True)
    y_p, hgrn_p, gla_p, delta_p, dconv_p, fconv_p, _ = out_p
    y_s, hgrn_s, gla_s, delta_s, dconv_s, fconv_s, v_s = out_s
    return (y_p, y_s, hgrn_p, hgrn_s, gla_p, gla_s, delta_p, delta_s,
            dconv_p, dconv_s, fconv_p, fconv_s, v_s)
```

```python
import functools

import jax
import jax.numpy as jnp
from jax import lax
from jax.experimental import pallas as pl
from jax.experimental.pallas import tpu as pltpu

F32 = jnp.float32
BF16 = jnp.bfloat16

D_MODEL = 2048
DEPTH = 2
PLE_DIM = 256
EPS = 1e-6
F_TINY = 1e-30
N_BRANCH = 4
BRANCH_WIDTH = D_MODEL // 4
A_GROUPS = 4
A_CHUNK = 128
HEADS = 4
HEAD_DIM = 128
C_KDIM = 64
C_RANK = 16
C_TAU = 16.0
D_CONV = 4
D_QKV = 3 * BRANCH_WIDTH
FFN_DIM = 5632
FFN_CONV = 3
GLA_CHUNK = 16
LOG2E = 1.4426950408889634

OFF_D_QKV = 0
OFF_A_U = 1536
OFF_A_V = 2048
OFF_B_Q = 2560
OFF_B_F = 3072
OFF_B_I = 3584
OFF_B_G = 4096
OFF_C_V = 4608
OFF_C_R = 5120
OFF_D_Z = 5632
OFF_C_Q = 6144
OFF_C_K = 6400
OFF_C_LR = 6656
OFF_D_BETA = 6784
OFF_D_DECAY = 6912
OFF_GATES = 7168
N_SEQ_COLS = 7168
N_PROJ = OFF_GATES + N_BRANCH * D_MODEL

VMEM_LIMIT = 56 * 1024 * 1024


def _cparams(sem):
    return pltpu.CompilerParams(dimension_semantics=sem, vmem_limit_bytes=VMEM_LIMIT)


def _gelu(x):
    return 0.5 * x * (1.0 + jnp.tanh(0.7978845608028654 * (x + 0.044715 * (x * x * x))))


def _silu(x):
    return x * jax.nn.sigmoid(x)


def _softplus(x):
    return jnp.maximum(x, 0.0) + jnp.log1p(jnp.exp(-jnp.abs(x)))


def _log_sigmoid(x):
    return -_softplus(-x)


def _rms_rows(x):
    return x * lax.rsqrt(jnp.mean(x * x, axis=-1, keepdims=True) + EPS)


def _dot(a, b):
    return jnp.dot(a, b, preferred_element_type=F32)


def _dot_nt(a, b):
    return lax.dot_general(a, b, (((1,), (1,)), ((), ())), preferred_element_type=F32)


def _dot_tn(a, b):
    return lax.dot_general(a, b, (((0,), (0,)), ((), ())), preferred_element_type=F32)


def _cumsum_rows(x):
    n = x.shape[0]
    row = lax.broadcasted_iota(jnp.int32, x.shape, 0)
    sh = 1
    while sh < n:
        x = x + jnp.where(row >= sh, pltpu.roll(x, sh, axis=0), 0.0)
        sh *= 2
    return x


NORM_ROWS = 128


def _norm_to_scratch(x_ref, g_ref, a_scr):
    def slab(n, carry):
        rows = pl.ds(pl.multiple_of(n * NORM_ROWS, NORM_ROWS), NORM_ROWS)
        a_scr[rows, :] = (_rms_rows(x_ref[rows, :]) * g_ref[...]).astype(BF16)
        return carry

    lax.fori_loop(0, x_ref.shape[0] // NORM_ROWS, slab, 0)


def _matmul_residual_body(x_ref, w_ref, r_ref, o_ref, w_scr):
    @pl.when(pl.program_id(1) == 0)
    def _():
        def slab(n, carry):
            rows = pl.ds(pl.multiple_of(n * NORM_ROWS, NORM_ROWS), NORM_ROWS)
            w_scr[rows, :] = w_ref[rows, :].astype(BF16)
            return carry

        lax.fori_loop(0, w_ref.shape[0] // NORM_ROWS, slab, 0)

    o_ref[...] = r_ref[...] + _dot(x_ref[...], w_scr[...])


def matmul_residual(x, w_layers, layer, res, *, tm, tn):
    m, k = x.shape
    n = w_layers.shape[2]
    return pl.pallas_call(
        _matmul_residual_body,
        grid=(n // tn, m // tm),
        in_specs=[pl.BlockSpec((tm, k), lambda j, i: (i, 0)),
                  pl.BlockSpec((None, k, tn), lambda j, i: (layer, 0, j)),
                  pl.BlockSpec((tm, tn), lambda j, i: (i, j))],
        out_specs=pl.BlockSpec((tm, tn), lambda j, i: (i, j)),
        out_shape=jax.ShapeDtypeStruct((m, n), F32),
        scratch_shapes=[pltpu.VMEM((k, tn), BF16)],
        compiler_params=_cparams(("parallel", "arbitrary")),
        name="matmul_residual",
    )(x, w_layers, res)


def _in_proj_body(n_seq_tiles, x_ref, g_ref, w_ref, seq_ref, gate_ref, a_scr):
    j = pl.program_id(1)

    @pl.when(j == 0)
    def _():
        _norm_to_scratch(x_ref, g_ref, a_scr)

    acc = _dot(a_scr[...], w_ref[...])

    @pl.when(j < n_seq_tiles)
    def _():
        seq_ref[...] = acc

    @pl.when(j >= n_seq_tiles)
    def _():
        gate_ref[...] = acc.astype(gate_ref.dtype)


def in_proj(x, g, w_layers, layer, *, tm, tn):
    m, k = x.shape
    n_seq = N_SEQ_COLS // tn
    n_gate = N_BRANCH * D_MODEL // tn
    return pl.pallas_call(
        functools.partial(_in_proj_body, n_seq),
        grid=(m // tm, n_seq + n_gate),
        in_specs=[pl.BlockSpec((tm, k), lambda i, j: (i, 0)),
                  pl.BlockSpec((1, k), lambda i, j: (0, 0)),
                  pl.BlockSpec((None, k, tn), lambda i, j: (layer, 0, j))],
        out_specs=[pl.BlockSpec((tm, tn), lambda i, j: (i, jnp.minimum(j, n_seq - 1))),
                   pl.BlockSpec((tm, tn), lambda i, j: (i, jnp.maximum(j - n_seq, 0)))],
        out_shape=[jax.ShapeDtypeStruct((m, N_SEQ_COLS), F32),
                   jax.ShapeDtypeStruct((m, N_BRANCH * D_MODEL), BF16)],
        scratch_shapes=[pltpu.VMEM((tm, k), BF16)],
        compiler_params=_cparams(("parallel", "arbitrary")),
        name="in_proj",
    )(x, g.reshape(1, k), w_layers)


def _merge_body(oa_ref, ob_ref, oc_ref, od_ref, ga_ref, gb_ref, gc_ref, gd_ref, w_ref, o_ref):
    acc = None
    for b, (o_b, g_b) in enumerate(((oa_ref, ga_ref), (ob_ref, gb_ref), (oc_ref, gc_ref), (od_ref, gd_ref))):
        term = jax.nn.sigmoid(g_b[...].astype(F32)) * _dot(o_b[...], w_ref[b])
        acc = term if acc is None else acc + term
    o_ref[...] = acc.astype(o_ref.dtype)


def merge_branches(o_branches, gates, w_branch, *, tm, tn):
    m = gates.shape[0]
    gate_specs = [pl.BlockSpec((tm, tn), functools.partial(
        lambda i, j, b: (i, b * (D_MODEL // tn) + j), b=b)) for b in range(N_BRANCH)]
    return pl.pallas_call(
        _merge_body,
        grid=(m // tm, D_MODEL // tn),
        in_specs=[pl.BlockSpec((tm, BRANCH_WIDTH), lambda i, j: (i, 0))] * N_BRANCH + gate_specs
        + [pl.BlockSpec((N_BRANCH, BRANCH_WIDTH, tn), lambda i, j: (0, 0, j))],
        out_specs=pl.BlockSpec((tm, tn), lambda i, j: (i, j)),
        out_shape=jax.ShapeDtypeStruct((m, D_MODEL), BF16),
        compiler_params=_cparams(("parallel", "arbitrary")),
        name="merge_branches",
    )(*o_branches, gates, gates, gates, gates, w_branch)


def _ple_body(x_ref, xres_ref, p_ref, g_ref, wg_ref, wp_ref, o_ref, a_scr, p_scr):
    @pl.when(pl.program_id(1) == 0)
    def _():
        _norm_to_scratch(x_ref, g_ref, a_scr)
        p_scr[...] = p_ref[...].astype(BF16)

    gate = jax.nn.sigmoid(_dot(a_scr[...], wg_ref[...]))
    o_ref[...] = xres_ref[...] + _dot(p_scr[...], wp_ref[...]) * gate


def ple_update(x, p, g, w_gate, w_proj, *, tm, tn):
    m, k = x.shape
    return pl.pallas_call(
        _ple_body,
        grid=(m // tm, D_MODEL // tn),
        in_specs=[pl.BlockSpec((tm, k), lambda i, j: (i, 0)),
                  pl.BlockSpec((tm, tn), lambda i, j: (i, j)),
                  pl.BlockSpec((tm, PLE_DIM), lambda i, j: (i, 0)),
                  pl.BlockSpec((1, k), lambda i, j: (0, 0)),
                  pl.BlockSpec((k, tn), lambda i, j: (0, j)),
                  pl.BlockSpec((PLE_DIM, tn), lambda i, j: (0, j))],
        out_specs=pl.BlockSpec((tm, tn), lambda i, j: (i, j)),
        out_shape=jax.ShapeDtypeStruct((m, D_MODEL), F32),
        scratch_shapes=[pltpu.VMEM((tm, k), BF16), pltpu.VMEM((tm, PLE_DIM), BF16)],
        compiler_params=_cparams(("parallel", "arbitrary")),
        name="ple_update",
    )(x, x, p, g.reshape(1, k), w_gate, w_proj)


def _final_norm_body(x_ref, g_ref, o_ref):
    def slab(n, carry):
        rows = pl.ds(pl.multiple_of(n * NORM_ROWS, NORM_ROWS), NORM_ROWS)
        o_ref[rows, :] = _rms_rows(x_ref[rows, :]) * g_ref[...]
        return carry

    lax.fori_loop(0, x_ref.shape[0] // NORM_ROWS, slab, 0)


def final_norm(x, g, *, tm):
    m, k = x.shape
    return pl.pallas_call(
        _final_norm_body,
        grid=(m // tm,),
        in_specs=[pl.BlockSpec((tm, k), lambda i: (i, 0)), pl.BlockSpec((1, k), lambda i: (0, 0))],
        out_specs=pl.BlockSpec((tm, k), lambda i: (i, 0)),
        out_shape=jax.ShapeDtypeStruct((m, k), F32),
        compiler_params=_cparams(("parallel",)),
        name="final_norm",
    )(x, g.reshape(1, k))


FFN_SUB = 256


def _cast_rows_to_bf16(src_ref, dst_ref):
    def slab(n, carry):
        rows = pl.ds(pl.multiple_of(n * NORM_ROWS, NORM_ROWS), NORM_ROWS)
        dst_ref[rows, :] = src_ref[rows, :].astype(BF16)
        return carry

    lax.fori_loop(0, src_ref.shape[0] // NORM_ROWS, slab, 0)


def _ffn_up_long_body(tiles_per_seq, a_ref, wg_ref, wu_ref, st_ref, cw_ref, cb_ref, o_ref, st_out_ref,
                      wg_s, wu_s, ext):
    i = pl.program_id(1)
    tm = a_ref.shape[0]

    @pl.when(i == 0)
    def _():
        _cast_rows_to_bf16(wg_ref, wg_s)
        _cast_rows_to_bf16(wu_ref, wu_s)

    @pl.when(i % tiles_per_seq == 0)
    def _():
        ext[6:8, :] = st_ref[0]

    for r in range(0, tm, FFN_SUB):
        a = a_ref[r:r + FFN_SUB, :]
        ext[8 + r:8 + r + FFN_SUB, :] = _dot(a, wg_s[...])
        conv = (ext[6 + r:6 + r + FFN_SUB, :] * cw_ref[0:1, :] + ext[7 + r:7 + r + FFN_SUB, :] * cw_ref[1:2, :]
                + ext[8 + r:8 + r + FFN_SUB, :] * cw_ref[2:3, :])
        o_ref[r:r + FFN_SUB, :] = (_gelu(conv + cb_ref[...]) * _dot(a, wu_s[...])).astype(o_ref.dtype)

    last = ext[6 + tm:8 + tm, :]
    ext[6:8, :] = last
    st_out_ref[0] = last


def _ffn_up_group_body(seg, a_ref, wg_ref, wu_ref, st_ref, cw_ref, cb_ref, o_ref, st_out_ref, wg_s, wu_s, stp, fg_s):
    tm = a_ref.shape[0]
    nseq = tm // seg
    hist = FFN_CONV - 1

    @pl.when(pl.program_id(1) == 0)
    def _():
        _cast_rows_to_bf16(wg_ref, wg_s)
        _cast_rows_to_bf16(wu_ref, wu_s)

    lane_blocks = stp.shape[0]
    stp[...] = jnp.zeros_like(stp)
    for j in range(hist):
        for c in range(lane_blocks):
            stp[c, pl.ds(j, nseq, stride=seg), :] = st_ref[:, j, c * 128:(c + 1) * 128]

    a = a_ref[...]
    fg = _dot(a, wg_s[...])
    for c in range(lane_blocks):
        fg_s[c] = fg[:, c * 128:(c + 1) * 128]
    h = jnp.concatenate([stp[c] for c in range(lane_blocks)], axis=1)
    tpos = lax.broadcasted_iota(jnp.int32, (tm, 1), 0) & (seg - 1)
    lag1 = jnp.where(tpos < 1, pltpu.roll(h, tm - 1, axis=0), pltpu.roll(fg, 1, axis=0))
    lag2 = jnp.where(tpos < 2, h, pltpu.roll(fg, 2, axis=0))
    conv = lag2 * cw_ref[0:1, :] + lag1 * cw_ref[1:2, :] + fg * cw_ref[2:3, :]
    o_ref[...] = (_gelu(conv + cb_ref[...]) * _dot(a, wu_s[...])).astype(o_ref.dtype)
    for j in range(hist):
        for c in range(lane_blocks):
            st_out_ref[:, j, c * 128:(c + 1) * 128] = fg_s[c, pl.ds(seg - hist + j, nseq, stride=seg), :]


def _rms_bf16_body(x_ref, g_ref, o_ref):
    _norm_to_scratch(x_ref, g_ref, o_ref)


def rms_bf16(x, g, *, tm):
    m, k = x.shape
    return pl.pallas_call(
        _rms_bf16_body,
        grid=(m // tm,),
        in_specs=[pl.BlockSpec((tm, k), lambda i: (i, 0)), pl.BlockSpec((1, k), lambda i: (0, 0))],
        out_specs=pl.BlockSpec((tm, k), lambda i: (i, 0)),
        out_shape=jax.ShapeDtypeStruct((m, k), BF16),
        compiler_params=_cparams(("parallel",)),
        name="rms_bf16",
    )(x, g.reshape(1, k))


def ffn_up_gate(x, g, w_up_layers, layer, state, conv_w, conv_b, *, seq_len, tm, tn):
    m, k = x.shape
    bsz = m // seq_len
    ncol = FFN_DIM // tn
    a = rms_bf16(x, g, tm=min(tm, 512))
    common_in = [pl.BlockSpec((tm, k), lambda j, i: (i, 0)),
                 pl.BlockSpec((None, k, tn), lambda j, i: (layer, 0, j)),
                 pl.BlockSpec((None, k, tn), lambda j, i: (layer, 0, ncol + j))]
    conv_in = [pl.BlockSpec((FFN_CONV, tn), lambda j, i: (0, j)), pl.BlockSpec((1, tn), lambda j, i: (0, j))]
    act_spec = pl.BlockSpec((tm, tn), lambda j, i: (i, j))
    act_shape = jax.ShapeDtypeStruct((m, FFN_DIM), BF16)
    w_scr = [pltpu.VMEM((k, tn), BF16)] * 2
    if seq_len >= tm:
        tps = seq_len // tm
        act, tile_last = pl.pallas_call(
            functools.partial(_ffn_up_long_body, tps),
            grid=(ncol, m // tm),
            in_specs=common_in + [pl.BlockSpec((1, FFN_CONV - 1, tn), lambda j, i: (i // tps, 0, j))] + conv_in,
            out_specs=[act_spec, pl.BlockSpec((1, FFN_CONV - 1, tn), lambda j, i: (i, 0, j))],
            out_shape=[act_shape, jax.ShapeDtypeStruct((m // tm, FFN_CONV - 1, FFN_DIM), F32)],
            scratch_shapes=w_scr + [pltpu.VMEM((tm + 8, tn), F32)],
            compiler_params=_cparams(("parallel", "arbitrary")),
            name="ffn_up_gate",
        )(a, w_up_layers, w_up_layers, state, conv_w, conv_b.reshape(1, FFN_DIM))
        return act, tile_last[tps - 1::tps]
    st_spec = pl.BlockSpec((tm // seq_len, FFN_CONV - 1, tn), lambda j, i: (i, 0, j))
    return pl.pallas_call(
        functools.partial(_ffn_up_group_body, seq_len),
        grid=(ncol, m // tm),
        in_specs=common_in + [st_spec] + conv_in,
        out_specs=[act_spec, st_spec],
        out_shape=[act_shape, jax.ShapeDtypeStruct((bsz, FFN_CONV - 1, FFN_DIM), F32)],
        scratch_shapes=w_scr + [pltpu.VMEM((tn // 128, tm, 128), F32)] * 2,
        compiler_params=_cparams(("parallel", "arbitrary")),
        name="ffn_up_gate_grouped",
    )(a, w_up_layers, w_up_layers, state, conv_w, conv_b.reshape(1, FFN_DIM))


def _mixer_a_body(u_ref, v_ref, lng_ref, lnb_ref, w_ref, bias_ref, o_ref, vout_ref):
    u = _gelu(u_ref[...])
    v = _gelu(v_ref[...])
    vc = v - jnp.mean(v, axis=-1, keepdims=True)
    var = jnp.mean(vc * vc, axis=-1, keepdims=True)
    vn = vc * lax.rsqrt(var + EPS) * lng_ref[...] + lnb_ref[...]
    vout_ref[...] = vn
    n = w_ref.shape[1]
    causal = lax.broadcasted_iota(jnp.int32, (n, n), 0) >= lax.broadcasted_iota(jnp.int32, (n, n), 1)
    gd = BRANCH_WIDTH // A_GROUPS
    for g in range(A_GROUPS):
        w = jnp.where(causal, w_ref[g], 0.0)
        mixed = _dot(w, vn[:, g * gd:(g + 1) * gd]) + bias_ref[:, g:g + 1]
        o_ref[:, g * gd:(g + 1) * gd] = (u[:, g * gd:(g + 1) * gd] * mixed).astype(o_ref.dtype)


def mixer_a(proj, ln_g, ln_b, w_mix, bias_rows):
    m = proj.shape[0]
    r = A_CHUNK
    bw = BRANCH_WIDTH
    return pl.pallas_call(
        _mixer_a_body,
        grid=(m // r,),
        in_specs=[pl.BlockSpec((r, bw), lambda i: (i, OFF_A_U // bw)),
                  pl.BlockSpec((r, bw), lambda i: (i, OFF_A_V // bw)),
                  pl.BlockSpec((1, bw), lambda i: (0, 0)),
                  pl.BlockSpec((1, bw), lambda i: (0, 0)),
                  pl.BlockSpec((A_GROUPS, r, r), lambda i: (0, 0, 0)),
                  pl.BlockSpec((r, A_GROUPS), lambda i: (0, 0))],
        out_specs=[pl.BlockSpec((r, bw), lambda i: (i, 0)), pl.BlockSpec((r, bw), lambda i: (i, 0))],
        out_shape=[jax.ShapeDtypeStruct((m, bw), BF16), jax.ShapeDtypeStruct((m, bw), F32)],
        compiler_params=_cparams(("parallel",)),
        name="mixer_a",
    )(proj, proj, ln_g.reshape(1, bw), ln_b.reshape(1, bw), w_mix, bias_rows)


def _gla_head_chunk(q, k, v, g, n_src):
    c = q.shape[0]
    g2 = _cumsum_rows(g) * LOG2E
    n_tiles = c // 8
    row8 = lax.broadcasted_iota(jnp.int32, (8, 1), 0)
    q_t = [q[i * 8:(i + 1) * 8] for i in range(n_tiles)]
    g_t = [g2[i * 8:(i + 1) * 8] for i in range(n_tiles)]
    o_t = [jnp.zeros((8, v.shape[1]), F32) for _ in range(n_tiles)]
    for s in range(n_src):
        for i in range(s // 8, n_tiles):
            a = jnp.sum(q_t[i] * jnp.exp2(g_t[i] - g2[s:s + 1, :]) * k[s:s + 1, :], axis=-1, keepdims=True)
            if i == s // 8:
                a = jnp.where(row8 >= s - 8 * i, a, 0.0)
            o_t[i] = o_t[i] + a * v[s:s + 1, :]
    o = jnp.concatenate(o_t, axis=0) if n_tiles > 1 else o_t[0]
    gl = g2[c - 1:c, :]
    return o, q * jnp.exp2(g2), k * jnp.exp2(gl - g2), jnp.exp2(gl)


def _head_block_diag(xs):
    z = jnp.zeros_like(xs[0])
    return jnp.concatenate(
        [jnp.concatenate([x if j == h else z for j in range(HEADS)], axis=1) for h, x in enumerate(xs)], axis=0)


def _gla_state_dots(qgs, kds, vs, es, st_ref, bi):
    st = st_ref[bi]
    o_inter = _dot_nt(_head_block_diag(qgs), st)
    st_ref[bi] = st * jnp.concatenate(es, axis=1) + _dot_tn(jnp.concatenate(vs, axis=0), _head_block_diag(kds))
    return o_inter


def _gla_state_step(parts, vs, st_ref, bi):
    c = vs[0].shape[0]
    o_inter = _gla_state_dots([p[1] for p in parts], [p[2] for p in parts], vs, [p[3] for p in parts], st_ref, bi)
    return [p[0] + o_inter[h * c:(h + 1) * c] for h, p in enumerate(parts)]


def _gla_pair_tile(q, k, v, g, seg):
    g2 = _segment_cumsum(g, seg) * LOG2E
    tpos = lax.broadcasted_iota(jnp.int32, (8, 1), 0) & (seg - 1)
    o = jnp.sum(q * k, axis=-1, keepdims=True) * v
    for d in range(1, seg):
        a = jnp.sum(q * jnp.exp2(g2 - pltpu.roll(g2, d, axis=0)) * pltpu.roll(k, d, axis=0), axis=-1, keepdims=True)
        o = o + jnp.where(tpos >= d, a, 0.0) * pltpu.roll(v, d, axis=0)
    first = lax.broadcasted_iota(jnp.int32, (8, 1), 0) < seg
    gl_a, gl_b = g2[seg - 1:seg, :], g2[2 * seg - 1:2 * seg, :]
    kd = k * jnp.exp2(jnp.where(first, gl_a, gl_b) - g2)
    return o, q * jnp.exp2(g2), kd, jnp.exp2(gl_a), jnp.exp2(gl_b)


def _gla_pair_step(tiles, vs, st_ref, p, seg):
    first = lax.broadcasted_iota(jnp.int32, (8, 1), 0) < seg
    qgs = [t[1] for t in tiles]
    oi_a = _gla_state_dots(qgs, [jnp.where(first, t[2], 0.0) for t in tiles], vs, [t[3] for t in tiles],
                           st_ref, 2 * p)
    oi_b = _gla_state_dots(qgs, [jnp.where(first, 0.0, t[2]) for t in tiles], vs, [t[4] for t in tiles],
                           st_ref, 2 * p + 1)
    return [t[0] + jnp.where(first, oi_a[h * 8:(h + 1) * 8], oi_b[h * 8:(h + 1) * 8]) for h, t in enumerate(tiles)]


def _load_state_t(s0_ref, st_ref, kdim):
    def per_seq(bi, carry):
        for h in range(HEADS):
            s = s0_ref[bi, h]
            if kdim < HEAD_DIM:
                s = jnp.concatenate([s, jnp.zeros((HEAD_DIM - kdim, HEAD_DIM), F32)], axis=0)
            st_ref[bi, :, h * HEAD_DIM:(h + 1) * HEAD_DIM] = s.T
        return carry

    lax.fori_loop(0, s0_ref.shape[0], per_seq, 0)


def _store_state_t(st_ref, sout_ref, kdim):
    def per_seq(bi, carry):
        for h in range(HEADS):
            sout_ref[bi, h] = st_ref[bi, :, h * HEAD_DIM:(h + 1) * HEAD_DIM].T[:kdim, :]
        return carry

    lax.fori_loop(0, sout_ref.shape[0], per_seq, 0)


def _seq_chunk_loop(nb, n_chunks, unroll, fn):
    def step(it, carry):
        n = 0 if n_chunks == 1 else it % n_chunks
        bj = 0 if nb == unroll else it // n_chunks
        for u in range(unroll):
            fn(bj * unroll + u, n)
        return carry

    lax.fori_loop(0, (nb // unroll) * n_chunks, step, 0, unroll=4)


def _chunk_rows(n, c):
    if isinstance(n, int):
        return n * c, pl.ds(n * c, c)
    r0 = pl.multiple_of(n * c, c)
    return r0, pl.ds(r0, c)


def _valid_rows(t_valid, tb, r0, c):
    return pl.program_id(1) * tb + r0 + lax.broadcasted_iota(jnp.int32, (c, 1), 0) < t_valid


def _mixer_b_body(t_valid, t_pad, unroll, q_ref, f_ref, i_ref, og_ref, lb_ref, ng_ref, s0_ref, o_ref, sout_ref,
                  st_ref):
    nb, tb = q_ref.shape[0], q_ref.shape[1]
    c = min(GLA_CHUNK, tb)
    kd = HEAD_DIM
    padded = t_valid < t_pad
    n_src = t_valid if padded else c

    @pl.when(pl.program_id(1) == 0)
    def _():
        _load_state_t(s0_ref, st_ref, kd)

    def chunk(bi, n):
        r0, rows = _chunk_rows(n, c)
        parts, vs = [], []
        for h in range(HEADS):
            cols = slice(h * kd, (h + 1) * kd)
            lb = lb_ref[:, cols]
            fp = f_ref[bi, rows, cols]
            f = lb + (1.0 - lb) * jax.nn.sigmoid(fp)
            g = jnp.log(jnp.maximum(f, F_TINY))
            k = (1.0 - lb) * jax.nn.sigmoid(-fp)
            if padded:
                ok = _valid_rows(t_valid, tb, r0, c)
                g = jnp.where(ok, g, 0.0)
                k = jnp.where(ok, k, 0.0)
            vs.append(i_ref[bi, rows, cols])
            parts.append(_gla_head_chunk(q_ref[bi, rows, cols], k, vs[h], g, n_src))
        for h, o in enumerate(_gla_state_step(parts, vs, st_ref, bi)):
            cols = slice(h * kd, (h + 1) * kd)
            o = _rms_rows(o) * ng_ref[...] * jax.nn.sigmoid(og_ref[bi, rows, cols])
            o_ref[bi, rows, cols] = o.astype(o_ref.dtype)

    _seq_chunk_loop(nb, tb // c, unroll, chunk)

    @pl.when(pl.program_id(1) == pl.num_programs(1) - 1)
    def _():
        _store_state_t(st_ref, sout_ref, kd)


def mixer_b(pseq, lb, norm_g, s0, *, t_valid, tb, nb, unroll):
    bsz, t_pad, _ = pseq.shape
    assert t_valid == t_pad or t_pad == tb <= GLA_CHUNK
    bw = BRANCH_WIDTH
    field = lambda off: pl.BlockSpec((nb, tb, bw), lambda b, s: (b, s, off // bw))
    state = pl.BlockSpec((nb, HEADS, HEAD_DIM, HEAD_DIM), lambda b, s: (b, 0, 0, 0))
    return pl.pallas_call(
        functools.partial(_mixer_b_body, t_valid, t_pad, unroll),
        grid=(bsz // nb, t_pad // tb),
        in_specs=[field(OFF_B_Q), field(OFF_B_F), field(OFF_B_I), field(OFF_B_G),
                  pl.BlockSpec((1, bw), lambda b, s: (0, 0)),
                  pl.BlockSpec((1, HEAD_DIM), lambda b, s: (0, 0)),
                  state],
        out_specs=[pl.BlockSpec((nb, tb, bw), lambda b, s: (b, s, 0)), state],
        out_shape=[jax.ShapeDtypeStruct((bsz, t_pad, bw), BF16),
                   jax.ShapeDtypeStruct((bsz, HEADS, HEAD_DIM, HEAD_DIM), F32)],
        scratch_shapes=[pltpu.VMEM((nb, HEAD_DIM, HEADS * HEAD_DIM), F32)],
        compiler_params=_cparams(("parallel", "arbitrary")),
        name="mixer_b",
    )(pseq, pseq, pseq, pseq, lb.reshape(1, bw), norm_g.reshape(1, HEAD_DIM), s0)


def _mixer_c_body(t_valid, t_pad, unroll, q_ref, k_ref, v_ref, r_ref, lr_ref, w2_ref, ba_ref, ng_ref, s0_ref,
                  o_ref, sout_ref, st_ref):
    nb, tb = q_ref.shape[0], q_ref.shape[1]
    c = min(GLA_CHUNK, tb)
    kd = C_KDIM
    padded = t_valid < t_pad
    n_src = t_valid if padded else c

    @pl.when(pl.program_id(1) == 0)
    def _():
        _load_state_t(s0_ref, st_ref, kd)

    zpad = jnp.zeros((c, HEAD_DIM - kd), F32)

    def chunk(bi, n):
        r0, rows = _chunk_rows(n, c)
        gate_in = _dot(lr_ref[bi, rows, 0:C_RANK], w2_ref[...]) + ba_ref[...]
        g_all = _log_sigmoid(gate_in) / C_TAU
        if padded:
            ok = _valid_rows(t_valid, tb, r0, c)
            g_all = jnp.where(ok, g_all, 0.0)
        parts, vs = [], []
        for h in range(HEADS):
            kcols = slice(h * kd, (h + 1) * kd)
            q = jnp.concatenate([q_ref[bi, rows, kcols] * (kd ** -0.5), zpad], axis=1)
            k = k_ref[bi, rows, kcols]
            if padded:
                k = jnp.where(ok, k, 0.0)
            k = jnp.concatenate([k, zpad], axis=1)
            g = jnp.concatenate([g_all[:, kcols], zpad], axis=1)
            vs.append(v_ref[bi, rows, h * HEAD_DIM:(h + 1) * HEAD_DIM])
            parts.append(_gla_head_chunk(q, k, vs[h], g, n_src))
        for h, o in enumerate(_gla_state_step(parts, vs, st_ref, bi)):
            vcols = slice(h * HEAD_DIM, (h + 1) * HEAD_DIM)
            o = _rms_rows(o) * ng_ref[...] * _silu(r_ref[bi, rows, vcols])
            o_ref[bi, rows, vcols] = o.astype(o_ref.dtype)

    _seq_chunk_loop(nb, tb // c, unroll, chunk)

    @pl.when(pl.program_id(1) == pl.num_programs(1) - 1)
    def _():
        _store_state_t(st_ref, sout_ref, kd)


def mixer_c(pseq, w_a2, b_a, norm_g, s0, *, t_valid, tb, nb, unroll):
    bsz, t_pad, _ = pseq.shape
    assert t_valid == t_pad or t_pad == tb <= GLA_CHUNK
    bw = BRANCH_WIDTH
    kw = HEADS * C_KDIM
    field = lambda off, w: pl.BlockSpec((nb, tb, w), lambda b, s: (b, s, off // w))
    state = pl.BlockSpec((nb, HEADS, C_KDIM, HEAD_DIM), lambda b, s: (b, 0, 0, 0))
    return pl.pallas_call(
        functools.partial(_mixer_c_body, t_valid, t_pad, unroll),
        grid=(bsz // nb, t_pad // tb),
        in_specs=[field(OFF_C_Q, kw), field(OFF_C_K, kw), field(OFF_C_V, bw), field(OFF_C_R, bw),
                  field(OFF_C_LR, 128),
                  pl.BlockSpec((C_RANK, kw), lambda b, s: (0, 0)),
                  pl.BlockSpec((1, kw), lambda b, s: (0, 0)),
                  pl.BlockSpec((1, HEAD_DIM), lambda b, s: (0, 0)),
                  state],
        out_specs=[pl.BlockSpec((nb, tb, bw), lambda b, s: (b, s, 0)), state],
        out_shape=[jax.ShapeDtypeStruct((bsz, t_pad, bw), BF16),
                   jax.ShapeDtypeStruct((bsz, HEADS, C_KDIM, HEAD_DIM), F32)],
        scratch_shapes=[pltpu.VMEM((nb, HEAD_DIM, HEADS * HEAD_DIM), F32)],
        compiler_params=_cparams(("parallel", "arbitrary")),
        name="mixer_c",
    )(pseq, pseq, pseq, pseq, pseq, w_a2, b_a.reshape(1, kw), norm_g.reshape(1, HEAD_DIM), s0)


def _grouped_pairs_loop(n_rows, seg, st_ref, head_inputs, head_outputs):
    def step(it, carry):
        rows = pl.ds(pl.multiple_of(it * 16, 16), 16)
        ins = [head_inputs(rows, h) for h in range(HEADS)]
        halves = []
        for half in range(2):
            sl = slice(half * 8, (half + 1) * 8)
            vs = [x[2][sl] for x in ins]
            tiles = [_gla_pair_tile(x[0][sl], x[1][sl], x[2][sl], x[3][sl], seg) for x in ins]
            halves.append(_gla_pair_step(tiles, vs, st_ref, 2 * it + half, seg))
        for h in range(HEADS):
            head_outputs(rows, h, jnp.concatenate([halves[0][h], halves[1][h]], axis=0))
        return carry

    lax.fori_loop(0, n_rows // 16, step, 0)


def _mixer_b_group_body(seg, q_ref, f_ref, i_ref, og_ref, lb_ref, ng_ref, s0_ref, o_ref, sout_ref, st_ref):
    kd = HEAD_DIM
    _load_state_t(s0_ref, st_ref, kd)

    def head_inputs(rows, h):
        cols = slice(h * kd, (h + 1) * kd)
        lb = lb_ref[:, cols]
        fp = f_ref[0, rows, cols]
        f = lb + (1.0 - lb) * jax.nn.sigmoid(fp)
        return (q_ref[0, rows, cols], (1.0 - lb) * jax.nn.sigmoid(-fp), i_ref[0, rows, cols],
                jnp.log(jnp.maximum(f, F_TINY)))

    def head_outputs(rows, h, o):
        cols = slice(h * kd, (h + 1) * kd)
        o = _rms_rows(o) * ng_ref[...] * jax.nn.sigmoid(og_ref[0, rows, cols])
        o_ref[0, rows, cols] = o.astype(o_ref.dtype)

    _grouped_pairs_loop(q_ref.shape[1], seg, st_ref, head_inputs, head_outputs)
    _store_state_t(st_ref, sout_ref, kd)


def _mixer_c_group_body(seg, q_ref, k_ref, v_ref, r_ref, lr_ref, w2_ref, ba_ref, ng_ref, s0_ref, o_ref, sout_ref,
                        st_ref, g_scr):
    kd = C_KDIM
    _load_state_t(s0_ref, st_ref, kd)
    g_scr[...] = _log_sigmoid(_dot(lr_ref[0, :, 0:C_RANK], w2_ref[...]) + ba_ref[...]) / C_TAU
    zpad = jnp.zeros((16, HEAD_DIM - kd), F32)

    def head_inputs(rows, h):
        kcols = slice(h * kd, (h + 1) * kd)
        pad = lambda x: jnp.concatenate([x, zpad], axis=1)
        return (pad(q_ref[0, rows, kcols] * (kd ** -0.5)), pad(k_ref[0, rows, kcols]),
                v_ref[0, rows, h * HEAD_DIM:(h + 1) * HEAD_DIM], pad(g_scr[rows, kcols]))

    def head_outputs(rows, h, o):
        vcols = slice(h * HEAD_DIM, (h + 1) * HEAD_DIM)
        o = _rms_rows(o) * ng_ref[...] * _silu(r_ref[0, rows, vcols])
        o_ref[0, rows, vcols] = o.astype(o_ref.dtype)

    _grouped_pairs_loop(q_ref.shape[1], seg, st_ref, head_inputs, head_outputs)
    _store_state_t(st_ref, sout_ref, kd)


def _layer_stacked_state(body, prev_states, state_ref_pos, seqs_per_step):
    n_prev = 0 if prev_states is None else prev_states.shape[0]

    def wrapped(*refs):
        refs = list(refs)
        if n_prev:
            prev_ref = refs.pop(0)
            refs[state_ref_pos][0:n_prev] = prev_ref[...]
        refs[state_ref_pos] = refs[state_ref_pos].at[n_prev]
        return body(*refs)

    def spec(shape_tail, layers):
        return pl.BlockSpec((layers, seqs_per_step) + shape_tail, lambda s: (0, s) + (0,) * len(shape_tail))

    def out_shape(s0):
        return jax.ShapeDtypeStruct((n_prev + 1,) + s0.shape[1:], F32)

    prev_specs = lambda s0: [spec(s0.shape[2:], n_prev)] if n_prev else []
    prev_inputs = [prev_states] if n_prev else []
    return wrapped, prev_specs, prev_inputs, lambda s0: spec(s0.shape[2:], n_prev + 1), out_shape


def _grouped_gla_call(body, prows, fields, params, s0, prev_states, kdim, *, seg, tb, extra_scratch=()):
    m = prows.shape[0]
    bw = BRANCH_WIDTH
    p3 = prows.reshape(1, m, -1)
    layer = 0 if prev_states is None else prev_states.shape[0]
    state = pl.BlockSpec((None, tb // seg, HEADS, kdim, HEAD_DIM), lambda s: (layer, s, 0, 0, 0))
    n_in = len(fields) + len(params) + 1
    wrapped, prev_specs, prev_inputs, state_out_spec, state_out_shape = _layer_stacked_state(
        functools.partial(body, seg), prev_states, n_in + 1, tb // seg)
    o, s_out = pl.pallas_call(
        wrapped,
        grid=(m // tb,),
        in_specs=prev_specs(s0)
        + [pl.BlockSpec((1, tb, w), functools.partial(lambda s, c: (0, s, c), c=off // w)) for off, w in fields]
        + [pl.BlockSpec(x.shape, functools.partial(lambda s, n: (0,) * n, n=x.ndim)) for x in params] + [state],
        out_specs=[pl.BlockSpec((1, tb, bw), lambda s: (0, s, 0)), state_out_spec(s0)],
        out_shape=[jax.ShapeDtypeStruct((1, m, bw), BF16), state_out_shape(s0)],
        scratch_shapes=[pltpu.VMEM((tb // seg, HEAD_DIM, HEADS * HEAD_DIM), F32), *extra_scratch],
        compiler_params=_cparams(("parallel",)),
        name=body.__name__.strip("_"),
    )(*prev_inputs, *([p3] * len(fields)), *params, s0)
    return o.reshape(m, bw), s_out


def mixer_b_grouped(prows, lb, norm_g, s0, prev_states, *, seg, tb):
    bw = BRANCH_WIDTH
    return _grouped_gla_call(
        _mixer_b_group_body, prows, [(OFF_B_Q, bw), (OFF_B_F, bw), (OFF_B_I, bw), (OFF_B_G, bw)],
        [lb.reshape(1, bw), norm_g.reshape(1, HEAD_DIM)], s0, prev_states, HEAD_DIM, seg=seg, tb=tb)


def mixer_c_grouped(prows, w_a2, b_a, norm_g, s0, prev_states, *, seg, tb):
    bw = BRANCH_WIDTH
    kw = HEADS * C_KDIM
    return _grouped_gla_call(
        _mixer_c_group_body, prows, [(OFF_C_Q, kw), (OFF_C_K, kw), (OFF_C_V, bw), (OFF_C_R, bw), (OFF_C_LR, 128)],
        [w_a2, b_a.reshape(1, kw), norm_g.reshape(1, HEAD_DIM)], s0, prev_states, C_KDIM, seg=seg, tb=tb,
        extra_scratch=(pltpu.VMEM((tb, kw), F32),))


def _unit_lower_inverse(a, order):
    c = a.shape[0]
    eye = (lax.broadcasted_iota(jnp.int32, (c, c), 0) == lax.broadcasted_iota(jnp.int32, (c, c), 1)).astype(F32)
    p = eye - a
    pw = a
    n = 2
    while n < order:
        pw = _dot(pw, pw)
        p = p + _dot(p, pw)
        n *= 2
    return p


D_ROWS = 64


def _stack_heads(x):
    return jnp.concatenate([x[:, h * HEAD_DIM:(h + 1) * HEAD_DIM] for h in range(HEADS)], axis=0)


def _stack_head_lanes(x):
    return jnp.concatenate([x[:, h:h + 1] for h in range(HEADS)], axis=0)


def _lane_pad_heads(x):
    return jnp.pad(x.reshape(1, HEADS), ((0, 0), (0, 128 - HEADS)))


def _segment_cumsum(x, seg):
    tpos = lax.broadcasted_iota(jnp.int32, x.shape, 0) & (seg - 1)
    sh = 1
    while sh < seg:
        x = x + jnp.where(tpos >= sh, pltpu.roll(x, sh, axis=0), 0.0)
        sh *= 2
    return x


def _delta_chunk_operands(seg, qkv, beta_all, g_all):
    hd = HEAD_DIM
    bw = BRANCH_WIDTH
    r = HEADS * qkv.shape[0]
    q = _stack_heads(qkv[:, 0:bw])
    k = _stack_heads(qkv[:, bw:2 * bw])
    v = _stack_heads(qkv[:, 2 * bw:3 * bw])
    q = q * lax.rsqrt(jnp.sum(q * q, axis=-1, keepdims=True) + EPS) * (hd ** -0.5)
    k = k * lax.rsqrt(jnp.sum(k * k, axis=-1, keepdims=True) + EPS)
    beta = _stack_head_lanes(beta_all)
    gc = _stack_head_lanes(_segment_cumsum(g_all, seg))
    ri = lax.broadcasted_iota(jnp.int32, (r, r), 0)
    ci = lax.broadcasted_iota(jnp.int32, (r, r), 1)
    shift = seg.bit_length() - 1
    same = (ri >> shift) == (ci >> shift)
    gr = jnp.sum(jnp.where(ri == ci, gc, 0.0), axis=0, keepdims=True)
    decay = jnp.where(same, jnp.where(ri >= ci, jnp.exp(jnp.minimum(gc - gr, 0.0)), 0.0), 0.0)
    a_mat = jnp.where(ri > ci, beta * decay * _dot_nt(k, k), 0.0)
    t_inv = _unit_lower_inverse(a_mat, seg)
    eg = jnp.exp(gc)
    sol = _dot(t_inv, jnp.concatenate([(beta * eg) * k, beta * v], axis=1))
    qk = _dot_nt(q, k) * decay
    is_last = (ci & (seg - 1)) == seg - 1
    gl = jnp.sum(jnp.where(same, jnp.where(is_last, gr, 0.0), 0.0), axis=1, keepdims=True)
    return sol[:, :hd], sol[:, hd:], qk, q * eg, k * jnp.exp(gl - gc), jnp.exp(gl)


def _mixer_d_body(x_ref, beta_ref, dec_ref, z_ref, cst_ref, s0_ref, cw_ref, alog_ref, dtb_ref, ng_ref,
                  o_ref, cst_out_ref, sout_ref, ext, conv, st_ref):
    nb, tb = x_ref.shape[0], x_ref.shape[1]
    c = D_ROWS
    hd = HEAD_DIM
    t = pl.program_id(1)
    nt = pl.num_programs(1)

    @pl.when(t == 0)
    def _():
        ext[:, 5:8, :] = cst_ref[...]
        st_ref[...] = s0_ref[...].reshape(nb * HEADS, hd, hd)

    for bi in range(nb):
        ext[bi, 8:8 + tb, :] = x_ref[bi]
        y = ext[bi, 5:5 + tb, :] * cw_ref[0:1, :]
        for j in range(1, D_CONV):
            y = y + ext[bi, 5 + j:5 + j + tb, :] * cw_ref[j:j + 1, :]
        conv[bi] = _silu(y)

    @pl.when(t == nt - 1)
    def _():
        cst_out_ref[...] = ext[:, 8 + tb - (D_CONV - 1):8 + tb, :]

    ext[:, 5:8, :] = ext[:, 5 + tb:8 + tb, :]

    a_neg = -jnp.exp(alog_ref[...])

    def chunk(n, carry):
        rows = pl.ds(pl.multiple_of(n * c, c), c)
        for bi in range(nb):
            beta_all = jax.nn.sigmoid(beta_ref[bi, rows, :])
            g_all = a_neg * _softplus(dec_ref[bi, rows, :] + dtb_ref[...])
            w, u, qk, qg, kdec, egl = _delta_chunk_operands(c, conv[bi, rows, :], beta_all, g_all)
            deltas, oqs = [], []
            for h in range(HEADS):
                hs = slice(h * c, (h + 1) * c)
                st = st_ref[bi * HEADS + h]
                x = _dot(jnp.concatenate([w[hs], qg[hs]], axis=0), st)
                delta = u[hs] - x[:c]
                deltas.append(delta)
                oqs.append(x[c:])
                st_ref[bi * HEADS + h] = egl[(h + 1) * c - 1:(h + 1) * c, :] * st + _dot_tn(kdec[hs], delta)
            o = jnp.concatenate(oqs, axis=0) + _dot(qk, jnp.concatenate(deltas, axis=0))
            o = _rms_rows(o) * ng_ref[...] * _silu(_stack_heads(z_ref[bi, rows, :]))
            for h in range(HEADS):
                o_ref[bi, rows, h * hd:(h + 1) * hd] = o[h * c:(h + 1) * c].astype(o_ref.dtype)
        return carry

    lax.fori_loop(0, tb // c, chunk, 0, unroll=2)

    @pl.when(t == nt - 1)
    def _():
        sout_ref[...] = st_ref[...].reshape(nb, HEADS, hd, hd)


def _mixer_d_group_body(seg, x_ref, cst_ref, beta_ref, dec_ref, z_ref, s0_ref, cw_ref, alog_ref, dtb_ref, ng_ref,
                        o_ref, cst_out_ref, sout_ref, conv, w_s, u_s, qg_s, kd_s, egl_s, delta_s, oq_s, stp_s, x_s):
    tb = x_ref.shape[1]
    nseq = tb // seg
    c = D_ROWS
    hd = HEAD_DIM
    pairs = c // 8
    hist_rows = D_CONV - 1

    lane_blocks = stp_s.shape[0]
    x = x_ref[0]
    stp_s[...] = jnp.zeros_like(stp_s)
    for cb in range(lane_blocks):
        lanes = slice(cb * 128, (cb + 1) * 128)
        x_s[cb] = x[:, lanes]
        for j in range(hist_rows):
            stp_s[cb, pl.ds(j, nseq, stride=seg), :] = cst_ref[:, j, lanes]
    stp = jnp.concatenate([stp_s[cb] for cb in range(lane_blocks)], axis=1)

    tpos = lax.broadcasted_iota(jnp.int32, (tb, 1), 0) & (seg - 1)
    y = x * cw_ref[3:4, :]
    for d in range(1, D_CONV):
        hist = stp if d == 3 else pltpu.roll(stp, tb - (3 - d), axis=0)
        y = y + jnp.where(tpos < d, hist, pltpu.roll(x, d, axis=0)) * cw_ref[3 - d:4 - d, :]
    conv[...] = _silu(y)
    for cb in range(lane_blocks):
        for j in range(hist_rows):
            cst_out_ref[:, j, cb * 128:(cb + 1) * 128] = x_s[cb, pl.ds(seg - hist_rows + j, nseq, stride=seg), :]

    a_neg = -jnp.exp(alog_ref[...])
    low = lax.broadcasted_iota(jnp.int32, (8, 1), 0) < seg

    for n in range(tb // c):
        rows = slice(n * c, (n + 1) * c)
        beta_all = jax.nn.sigmoid(beta_ref[0, rows, :])
        g_all = a_neg * _softplus(dec_ref[0, rows, :] + dtb_ref[...])
        w, u, qk, qg, kdec, egl = _delta_chunk_operands(seg, conv[rows, :], beta_all, g_all)
        w_s[...] = w
        u_s[...] = u
        qg_s[...] = qg
        kd_s[...] = kdec
        egl_s[...] = jnp.broadcast_to(egl, (HEADS * c, hd))

        def pair(p, carry):
            rp = pl.ds(pl.multiple_of(p * 8, 8), 8)
            h = p // pairs
            ja = n * (c // seg) + (p % pairs) * 2
            sa = s0_ref[ja, h]
            sb = s0_ref[ja + 1, h]
            lhs = jnp.concatenate([w_s[rp, :], qg_s[rp, :]], axis=0)
            xa = _dot(lhs, sa)
            xb = _dot(lhs, sb)
            delta = u_s[rp, :] - jnp.where(low, xa[:8], xb[:8])
            delta_s[rp, :] = delta
            oq_s[rp, :] = jnp.where(low, xa[8:], xb[8:])
            kd = kd_s[rp, :]
            e = egl_s[rp, :]
            sout_ref[ja, h] = e[seg - 1:seg, :] * sa + _dot_tn(jnp.where(low, kd, 0.0), delta)
            sout_ref[ja + 1, h] = e[2 * seg - 1:2 * seg, :] * sb + _dot_tn(jnp.where(low, 0.0, kd), delta)
            return carry

        lax.fori_loop(0, HEADS * pairs, pair, 0)
        o = oq_s[...] + _dot(qk, delta_s[...])
        o = _rms_rows(o) * ng_ref[...] * _silu(_stack_heads(z_ref[0, rows, :]))
        for h in range(HEADS):
            o_ref[0, rows, h * hd:(h + 1) * hd] = o[h * c:(h + 1) * c].astype(o_ref.dtype)


def mixer_d(pseq, conv_state, s0, conv_w, a_log, dt_bias, norm_g, *, tb, nb):
    bsz, t, _ = pseq.shape
    bw = BRANCH_WIDTH
    field = lambda off, w: pl.BlockSpec((nb, tb, w), lambda b, s: (b, s, off // w))
    state = pl.BlockSpec((nb, HEADS, HEAD_DIM, HEAD_DIM), lambda b, s: (b, 0, 0, 0))
    cstate = pl.BlockSpec((nb, D_CONV - 1, D_QKV), lambda b, s: (b, 0, 0))
    return pl.pallas_call(
        _mixer_d_body,
        grid=(bsz // nb, t // tb),
        in_specs=[field(OFF_D_QKV, D_QKV), field(OFF_D_BETA, 128), field(OFF_D_DECAY, 128), field(OFF_D_Z, bw),
                  cstate, state,
                  pl.BlockSpec((D_CONV, D_QKV), lambda b, s: (0, 0)),
                  pl.BlockSpec((1, 128), lambda b, s: (0, 0)),
                  pl.BlockSpec((1, 128), lambda b, s: (0, 0)),
                  pl.BlockSpec((1, HEAD_DIM), lambda b, s: (0, 0))],
        out_specs=[pl.BlockSpec((nb, tb, bw), lambda b, s: (b, s, 0)), cstate, state],
        out_shape=[jax.ShapeDtypeStruct((bsz, t, bw), BF16),
                   jax.ShapeDtypeStruct((bsz, D_CONV - 1, D_QKV), F32),
                   jax.ShapeDtypeStruct((bsz, HEADS, HEAD_DIM, HEAD_DIM), F32)],
        scratch_shapes=[pltpu.VMEM((nb, tb + 8, D_QKV), F32), pltpu.VMEM((nb, tb, D_QKV), F32),
                        pltpu.VMEM((nb * HEADS, HEAD_DIM, HEAD_DIM), F32)],
        compiler_params=_cparams(("parallel", "arbitrary")),
        name="mixer_d",
    )(pseq, pseq, pseq, pseq, conv_state, s0, conv_w, _lane_pad_heads(a_log), _lane_pad_heads(dt_bias),
      norm_g.reshape(1, HEAD_DIM))


def mixer_d_grouped(prows, conv_state, s0, conv_w, a_log, dt_bias, norm_g, prev_states, *, seg, tb):
    m = prows.shape[0]
    bsz = m // seg
    bw = BRANCH_WIDTH
    field = lambda off, w: pl.BlockSpec((1, tb, w), lambda s: (0, s, off // w))
    layer = 0 if prev_states is None else prev_states.shape[0]
    state = pl.BlockSpec((None, tb // seg, HEADS, HEAD_DIM, HEAD_DIM), lambda s: (layer, s, 0, 0, 0))
    cstate = pl.BlockSpec((tb // seg, D_CONV - 1, D_QKV), lambda s: (s, 0, 0))
    rows128 = pltpu.VMEM((HEADS * D_ROWS, HEAD_DIM), F32)
    lane_rows = pltpu.VMEM((D_QKV // 128, tb, 128), F32)
    n_in, state_out_pos = 10, 2
    wrapped, prev_specs, prev_inputs, state_out_spec, state_out_shape = _layer_stacked_state(
        functools.partial(_mixer_d_group_body, seg), prev_states, n_in + state_out_pos, tb // seg)
    o, cst, s_out = pl.pallas_call(
        wrapped,
        grid=(m // tb,),
        in_specs=prev_specs(s0) + [
                  field(OFF_D_QKV, D_QKV), cstate,
                  field(OFF_D_BETA, 128), field(OFF_D_DECAY, 128), field(OFF_D_Z, bw),
                  state,
                  pl.BlockSpec((D_CONV, D_QKV), lambda s: (0, 0)),
                  pl.BlockSpec((1, 128), lambda s: (0, 0)),
                  pl.BlockSpec((1, 128), lambda s: (0, 0)),
                  pl.BlockSpec((1, HEAD_DIM), lambda s: (0, 0))],
        out_specs=[pl.BlockSpec((1, tb, bw), lambda s: (0, s, 0)), cstate, state_out_spec(s0)],
        out_shape=[jax.ShapeDtypeStruct((1, m, bw), BF16),
                   jax.ShapeDtypeStruct((bsz, D_CONV - 1, D_QKV), F32),
                   state_out_shape(s0)],
        scratch_shapes=[pltpu.VMEM((tb, D_QKV), F32)] + [rows128] * 7 + [lane_rows] * 2,
        compiler_params=_cparams(("parallel",)),
        name="mixer_d_grouped",
    )(*prev_inputs, prows.reshape(1, m, -1), conv_state, prows.reshape(1, m, -1), prows.reshape(1, m, -1),
      prows.reshape(1, m, -1), s0, conv_w, _lane_pad_heads(a_log), _lane_pad_heads(dt_bias),
      norm_g.reshape(1, HEAD_DIM))
    return o.reshape(m, bw), cst, s_out


PACK_TILE = 512
PACK_SRC = ([4624, 5136, 5648, 0, 512, 1024, 1536, 2048, 2560, 3584, 4112, 6168, 3072, None]
            + [6680 + PACK_TILE * k for k in range(N_BRANCH * D_MODEL // PACK_TILE)])
PACK_SMALL_TILE = PACK_SRC.index(None)


def _pack_lookup(t, values):
    out = jnp.int32(0)
    for i, v in enumerate(values):
        out = jnp.where(t == i, jnp.int32(v), out)
    return out


def _pack_w_in_body(a_ref, b_ref, tail_ref, small_ref, o_ref):
    t = pl.program_id(1)
    last = len(PACK_SRC) - 1
    shifts = [0 if s is None else s % 128 for s in PACK_SRC]

    def emit(shift, right_ref):
        def slab(n, carry):
            rows = pl.ds(pl.multiple_of(n * NORM_ROWS, NORM_ROWS), NORM_ROWS)
            w = jnp.concatenate([a_ref[rows, :], right_ref[rows, :]], axis=1)
            o_ref[rows, :] = w[:, shift:shift + PACK_TILE].astype(BF16)
            return carry

        lax.fori_loop(0, a_ref.shape[0] // NORM_ROWS, slab, 0)

    for shift in sorted(set(shifts)):
        tiles = [i for i, s in enumerate(shifts) if s == shift and i not in (PACK_SMALL_TILE, last)]
        pl.when(functools.reduce(jnp.logical_or, [t == i for i in tiles]))(functools.partial(emit, shift, b_ref))
    pl.when(t == last)(functools.partial(emit, shifts[last], tail_ref))

    @pl.when(t == PACK_SMALL_TILE)
    def _():
        o_ref[...] = small_ref[...]


def pack_w_in(w_in):
    layers, k, n_src = w_in.shape
    bases = [0 if s is None else s - s % 128 for s in PACK_SRC]
    assert all(b % PACK_TILE == 0 for b in bases)
    right = [(b + PACK_TILE) // 128 for b in bases]
    full_blocks = n_src // 128
    assert all(r < full_blocks for r in right[:-1]) and right[-1] == full_blocks
    right[-1] = 0
    z = lambda n: jnp.zeros((layers, k, n), w_in.dtype)
    tail = jnp.concatenate([w_in[:, :, full_blocks * 128:], z(128 - n_src % 128)], axis=2)
    small = jnp.concatenate([w_in[:, :, 4096:4112], z(112),
                             w_in[:, :, 6160:6164], z(124),
                             w_in[:, :, 6164:6168], z(252)], axis=2).astype(BF16)
    return pl.pallas_call(
        _pack_w_in_body,
        grid=(layers, len(PACK_SRC)),
        in_specs=[pl.BlockSpec((None, k, PACK_TILE),
                               lambda l, t: (l, 0, _pack_lookup(t, [b // PACK_TILE for b in bases]))),
                  pl.BlockSpec((None, k, 128), lambda l, t: (l, 0, _pack_lookup(t, right))),
                  pl.BlockSpec((None, k, 128), lambda l, t: (l, 0, 0)),
                  pl.BlockSpec((None, k, PACK_TILE), lambda l, t: (l, 0, 0))],
        out_specs=pl.BlockSpec((None, k, PACK_TILE), lambda l, t: (l, 0, t)),
        out_shape=jax.ShapeDtypeStruct((layers, k, N_PROJ), BF16),
        compiler_params=_cparams(("parallel", "arbitrary")),
        name="pack_w_in",
    )(w_in, w_in, tail, small)


def _trunk(x, p, s_hgrn, s_gla, s_delta, c_dconv, c_fconv, wts, lbs, *, sample):
    bsz, t, _ = x.shape
    m = bsz * t
    tm = m if sample else 1024
    h = x.reshape(m, D_MODEL)
    outs = {k: [] for k in ("hgrn", "gla", "delta", "dconv", "fconv", "v")}
    s_b = s_c = s_d = None
    for l in range(DEPTH):
        w = wts[l]
        proj, gates = in_proj(h, w["g_mix"], w["w_in_layers"], l, tm=tm, tn=1024)
        o_a, v_a = mixer_a(proj, w["a_ln_g"], w["a_ln_b"], w["a_w_mix"], w["a_bias_rows"])
        st_b, st_c, st_d = (s_hgrn, s_gla, s_delta) if sample else (s_hgrn[l], s_gla[l], s_delta[l])
        b_args = (lbs[l], w["b_norm_g"], st_b)
        c_args = (w["c_w_a2"], w["c_b_a"], w["c_norm_g"], st_c)
        d_args = (c_dconv[l], st_d, w["d_conv_w"], w["d_a_log"], w["d_dt_bias"], w["d_norm_g"])
        if sample:
            o_b, s_b = mixer_b_grouped(proj, *b_args, s_b, seg=t, tb=D_ROWS)
            o_c, s_c = mixer_c_grouped(proj, *c_args, s_c, seg=t, tb=D_ROWS)
            o_d, nb_d, s_d = mixer_d_grouped(proj, *d_args, s_d, seg=t, tb=D_ROWS)
        else:
            pseq = proj.reshape(bsz, t, N_SEQ_COLS)
            flat = lambda o: o.reshape(m, BRANCH_WIDTH)
            o_b, s_b = mixer_b(pseq, *b_args, t_valid=t, tb=256, nb=1, unroll=1)
            o_c, s_c = mixer_c(pseq, *c_args, t_valid=t, tb=256, nb=1, unroll=1)
            o_d, nb_d, s_d = mixer_d(pseq, *d_args, tb=256, nb=2 if bsz % 2 == 0 else 1)
            o_b, o_c, o_d = flat(o_b), flat(o_c), flat(o_d)
        merged = merge_branches((o_a, o_b, o_c, o_d), gates, w["w_branch"], tm=tm, tn=512)
        h = matmul_residual(merged, w["w_out_layers"], l, h, tm=tm, tn=1024)

        act, nb_f = ffn_up_gate(h, w["g_ffn"], w["w_ffn_up_layers"], l, c_fconv[l], w["ffn_conv_w"],
                                w["ffn_conv_b"], seq_len=t, tm=tm, tn=512)
        h = matmul_residual(act, w["w_ffn_down_layers"], l, h, tm=min(tm, 512), tn=512)

        h = ple_update(h, p[l].reshape(m, PLE_DIM), w["g_ple"], w["w_ple_gate"], w["w_ple_proj"], tm=tm, tn=512)

        if not sample:
            outs["hgrn"].append(s_b)
            outs["gla"].append(s_c)
            outs["delta"].append(s_d)
        outs["dconv"].append(nb_d)
        outs["fconv"].append(nb_f)
        outs["v"].append(v_a.reshape(bsz, t, BRANCH_WIDTH))
    y = final_norm(h, wts[0]["g_final"], tm=min(tm, 512)).reshape(bsz, t, D_MODEL)
    states = (s_b, s_c, s_d) if sample else tuple(jnp.stack(outs[k]) for k in ("hgrn", "gla", "delta"))
    return (y,) + states + tuple(jnp.stack(outs[k]) for k in ("dconv", "fconv", "v"))


def kernel(x_prompt, x_sample, state_hgrn, state_gla, state_delta, state_delta_conv, state_ffn_conv, p_prompt, p_sample, g_mix, w_in, a_ln_g, a_ln_b, a_w_s, a_b_s, b_lb, b_norm_g, c_w_a2, c_b_a, c_norm_g, d_conv_w, d_a_log, d_dt_bias, d_norm_g, w_branch, w_out, g_ffn, w_ffn_up, ffn_conv_w, ffn_conv_b, w_ffn_down, g_ple, w_ple_gate, w_ple_proj, g_final):
    bp, t_p, _ = x_prompt.shape
    bs, t_s, _ = x_sample.shape
    sm = jax.nn.softmax(b_lb.astype(F32), axis=0)
    lbs = jnp.cumsum(sm, axis=0) - sm[0]

    w_in_packed = pack_w_in(w_in)
    shared = [dict(
        g_mix=g_mix[l], w_in_layers=w_in_packed, a_ln_g=a_ln_g[l], a_ln_b=a_ln_b[l],
        b_norm_g=b_norm_g[l], c_w_a2=c_w_a2[l], c_b_a=c_b_a[l], c_norm_g=c_norm_g[l],
        d_conv_w=d_conv_w[l], d_a_log=d_a_log[l], d_dt_bias=d_dt_bias[l], d_norm_g=d_norm_g[l],
        w_branch=w_branch[l].astype(BF16), w_out_layers=w_out, g_ffn=g_ffn[l],
        w_ffn_up_layers=w_ffn_up, ffn_conv_w=ffn_conv_w[l], ffn_conv_b=ffn_conv_b[l],
        w_ffn_down_layers=w_ffn_down, g_ple=g_ple[l], w_ple_gate=w_ple_gate[l].astype(BF16),
        w_ple_proj=w_ple_proj[l].astype(BF16), g_final=g_final) for l in range(DEPTH)]

    def layer_weights(l, sample):
        if sample:
            seqs = A_CHUNK // t_s
            idx = jnp.arange(A_CHUNK) // t_s
            same_seq = idx[:, None] == idx[None, :]
            w_mix = jnp.where(same_seq, jnp.tile(a_w_s[l, :, :t_s, :t_s], (1, seqs, seqs)), 0.0)
            bias_rows = jnp.tile(a_b_s[l, :, :t_s].T, (seqs, 1))
        else:
            w_mix = a_w_s[l]
            bias_rows = a_b_s[l].T
        return dict(shared[l], a_w_mix=w_mix, a_bias_rows=bias_rows)

    dt = x_prompt.dtype
    zeros = lambda *s: jnp.zeros((DEPTH, bp) + s, dt)
    out_p = _trunk(x_prompt, p_prompt, zeros(HEADS, HEAD_DIM, HEAD_DIM), zeros(HEADS, C_KDIM, HEAD_DIM),
                   zeros(HEADS, HEAD_DIM, HEAD_DIM), zeros(D_CONV - 1, D_QKV), zeros(FFN_CONV - 1, FFN_DIM),
                   [layer_weights(l, False) for l in range(DEPTH)], lbs, sample=False)
    out_s = _trunk(x_sample, p_sample, state_hgrn, state_gla, state_delta, state_delta_conv, state_ffn_conv,
                   [layer_weights(l, True) for l in range(DEPTH)], lbs, sample=True)
    y_p, hgrn_p, gla_p, delta_p, dconv_p, fconv_p, _ = out_p
    y_s, hgrn_s, gla_s, delta_s, dconv_s, fconv_s, v_s = out_s
    return (y_p, y_s, hgrn_p, hgrn_s, gla_p, gla_s, delta_p, delta_s,
            dconv_p, dconv_s, fconv_p, fconv_s, v_s)
```

```python
import functools

import jax
import jax.numpy as jnp
from jax import lax
from jax.experimental import pallas as pl
from jax.experimental.pallas import tpu as pltpu

F32 = jnp.float32
BF16 = jnp.bfloat16

D_MODEL = 2048
DEPTH = 2
PLE_DIM = 256
EPS = 1e-6
F_TINY = 1e-30
N_BRANCH = 4
BRANCH_WIDTH = D_MODEL // 4
A_GROUPS = 4
A_CHUNK = 128
HEADS = 4
HEAD_DIM = 128
C_KDIM = 64
C_RANK = 16
C_TAU = 16.0
D_CONV = 4
D_QKV = 3 * BRANCH_WIDTH
FFN_DIM = 5632
FFN_CONV = 3
GLA_CHUNK = 16
LOG2E = 1.4426950408889634

OFF_D_QKV = 0
OFF_A_U = 1536
OFF_A_V = 2048
OFF_B_Q = 2560
OFF_B_F = 3072
OFF_B_I = 3584
OFF_B_G = 4096
OFF_C_V = 4608
OFF_C_R = 5120
OFF_D_Z = 5632
OFF_C_Q = 6144
OFF_C_K = 6400
OFF_C_LR = 6656
OFF_D_BETA = 6784
OFF_D_DECAY = 6912
OFF_GATES = 7168
N_SEQ_COLS = 7168
N_PROJ = OFF_GATES + N_BRANCH * D_MODEL

VMEM_LIMIT = 56 * 1024 * 1024


def _cparams(sem):
    return pltpu.CompilerParams(dimension_semantics=sem, vmem_limit_bytes=VMEM_LIMIT)


def _gelu(x):
    return 0.5 * x * (1.0 + jnp.tanh(0.7978845608028654 * (x + 0.044715 * (x * x * x))))


def _silu(x):
    return x * jax.nn.sigmoid(x)


def _softplus(x):
    return jnp.maximum(x, 0.0) + jnp.log1p(jnp.exp(-jnp.abs(x)))


def _log_sigmoid(x):
    return -_softplus(-x)


def _rms_rows(x):
    return x * lax.rsqrt(jnp.mean(x * x, axis=-1, keepdims=True) + EPS)


def _dot(a, b):
    return jnp.dot(a, b, preferred_element_type=F32)


def _dot_nt(a, b):
    return lax.dot_general(a, b, (((1,), (1,)), ((), ())), preferred_element_type=F32)


def _dot_tn(a, b):
    return lax.dot_general(a, b, (((0,), (0,)), ((), ())), preferred_element_type=F32)


def _cumsum_rows(x):
    n = x.shape[0]
    row = lax.broadcasted_iota(jnp.int32, x.shape, 0)
    sh = 1
    while sh < n:
        x = x + jnp.where(row >= sh, pltpu.roll(x, sh, axis=0), 0.0)
        sh *= 2
    return x


NORM_ROWS = 128


def _norm_to_scratch(x_ref, g_ref, a_scr):
    def slab(n, carry):
        rows = pl.ds(pl.multiple_of(n * NORM_ROWS, NORM_ROWS), NORM_ROWS)
        a_scr[rows, :] = (_rms_rows(x_ref[rows, :]) * g_ref[...]).astype(BF16)
        return carry

    lax.fori_loop(0, x_ref.shape[0] // NORM_ROWS, slab, 0)


def _matmul_residual_body(x_ref, w_ref, r_ref, o_ref, w_scr):
    @pl.when(pl.program_id(1) == 0)
    def _():
        def slab(n, carry):
            rows = pl.ds(pl.multiple_of(n * NORM_ROWS, NORM_ROWS), NORM_ROWS)
            w_scr[rows, :] = w_ref[rows, :].astype(BF16)
            return carry

        lax.fori_loop(0, w_ref.shape[0] // NORM_ROWS, slab, 0)

    o_ref[...] = r_ref[...] + _dot(x_ref[...], w_scr[...])


def matmul_residual(x, w_layers, layer, res, *, tm, tn):
    m, k = x.shape
    n = w_layers.shape[2]
    return pl.pallas_call(
        _matmul_residual_body,
        grid=(n // tn, m // tm),
        in_specs=[pl.BlockSpec((tm, k), lambda j, i: (i, 0)),
                  pl.BlockSpec((None, k, tn), lambda j, i: (layer, 0, j)),
                  pl.BlockSpec((tm, tn), lambda j, i: (i, j))],
        out_specs=pl.BlockSpec((tm, tn), lambda j, i: (i, j)),
        out_shape=jax.ShapeDtypeStruct((m, n), F32),
        scratch_shapes=[pltpu.VMEM((k, tn), BF16)],
        compiler_params=_cparams(("parallel", "arbitrary")),
        name="matmul_residual",
    )(x, w_layers, res)


def _in_proj_body(n_seq_tiles, x_ref, g_ref, w_ref, seq_ref, gate_ref, a_scr):
    j = pl.program_id(1)

    @pl.when(j == 0)
    def _():
        _norm_to_scratch(x_ref, g_ref, a_scr)

    acc = _dot(a_scr[...], w_ref[...])

    @pl.when(j < n_seq_tiles)
    def _():
        seq_ref[...] = acc

    @pl.when(j >= n_seq_tiles)
    def _():
        gate_ref[...] = acc.astype(gate_ref.dtype)


def in_proj(x, g, w_layers, layer, *, tm, tn):
    m, k = x.shape
    n_seq = N_SEQ_COLS // tn
    n_gate = N_BRANCH * D_MODEL // tn
    return pl.pallas_call(
        functools.partial(_in_proj_body, n_seq),
        grid=(m // tm, n_seq + n_gate),
        in_specs=[pl.BlockSpec((tm, k), lambda i, j: (i, 0)),
                  pl.BlockSpec((1, k), lambda i, j: (0, 0)),
                  pl.BlockSpec((None, k, tn), lambda i, j: (layer, 0, j))],
        out_specs=[pl.BlockSpec((tm, tn), lambda i, j: (i, jnp.minimum(j, n_seq - 1))),
                   pl.BlockSpec((tm, tn), lambda i, j: (i, jnp.maximum(j - n_seq, 0)))],
        out_shape=[jax.ShapeDtypeStruct((m, N_SEQ_COLS), F32),
                   jax.ShapeDtypeStruct((m, N_BRANCH * D_MODEL), BF16)],
        scratch_shapes=[pltpu.VMEM((tm, k), BF16)],
        compiler_params=_cparams(("parallel", "arbitrary")),
        name="in_proj",
    )(x, g.reshape(1, k), w_layers)


def _merge_body(oa_ref, ob_ref, oc_ref, od_ref, ga_ref, gb_ref, gc_ref, gd_ref, w_ref, o_ref):
    acc = None
    for b, (o_b, g_b) in enumerate(((oa_ref, ga_ref), (ob_ref, gb_ref), (oc_ref, gc_ref), (od_ref, gd_ref))):
        term = jax.nn.sigmoid(g_b[...].astype(F32)) * _dot(o_b[...], w_ref[b])
        acc = term if acc is None else acc + term
    o_ref[...] = acc.astype(o_ref.dtype)


def merge_branches(o_branches, gates, w_branch, *, tm, tn):
    m = gates.shape[0]
    gate_specs = [pl.BlockSpec((tm, tn), functools.partial(
        lambda i, j, b: (i, b * (D_MODEL // tn) + j), b=b)) for b in range(N_BRANCH)]
    return pl.pallas_call(
        _merge_body,
        grid=(m // tm, D_MODEL // tn),
        in_specs=[pl.BlockSpec((tm, BRANCH_WIDTH), lambda i, j: (i, 0))] * N_BRANCH + gate_specs
        + [pl.BlockSpec((N_BRANCH, BRANCH_WIDTH, tn), lambda i, j: (0, 0, j))],
        out_specs=pl.BlockSpec((tm, tn), lambda i, j: (i, j)),
        out_shape=jax.ShapeDtypeStruct((m, D_MODEL), BF16),
        compiler_params=_cparams(("parallel", "arbitrary")),
        name="merge_branches",
    )(*o_branches, gates, gates, gates, gates, w_branch)


def _ple_body(x_ref, xres_ref, p_ref, g_ref, wg_ref, wp_ref, o_ref, a_scr, p_scr):
    @pl.when(pl.program_id(1) == 0)
    def _():
        _norm_to_scratch(x_ref, g_ref, a_scr)
        p_scr[...] = p_ref[...].astype(BF16)

    gate = jax.nn.sigmoid(_dot(a_scr[...], wg_ref[...]))
    o_ref[...] = xres_ref[...] + _dot(p_scr[...], wp_ref[...]) * gate


def ple_update(x, p, g, w_gate, w_proj, *, tm, tn):
    m, k = x.shape
    return pl.pallas_call(
        _ple_body,
        grid=(m // tm, D_MODEL // tn),
        in_specs=[pl.BlockSpec((tm, k), lambda i, j: (i, 0)),
                  pl.BlockSpec((tm, tn), lambda i, j: (i, j)),
                  pl.BlockSpec((tm, PLE_DIM), lambda i, j: (i, 0)),
                  pl.BlockSpec((1, k), lambda i, j: (0, 0)),
                  pl.BlockSpec((k, tn), lambda i, j: (0, j)),
                  pl.BlockSpec((PLE_DIM, tn), lambda i, j: (0, j))],
        out_specs=pl.BlockSpec((tm, tn), lambda i, j: (i, j)),
        out_shape=jax.ShapeDtypeStruct((m, D_MODEL), F32),
        scratch_shapes=[pltpu.VMEM((tm, k), BF16), pltpu.VMEM((tm, PLE_DIM), BF16)],
        compiler_params=_cparams(("parallel", "arbitrary")),
        name="ple_update",
    )(x, x, p, g.reshape(1, k), w_gate, w_proj)


def _final_norm_body(x_ref, g_ref, o_ref):
    def slab(n, carry):
        rows = pl.ds(pl.multiple_of(n * NORM_ROWS, NORM_ROWS), NORM_ROWS)
        o_ref[rows, :] = _rms_rows(x_ref[rows, :]) * g_ref[...]
        return carry

    lax.fori_loop(0, x_ref.shape[0] // NORM_ROWS, slab, 0)


def final_norm(x, g, *, tm):
    m, k = x.shape
    return pl.pallas_call(
        _final_norm_body,
        grid=(m // tm,),
        in_specs=[pl.BlockSpec((tm, k), lambda i: (i, 0)), pl.BlockSpec((1, k), lambda i: (0, 0))],
        out_specs=pl.BlockSpec((tm, k), lambda i: (i, 0)),
        out_shape=jax.ShapeDtypeStruct((m, k), F32),
        compiler_params=_cparams(("parallel",)),
        name="final_norm",
    )(x, g.reshape(1, k))


FFN_SUB = 256


def _cast_rows_to_bf16(src_ref, dst_ref):
    def slab(n, carry):
        rows = pl.ds(pl.multiple_of(n * NORM_ROWS, NORM_ROWS), NORM_ROWS)
        dst_ref[rows, :] = src_ref[rows, :].astype(BF16)
        return carry

    lax.fori_loop(0, src_ref.shape[0] // NORM_ROWS, slab, 0)


def _ffn_up_long_body(tiles_per_seq, a_ref, wg_ref, wu_ref, st_ref, cw_ref, cb_ref, o_ref, st_out_ref,
                      wg_s, wu_s, ext):
    i = pl.program_id(1)
    tm = a_ref.shape[0]

    @pl.when(i == 0)
    def _():
        _cast_rows_to_bf16(wg_ref, wg_s)
        _cast_rows_to_bf16(wu_ref, wu_s)

    @pl.when(i % tiles_per_seq == 0)
    def _():
        ext[6:8, :] = st_ref[0]

    for r in range(0, tm, FFN_SUB):
        a = a_ref[r:r + FFN_SUB, :]
        ext[8 + r:8 + r + FFN_SUB, :] = _dot(a, wg_s[...])
        conv = (ext[6 + r:6 + r + FFN_SUB, :] * cw_ref[0:1, :] + ext[7 + r:7 + r + FFN_SUB, :] * cw_ref[1:2, :]
                + ext[8 + r:8 + r + FFN_SUB, :] * cw_ref[2:3, :])
        o_ref[r:r + FFN_SUB, :] = (_gelu(conv + cb_ref[...]) * _dot(a, wu_s[...])).astype(o_ref.dtype)

    last = ext[6 + tm:8 + tm, :]
    ext[6:8, :] = last
    st_out_ref[0] = last


def _ffn_up_group_body(seg, a_ref, wg_ref, wu_ref, st_ref, cw_ref, cb_ref, o_ref, st_out_ref, wg_s, wu_s, stp, fg_s):
    tm = a_ref.shape[0]
    nseq = tm // seg
    hist = FFN_CONV - 1

    @pl.when(pl.program_id(1) == 0)
    def _():
        _cast_rows_to_bf16(wg_ref, wg_s)
        _cast_rows_to_bf16(wu_ref, wu_s)

    lane_blocks = stp.shape[0]
    stp[...] = jnp.zeros_like(stp)
    for j in range(hist):
        for c in range(lane_blocks):
            stp[c, pl.ds(j, nseq, stride=seg), :] = st_ref[:, j, c * 128:(c + 1) * 128]

    a = a_ref[...]
    fg = _dot(a, wg_s[...])
    for c in range(lane_blocks):
        fg_s[c] = fg[:, c * 128:(c + 1) * 128]
    h = jnp.concatenate([stp[c] for c in range(lane_blocks)], axis=1)
    tpos = lax.broadcasted_iota(jnp.int32, (tm, 1), 0) & (seg - 1)
    lag1 = jnp.where(tpos < 1, pltpu.roll(h, tm - 1, axis=0), pltpu.roll(fg, 1, axis=0))
    lag2 = jnp.where(tpos < 2, h, pltpu.roll(fg, 2, axis=0))
    conv = lag2 * cw_ref[0:1, :] + lag1 * cw_ref[1:2, :] + fg * cw_ref[2:3, :]
    o_ref[...] = (_gelu(conv + cb_ref[...]) * _dot(a, wu_s[...])).astype(o_ref.dtype)
    for j in range(hist):
        for c in range(lane_blocks):
            st_out_ref[:, j, c * 128:(c + 1) * 128] = fg_s[c, pl.ds(seg - hist + j, nseq, stride=seg), :]


def _rms_bf16_body(x_ref, g_ref, o_ref):
    _norm_to_scratch(x_ref, g_ref, o_ref)


def rms_bf16(x, g, *, tm):
    m, k = x.shape
    return pl.pallas_call(
        _rms_bf16_body,
        grid=(m // tm,),
        in_specs=[pl.BlockSpec((tm, k), lambda i: (i, 0)), pl.BlockSpec((1, k), lambda i: (0, 0))],
        out_specs=pl.BlockSpec((tm, k), lambda i: (i, 0)),
        out_shape=jax.ShapeDtypeStruct((m, k), BF16),
        compiler_params=_cparams(("parallel",)),
        name="rms_bf16",
    )(x, g.reshape(1, k))


def ffn_up_gate(x, g, w_up_layers, layer, state, conv_w, conv_b, *, seq_len, tm, tn):
    m, k = x.shape
    bsz = m // seq_len
    ncol = FFN_DIM // tn
    a = rms_bf16(x, g, tm=min(tm, 512))
    common_in = [pl.BlockSpec((tm, k), lambda j, i: (i, 0)),
                 pl.BlockSpec((None, k, tn), lambda j, i: (layer, 0, j)),
                 pl.BlockSpec((None, k, tn), lambda j, i: (layer, 0, ncol + j))]
    conv_in = [pl.BlockSpec((FFN_CONV, tn), lambda j, i: (0, j)), pl.BlockSpec((1, tn), lambda j, i: (0, j))]
    act_spec = pl.BlockSpec((tm, tn), lambda j, i: (i, j))
    act_shape = jax.ShapeDtypeStruct((m, FFN_DIM), BF16)
    w_scr = [pltpu.VMEM((k, tn), BF16)] * 2
    if seq_len >= tm:
        tps = seq_len // tm
        act, tile_last = pl.pallas_call(
            functools.partial(_ffn_up_long_body, tps),
            grid=(ncol, m // tm),
            in_specs=common_in + [pl.BlockSpec((1, FFN_CONV - 1, tn), lambda j, i: (i // tps, 0, j))] + conv_in,
            out_specs=[act_spec, pl.BlockSpec((1, FFN_CONV - 1, tn), lambda j, i: (i, 0, j))],
            out_shape=[act_shape, jax.ShapeDtypeStruct((m // tm, FFN_CONV - 1, FFN_DIM), F32)],
            scratch_shapes=w_scr + [pltpu.VMEM((tm + 8, tn), F32)],
            compiler_params=_cparams(("parallel", "arbitrary")),
            name="ffn_up_gate",
        )(a, w_up_layers, w_up_layers, state, conv_w, conv_b.reshape(1, FFN_DIM))
        return act, tile_last[tps - 1::tps]
    st_spec = pl.BlockSpec((tm // seq_len, FFN_CONV - 1, tn), lambda j, i: (i, 0, j))
    return pl.pallas_call(
        functools.partial(_ffn_up_group_body, seq_len),
        grid=(ncol, m // tm),
        in_specs=common_in + [st_spec] + conv_in,
        out_specs=[act_spec, st_spec],
        out_shape=[act_shape, jax.ShapeDtypeStruct((bsz, FFN_CONV - 1, FFN_DIM), F32)],
        scratch_shapes=w_scr + [pltpu.VMEM((tn // 128, tm, 128), F32)] * 2,
        compiler_params=_cparams(("parallel", "arbitrary")),
        name="ffn_up_gate_grouped",
    )(a, w_up_layers, w_up_layers, state, conv_w, conv_b.reshape(1, FFN_DIM))


def _mixer_a_body(u_ref, v_ref, lng_ref, lnb_ref, w_ref, bias_ref, o_ref, vout_ref):
    u = _gelu(u_ref[...])
    v = _gelu(v_ref[...])
    vc = v - jnp.mean(v, axis=-1, keepdims=True)
    var = jnp.mean(vc * vc, axis=-1, keepdims=True)
    vn = vc * lax.rsqrt(var + EPS) * lng_ref[...] + lnb_ref[...]
    vout_ref[...] = vn
    n = w_ref.shape[1]
    causal = lax.broadcasted_iota(jnp.int32, (n, n), 0) >= lax.broadcasted_iota(jnp.int32, (n, n), 1)
    gd = BRANCH_WIDTH // A_GROUPS
    for g in range(A_GROUPS):
        w = jnp.where(causal, w_ref[g], 0.0)
        mixed = _dot(w, vn[:, g * gd:(g + 1) * gd]) + bias_ref[:, g:g + 1]
        o_ref[:, g * gd:(g + 1) * gd] = (u[:, g * gd:(g + 1) * gd] * mixed).astype(o_ref.dtype)


def mixer_a(proj, ln_g, ln_b, w_mix, bias_rows):
    m = proj.shape[0]
    r = A_CHUNK
    bw = BRANCH_WIDTH
    return pl.pallas_call(
        _mixer_a_body,
        grid=(m // r,),
        in_specs=[pl.BlockSpec((r, bw), lambda i: (i, OFF_A_U // bw)),
                  pl.BlockSpec((r, bw), lambda i: (i, OFF_A_V // bw)),
                  pl.BlockSpec((1, bw), lambda i: (0, 0)),
                  pl.BlockSpec((1, bw), lambda i: (0, 0)),
                  pl.BlockSpec((A_GROUPS, r, r), lambda i: (0, 0, 0)),
                  pl.BlockSpec((r, A_GROUPS), lambda i: (0, 0))],
        out_specs=[pl.BlockSpec((r, bw), lambda i: (i, 0)), pl.BlockSpec((r, bw), lambda i: (i, 0))],
        out_shape=[jax.ShapeDtypeStruct((m, bw), BF16), jax.ShapeDtypeStruct((m, bw), F32)],
        compiler_params=_cparams(("parallel",)),
        name="mixer_a",
    )(proj, proj, ln_g.reshape(1, bw), ln_b.reshape(1, bw), w_mix, bias_rows)


def _gla_head_chunk(q, k, v, g, n_src):
    c = q.shape[0]
    g2 = _cumsum_rows(g) * LOG2E
    n_tiles = c // 8
    row8 = lax.broadcasted_iota(jnp.int32, (8, 1), 0)
    q_t = [q[i * 8:(i + 1) * 8] for i in range(n_tiles)]
    g_t = [g2[i * 8:(i + 1) * 8] for i in range(n_tiles)]
    o_t = [jnp.zeros((8, v.shape[1]), F32) for _ in range(n_tiles)]
    for s in range(n_src):
        for i in range(s // 8, n_tiles):
            a = jnp.sum(q_t[i] * jnp.exp2(g_t[i] - g2[s:s + 1, :]) * k[s:s + 1, :], axis=-1, keepdims=True)
            if i == s // 8:
                a = jnp.where(row8 >= s - 8 * i, a, 0.0)
            o_t[i] = o_t[i] + a * v[s:s + 1, :]
    o = jnp.concatenate(o_t, axis=0) if n_tiles > 1 else o_t[0]
    gl = g2[c - 1:c, :]
    return o, q * jnp.exp2(g2), k * jnp.exp2(gl - g2), jnp.exp2(gl)


def _head_block_diag(xs):
    z = jnp.zeros_like(xs[0])
    return jnp.concatenate(
        [jnp.concatenate([x if j == h else z for j in range(HEADS)], axis=1) for h, x in enumerate(xs)], axis=0)


def _gla_state_dots(qgs, kds, vs, es, st_ref, bi):
    st = st_ref[bi]
    o_inter = _dot_nt(_head_block_diag(qgs), st)
    st_ref[bi] = st * jnp.concatenate(es, axis=1) + _dot_tn(jnp.concatenate(vs, axis=0), _head_block_diag(kds))
    return o_inter


def _gla_state_step(parts, vs, st_ref, bi):
    c = vs[0].shape[0]
    o_inter = _gla_state_dots([p[1] for p in parts], [p[2] for p in parts], vs, [p[3] for p in parts], st_ref, bi)
    return [p[0] + o_inter[h * c:(h + 1) * c] for h, p in enumerate(parts)]


def _gla_pair_tile(q, k, v, g, seg):
    g2 = _segment_cumsum(g, seg) * LOG2E
    tpos = lax.broadcasted_iota(jnp.int32, (8, 1), 0) & (seg - 1)
    o = jnp.sum(q * k, axis=-1, keepdims=True) * v
    for d in range(1, seg):
        a = jnp.sum(q * jnp.exp2(g2 - pltpu.roll(g2, d, axis=0)) * pltpu.roll(k, d, axis=0), axis=-1, keepdims=True)
        o = o + jnp.where(tpos >= d, a, 0.0) * pltpu.roll(v, d, axis=0)
    first = lax.broadcasted_iota(jnp.int32, (8, 1), 0) < seg
    gl_a, gl_b = g2[seg - 1:seg, :], g2[2 * seg - 1:2 * seg, :]
    kd = k * jnp.exp2(jnp.where(first, gl_a, gl_b) - g2)
    return o, q * jnp.exp2(g2), kd, jnp.exp2(gl_a), jnp.exp2(gl_b)


def _gla_pair_step(tiles, vs, st_ref, p, seg):
    first = lax.broadcasted_iota(jnp.int32, (8, 1), 0) < seg
    qgs = [t[1] for t in tiles]
    oi_a = _gla_state_dots(qgs, [jnp.where(first, t[2], 0.0) for t in tiles], vs, [t[3] for t in tiles],
                           st_ref, 2 * p)
    oi_b = _gla_state_dots(qgs, [jnp.where(first, 0.0, t[2]) for t in tiles], vs, [t[4] for t in tiles],
                           st_ref, 2 * p + 1)
    return [t[0] + jnp.where(first, oi_a[h * 8:(h + 1) * 8], oi_b[h * 8:(h + 1) * 8]) for h, t in enumerate(tiles)]


def _load_state_t(s0_ref, st_ref, kdim):
    def per_seq(bi, carry):
        for h in range(HEADS):
            s = s0_ref[bi, h]
            if kdim < HEAD_DIM:
                s = jnp.concatenate([s, jnp.zeros((HEAD_DIM - kdim, HEAD_DIM), F32)], axis=0)
            st_ref[bi, :, h * HEAD_DIM:(h + 1) * HEAD_DIM] = s.T
        return carry

    lax.fori_loop(0, s0_ref.shape[0], per_seq, 0)


def _store_state_t(st_ref, sout_ref, kdim):
    def per_seq(bi, carry):
        for h in range(HEADS):
            sout_ref[bi, h] = st_ref[bi, :, h * HEAD_DIM:(h + 1) * HEAD_DIM].T[:kdim, :]
        return carry

    lax.fori_loop(0, sout_ref.shape[0], per_seq, 0)


def _seq_chunk_loop(nb, n_chunks, unroll, fn):
    def step(it, carry):
        n = 0 if n_chunks == 1 else it % n_chunks
        bj = 0 if nb == unroll else it // n_chunks
        for u in range(unroll):
            fn(bj * unroll + u, n)
        return carry

    lax.fori_loop(0, (nb // unroll) * n_chunks, step, 0, unroll=4)


def _chunk_rows(n, c):
    if isinstance(n, int):
        return n * c, pl.ds(n * c, c)
    r0 = pl.multiple_of(n * c, c)
    return r0, pl.ds(r0, c)


def _valid_rows(t_valid, tb, r0, c):
    return pl.program_id(1) * tb + r0 + lax.broadcasted_iota(jnp.int32, (c, 1), 0) < t_valid


def _mixer_b_body(t_valid, t_pad, unroll, q_ref, f_ref, i_ref, og_ref, lb_ref, ng_ref, s0_ref, o_ref, sout_ref,
                  st_ref):
    nb, tb = q_ref.shape[0], q_ref.shape[1]
    c = min(GLA_CHUNK, tb)
    kd = HEAD_DIM
    padded = t_valid < t_pad
    n_src = t_valid if padded else c

    @pl.when(pl.program_id(1) == 0)
    def _():
        _load_state_t(s0_ref, st_ref, kd)

    def chunk(bi, n):
        r0, rows = _chunk_rows(n, c)
        parts, vs = [], []
        for h in range(HEADS):
            cols = slice(h * kd, (h + 1) * kd)
            lb = lb_ref[:, cols]
            fp = f_ref[bi, rows, cols]
            f = lb + (1.0 - lb) * jax.nn.sigmoid(fp)
            g = jnp.log(jnp.maximum(f, F_TINY))
            k = (1.0 - lb) * jax.nn.sigmoid(-fp)
            if padded:
                ok = _valid_rows(t_valid, tb, r0, c)
                g = jnp.where(ok, g, 0.0)
                k = jnp.where(ok, k, 0.0)
            vs.append(i_ref[bi, rows, cols])
            parts.append(_gla_head_chunk(q_ref[bi, rows, cols], k, vs[h], g, n_src))
        for h, o in enumerate(_gla_state_step(parts, vs, st_ref, bi)):
            cols = slice(h * kd, (h + 1) * kd)
            o = _rms_rows(o) * ng_ref[...] * jax.nn.sigmoid(og_ref[bi, rows, cols])
            o_ref[bi, rows, cols] = o.astype(o_ref.dtype)

    _seq_chunk_loop(nb, tb // c, unroll, chunk)

    @pl.when(pl.program_id(1) == pl.num_programs(1) - 1)
    def _():
        _store_state_t(st_ref, sout_ref, kd)


def mixer_b(pseq, lb, norm_g, s0, *, t_valid, tb, nb, unroll):
    bsz, t_pad, _ = pseq.shape
    assert t_valid == t_pad or t_pad == tb <= GLA_CHUNK
    bw = BRANCH_WIDTH
    field = lambda off: pl.BlockSpec((nb, tb, bw), lambda b, s: (b, s, off // bw))
    state = pl.BlockSpec((nb, HEADS, HEAD_DIM, HEAD_DIM), lambda b, s: (b, 0, 0, 0))
    return pl.pallas_call(
        functools.partial(_mixer_b_body, t_valid, t_pad, unroll),
        grid=(bsz // nb, t_pad // tb),
        in_specs=[field(OFF_B_Q), field(OFF_B_F), field(OFF_B_I), field(OFF_B_G),
                  pl.BlockSpec((1, bw), lambda b, s: (0, 0)),
                  pl.BlockSpec((1, HEAD_DIM), lambda b, s: (0, 0)),
                  state],
        out_specs=[pl.BlockSpec((nb, tb, bw), lambda b, s: (b, s, 0)), state],
        out_shape=[jax.ShapeDtypeStruct((bsz, t_pad, bw), BF16),
                   jax.ShapeDtypeStruct((bsz, HEADS, HEAD_DIM, HEAD_DIM), F32)],
        scratch_shapes=[pltpu.VMEM((nb, HEAD_DIM, HEADS * HEAD_DIM), F32)],
        compiler_params=_cparams(("parallel", "arbitrary")),
        name="mixer_b",
    )(pseq, pseq, pseq, pseq, lb.reshape(1, bw), norm_g.reshape(1, HEAD_DIM), s0)


def _mixer_c_body(t_valid, t_pad, unroll, q_ref, k_ref, v_ref, r_ref, lr_ref, w2_ref, ba_ref, ng_ref, s0_ref,
                  o_ref, sout_ref, st_ref):
    nb, tb = q_ref.shape[0], q_ref.shape[1]
    c = min(GLA_CHUNK, tb)
    kd = C_KDIM
    padded = t_valid < t_pad
    n_src = t_valid if padded else c

    @pl.when(pl.program_id(1) == 0)
    def _():
        _load_state_t(s0_ref, st_ref, kd)

    zpad = jnp.zeros((c, HEAD_DIM - kd), F32)

    def chunk(bi, n):
        r0, rows = _chunk_rows(n, c)
        gate_in = _dot(lr_ref[bi, rows, 0:C_RANK], w2_ref[...]) + ba_ref[...]
        g_all = _log_sigmoid(gate_in) / C_TAU
        if padded:
            ok = _valid_rows(t_valid, tb, r0, c)
            g_all = jnp.where(ok, g_all, 0.0)
        parts, vs = [], []
        for h in range(HEADS):
            kcols = slice(h * kd, (h + 1) * kd)
            q = jnp.concatenate([q_ref[bi, rows, kcols] * (kd ** -0.5), zpad], axis=1)
            k = k_ref[bi, rows, kcols]
            if padded:
                k = jnp.where(ok, k, 0.0)
            k = jnp.concatenate([k, zpad], axis=1)
            g = jnp.concatenate([g_all[:, kcols], zpad], axis=1)
            vs.append(v_ref[bi, rows, h * HEAD_DIM:(h + 1) * HEAD_DIM])
            parts.append(_gla_head_chunk(q, k, vs[h], g, n_src))
        for h, o in enumerate(_gla_state_step(parts, vs, st_ref, bi)):
            vcols = slice(h * HEAD_DIM, (h + 1) * HEAD_DIM)
            o = _rms_rows(o) * ng_ref[...] * _silu(r_ref[bi, rows, vcols])
            o_ref[bi, rows, vcols] = o.astype(o_ref.dtype)

    _seq_chunk_loop(nb, tb // c, unroll, chunk)

    @pl.when(pl.program_id(1) == pl.num_programs(1) - 1)
    def _():
        _store_state_t(st_ref, sout_ref, kd)


def mixer_c(pseq, w_a2, b_a, norm_g, s0, *, t_valid, tb, nb, unroll):
    bsz, t_pad, _ = pseq.shape
    assert t_valid == t_pad or t_pad == tb <= GLA_CHUNK
    bw = BRANCH_WIDTH
    kw = HEADS * C_KDIM
    field = lambda off, w: pl.BlockSpec((nb, tb, w), lambda b, s: (b, s, off // w))
    state = pl.BlockSpec((nb, HEADS, C_KDIM, HEAD_DIM), lambda b, s: (b, 0, 0, 0))
    return pl.pallas_call(
        functools.partial(_mixer_c_body, t_valid, t_pad, unroll),
        grid=(bsz // nb, t_pad // tb),
        in_specs=[field(OFF_C_Q, kw), field(OFF_C_K, kw), field(OFF_C_V, bw), field(OFF_C_R, bw),
                  field(OFF_C_LR, 128),
                  pl.BlockSpec((C_RANK, kw), lambda b, s: (0, 0)),
                  pl.BlockSpec((1, kw), lambda b, s: (0, 0)),
                  pl.BlockSpec((1, HEAD_DIM), lambda b, s: (0, 0)),
                  state],
        out_specs=[pl.BlockSpec((nb, tb, bw), lambda b, s: (b, s, 0)), state],
        out_shape=[jax.ShapeDtypeStruct((bsz, t_pad, bw), BF16),
                   jax.ShapeDtypeStruct((bsz, HEADS, C_KDIM, HEAD_DIM), F32)],
        scratch_shapes=[pltpu.VMEM((nb, HEAD_DIM, HEADS * HEAD_DIM), F32)],
        compiler_params=_cparams(("parallel", "arbitrary")),
        name="mixer_c",
    )(pseq, pseq, pseq, pseq, pseq, w_a2, b_a.reshape(1, kw), norm_g.reshape(1, HEAD_DIM), s0)


def _grouped_pairs_loop(n_rows, seg, st_ref, head_inputs, head_outputs):
    def step(it, carry):
        rows = pl.ds(pl.multiple_of(it * 16, 16), 16)
        ins = [head_inputs(rows, h) for h in range(HEADS)]
        halves = []
        for half in range(2):
            sl = slice(half * 8, (half + 1) * 8)
            vs = [x[2][sl] for x in ins]
            tiles = [_gla_pair_tile(x[0][sl], x[1][sl], x[2][sl], x[3][sl], seg) for x in ins]
            halves.append(_gla_pair_step(tiles, vs, st_ref, 2 * it + half, seg))
        for h in range(HEADS):
            head_outputs(rows, h, jnp.concatenate([halves[0][h], halves[1][h]], axis=0))
        return carry

    lax.fori_loop(0, n_rows // 16, step, 0)


def _mixer_b_group_body(seg, q_ref, f_ref, i_ref, og_ref, lb_ref, ng_ref, s0_ref, o_ref, sout_ref, st_ref):
    kd = HEAD_DIM
    _load_state_t(s0_ref, st_ref, kd)

    def head_inputs(rows, h):
        cols = slice(h * kd, (h + 1) * kd)
        lb = lb_ref[:, cols]
        fp = f_ref[0, rows, cols]
        f = lb + (1.0 - lb) * jax.nn.sigmoid(fp)
        return (q_ref[0, rows, cols], (1.0 - lb) * jax.nn.sigmoid(-fp), i_ref[0, rows, cols],
                jnp.log(jnp.maximum(f, F_TINY)))

    def head_outputs(rows, h, o):
        cols = slice(h * kd, (h + 1) * kd)
        o = _rms_rows(o) * ng_ref[...] * jax.nn.sigmoid(og_ref[0, rows, cols])
        o_ref[0, rows, cols] = o.astype(o_ref.dtype)

    _grouped_pairs_loop(q_ref.shape[1], seg, st_ref, head_inputs, head_outputs)
    _store_state_t(st_ref, sout_ref, kd)


def _mixer_c_group_body(seg, q_ref, k_ref, v_ref, r_ref, lr_ref, w2_ref, ba_ref, ng_ref, s0_ref, o_ref, sout_ref,
                        st_ref, g_scr):
    kd = C_KDIM
    _load_state_t(s0_ref, st_ref, kd)
    g_scr[...] = _log_sigmoid(_dot(lr_ref[0, :, 0:C_RANK], w2_ref[...]) + ba_ref[...]) / C_TAU
    zpad = jnp.zeros((16, HEAD_DIM - kd), F32)

    def head_inputs(rows, h):
        kcols = slice(h * kd, (h + 1) * kd)
        pad = lambda x: jnp.concatenate([x, zpad], axis=1)
        return (pad(q_ref[0, rows, kcols] * (kd ** -0.5)), pad(k_ref[0, rows, kcols]),
                v_ref[0, rows, h * HEAD_DIM:(h + 1) * HEAD_DIM], pad(g_scr[rows, kcols]))

    def head_outputs(rows, h, o):
        vcols = slice(h * HEAD_DIM, (h + 1) * HEAD_DIM)
        o = _rms_rows(o) * ng_ref[...] * _silu(r_ref[0, rows, vcols])
        o_ref[0, rows, vcols] = o.astype(o_ref.dtype)

    _grouped_pairs_loop(q_ref.shape[1], seg, st_ref, head_inputs, head_outputs)
    _store_state_t(st_ref, sout_ref, kd)


def _layer_stacked_state(body, prev_states, state_ref_pos, seqs_per_step):
    n_prev = 0 if prev_states is None else prev_states.shape[0]

    def wrapped(*refs):
        refs = list(refs)
        if n_prev:
            prev_ref = refs.pop(0)
            refs[state_ref_pos][0:n_prev] = prev_ref[...]
        refs[state_ref_pos] = refs[state_ref_pos].at[n_prev]
        return body(*refs)

    def spec(shape_tail, layers):
        return pl.BlockSpec((layers, seqs_per_step) + shape_tail, lambda s: (0, s) + (0,) * len(shape_tail))

    def out_shape(s0):
        return jax.ShapeDtypeStruct((n_prev + 1,) + s0.shape[1:], F32)

    prev_specs = lambda s0: [spec(s0.shape[2:], n_prev)] if n_prev else []
    prev_inputs = [prev_states] if n_prev else []
    return wrapped, prev_specs, prev_inputs, lambda s0: spec(s0.shape[2:], n_prev + 1), out_shape


def _grouped_gla_call(body, prows, fields, params, s0, prev_states, kdim, *, seg, tb, extra_scratch=()):
    m = prows.shape[0]
    bw = BRANCH_WIDTH
    p3 = prows.reshape(1, m, -1)
    layer = 0 if prev_states is None else prev_states.shape[0]
    state = pl.BlockSpec((None, tb // seg, HEADS, kdim, HEAD_DIM), lambda s: (layer, s, 0, 0, 0))
    n_in = len(fields) + len(params) + 1
    wrapped, prev_specs, prev_inputs, state_out_spec, state_out_shape = _layer_stacked_state(
        functools.partial(body, seg), prev_states, n_in + 1, tb // seg)
    o, s_out = pl.pallas_call(
        wrapped,
        grid=(m // tb,),
        in_specs=prev_specs(s0)
        + [pl.BlockSpec((1, tb, w), functools.partial(lambda s, c: (0, s, c), c=off // w)) for off, w in fields]
        + [pl.BlockSpec(x.shape, functools.partial(lambda s, n: (0,) * n, n=x.ndim)) for x in params] + [state],
        out_specs=[pl.BlockSpec((1, tb, bw), lambda s: (0, s, 0)), state_out_spec(s0)],
        out_shape=[jax.ShapeDtypeStruct((1, m, bw), BF16), state_out_shape(s0)],
        scratch_shapes=[pltpu.VMEM((tb // seg, HEAD_DIM, HEADS * HEAD_DIM), F32), *extra_scratch],
        compiler_params=_cparams(("parallel",)),
        name=body.__name__.strip("_"),
    )(*prev_inputs, *([p3] * len(fields)), *params, s0)
    return o.reshape(m, bw), s_out


def mixer_b_grouped(prows, lb, norm_g, s0, prev_states, *, seg, tb):
    bw = BRANCH_WIDTH
    return _grouped_gla_call(
        _mixer_b_group_body, prows, [(OFF_B_Q, bw), (OFF_B_F, bw), (OFF_B_I, bw), (OFF_B_G, bw)],
        [lb.reshape(1, bw), norm_g.reshape(1, HEAD_DIM)], s0, prev_states, HEAD_DIM, seg=seg, tb=tb)


def mixer_c_grouped(prows, w_a2, b_a, norm_g, s0, prev_states, *, seg, tb):
    bw = BRANCH_WIDTH
    kw = HEADS * C_KDIM
    return _grouped_gla_call(
        _mixer_c_group_body, prows, [(OFF_C_Q, kw), (OFF_C_K, kw), (OFF_C_V, bw), (OFF_C_R, bw), (OFF_C_LR, 128)],
        [w_a2, b_a.reshape(1, kw), norm_g.reshape(1, HEAD_DIM)], s0, prev_states, C_KDIM, seg=seg, tb=tb,
        extra_scratch=(pltpu.VMEM((tb, kw), F32),))


def _unit_lower_inverse(a, order):
    c = a.shape[0]
    eye = (lax.broadcasted_iota(jnp.int32, (c, c), 0) == lax.broadcasted_iota(jnp.int32, (c, c), 1)).astype(F32)
    p = eye - a
    pw = a
    n = 2
    while n < order:
        pw = _dot(pw, pw)
        p = p + _dot(p, pw)
        n *= 2
    return p


D_ROWS = 64


def _stack_heads(x):
    return jnp.concatenate([x[:, h * HEAD_DIM:(h + 1) * HEAD_DIM] for h in range(HEADS)], axis=0)


def _stack_head_lanes(x):
    return jnp.concatenate([x[:, h:h + 1] for h in range(HEADS)], axis=0)


def _lane_pad_heads(x):
    return jnp.pad(x.reshape(1, HEADS), ((0, 0), (0, 128 - HEADS)))


def _segment_cumsum(x, seg):
    tpos = lax.broadcasted_iota(jnp.int32, x.shape, 0) & (seg - 1)
    sh = 1
    while sh < seg:
        x = x + jnp.where(tpos >= sh, pltpu.roll(x, sh, axis=0), 0.0)
        sh *= 2
    return x


def _delta_chunk_operands(seg, qkv, beta_all, g_all):
    hd = HEAD_DIM
    bw = BRANCH_WIDTH
    r = HEADS * qkv.shape[0]
    q = _stack_heads(qkv[:, 0:bw])
    k = _stack_heads(qkv[:, bw:2 * bw])
    v = _stack_heads(qkv[:, 2 * bw:3 * bw])
    q = q * lax.rsqrt(jnp.sum(q * q, axis=-1, keepdims=True) + EPS) * (hd ** -0.5)
    k = k * lax.rsqrt(jnp.sum(k * k, axis=-1, keepdims=True) + EPS)
    beta = _stack_head_lanes(beta_all)
    gc = _stack_head_lanes(_segment_cumsum(g_all, seg))
    ri = lax.broadcasted_iota(jnp.int32, (r, r), 0)
    ci = lax.broadcasted_iota(jnp.int32, (r, r), 1)
    shift = seg.bit_length() - 1
    same = (ri >> shift) == (ci >> shift)
    gr = jnp.sum(jnp.where(ri == ci, gc, 0.0), axis=0, keepdims=True)
    decay = jnp.where(same, jnp.where(ri >= ci, jnp.exp(jnp.minimum(gc - gr, 0.0)), 0.0), 0.0)
    a_mat = jnp.where(ri > ci, beta * decay * _dot_nt(k, k), 0.0)
    t_inv = _unit_lower_inverse(a_mat, seg)
    eg = jnp.exp(gc)
    sol = _dot(t_inv, jnp.concatenate([(beta * eg) * k, beta * v], axis=1))
    qk = _dot_nt(q, k) * decay
    is_last = (ci & (seg - 1)) == seg - 1
    gl = jnp.sum(jnp.where(same, jnp.where(is_last, gr, 0.0), 0.0), axis=1, keepdims=True)
    return sol[:, :hd], sol[:, hd:], qk, q * eg, k * jnp.exp(gl - gc), jnp.exp(gl)


def _mixer_d_body(x_ref, beta_ref, dec_ref, z_ref, cst_ref, s0_ref, cw_ref, alog_ref, dtb_ref, ng_ref,
                  o_ref, cst_out_ref, sout_ref, ext, conv, st_ref):
    nb, tb = x_ref.shape[0], x_ref.shape[1]
    c = D_ROWS
    hd = HEAD_DIM
    t = pl.program_id(1)
    nt = pl.num_programs(1)

    @pl.when(t == 0)
    def _():
        ext[:, 5:8, :] = cst_ref[...]
        st_ref[...] = s0_ref[...].reshape(nb * HEADS, hd, hd)

    for bi in range(nb):
        ext[bi, 8:8 + tb, :] = x_ref[bi]
        y = ext[bi, 5:5 + tb, :] * cw_ref[0:1, :]
        for j in range(1, D_CONV):
            y = y + ext[bi, 5 + j:5 + j + tb, :] * cw_ref[j:j + 1, :]
        conv[bi] = _silu(y)

    @pl.when(t == nt - 1)
    def _():
        cst_out_ref[...] = ext[:, 8 + tb - (D_CONV - 1):8 + tb, :]

    ext[:, 5:8, :] = ext[:, 5 + tb:8 + tb, :]

    a_neg = -jnp.exp(alog_ref[...])

    def chunk(n, carry):
        rows = pl.ds(pl.multiple_of(n * c, c), c)
        for bi in range(nb):
            beta_all = jax.nn.sigmoid(beta_ref[bi, rows, :])
            g_all = a_neg * _softplus(dec_ref[bi, rows, :] + dtb_ref[...])
            w, u, qk, qg, kdec, egl = _delta_chunk_operands(c, conv[bi, rows, :], beta_all, g_all)
            deltas, oqs = [], []
            for h in range(HEADS):
                hs = slice(h * c, (h + 1) * c)
                st = st_ref[bi * HEADS + h]
                x = _dot(jnp.concatenate([w[hs], qg[hs]], axis=0), st)
                delta = u[hs] - x[:c]
                deltas.append(delta)
                oqs.append(x[c:])
                st_ref[bi * HEADS + h] = egl[(h + 1) * c - 1:(h + 1) * c, :] * st + _dot_tn(kdec[hs], delta)
            o = jnp.concatenate(oqs, axis=0) + _dot(qk, jnp.concatenate(deltas, axis=0))
            o = _rms_rows(o) * ng_ref[...] * _silu(_stack_heads(z_ref[bi, rows, :]))
            for h in range(HEADS):
                o_ref[bi, rows, h * hd:(h + 1) * hd] = o[h * c:(h + 1) * c].astype(o_ref.dtype)
        return carry

    lax.fori_loop(0, tb // c, chunk, 0, unroll=2)

    @pl.when(t == nt - 1)
    def _():
        sout_ref[...] = st_ref[...].reshape(nb, HEADS, hd, hd)


def _mixer_d_group_body(seg, x_ref, cst_ref, beta_ref, dec_ref, z_ref, s0_ref, cw_ref, alog_ref, dtb_ref, ng_ref,
                        o_ref, cst_out_ref, sout_ref, conv, w_s, u_s, qg_s, kd_s, egl_s, delta_s, oq_s, stp_s, x_s):
    tb = x_ref.shape[1]
    nseq = tb // seg
    c = D_ROWS
    hd = HEAD_DIM
    pairs = c // 8
    hist_rows = D_CONV - 1

    lane_blocks = stp_s.shape[0]
    x = x_ref[0]
    stp_s[...] = jnp.zeros_like(stp_s)
    for cb in range(lane_blocks):
        lanes = slice(cb * 128, (cb + 1) * 128)
        x_s[cb] = x[:, lanes]
        for j in range(hist_rows):
            stp_s[cb, pl.ds(j, nseq, stride=seg), :] = cst_ref[:, j, lanes]
    stp = jnp.concatenate([stp_s[cb] for cb in range(lane_blocks)], axis=1)

    tpos = lax.broadcasted_iota(jnp.int32, (tb, 1), 0) & (seg - 1)
    y = x * cw_ref[3:4, :]
    for d in range(1, D_CONV):
        hist = stp if d == 3 else pltpu.roll(stp, tb - (3 - d), axis=0)
        y = y + jnp.where(tpos < d, hist, pltpu.roll(x, d, axis=0)) * cw_ref[3 - d:4 - d, :]
    conv[...] = _silu(y)
    for cb in range(lane_blocks):
        for j in range(hist_rows):
            cst_out_ref[:, j, cb * 128:(cb + 1) * 128] = x_s[cb, pl.ds(seg - hist_rows + j, nseq, stride=seg), :]

    a_neg = -jnp.exp(alog_ref[...])
    low = lax.broadcasted_iota(jnp.int32, (8, 1), 0) < seg

    for n in range(tb // c):
        rows = slice(n * c, (n + 1) * c)
        beta_all = jax.nn.sigmoid(beta_ref[0, rows, :])
        g_all = a_neg * _softplus(dec_ref[0, rows, :] + dtb_ref[...])
        w, u, qk, qg, kdec, egl = _delta_chunk_operands(seg, conv[rows, :], beta_all, g_all)
        w_s[...] = w
        u_s[...] = u
        qg_s[...] = qg
        kd_s[...] = kdec
        egl_s[...] = jnp.broadcast_to(egl, (HEADS * c, hd))

        def pair(p, carry):
            rp = pl.ds(pl.multiple_of(p * 8, 8), 8)
            h = p // pairs
            ja = n * (c // seg) + (p % pairs) * 2
            sa = s0_ref[ja, h]
            sb = s0_ref[ja + 1, h]
            lhs = jnp.concatenate([w_s[rp, :], qg_s[rp, :]], axis=0)
            xa = _dot(lhs, sa)
            xb = _dot(lhs, sb)
            delta = u_s[rp, :] - jnp.where(low, xa[:8], xb[:8])
            delta_s[rp, :] = delta
            oq_s[rp, :] = jnp.where(low, xa[8:], xb[8:])
            kd = kd_s[rp, :]
            e = egl_s[rp, :]
            sout_ref[ja, h] = e[seg - 1:seg, :] * sa + _dot_tn(jnp.where(low, kd, 0.0), delta)
            sout_ref[ja + 1, h] = e[2 * seg - 1:2 * seg, :] * sb + _dot_tn(jnp.where(low, 0.0, kd), delta)
            return carry

        lax.fori_loop(0, HEADS * pairs, pair, 0)
        o = oq_s[...] + _dot(qk, delta_s[...])
        o = _rms_rows(o) * ng_ref[...] * _silu(_stack_heads(z_ref[0, rows, :]))
        for h in range(HEADS):
            o_ref[0, rows, h * hd:(h + 1) * hd] = o[h * c:(h + 1) * c].astype(o_ref.dtype)


def mixer_d(pseq, conv_state, s0, conv_w, a_log, dt_bias, norm_g, *, tb, nb):
    bsz, t, _ = pseq.shape
    bw = BRANCH_WIDTH
    field = lambda off, w: pl.BlockSpec((nb, tb, w), lambda b, s: (b, s, off // w))
    state = pl.BlockSpec((nb, HEADS, HEAD_DIM, HEAD_DIM), lambda b, s: (b, 0, 0, 0))
    cstate = pl.BlockSpec((nb, D_CONV - 1, D_QKV), lambda b, s: (b, 0, 0))
    return pl.pallas_call(
        _mixer_d_body,
        grid=(bsz // nb, t // tb),
        in_specs=[field(OFF_D_QKV, D_QKV), field(OFF_D_BETA, 128), field(OFF_D_DECAY, 128), field(OFF_D_Z, bw),
                  cstate, state,
                  pl.BlockSpec((D_CONV, D_QKV), lambda b, s: (0, 0)),
                  pl.BlockSpec((1, 128), lambda b, s: (0, 0)),
                  pl.BlockSpec((1, 128), lambda b, s: (0, 0)),
                  pl.BlockSpec((1, HEAD_DIM), lambda b, s: (0, 0))],
        out_specs=[pl.BlockSpec((nb, tb, bw), lambda b, s: (b, s, 0)), cstate, state],
        out_shape=[jax.ShapeDtypeStruct((bsz, t, bw), BF16),
                   jax.ShapeDtypeStruct((bsz, D_CONV - 1, D_QKV), F32),
                   jax.ShapeDtypeStruct((bsz, HEADS, HEAD_DIM, HEAD_DIM), F32)],
        scratch_shapes=[pltpu.VMEM((nb, tb + 8, D_QKV), F32), pltpu.VMEM((nb, tb, D_QKV), F32),
                        pltpu.VMEM((nb * HEADS, HEAD_DIM, HEAD_DIM), F32)],
        compiler_params=_cparams(("parallel", "arbitrary")),
        name="mixer_d",
    )(pseq, pseq, pseq, pseq, conv_state, s0, conv_w, _lane_pad_heads(a_log), _lane_pad_heads(dt_bias),
      norm_g.reshape(1, HEAD_DIM))


def mixer_d_grouped(prows, conv_state, s0, conv_w, a_log, dt_bias, norm_g, prev_states, *, seg, tb):
    m = prows.shape[0]
    bsz = m // seg
    bw = BRANCH_WIDTH
    field = lambda off, w: pl.BlockSpec((1, tb, w), lambda s: (0, s, off // w))
    layer = 0 if prev_states is None else prev_states.shape[0]
    state = pl.BlockSpec((None, tb // seg, HEADS, HEAD_DIM, HEAD_DIM), lambda s: (layer, s, 0, 0, 0))
    cstate = pl.BlockSpec((tb // seg, D_CONV - 1, D_QKV), lambda s: (s, 0, 0))
    rows128 = pltpu.VMEM((HEADS * D_ROWS, HEAD_DIM), F32)
    lane_rows = pltpu.VMEM((D_QKV // 128, tb, 128), F32)
    n_in, state_out_pos = 10, 2
    wrapped, prev_specs, prev_inputs, state_out_spec, state_out_shape = _layer_stacked_state(
        functools.partial(_mixer_d_group_body, seg), prev_states, n_in + state_out_pos, tb // seg)
    o, cst, s_out = pl.pallas_call(
        wrapped,
        grid=(m // tb,),
        in_specs=prev_specs(s0) + [
                  field(OFF_D_QKV, D_QKV), cstate,
                  field(OFF_D_BETA, 128), field(OFF_D_DECAY, 128), field(OFF_D_Z, bw),
                  state,
                  pl.BlockSpec((D_CONV, D_QKV), lambda s: (0, 0)),
                  pl.BlockSpec((1, 128), lambda s: (0, 0)),
                  pl.BlockSpec((1, 128), lambda s: (0, 0)),
                  pl.BlockSpec((1, HEAD_DIM), lambda s: (0, 0))],
        out_specs=[pl.BlockSpec((1, tb, bw), lambda s: (0, s, 0)), cstate, state_out_spec(s0)],
        out_shape=[jax.ShapeDtypeStruct((1, m, bw), BF16),
                   jax.ShapeDtypeStruct((bsz, D_CONV - 1, D_QKV), F32),
                   state_out_shape(s0)],
        scratch_shapes=[pltpu.VMEM((tb, D_QKV), F32)] + [rows128] * 7 + [lane_rows] * 2,
        compiler_params=_cparams(("parallel",)),
        name="mixer_d_grouped",
    )(*prev_inputs, prows.reshape(1, m, -1), conv_state, prows.reshape(1, m, -1), prows.reshape(1, m, -1),
      prows.reshape(1, m, -1), s0, conv_w, _lane_pad_heads(a_log), _lane_pad_heads(dt_bias),
      norm_g.reshape(1, HEAD_DIM))
    return o.reshape(m, bw), cst, s_out


PACK_TILE = 512
PACK_SRC = ([4624, 5136, 5648, 0, 512, 1024, 1536, 2048, 2560, 3584, 4112, 6168, 3072, None]
            + [6680 + PACK_TILE * k for k in range(N_BRANCH * D_MODEL // PACK_TILE)])
PACK_SMALL_TILE = PACK_SRC.index(None)


def _pack_lookup(t, values):
    out = jnp.int32(0)
    for i, v in enumerate(values):
        out = jnp.where(t == i, jnp.int32(v), out)
    return out


PACK_EXTRA = 32


def _pack_w_in_body(a_ref, b_ref, tail_ref, small_ref, o_ref):
    t = pl.program_id(1)
    last = len(PACK_SRC) - 1
    shifts = [0 if s is None else s % PACK_TILE for s in PACK_SRC]
    assert max(shifts) <= PACK_EXTRA and all(s % 8 == 0 for s in shifts)

    def emit(shift, left_ref, right_ref):
        for c in range(a_ref.shape[1] // 128):
            lanes = slice(c * 128, (c + 1) * 128)
            w = left_ref[shift:, lanes]
            if shift:
                w = jnp.concatenate([w, right_ref[0:shift, lanes]], axis=0)
            o_ref[lanes, :] = w.T.astype(BF16)

    for shift in sorted(set(shifts)):
        tiles = [i for i, s in enumerate(shifts) if s == shift and i not in (PACK_SMALL_TILE, last)]
        pl.when(functools.reduce(jnp.logical_or, [t == i for i in tiles]))(
            functools.partial(emit, shift, a_ref, b_ref))
    pl.when(t == last)(functools.partial(emit, shifts[last], a_ref, tail_ref))
    pl.when(t == PACK_SMALL_TILE)(functools.partial(emit, 0, small_ref, None))


def pack_w_in(w_in):
    layers, k, n_src = w_in.shape
    w_t = jnp.swapaxes(w_in, 1, 2)
    bases = [0 if s is None else s - s % PACK_TILE for s in PACK_SRC]
    extra = [(b + PACK_TILE) // PACK_EXTRA for b in bases]
    full_blocks = n_src // PACK_EXTRA
    assert all(e < full_blocks for e in extra[:-1]) and extra[-1] == full_blocks
    extra[-1] = 0
    z = lambda n: jnp.zeros((layers, n, k), w_in.dtype)
    tail = jnp.concatenate([w_t[:, full_blocks * PACK_EXTRA:], z(PACK_EXTRA - n_src % PACK_EXTRA)], axis=1)
    small = jnp.concatenate([w_t[:, 4096:4112], z(112),
                             w_t[:, 6160:6164], z(124),
                             w_t[:, 6164:6168], z(252)], axis=1)
    return pl.pallas_call(
        _pack_w_in_body,
        grid=(layers, len(PACK_SRC)),
        in_specs=[pl.BlockSpec((None, PACK_TILE, k),
                               lambda l, t: (l, _pack_lookup(t, [b // PACK_TILE for b in bases]), 0)),
                  pl.BlockSpec((None, PACK_EXTRA, k), lambda l, t: (l, _pack_lookup(t, extra), 0)),
                  pl.BlockSpec((None, PACK_EXTRA, k), lambda l, t: (l, 0, 0)),
                  pl.BlockSpec((None, PACK_TILE, k), lambda l, t: (l, 0, 0))],
        out_specs=pl.BlockSpec((None, k, PACK_TILE), lambda l, t: (l, 0, t)),
        out_shape=jax.ShapeDtypeStruct((layers, k, N_PROJ), BF16),
        compiler_params=_cparams(("parallel", "arbitrary")),
        name="pack_w_in",
    )(w_t, w_t, tail, small)


def _trunk(x, p, s_hgrn, s_gla, s_delta, c_dconv, c_fconv, wts, lbs, *, sample):
    bsz, t, _ = x.shape
    m = bsz * t
    tm = m if sample else 1024
    h = x.reshape(m, D_MODEL)
    outs = {k: [] for k in ("hgrn", "gla", "delta", "dconv", "fconv", "v")}
    s_b = s_c = s_d = None
    for l in range(DEPTH):
        w = wts[l]
        proj, gates = in_proj(h, w["g_mix"], w["w_in_layers"], l, tm=tm, tn=1024)
        o_a, v_a = mixer_a(proj, w["a_ln_g"], w["a_ln_b"], w["a_w_mix"], w["a_bias_rows"])
        st_b, st_c, st_d = (s_hgrn, s_gla, s_delta) if sample else (s_hgrn[l], s_gla[l], s_delta[l])
        b_args = (lbs[l], w["b_norm_g"], st_b)
        c_args = (w["c_w_a2"], w["c_b_a"], w["c_norm_g"], st_c)
        d_args = (c_dconv[l], st_d, w["d_conv_w"], w["d_a_log"], w["d_dt_bias"], w["d_norm_g"])
        if sample:
            o_b, s_b = mixer_b_grouped(proj, *b_args, s_b, seg=t, tb=D_ROWS)
            o_c, s_c = mixer_c_grouped(proj, *c_args, s_c, seg=t, tb=D_ROWS)
            o_d, nb_d, s_d = mixer_d_grouped(proj, *d_args, s_d, seg=t, tb=D_ROWS)
        else:
            pseq = proj.reshape(bsz, t, N_SEQ_COLS)
            flat = lambda o: o.reshape(m, BRANCH_WIDTH)
            o_b, s_b = mixer_b(pseq, *b_args, t_valid=t, tb=256, nb=1, unroll=1)
            o_c, s_c = mixer_c(pseq, *c_args, t_valid=t, tb=256, nb=1, unroll=1)
            o_d, nb_d, s_d = mixer_d(pseq, *d_args, tb=256, nb=2 if bsz % 2 == 0 else 1)
            o_b, o_c, o_d = flat(o_b), flat(o_c), flat(o_d)
        merged = merge_branches((o_a, o_b, o_c, o_d), gates, w["w_branch"], tm=tm, tn=512)
        h = matmul_residual(merged, w["w_out_layers"], l, h, tm=tm, tn=1024)

        act, nb_f = ffn_up_gate(h, w["g_ffn"], w["w_ffn_up_layers"], l, c_fconv[l], w["ffn_conv_w"],
                                w["ffn_conv_b"], seq_len=t, tm=tm, tn=512)
        h = matmul_residual(act, w["w_ffn_down_layers"], l, h, tm=min(tm, 512), tn=512)

        h = ple_update(h, p[l].reshape(m, PLE_DIM), w["g_ple"], w["w_ple_gate"], w["w_ple_proj"], tm=tm, tn=512)

        if not sample:
            outs["hgrn"].append(s_b)
            outs["gla"].append(s_c)
            outs["delta"].append(s_d)
        outs["dconv"].append(nb_d)
        outs["fconv"].append(nb_f)
        outs["v"].append(v_a.reshape(bsz, t, BRANCH_WIDTH))
    y = final_norm(h, wts[0]["g_final"], tm=min(tm, 512)).reshape(bsz, t, D_MODEL)
    states = (s_b, s_c, s_d) if sample else tuple(jnp.stack(outs[k]) for k in ("hgrn", "gla", "delta"))
    return (y,) + states + tuple(jnp.stack(outs[k]) for k in ("dconv", "fconv", "v"))


def kernel(x_prompt, x_sample, state_hgrn, state_gla, state_delta, state_delta_conv, state_ffn_conv, p_prompt, p_sample, g_mix, w_in, a_ln_g, a_ln_b, a_w_s, a_b_s, b_lb, b_norm_g, c_w_a2, c_b_a, c_norm_g, d_conv_w, d_a_log, d_dt_bias, d_norm_g, w_branch, w_out, g_ffn, w_ffn_up, ffn_conv_w, ffn_conv_b, w_ffn_down, g_ple, w_ple_gate, w_ple_proj, g_final):
    bp, t_p, _ = x_prompt.shape
    bs, t_s, _ = x_sample.shape
    sm = jax.nn.softmax(b_lb.astype(F32), axis=0)
    lbs = jnp.cumsum(sm, axis=0) - sm[0]

    w_in_packed = pack_w_in(w_in)
    shared = [dict(
        g_mix=g_mix[l], w_in_layers=w_in_packed, a_ln_g=a_ln_g[l], a_ln_b=a_ln_b[l],
        b_norm_g=b_norm_g[l], c_w_a2=c_w_a2[l], c_b_a=c_b_a[l], c_norm_g=c_norm_g[l],
        d_conv_w=d_conv_w[l], d_a_log=d_a_log[l], d_dt_bias=d_dt_bias[l], d_norm_g=d_norm_g[l],
        w_branch=w_branch[l].astype(BF16), w_out_layers=w_out, g_ffn=g_ffn[l],
        w_ffn_up_layers=w_ffn_up, ffn_conv_w=ffn_conv_w[l], ffn_conv_b=ffn_conv_b[l],
        w_ffn_down_layers=w_ffn_down, g_ple=g_ple[l], w_ple_gate=w_ple_gate[l].astype(BF16),
        w_ple_proj=w_ple_proj[l].astype(BF16), g_final=g_final) for l in range(DEPTH)]

    def layer_weights(l, sample):
        if sample:
            seqs = A_CHUNK // t_s
            idx = jnp.arange(A_CHUNK) // t_s
            same_seq = idx[:, None] == idx[None, :]
            w_mix = jnp.where(same_seq, jnp.tile(a_w_s[l, :, :t_s, :t_s], (1, seqs, seqs)), 0.0)
            bias_rows = jnp.tile(a_b_s[l, :, :t_s].T, (seqs, 1))
        else:
            w_mix = a_w_s[l]
            bias_rows = a_b_s[l].T
        return dict(shared[l], a_w_mix=w_mix, a_bias_rows=bias_rows)

    dt = x_prompt.dtype
    zeros = lambda *s: jnp.zeros((DEPTH, bp) + s, dt)
    out_p = _trunk(x_prompt, p_prompt, zeros(HEADS, HEAD_DIM, HEAD_DIM), zeros(HEADS, C_KDIM, HEAD_DIM),
                   zeros(HEADS, HEAD_DIM, HEAD_DIM), zeros(D_CONV - 1, D_QKV), zeros(FFN_CONV - 1, FFN_DIM),
                   [layer_weights(l, False) for l in range(DEPTH)], lbs, sample=False)
    out_s = _trunk(x_sample, p_sample, state_hgrn, state_gla, state_delta, state_delta_conv, state_ffn_conv,
                   [layer_weights(l, True) for l in range(DEPTH)], lbs, sample=True)
    y_p, hgrn_p, gla_p, delta_p, dconv_p, fconv_p, _ = out_p
    y_s, hgrn_s, gla_s, delta_s, dconv_s, fconv_s, v_s = out_s
    return (y_p, y_s, hgrn_p, hgrn_s, gla_p, gla_s, delta_p, delta_s,
            dconv_p, dconv_s, fconv_p, fconv_s, v_s)
```

```python
import functools

import jax
import jax.numpy as jnp
from jax import lax
from jax.experimental import pallas as pl
from jax.experimental.pallas import tpu as pltpu

F32 = jnp.float32
BF16 = jnp.bfloat16

D_MODEL = 2048
DEPTH = 2
PLE_DIM = 256
EPS = 1e-6
F_TINY = 1e-30
N_BRANCH = 4
BRANCH_WIDTH = D_MODEL // 4
A_GROUPS = 4
A_CHUNK = 128
HEADS = 4
HEAD_DIM = 128
C_KDIM = 64
C_RANK = 16
C_TAU = 16.0
D_CONV = 4
D_QKV = 3 * BRANCH_WIDTH
FFN_DIM = 5632
FFN_CONV = 3
GLA_CHUNK = 16
LOG2E = 1.4426950408889634

OFF_D_QKV = 0
OFF_A_U = 1536
OFF_A_V = 2048
OFF_B_Q = 2560
OFF_B_F = 3072
OFF_B_I = 3584
OFF_B_G = 4096
OFF_C_V = 4608
OFF_C_R = 5120
OFF_D_Z = 5632
OFF_C_Q = 6144
OFF_C_K = 6400
OFF_C_LR = 6656
OFF_D_BETA = 6784
OFF_D_DECAY = 6912
OFF_GATES = 7168
N_SEQ_COLS = 7168
N_PROJ = OFF_GATES + N_BRANCH * D_MODEL

VMEM_LIMIT = 56 * 1024 * 1024


def _cparams(sem):
    return pltpu.CompilerParams(dimension_semantics=sem, vmem_limit_bytes=VMEM_LIMIT)


def _gelu(x):
    return 0.5 * x * (1.0 + jnp.tanh(0.7978845608028654 * (x + 0.044715 * (x * x * x))))


def _silu(x):
    return x * jax.nn.sigmoid(x)


def _softplus(x):
    return jnp.maximum(x, 0.0) + jnp.log1p(jnp.exp(-jnp.abs(x)))


def _log_sigmoid(x):
    return -_softplus(-x)


def _rms_rows(x):
    return x * lax.rsqrt(jnp.mean(x * x, axis=-1, keepdims=True) + EPS)


def _dot(a, b):
    return jnp.dot(a, b, preferred_element_type=F32)


def _dot_nt(a, b):
    return lax.dot_general(a, b, (((1,), (1,)), ((), ())), preferred_element_type=F32)


def _dot_tn(a, b):
    return lax.dot_general(a, b, (((0,), (0,)), ((), ())), preferred_element_type=F32)


def _cumsum_rows(x):
    n = x.shape[0]
    row = lax.broadcasted_iota(jnp.int32, x.shape, 0)
    sh = 1
    while sh < n:
        x = x + jnp.where(row >= sh, pltpu.roll(x, sh, axis=0), 0.0)
        sh *= 2
    return x


NORM_ROWS = 128


def _norm_to_scratch(x_ref, g_ref, a_scr):
    def slab(n, carry):
        rows = pl.ds(pl.multiple_of(n * NORM_ROWS, NORM_ROWS), NORM_ROWS)
        a_scr[rows, :] = (_rms_rows(x_ref[rows, :]) * g_ref[...]).astype(BF16)
        return carry

    lax.fori_loop(0, x_ref.shape[0] // NORM_ROWS, slab, 0)


def _matmul_residual_body(x_ref, w_ref, r_ref, o_ref, w_scr):
    @pl.when(pl.program_id(1) == 0)
    def _():
        def slab(n, carry):
            rows = pl.ds(pl.multiple_of(n * NORM_ROWS, NORM_ROWS), NORM_ROWS)
            w_scr[rows, :] = w_ref[rows, :].astype(BF16)
            return carry

        lax.fori_loop(0, w_ref.shape[0] // NORM_ROWS, slab, 0)

    o_ref[...] = r_ref[...] + _dot(x_ref[...], w_scr[...])


def matmul_residual(x, w_layers, layer, res, *, tm, tn):
    m, k = x.shape
    n = w_layers.shape[2]
    return pl.pallas_call(
        _matmul_residual_body,
        grid=(n // tn, m // tm),
        in_specs=[pl.BlockSpec((tm, k), lambda j, i: (i, 0)),
                  pl.BlockSpec((None, k, tn), lambda j, i: (layer, 0, j)),
                  pl.BlockSpec((tm, tn), lambda j, i: (i, j))],
        out_specs=pl.BlockSpec((tm, tn), lambda j, i: (i, j)),
        out_shape=jax.ShapeDtypeStruct((m, n), F32),
        scratch_shapes=[pltpu.VMEM((k, tn), BF16)],
        compiler_params=_cparams(("parallel", "arbitrary")),
        name="matmul_residual",
    )(x, w_layers, res)


def _in_proj_body(n_seq_tiles, x_ref, g_ref, w_ref, seq_ref, gate_ref, a_scr):
    j = pl.program_id(1)

    @pl.when(j == 0)
    def _():
        _norm_to_scratch(x_ref, g_ref, a_scr)

    acc = _dot(a_scr[...], w_ref[...])

    @pl.when(j < n_seq_tiles)
    def _():
        seq_ref[...] = acc

    @pl.when(j >= n_seq_tiles)
    def _():
        gate_ref[...] = acc.astype(gate_ref.dtype)


def in_proj(x, g, w_layers, layer, *, tm, tn):
    m, k = x.shape
    n_seq = N_SEQ_COLS // tn
    n_gate = N_BRANCH * D_MODEL // tn
    return pl.pallas_call(
        functools.partial(_in_proj_body, n_seq),
        grid=(m // tm, n_seq + n_gate),
        in_specs=[pl.BlockSpec((tm, k), lambda i, j: (i, 0)),
                  pl.BlockSpec((1, k), lambda i, j: (0, 0)),
                  pl.BlockSpec((None, k, tn), lambda i, j: (layer, 0, j))],
        out_specs=[pl.BlockSpec((tm, tn), lambda i, j: (i, jnp.minimum(j, n_seq - 1))),
                   pl.BlockSpec((tm, tn), lambda i, j: (i, jnp.maximum(j - n_seq, 0)))],
        out_shape=[jax.ShapeDtypeStruct((m, N_SEQ_COLS), F32),
                   jax.ShapeDtypeStruct((m, N_BRANCH * D_MODEL), BF16)],
        scratch_shapes=[pltpu.VMEM((tm, k), BF16)],
        compiler_params=_cparams(("parallel", "arbitrary")),
        name="in_proj",
    )(x, g.reshape(1, k), w_layers)


def _merge_body(oa_ref, ob_ref, oc_ref, od_ref, ga_ref, gb_ref, gc_ref, gd_ref, w_ref, o_ref):
    acc = None
    for b, (o_b, g_b) in enumerate(((oa_ref, ga_ref), (ob_ref, gb_ref), (oc_ref, gc_ref), (od_ref, gd_ref))):
        term = jax.nn.sigmoid(g_b[...].astype(F32)) * _dot(o_b[...], w_ref[b])
        acc = term if acc is None else acc + term
    o_ref[...] = acc.astype(o_ref.dtype)


def merge_branches(o_branches, gates, w_branch, *, tm, tn):
    m = gates.shape[0]
    gate_specs = [pl.BlockSpec((tm, tn), functools.partial(
        lambda i, j, b: (i, b * (D_MODEL // tn) + j), b=b)) for b in range(N_BRANCH)]
    return pl.pallas_call(
        _merge_body,
        grid=(m // tm, D_MODEL // tn),
        in_specs=[pl.BlockSpec((tm, BRANCH_WIDTH), lambda i, j: (i, 0))] * N_BRANCH + gate_specs
        + [pl.BlockSpec((N_BRANCH, BRANCH_WIDTH, tn), lambda i, j: (0, 0, j))],
        out_specs=pl.BlockSpec((tm, tn), lambda i, j: (i, j)),
        out_shape=jax.ShapeDtypeStruct((m, D_MODEL), BF16),
        compiler_params=_cparams(("parallel", "arbitrary")),
        name="merge_branches",
    )(*o_branches, gates, gates, gates, gates, w_branch)


def _ple_body(x_ref, xres_ref, p_ref, g_ref, wg_ref, wp_ref, o_ref, a_scr, p_scr):
    @pl.when(pl.program_id(1) == 0)
    def _():
        _norm_to_scratch(x_ref, g_ref, a_scr)
        p_scr[...] = p_ref[...].astype(BF16)

    gate = jax.nn.sigmoid(_dot(a_scr[...], wg_ref[...]))
    o_ref[...] = xres_ref[...] + _dot(p_scr[...], wp_ref[...]) * gate


def ple_update(x, p, g, w_gate, w_proj, *, tm, tn):
    m, k = x.shape
    return pl.pallas_call(
        _ple_body,
        grid=(m // tm, D_MODEL // tn),
        in_specs=[pl.BlockSpec((tm, k), lambda i, j: (i, 0)),
                  pl.BlockSpec((tm, tn), lambda i, j: (i, j)),
                  pl.BlockSpec((tm, PLE_DIM), lambda i, j: (i, 0)),
                  pl.BlockSpec((1, k), lambda i, j: (0, 0)),
                  pl.BlockSpec((k, tn), lambda i, j: (0, j)),
                  pl.BlockSpec((PLE_DIM, tn), lambda i, j: (0, j))],
        out_specs=pl.BlockSpec((tm, tn), lambda i, j: (i, j)),
        out_shape=jax.ShapeDtypeStruct((m, D_MODEL), F32),
        scratch_shapes=[pltpu.VMEM((tm, k), BF16), pltpu.VMEM((tm, PLE_DIM), BF16)],
        compiler_params=_cparams(("parallel", "arbitrary")),
        name="ple_update",
    )(x, x, p, g.reshape(1, k), w_gate, w_proj)


def _final_norm_body(x_ref, g_ref, o_ref):
    def slab(n, carry):
        rows = pl.ds(pl.multiple_of(n * NORM_ROWS, NORM_ROWS), NORM_ROWS)
        o_ref[rows, :] = _rms_rows(x_ref[rows, :]) * g_ref[...]
        return carry

    lax.fori_loop(0, x_ref.shape[0] // NORM_ROWS, slab, 0)


def final_norm(x, g, *, tm):
    m, k = x.shape
    return pl.pallas_call(
        _final_norm_body,
        grid=(m // tm,),
        in_specs=[pl.BlockSpec((tm, k), lambda i: (i, 0)), pl.BlockSpec((1, k), lambda i: (0, 0))],
        out_specs=pl.BlockSpec((tm, k), lambda i: (i, 0)),
        out_shape=jax.ShapeDtypeStruct((m, k), F32),
        compiler_params=_cparams(("parallel",)),
        name="final_norm",
    )(x, g.reshape(1, k))


FFN_SUB = 256


def _cast_rows_to_bf16(src_ref, dst_ref):
    def slab(n, carry):
        rows = pl.ds(pl.multiple_of(n * NORM_ROWS, NORM_ROWS), NORM_ROWS)
        dst_ref[rows, :] = src_ref[rows, :].astype(BF16)
        return carry

    lax.fori_loop(0, src_ref.shape[0] // NORM_ROWS, slab, 0)


def _ffn_up_long_body(tiles_per_seq, a_ref, wg_ref, wu_ref, st_ref, cw_ref, cb_ref, o_ref, st_out_ref,
                      wg_s, wu_s, ext):
    i = pl.program_id(1)
    tm = a_ref.shape[0]

    @pl.when(i == 0)
    def _():
        _cast_rows_to_bf16(wg_ref, wg_s)
        _cast_rows_to_bf16(wu_ref, wu_s)

    @pl.when(i % tiles_per_seq == 0)
    def _():
        ext[6:8, :] = st_ref[0]

    for r in range(0, tm, FFN_SUB):
        a = a_ref[r:r + FFN_SUB, :]
        ext[8 + r:8 + r + FFN_SUB, :] = _dot(a, wg_s[...])
        conv = (ext[6 + r:6 + r + FFN_SUB, :] * cw_ref[0:1, :] + ext[7 + r:7 + r + FFN_SUB, :] * cw_ref[1:2, :]
                + ext[8 + r:8 + r + FFN_SUB, :] * cw_ref[2:3, :])
        o_ref[r:r + FFN_SUB, :] = (_gelu(conv + cb_ref[...]) * _dot(a, wu_s[...])).astype(o_ref.dtype)

    last = ext[6 + tm:8 + tm, :]
    ext[6:8, :] = last
    st_out_ref[0] = last


def _ffn_up_group_body(seg, a_ref, wg_ref, wu_ref, st_ref, cw_ref, cb_ref, o_ref, st_out_ref, wg_s, wu_s, stp, fg_s):
    tm = a_ref.shape[0]
    nseq = tm // seg
    hist = FFN_CONV - 1

    @pl.when(pl.program_id(1) == 0)
    def _():
        _cast_rows_to_bf16(wg_ref, wg_s)
        _cast_rows_to_bf16(wu_ref, wu_s)

    lane_blocks = stp.shape[0]
    stp[...] = jnp.zeros_like(stp)
    for j in range(hist):
        for c in range(lane_blocks):
            stp[c, pl.ds(j, nseq, stride=seg), :] = st_ref[:, j, c * 128:(c + 1) * 128]

    a = a_ref[...]
    fg = _dot(a, wg_s[...])
    for c in range(lane_blocks):
        fg_s[c] = fg[:, c * 128:(c + 1) * 128]
    h = jnp.concatenate([stp[c] for c in range(lane_blocks)], axis=1)
    tpos = lax.broadcasted_iota(jnp.int32, (tm, 1), 0) & (seg - 1)
    lag1 = jnp.where(tpos < 1, pltpu.roll(h, tm - 1, axis=0), pltpu.roll(fg, 1, axis=0))
    lag2 = jnp.where(tpos < 2, h, pltpu.roll(fg, 2, axis=0))
    conv = lag2 * cw_ref[0:1, :] + lag1 * cw_ref[1:2, :] + fg * cw_ref[2:3, :]
    o_ref[...] = (_gelu(conv + cb_ref[...]) * _dot(a, wu_s[...])).astype(o_ref.dtype)
    for j in range(hist):
        for c in range(lane_blocks):
            st_out_ref[:, j, c * 128:(c + 1) * 128] = fg_s[c, pl.ds(seg - hist + j, nseq, stride=seg), :]


def _rms_bf16_body(x_ref, g_ref, o_ref):
    _norm_to_scratch(x_ref, g_ref, o_ref)


def rms_bf16(x, g, *, tm):
    m, k = x.shape
    return pl.pallas_call(
        _rms_bf16_body,
        grid=(m // tm,),
        in_specs=[pl.BlockSpec((tm, k), lambda i: (i, 0)), pl.BlockSpec((1, k), lambda i: (0, 0))],
        out_specs=pl.BlockSpec((tm, k), lambda i: (i, 0)),
        out_shape=jax.ShapeDtypeStruct((m, k), BF16),
        compiler_params=_cparams(("parallel",)),
        name="rms_bf16",
    )(x, g.reshape(1, k))


def ffn_up_gate(x, g, w_up_layers, layer, state, conv_w, conv_b, *, seq_len, tm, tn):
    m, k = x.shape
    bsz = m // seq_len
    ncol = FFN_DIM // tn
    a = rms_bf16(x, g, tm=min(tm, 512))
    common_in = [pl.BlockSpec((tm, k), lambda j, i: (i, 0)),
                 pl.BlockSpec((None, k, tn), lambda j, i: (layer, 0, j)),
                 pl.BlockSpec((None, k, tn), lambda j, i: (layer, 0, ncol + j))]
    conv_in = [pl.BlockSpec((FFN_CONV, tn), lambda j, i: (0, j)), pl.BlockSpec((1, tn), lambda j, i: (0, j))]
    act_spec = pl.BlockSpec((tm, tn), lambda j, i: (i, j))
    act_shape = jax.ShapeDtypeStruct((m, FFN_DIM), BF16)
    w_scr = [pltpu.VMEM((k, tn), BF16)] * 2
    if seq_len >= tm:
        tps = seq_len // tm
        act, tile_last = pl.pallas_call(
            functools.partial(_ffn_up_long_body, tps),
            grid=(ncol, m // tm),
            in_specs=common_in + [pl.BlockSpec((1, FFN_CONV - 1, tn), lambda j, i: (i // tps, 0, j))] + conv_in,
            out_specs=[act_spec, pl.BlockSpec((1, FFN_CONV - 1, tn), lambda j, i: (i, 0, j))],
            out_shape=[act_shape, jax.ShapeDtypeStruct((m // tm, FFN_CONV - 1, FFN_DIM), F32)],
            scratch_shapes=w_scr + [pltpu.VMEM((tm + 8, tn), F32)],
            compiler_params=_cparams(("parallel", "arbitrary")),
            name="ffn_up_gate",
        )(a, w_up_layers, w_up_layers, state, conv_w, conv_b.reshape(1, FFN_DIM))
        return act, tile_last[tps - 1::tps]
    st_spec = pl.BlockSpec((tm // seq_len, FFN_CONV - 1, tn), lambda j, i: (i, 0, j))
    return pl.pallas_call(
        functools.partial(_ffn_up_group_body, seq_len),
        grid=(ncol, m // tm),
        in_specs=common_in + [st_spec] + conv_in,
        out_specs=[act_spec, st_spec],
        out_shape=[act_shape, jax.ShapeDtypeStruct((bsz, FFN_CONV - 1, FFN_DIM), F32)],
        scratch_shapes=w_scr + [pltpu.VMEM((tn // 128, tm, 128), F32)] * 2,
        compiler_params=_cparams(("parallel", "arbitrary")),
        name="ffn_up_gate_grouped",
    )(a, w_up_layers, w_up_layers, state, conv_w, conv_b.reshape(1, FFN_DIM))


def _mixer_a_body(u_ref, v_ref, lng_ref, lnb_ref, w_ref, bias_ref, o_ref, vout_ref):
    u = _gelu(u_ref[...])
    v = _gelu(v_ref[...])
    vc = v - jnp.mean(v, axis=-1, keepdims=True)
    var = jnp.mean(vc * vc, axis=-1, keepdims=True)
    vn = vc * lax.rsqrt(var + EPS) * lng_ref[...] + lnb_ref[...]
    vout_ref[...] = vn
    n = w_ref.shape[1]
    causal = lax.broadcasted_iota(jnp.int32, (n, n), 0) >= lax.broadcasted_iota(jnp.int32, (n, n), 1)
    gd = BRANCH_WIDTH // A_GROUPS
    for g in range(A_GROUPS):
        w = jnp.where(causal, w_ref[g], 0.0)
        mixed = _dot(w, vn[:, g * gd:(g + 1) * gd]) + bias_ref[:, g:g + 1]
        o_ref[:, g * gd:(g + 1) * gd] = (u[:, g * gd:(g + 1) * gd] * mixed).astype(o_ref.dtype)


def mixer_a(proj, ln_g, ln_b, w_mix, bias_rows):
    m = proj.shape[0]
    r = A_CHUNK
    bw = BRANCH_WIDTH
    return pl.pallas_call(
        _mixer_a_body,
        grid=(m // r,),
        in_specs=[pl.BlockSpec((r, bw), lambda i: (i, OFF_A_U // bw)),
                  pl.BlockSpec((r, bw), lambda i: (i, OFF_A_V // bw)),
                  pl.BlockSpec((1, bw), lambda i: (0, 0)),
                  pl.BlockSpec((1, bw), lambda i: (0, 0)),
                  pl.BlockSpec((A_GROUPS, r, r), lambda i: (0, 0, 0)),
                  pl.BlockSpec((r, A_GROUPS), lambda i: (0, 0))],
        out_specs=[pl.BlockSpec((r, bw), lambda i: (i, 0)), pl.BlockSpec((r, bw), lambda i: (i, 0))],
        out_shape=[jax.ShapeDtypeStruct((m, bw), BF16), jax.ShapeDtypeStruct((m, bw), F32)],
        compiler_params=_cparams(("parallel",)),
        name="mixer_a",
    )(proj, proj, ln_g.reshape(1, bw), ln_b.reshape(1, bw), w_mix, bias_rows)


def _gla_head_chunk(q, k, v, g, n_src):
    c = q.shape[0]
    g2 = _cumsum_rows(g) * LOG2E
    n_tiles = c // 8
    row8 = lax.broadcasted_iota(jnp.int32, (8, 1), 0)
    q_t = [q[i * 8:(i + 1) * 8] for i in range(n_tiles)]
    g_t = [g2[i * 8:(i + 1) * 8] for i in range(n_tiles)]
    o_t = [jnp.zeros((8, v.shape[1]), F32) for _ in range(n_tiles)]
    for s in range(n_src):
        for i in range(s // 8, n_tiles):
            a = jnp.sum(q_t[i] * jnp.exp2(g_t[i] - g2[s:s + 1, :]) * k[s:s + 1, :], axis=-1, keepdims=True)
            if i == s // 8:
                a = jnp.where(row8 >= s - 8 * i, a, 0.0)
            o_t[i] = o_t[i] + a * v[s:s + 1, :]
    o = jnp.concatenate(o_t, axis=0) if n_tiles > 1 else o_t[0]
    gl = g2[c - 1:c, :]
    return o, q * jnp.exp2(g2), k * jnp.exp2(gl - g2), jnp.exp2(gl)


def _block_diag(xs):
    z = jnp.zeros_like(xs[0])
    return jnp.concatenate(
        [jnp.concatenate([x if j == i else z for j in range(len(xs))], axis=1) for i, x in enumerate(xs)], axis=0)


def _gla_state_dots(qgs, kds, vs, es, st_ref, bi):
    st = st_ref[bi]
    o_inter = _dot_nt(_block_diag(qgs), st)
    st_ref[bi] = st * jnp.concatenate(es, axis=1) + _dot_tn(jnp.concatenate(vs, axis=0), _block_diag(kds))
    return o_inter


def _gla_state_step(parts, vs, st_ref, bi):
    c = vs[0].shape[0]
    o_inter = _gla_state_dots([p[1] for p in parts], [p[2] for p in parts], vs, [p[3] for p in parts], st_ref, bi)
    return [p[0] + o_inter[h * c:(h + 1) * c] for h, p in enumerate(parts)]


def _gla_pair_tile(q, k, v, g, seg):
    g2 = _segment_cumsum(g, seg) * LOG2E
    tpos = lax.broadcasted_iota(jnp.int32, (8, 1), 0) & (seg - 1)
    o = jnp.sum(q * k, axis=-1, keepdims=True) * v
    for d in range(1, seg):
        a = jnp.sum(q * jnp.exp2(g2 - pltpu.roll(g2, d, axis=0)) * pltpu.roll(k, d, axis=0), axis=-1, keepdims=True)
        o = o + jnp.where(tpos >= d, a, 0.0) * pltpu.roll(v, d, axis=0)
    first = lax.broadcasted_iota(jnp.int32, (8, 1), 0) < seg
    gl_a, gl_b = g2[seg - 1:seg, :], g2[2 * seg - 1:2 * seg, :]
    kd = k * jnp.exp2(jnp.where(first, gl_a, gl_b) - g2)
    return o, q * jnp.exp2(g2), kd, jnp.exp2(gl_a), jnp.exp2(gl_b)


def _lane_broadcast_column(row):
    hi = row.astype(BF16).astype(F32)
    r1 = row - hi
    mid = r1.astype(BF16).astype(F32)
    lo = r1 - mid
    r = lax.broadcasted_iota(jnp.int32, (8, 1), 0)
    pieces = jnp.where(r == 0, hi, jnp.where(r == 1, mid, jnp.where(r == 2, lo, 0.0)))
    return _dot_tn(pieces.astype(BF16), jnp.ones((8, 128), BF16))


def _gla_state_dots_natural(qgs, kds, vs, es, st_ref, bi):
    st = st_ref[bi]
    o_inter = _dot(_block_diag(qgs), st)
    decay = _lane_broadcast_column(jnp.concatenate(es, axis=1))
    st_ref[bi] = decay * st + _dot_tn(_block_diag(kds), jnp.concatenate(vs, axis=0))
    return o_inter


def _gla_pair_step(tiles, vs, st_ref, p, seg):
    first = lax.broadcasted_iota(jnp.int32, (8, 1), 0) < seg
    qgs = [t[1] for t in tiles]
    oi_a = _gla_state_dots_natural(qgs, [jnp.where(first, t[2], 0.0) for t in tiles], vs, [t[3] for t in tiles],
                                   st_ref, 2 * p)
    oi_b = _gla_state_dots_natural(qgs, [jnp.where(first, 0.0, t[2]) for t in tiles], vs, [t[4] for t in tiles],
                                   st_ref, 2 * p + 1)
    return [t[0] + jnp.where(first, oi_a[h * 8:(h + 1) * 8], oi_b[h * 8:(h + 1) * 8]) for h, t in enumerate(tiles)]


def _load_state_natural(s0_ref, st_ref, kdim):
    def per_seq(bi, carry):
        for h in range(HEADS):
            s = s0_ref[bi, h]
            if kdim < HEAD_DIM:
                s = jnp.concatenate([s, jnp.zeros((HEAD_DIM - kdim, HEAD_DIM), F32)], axis=0)
            st_ref[bi, h * HEAD_DIM:(h + 1) * HEAD_DIM, :] = s
        return carry

    lax.fori_loop(0, s0_ref.shape[0], per_seq, 0)


def _store_state_natural(st_ref, sout_ref, kdim):
    def per_seq(bi, carry):
        for h in range(HEADS):
            sout_ref[bi, h] = st_ref[bi, h * HEAD_DIM:h * HEAD_DIM + kdim, :]
        return carry

    lax.fori_loop(0, sout_ref.shape[0], per_seq, 0)


def _load_state_t(s0_ref, st_ref, kdim):
    def per_seq(bi, carry):
        for h in range(HEADS):
            s = s0_ref[bi, h]
            if kdim < HEAD_DIM:
                s = jnp.concatenate([s, jnp.zeros((HEAD_DIM - kdim, HEAD_DIM), F32)], axis=0)
            st_ref[bi, :, h * HEAD_DIM:(h + 1) * HEAD_DIM] = s.T
        return carry

    lax.fori_loop(0, s0_ref.shape[0], per_seq, 0)


def _store_state_t(st_ref, sout_ref, kdim):
    def per_seq(bi, carry):
        for h in range(HEADS):
            sout_ref[bi, h] = st_ref[bi, :, h * HEAD_DIM:(h + 1) * HEAD_DIM].T[:kdim, :]
        return carry

    lax.fori_loop(0, sout_ref.shape[0], per_seq, 0)


def _seq_chunk_loop(nb, n_chunks, unroll, fn):
    def step(it, carry):
        n = 0 if n_chunks == 1 else it % n_chunks
        bj = 0 if nb == unroll else it // n_chunks
        for u in range(unroll):
            fn(bj * unroll + u, n)
        return carry

    lax.fori_loop(0, (nb // unroll) * n_chunks, step, 0, unroll=8)


def _chunk_rows(n, c):
    if isinstance(n, int):
        return n * c, pl.ds(n * c, c)
    r0 = pl.multiple_of(n * c, c)
    return r0, pl.ds(r0, c)


def _valid_rows(t_valid, tb, r0, c):
    return pl.program_id(1) * tb + r0 + lax.broadcasted_iota(jnp.int32, (c, 1), 0) < t_valid


def _mixer_b_body(t_valid, t_pad, unroll, q_ref, f_ref, i_ref, og_ref, lb_ref, ng_ref, s0_ref, o_ref, sout_ref,
                  st_ref):
    nb, tb = q_ref.shape[0], q_ref.shape[1]
    c = min(GLA_CHUNK, tb)
    kd = HEAD_DIM
    padded = t_valid < t_pad
    n_src = t_valid if padded else c

    @pl.when(pl.program_id(1) == 0)
    def _():
        _load_state_t(s0_ref, st_ref, kd)

    def chunk(bi, n):
        r0, rows = _chunk_rows(n, c)
        parts, vs = [], []
        for h in range(HEADS):
            cols = slice(h * kd, (h + 1) * kd)
            lb = lb_ref[:, cols]
            fp = f_ref[bi, rows, cols]
            f = lb + (1.0 - lb) * jax.nn.sigmoid(fp)
            g = jnp.log(jnp.maximum(f, F_TINY))
            k = (1.0 - lb) * jax.nn.sigmoid(-fp)
            if padded:
                ok = _valid_rows(t_valid, tb, r0, c)
                g = jnp.where(ok, g, 0.0)
                k = jnp.where(ok, k, 0.0)
            vs.append(i_ref[bi, rows, cols])
            parts.append(_gla_head_chunk(q_ref[bi, rows, cols], k, vs[h], g, n_src))
        for h, o in enumerate(_gla_state_step(parts, vs, st_ref, bi)):
            cols = slice(h * kd, (h + 1) * kd)
            o = _rms_rows(o) * ng_ref[...] * jax.nn.sigmoid(og_ref[bi, rows, cols])
            o_ref[bi, rows, cols] = o.astype(o_ref.dtype)

    _seq_chunk_loop(nb, tb // c, unroll, chunk)

    @pl.when(pl.program_id(1) == pl.num_programs(1) - 1)
    def _():
        _store_state_t(st_ref, sout_ref, kd)


def mixer_b(pseq, lb, norm_g, s0, *, t_valid, tb, nb, unroll):
    bsz, t_pad, _ = pseq.shape
    assert t_valid == t_pad or t_pad == tb <= GLA_CHUNK
    bw = BRANCH_WIDTH
    field = lambda off: pl.BlockSpec((nb, tb, bw), lambda b, s: (b, s, off // bw))
    state = pl.BlockSpec((nb, HEADS, HEAD_DIM, HEAD_DIM), lambda b, s: (b, 0, 0, 0))
    return pl.pallas_call(
        functools.partial(_mixer_b_body, t_valid, t_pad, unroll),
        grid=(bsz // nb, t_pad // tb),
        in_specs=[field(OFF_B_Q), field(OFF_B_F), field(OFF_B_I), field(OFF_B_G),
                  pl.BlockSpec((1, bw), lambda b, s: (0, 0)),
                  pl.BlockSpec((1, HEAD_DIM), lambda b, s: (0, 0)),
                  state],
        out_specs=[pl.BlockSpec((nb, tb, bw), lambda b, s: (b, s, 0)), state],
        out_shape=[jax.ShapeDtypeStruct((bsz, t_pad, bw), BF16),
                   jax.ShapeDtypeStruct((bsz, HEADS, HEAD_DIM, HEAD_DIM), F32)],
        scratch_shapes=[pltpu.VMEM((nb, HEAD_DIM, HEADS * HEAD_DIM), F32)],
        compiler_params=_cparams(("parallel", "arbitrary")),
        name="mixer_b",
    )(pseq, pseq, pseq, pseq, lb.reshape(1, bw), norm_g.reshape(1, HEAD_DIM), s0)


def _mixer_c_body(t_valid, t_pad, unroll, q_ref, k_ref, v_ref, r_ref, lr_ref, w2_ref, ba_ref, ng_ref, s0_ref,
                  o_ref, sout_ref, st_ref):
    nb, tb = q_ref.shape[0], q_ref.shape[1]
    c = min(GLA_CHUNK, tb)
    kd = C_KDIM
    padded = t_valid < t_pad
    n_src = t_valid if padded else c

    @pl.when(pl.program_id(1) == 0)
    def _():
        _load_state_t(s0_ref, st_ref, kd)

    zpad = jnp.zeros((c, HEAD_DIM - kd), F32)

    def chunk(bi, n):
        r0, rows = _chunk_rows(n, c)
        gate_in = _dot(lr_ref[bi, rows, 0:C_RANK], w2_ref[...]) + ba_ref[...]
        g_all = _log_sigmoid(gate_in) / C_TAU
        if padded:
            ok = _valid_rows(t_valid, tb, r0, c)
            g_all = jnp.where(ok, g_all, 0.0)
        parts, vs = [], []
        for h in range(HEADS):
            kcols = slice(h * kd, (h + 1) * kd)
            q = jnp.concatenate([q_ref[bi, rows, kcols] * (kd ** -0.5), zpad], axis=1)
            k = k_ref[bi, rows, kcols]
            if padded:
                k = jnp.where(ok, k, 0.0)
            k = jnp.concatenate([k, zpad], axis=1)
            g = jnp.concatenate([g_all[:, kcols], zpad], axis=1)
            vs.append(v_ref[bi, rows, h * HEAD_DIM:(h + 1) * HEAD_DIM])
            parts.append(_gla_head_chunk(q, k, vs[h], g, n_src))
        for h, o in enumerate(_gla_state_step(parts, vs, st_ref, bi)):
            vcols = slice(h * HEAD_DIM, (h + 1) * HEAD_DIM)
            o = _rms_rows(o) * ng_ref[...] * _silu(r_ref[bi, rows, vcols])
            o_ref[bi, rows, vcols] = o.astype(o_ref.dtype)

    _seq_chunk_loop(nb, tb // c, unroll, chunk)

    @pl.when(pl.program_id(1) == pl.num_programs(1) - 1)
    def _():
        _store_state_t(st_ref, sout_ref, kd)


def mixer_c(pseq, w_a2, b_a, norm_g, s0, *, t_valid, tb, nb, unroll):
    bsz, t_pad, _ = pseq.shape
    assert t_valid == t_pad or t_pad == tb <= GLA_CHUNK
    bw = BRANCH_WIDTH
    kw = HEADS * C_KDIM
    field = lambda off, w: pl.BlockSpec((nb, tb, w), lambda b, s: (b, s, off // w))
    state = pl.BlockSpec((nb, HEADS, C_KDIM, HEAD_DIM), lambda b, s: (b, 0, 0, 0))
    return pl.pallas_call(
        functools.partial(_mixer_c_body, t_valid, t_pad, unroll),
        grid=(bsz // nb, t_pad // tb),
        in_specs=[field(OFF_C_Q, kw), field(OFF_C_K, kw), field(OFF_C_V, bw), field(OFF_C_R, bw),
                  field(OFF_C_LR, 128),
                  pl.BlockSpec((C_RANK, kw), lambda b, s: (0, 0)),
                  pl.BlockSpec((1, kw), lambda b, s: (0, 0)),
                  pl.BlockSpec((1, HEAD_DIM), lambda b, s: (0, 0)),
                  state],
        out_specs=[pl.BlockSpec((nb, tb, bw), lambda b, s: (b, s, 0)), state],
        out_shape=[jax.ShapeDtypeStruct((bsz, t_pad, bw), BF16),
                   jax.ShapeDtypeStruct((bsz, HEADS, C_KDIM, HEAD_DIM), F32)],
        scratch_shapes=[pltpu.VMEM((nb, HEAD_DIM, HEADS * HEAD_DIM), F32)],
        compiler_params=_cparams(("parallel", "arbitrary")),
        name="mixer_c",
    )(pseq, pseq, pseq, pseq, pseq, w_a2, b_a.reshape(1, kw), norm_g.reshape(1, HEAD_DIM), s0)


def _grouped_pairs_loop(n_rows, seg, st_ref, head_inputs, head_outputs):
    def step(it, carry):
        rows = pl.ds(pl.multiple_of(it * 16, 16), 16)
        ins = [head_inputs(rows, h) for h in range(HEADS)]
        halves = []
        for half in range(2):
            sl = slice(half * 8, (half + 1) * 8)
            vs = [x[2][sl] for x in ins]
            tiles = [_gla_pair_tile(x[0][sl], x[1][sl], x[2][sl], x[3][sl], seg) for x in ins]
            halves.append(_gla_pair_step(tiles, vs, st_ref, 2 * it + half, seg))
        for h in range(HEADS):
            head_outputs(rows, h, jnp.concatenate([halves[0][h], halves[1][h]], axis=0))
        return carry

    lax.fori_loop(0, n_rows // 16, step, 0, unroll=2)


def _mixer_b_group_body(seg, q_ref, f_ref, i_ref, og_ref, lb_ref, ng_ref, s0_ref, o_ref, sout_ref, st_ref):
    kd = HEAD_DIM
    _load_state_natural(s0_ref, st_ref, kd)

    def head_inputs(rows, h):
        cols = slice(h * kd, (h + 1) * kd)
        lb = lb_ref[:, cols]
        fp = f_ref[0, rows, cols]
        f = lb + (1.0 - lb) * jax.nn.sigmoid(fp)
        return (q_ref[0, rows, cols], (1.0 - lb) * jax.nn.sigmoid(-fp), i_ref[0, rows, cols],
                jnp.log(jnp.maximum(f, F_TINY)))

    def head_outputs(rows, h, o):
        cols = slice(h * kd, (h + 1) * kd)
        o = _rms_rows(o) * ng_ref[...] * jax.nn.sigmoid(og_ref[0, rows, cols])
        o_ref[0, rows, cols] = o.astype(o_ref.dtype)

    _grouped_pairs_loop(q_ref.shape[1], seg, st_ref, head_inputs, head_outputs)
    _store_state_natural(st_ref, sout_ref, kd)


def _mixer_c_group_body(seg, q_ref, k_ref, v_ref, r_ref, lr_ref, w2_ref, ba_ref, ng_ref, s0_ref, o_ref, sout_ref,
                        st_ref, g_scr):
    kd = C_KDIM
    _load_state_natural(s0_ref, st_ref, kd)
    g_scr[...] = _log_sigmoid(_dot(lr_ref[0, :, 0:C_RANK], w2_ref[...]) + ba_ref[...]) / C_TAU
    zpad = jnp.zeros((16, HEAD_DIM - kd), F32)

    def head_inputs(rows, h):
        kcols = slice(h * kd, (h + 1) * kd)
        pad = lambda x: jnp.concatenate([x, zpad], axis=1)
        return (pad(q_ref[0, rows, kcols] * (kd ** -0.5)), pad(k_ref[0, rows, kcols]),
                v_ref[0, rows, h * HEAD_DIM:(h + 1) * HEAD_DIM], pad(g_scr[rows, kcols]))

    def head_outputs(rows, h, o):
        vcols = slice(h * HEAD_DIM, (h + 1) * HEAD_DIM)
        o = _rms_rows(o) * ng_ref[...] * _silu(r_ref[0, rows, vcols])
        o_ref[0, rows, vcols] = o.astype(o_ref.dtype)

    _grouped_pairs_loop(q_ref.shape[1], seg, st_ref, head_inputs, head_outputs)
    _store_state_natural(st_ref, sout_ref, kd)


def _layer_stacked_state(body, prev_states, state_ref_pos, seqs_per_step):
    n_prev = 0 if prev_states is None else prev_states.shape[0]

    def wrapped(*refs):
        refs = list(refs)
        if n_prev:
            prev_ref = refs.pop(0)
            refs[state_ref_pos][0:n_prev] = prev_ref[...]
        refs[state_ref_pos] = refs[state_ref_pos].at[n_prev]
        return body(*refs)

    def spec(shape_tail, layers):
        return pl.BlockSpec((layers, seqs_per_step) + shape_tail, lambda s: (0, s) + (0,) * len(shape_tail))

    def out_shape(s0):
        return jax.ShapeDtypeStruct((n_prev + 1,) + s0.shape[1:], F32)

    prev_specs = lambda s0: [spec(s0.shape[2:], n_prev)] if n_prev else []
    prev_inputs = [prev_states] if n_prev else []
    return wrapped, prev_specs, prev_inputs, lambda s0: spec(s0.shape[2:], n_prev + 1), out_shape


def _grouped_gla_call(body, prows, fields, params, s0, prev_states, kdim, *, seg, tb, extra_scratch=()):
    m = prows.shape[0]
    bw = BRANCH_WIDTH
    p3 = prows.reshape(1, m, -1)
    layer = 0 if prev_states is None else prev_states.shape[0]
    state = pl.BlockSpec((None, tb // seg, HEADS, kdim, HEAD_DIM), lambda s: (layer, s, 0, 0, 0))
    n_in = len(fields) + len(params) + 1
    wrapped, prev_specs, prev_inputs, state_out_spec, state_out_shape = _layer_stacked_state(
        functools.partial(body, seg), prev_states, n_in + 1, tb // seg)
    o, s_out = pl.pallas_call(
        wrapped,
        grid=(m // tb,),
        in_specs=prev_specs(s0)
        + [pl.BlockSpec((1, tb, w), functools.partial(lambda s, c: (0, s, c), c=off // w)) for off, w in fields]
        + [pl.BlockSpec(x.shape, functools.partial(lambda s, n: (0,) * n, n=x.ndim)) for x in params] + [state],
        out_specs=[pl.BlockSpec((1, tb, bw), lambda s: (0, s, 0)), state_out_spec(s0)],
        out_shape=[jax.ShapeDtypeStruct((1, m, bw), BF16), state_out_shape(s0)],
        scratch_shapes=[pltpu.VMEM((tb // seg, HEADS * HEAD_DIM, HEAD_DIM), F32), *extra_scratch],
        compiler_params=_cparams(("parallel",)),
        name=body.__name__.strip("_"),
    )(*prev_inputs, *([p3] * len(fields)), *params, s0)
    return o.reshape(m, bw), s_out


def mixer_b_grouped(prows, lb, norm_g, s0, prev_states, *, seg, tb):
    bw = BRANCH_WIDTH
    return _grouped_gla_call(
        _mixer_b_group_body, prows, [(OFF_B_Q, bw), (OFF_B_F, bw), (OFF_B_I, bw), (OFF_B_G, bw)],
        [lb.reshape(1, bw), norm_g.reshape(1, HEAD_DIM)], s0, prev_states, HEAD_DIM, seg=seg, tb=tb)


def mixer_c_grouped(prows, w_a2, b_a, norm_g, s0, prev_states, *, seg, tb):
    bw = BRANCH_WIDTH
    kw = HEADS * C_KDIM
    return _grouped_gla_call(
        _mixer_c_group_body, prows, [(OFF_C_Q, kw), (OFF_C_K, kw), (OFF_C_V, bw), (OFF_C_R, bw), (OFF_C_LR, 128)],
        [w_a2, b_a.reshape(1, kw), norm_g.reshape(1, HEAD_DIM)], s0, prev_states, C_KDIM, seg=seg, tb=tb,
        extra_scratch=(pltpu.VMEM((tb, kw), F32),))


def _unit_lower_inverse(a, order):
    c = a.shape[0]
    eye = (lax.broadcasted_iota(jnp.int32, (c, c), 0) == lax.broadcasted_iota(jnp.int32, (c, c), 1)).astype(F32)
    p = eye - a
    pw = a
    n = 2
    while n < order:
        pw = _dot(pw, pw)
        p = p + _dot(p, pw)
        n *= 2
    return p


D_ROWS = 64


def _stack_heads(x):
    return jnp.concatenate([x[:, h * HEAD_DIM:(h + 1) * HEAD_DIM] for h in range(HEADS)], axis=0)


def _stack_head_lanes(x):
    return jnp.concatenate([x[:, h:h + 1] for h in range(HEADS)], axis=0)


def _lane_pad_heads(x):
    return jnp.pad(x.reshape(1, HEADS), ((0, 0), (0, 128 - HEADS)))


def _segment_cumsum(x, seg):
    tpos = lax.broadcasted_iota(jnp.int32, x.shape, 0) & (seg - 1)
    sh = 1
    while sh < seg:
        x = x + jnp.where(tpos >= sh, pltpu.roll(x, sh, axis=0), 0.0)
        sh *= 2
    return x


def _delta_chunk_operands(seg, qkv, beta_all, g_all):
    hd = HEAD_DIM
    bw = BRANCH_WIDTH
    r = HEADS * qkv.shape[0]
    q = _stack_heads(qkv[:, 0:bw])
    k = _stack_heads(qkv[:, bw:2 * bw])
    v = _stack_heads(qkv[:, 2 * bw:3 * bw])
    q = q * lax.rsqrt(jnp.sum(q * q, axis=-1, keepdims=True) + EPS) * (hd ** -0.5)
    k = k * lax.rsqrt(jnp.sum(k * k, axis=-1, keepdims=True) + EPS)
    beta = _stack_head_lanes(beta_all)
    gc = _stack_head_lanes(_segment_cumsum(g_all, seg))
    ri = lax.broadcasted_iota(jnp.int32, (r, r), 0)
    ci = lax.broadcasted_iota(jnp.int32, (r, r), 1)
    shift = seg.bit_length() - 1
    same = (ri >> shift) == (ci >> shift)
    gr = jnp.sum(jnp.where(ri == ci, gc, 0.0), axis=0, keepdims=True)
    decay = jnp.where(same, jnp.where(ri >= ci, jnp.exp(jnp.minimum(gc - gr, 0.0)), 0.0), 0.0)
    a_mat = jnp.where(ri > ci, beta * decay * _dot_nt(k, k), 0.0)
    t_inv = _unit_lower_inverse(a_mat, seg)
    eg = jnp.exp(gc)
    sol = _dot(t_inv, jnp.concatenate([(beta * eg) * k, beta * v], axis=1))
    qk = _dot_nt(q, k) * decay
    is_last = (ci & (seg - 1)) == seg - 1
    gl = jnp.sum(jnp.where(same, jnp.where(is_last, gr, 0.0), 0.0), axis=1, keepdims=True)
    return sol[:, :hd], sol[:, hd:], qk, q * eg, k * jnp.exp(gl - gc), jnp.exp(gl)


def _mixer_d_body(x_ref, beta_ref, dec_ref, z_ref, cst_ref, s0_ref, cw_ref, alog_ref, dtb_ref, ng_ref,
                  o_ref, cst_out_ref, sout_ref, ext, conv, st_ref):
    nb, tb = x_ref.shape[0], x_ref.shape[1]
    c = D_ROWS
    hd = HEAD_DIM
    t = pl.program_id(1)
    nt = pl.num_programs(1)

    @pl.when(t == 0)
    def _():
        ext[:, 5:8, :] = cst_ref[...]
        st_ref[...] = s0_ref[...].reshape(nb * HEADS, hd, hd)

    for bi in range(nb):
        ext[bi, 8:8 + tb, :] = x_ref[bi]
        y = ext[bi, 5:5 + tb, :] * cw_ref[0:1, :]
        for j in range(1, D_CONV):
            y = y + ext[bi, 5 + j:5 + j + tb, :] * cw_ref[j:j + 1, :]
        conv[bi] = _silu(y)

    @pl.when(t == nt - 1)
    def _():
        cst_out_ref[...] = ext[:, 8 + tb - (D_CONV - 1):8 + tb, :]

    ext[:, 5:8, :] = ext[:, 5 + tb:8 + tb, :]

    a_neg = -jnp.exp(alog_ref[...])

    def chunk(n, carry):
        rows = pl.ds(pl.multiple_of(n * c, c), c)
        for bi in range(nb):
            beta_all = jax.nn.sigmoid(beta_ref[bi, rows, :])
            g_all = a_neg * _softplus(dec_ref[bi, rows, :] + dtb_ref[...])
            w, u, qk, qg, kdec, egl = _delta_chunk_operands(c, conv[bi, rows, :], beta_all, g_all)
            deltas, oqs = [], []
            for h in range(HEADS):
                hs = slice(h * c, (h + 1) * c)
                st = st_ref[bi * HEADS + h]
                x = _dot(jnp.concatenate([w[hs], qg[hs]], axis=0), st)
                delta = u[hs] - x[:c]
                deltas.append(delta)
                oqs.append(x[c:])
                st_ref[bi * HEADS + h] = egl[(h + 1) * c - 1:(h + 1) * c, :] * st + _dot_tn(kdec[hs], delta)
            o = jnp.concatenate(oqs, axis=0) + _dot(qk, jnp.concatenate(deltas, axis=0))
            o = _rms_rows(o) * ng_ref[...] * _silu(_stack_heads(z_ref[bi, rows, :]))
            for h in range(HEADS):
                o_ref[bi, rows, h * hd:(h + 1) * hd] = o[h * c:(h + 1) * c].astype(o_ref.dtype)
        return carry

    lax.fori_loop(0, tb // c, chunk, 0, unroll=2)

    @pl.when(t == nt - 1)
    def _():
        sout_ref[...] = st_ref[...].reshape(nb, HEADS, hd, hd)


def _mixer_d_group_body(seg, x_ref, cst_ref, beta_ref, dec_ref, z_ref, s0_ref, cw_ref, alog_ref, dtb_ref, ng_ref,
                        o_ref, cst_out_ref, sout_ref, conv, w_s, u_s, qg_s, kd_s, egl_s, delta_s, oq_s, stp_s, x_s):
    tb = x_ref.shape[1]
    nseq = tb // seg
    c = D_ROWS
    hd = HEAD_DIM
    pairs = c // 8
    hist_rows = D_CONV - 1

    lane_blocks = stp_s.shape[0]
    x = x_ref[0]
    stp_s[...] = jnp.zeros_like(stp_s)
    for cb in range(lane_blocks):
        lanes = slice(cb * 128, (cb + 1) * 128)
        x_s[cb] = x[:, lanes]
        for j in range(hist_rows):
            stp_s[cb, pl.ds(j, nseq, stride=seg), :] = cst_ref[:, j, lanes]
    stp = jnp.concatenate([stp_s[cb] for cb in range(lane_blocks)], axis=1)

    tpos = lax.broadcasted_iota(jnp.int32, (tb, 1), 0) & (seg - 1)
    y = x * cw_ref[3:4, :]
    for d in range(1, D_CONV):
        hist = stp if d == 3 else pltpu.roll(stp, tb - (3 - d), axis=0)
        y = y + jnp.where(tpos < d, hist, pltpu.roll(x, d, axis=0)) * cw_ref[3 - d:4 - d, :]
    conv[...] = _silu(y)
    for cb in range(lane_blocks):
        for j in range(hist_rows):
            cst_out_ref[:, j, cb * 128:(cb + 1) * 128] = x_s[cb, pl.ds(seg - hist_rows + j, nseq, stride=seg), :]

    a_neg = -jnp.exp(alog_ref[...])
    low = lax.broadcasted_iota(jnp.int32, (8, 1), 0) < seg

    for n in range(tb // c):
        rows = slice(n * c, (n + 1) * c)
        beta_all = jax.nn.sigmoid(beta_ref[0, rows, :])
        g_all = a_neg * _softplus(dec_ref[0, rows, :] + dtb_ref[...])
        w, u, qk, qg, kdec, egl = _delta_chunk_operands(seg, conv[rows, :], beta_all, g_all)
        w_s[...] = w
        u_s[...] = u
        qg_s[...] = qg
        kd_s[...] = kdec
        egl_s[...] = jnp.broadcast_to(egl, (HEADS * c, hd))

        def pair(p, carry):
            rp = pl.ds(pl.multiple_of(p * 8, 8), 8)
            h = p // pairs
            ja = n * (c // seg) + (p % pairs) * 2
            sa = s0_ref[ja, h]
            sb = s0_ref[ja + 1, h]
            lhs = jnp.concatenate([w_s[rp, :], qg_s[rp, :]], axis=0)
            xa = _dot(lhs, sa)
            xb = _dot(lhs, sb)
            delta = u_s[rp, :] - jnp.where(low, xa[:8], xb[:8])
            delta_s[rp, :] = delta
            oq_s[rp, :] = jnp.where(low, xa[8:], xb[8:])
            kd = kd_s[rp, :]
            e = egl_s[rp, :]
            sout_ref[ja, h] = e[seg - 1:seg, :] * sa + _dot_tn(jnp.where(low, kd, 0.0), delta)
            sout_ref[ja + 1, h] = e[2 * seg - 1:2 * seg, :] * sb + _dot_tn(jnp.where(low, 0.0, kd), delta)
            return carry

        lax.fori_loop(0, HEADS * pairs, pair, 0, unroll=4)
        o = oq_s[...] + _dot(qk, delta_s[...])
        o = _rms_rows(o) * ng_ref[...] * _silu(_stack_heads(z_ref[0, rows, :]))
        for h in range(HEADS):
            o_ref[0, rows, h * hd:(h + 1) * hd] = o[h * c:(h + 1) * c].astype(o_ref.dtype)


def mixer_d(pseq, conv_state, s0, conv_w, a_log, dt_bias, norm_g, *, tb, nb):
    bsz, t, _ = pseq.shape
    bw = BRANCH_WIDTH
    field = lambda off, w: pl.BlockSpec((nb, tb, w), lambda b, s: (b, s, off // w))
    state = pl.BlockSpec((nb, HEADS, HEAD_DIM, HEAD_DIM), lambda b, s: (b, 0, 0, 0))
    cstate = pl.BlockSpec((nb, D_CONV - 1, D_QKV), lambda b, s: (b, 0, 0))
    return pl.pallas_call(
        _mixer_d_body,
        grid=(bsz // nb, t // tb),
        in_specs=[field(OFF_D_QKV, D_QKV), field(OFF_D_BETA, 128), field(OFF_D_DECAY, 128), field(OFF_D_Z, bw),
                  cstate, state,
                  pl.BlockSpec((D_CONV, D_QKV), lambda b, s: (0, 0)),
                  pl.BlockSpec((1, 128), lambda b, s: (0, 0)),
                  pl.BlockSpec((1, 128), lambda b, s: (0, 0)),
                  pl.BlockSpec((1, HEAD_DIM), lambda b, s: (0, 0))],
        out_specs=[pl.BlockSpec((nb, tb, bw), lambda b, s: (b, s, 0)), cstate, state],
        out_shape=[jax.ShapeDtypeStruct((bsz, t, bw), BF16),
                   jax.ShapeDtypeStruct((bsz, D_CONV - 1, D_QKV), F32),
                   jax.ShapeDtypeStruct((bsz, HEADS, HEAD_DIM, HEAD_DIM), F32)],
        scratch_shapes=[pltpu.VMEM((nb, tb + 8, D_QKV), F32), pltpu.VMEM((nb, tb, D_QKV), F32),
                        pltpu.VMEM((nb * HEADS, HEAD_DIM, HEAD_DIM), F32)],
        compiler_params=_cparams(("parallel", "arbitrary")),
        name="mixer_d",
    )(pseq, pseq, pseq, pseq, conv_state, s0, conv_w, _lane_pad_heads(a_log), _lane_pad_heads(dt_bias),
      norm_g.reshape(1, HEAD_DIM))


def mixer_d_grouped(prows, conv_state, s0, conv_w, a_log, dt_bias, norm_g, prev_states, *, seg, tb):
    m = prows.shape[0]
    bsz = m // seg
    bw = BRANCH_WIDTH
    field = lambda off, w: pl.BlockSpec((1, tb, w), lambda s: (0, s, off // w))
    layer = 0 if prev_states is None else prev_states.shape[0]
    state = pl.BlockSpec((None, tb // seg, HEADS, HEAD_DIM, HEAD_DIM), lambda s: (layer, s, 0, 0, 0))
    cstate = pl.BlockSpec((tb // seg, D_CONV - 1, D_QKV), lambda s: (s, 0, 0))
    rows128 = pltpu.VMEM((HEADS * D_ROWS, HEAD_DIM), F32)
    lane_rows = pltpu.VMEM((D_QKV // 128, tb, 128), F32)
    n_in, state_out_pos = 10, 2
    wrapped, prev_specs, prev_inputs, state_out_spec, state_out_shape = _layer_stacked_state(
        functools.partial(_mixer_d_group_body, seg), prev_states, n_in + state_out_pos, tb // seg)
    o, cst, s_out = pl.pallas_call(
        wrapped,
        grid=(m // tb,),
        in_specs=prev_specs(s0) + [
                  field(OFF_D_QKV, D_QKV), cstate,
                  field(OFF_D_BETA, 128), field(OFF_D_DECAY, 128), field(OFF_D_Z, bw),
                  state,
                  pl.BlockSpec((D_CONV, D_QKV), lambda s: (0, 0)),
                  pl.BlockSpec((1, 128), lambda s: (0, 0)),
                  pl.BlockSpec((1, 128), lambda s: (0, 0)),
                  pl.BlockSpec((1, HEAD_DIM), lambda s: (0, 0))],
        out_specs=[pl.BlockSpec((1, tb, bw), lambda s: (0, s, 0)), cstate, state_out_spec(s0)],
        out_shape=[jax.ShapeDtypeStruct((1, m, bw), BF16),
                   jax.ShapeDtypeStruct((bsz, D_CONV - 1, D_QKV), F32),
                   state_out_shape(s0)],
        scratch_shapes=[pltpu.VMEM((tb, D_QKV), F32)] + [rows128] * 7 + [lane_rows] * 2,
        compiler_params=_cparams(("parallel",)),
        name="mixer_d_grouped",
    )(*prev_inputs, prows.reshape(1, m, -1), conv_state, prows.reshape(1, m, -1), prows.reshape(1, m, -1),
      prows.reshape(1, m, -1), s0, conv_w, _lane_pad_heads(a_log), _lane_pad_heads(dt_bias),
      norm_g.reshape(1, HEAD_DIM))
    return o.reshape(m, bw), cst, s_out


PACK_TILE = 512
PACK_SRC = ([4624, 5136, 5648, 0, 512, 1024, 1536, 2048, 2560, 3584, 4112, 6168, 3072, None]
            + [6680 + PACK_TILE * k for k in range(N_BRANCH * D_MODEL // PACK_TILE)])
PACK_SMALL_TILE = PACK_SRC.index(None)


def _pack_lookup(t, values):
    out = jnp.int32(0)
    for i, v in enumerate(values):
        out = jnp.where(t == i, jnp.int32(v), out)
    return out


PACK_EXTRA = 32


def _pack_w_in_body(a_ref, b_ref, tail_ref, small_ref, o_ref):
    t = pl.program_id(1)
    last = len(PACK_SRC) - 1
    shifts = [0 if s is None else s % PACK_TILE for s in PACK_SRC]
    assert max(shifts) <= PACK_EXTRA and all(s % 8 == 0 for s in shifts)

    def emit(shift, left_ref, right_ref):
        for c in range(a_ref.shape[1] // 128):
            lanes = slice(c * 128, (c + 1) * 128)
            w = left_ref[shift:, lanes]
            if shift:
                w = jnp.concatenate([w, right_ref[0:shift, lanes]], axis=0)
            o_ref[lanes, :] = w.T.astype(BF16)

    for shift in sorted(set(shifts)):
        tiles = [i for i, s in enumerate(shifts) if s == shift and i not in (PACK_SMALL_TILE, last)]
        pl.when(functools.reduce(jnp.logical_or, [t == i for i in tiles]))(
            functools.partial(emit, shift, a_ref, b_ref))
    pl.when(t == last)(functools.partial(emit, shifts[last], a_ref, tail_ref))
    pl.when(t == PACK_SMALL_TILE)(functools.partial(emit, 0, small_ref, None))


def pack_w_in(w_in):
    layers, k, n_src = w_in.shape
    w_t = jnp.swapaxes(w_in, 1, 2)
    bases = [0 if s is None else s - s % PACK_TILE for s in PACK_SRC]
    extra = [(b + PACK_TILE) // PACK_EXTRA for b in bases]
    full_blocks = n_src // PACK_EXTRA
    assert all(e < full_blocks for e in extra[:-1]) and extra[-1] == full_blocks
    extra[-1] = 0
    z = lambda n: jnp.zeros((layers, n, k), w_in.dtype)
    tail = jnp.concatenate([w_t[:, full_blocks * PACK_EXTRA:], z(PACK_EXTRA - n_src % PACK_EXTRA)], axis=1)
    small = jnp.concatenate([w_t[:, 4096:4112], z(112),
                             w_t[:, 6160:6164], z(124),
                             w_t[:, 6164:6168], z(252)], axis=1)
    return pl.pallas_call(
        _pack_w_in_body,
        grid=(layers, len(PACK_SRC)),
        in_specs=[pl.BlockSpec((None, PACK_TILE, k),
                               lambda l, t: (l, _pack_lookup(t, [b // PACK_TILE for b in bases]), 0)),
                  pl.BlockSpec((None, PACK_EXTRA, k), lambda l, t: (l, _pack_lookup(t, extra), 0)),
                  pl.BlockSpec((None, PACK_EXTRA, k), lambda l, t: (l, 0, 0)),
                  pl.BlockSpec((None, PACK_TILE, k), lambda l, t: (l, 0, 0))],
        out_specs=pl.BlockSpec((None, k, PACK_TILE), lambda l, t: (l, 0, t)),
        out_shape=jax.ShapeDtypeStruct((layers, k, N_PROJ), BF16),
        compiler_params=_cparams(("parallel", "arbitrary")),
        name="pack_w_in",
    )(w_t, w_t, tail, small)


def _trunk(x, p, s_hgrn, s_gla, s_delta, c_dconv, c_fconv, wts, lbs, *, sample):
    bsz, t, _ = x.shape
    m = bsz * t
    tm = m if sample else 1024
    h = x.reshape(m, D_MODEL)
    outs = {k: [] for k in ("hgrn", "gla", "delta", "dconv", "fconv", "v")}
    s_b = s_c = s_d = None
    for l in range(DEPTH):
        w = wts[l]
        proj, gates = in_proj(h, w["g_mix"], w["w_in_layers"], l, tm=tm, tn=1024)
        o_a, v_a = mixer_a(proj, w["a_ln_g"], w["a_ln_b"], w["a_w_mix"], w["a_bias_rows"])
        st_b, st_c, st_d = (s_hgrn, s_gla, s_delta) if sample else (s_hgrn[l], s_gla[l], s_delta[l])
        b_args = (lbs[l], w["b_norm_g"], st_b)
        c_args = (w["c_w_a2"], w["c_b_a"], w["c_norm_g"], st_c)
        d_args = (c_dconv[l], st_d, w["d_conv_w"], w["d_a_log"], w["d_dt_bias"], w["d_norm_g"])
        if sample:
            o_b, s_b = mixer_b_grouped(proj, *b_args, s_b, seg=t, tb=D_ROWS)
            o_c, s_c = mixer_c_grouped(proj, *c_args, s_c, seg=t, tb=D_ROWS)
            o_d, nb_d, s_d = mixer_d_grouped(proj, *d_args, s_d, seg=t, tb=D_ROWS)
        else:
            pseq = proj.reshape(bsz, t, N_SEQ_COLS)
            flat = lambda o: o.reshape(m, BRANCH_WIDTH)
            o_b, s_b = mixer_b(pseq, *b_args, t_valid=t, tb=256, nb=1, unroll=1)
            o_c, s_c = mixer_c(pseq, *c_args, t_valid=t, tb=256, nb=1, unroll=1)
            o_d, nb_d, s_d = mixer_d(pseq, *d_args, tb=256, nb=2 if bsz % 2 == 0 else 1)
            o_b, o_c, o_d = flat(o_b), flat(o_c), flat(o_d)
        merged = merge_branches((o_a, o_b, o_c, o_d), gates, w["w_branch"], tm=tm, tn=512)
        h = matmul_residual(merged, w["w_out_layers"], l, h, tm=tm, tn=1024)

        act, nb_f = ffn_up_gate(h, w["g_ffn"], w["w_ffn_up_layers"], l, c_fconv[l], w["ffn_conv_w"],
                                w["ffn_conv_b"], seq_len=t, tm=tm, tn=512)
        h = matmul_residual(act, w["w_ffn_down_layers"], l, h, tm=min(tm, 512), tn=512)

        h = ple_update(h, p[l].reshape(m, PLE_DIM), w["g_ple"], w["w_ple_gate"], w["w_ple_proj"], tm=tm, tn=512)

        if not sample:
            outs["hgrn"].append(s_b)
            outs["gla"].append(s_c)
            outs["delta"].append(s_d)
        outs["dconv"].append(nb_d)
        outs["fconv"].append(nb_f)
        outs["v"].append(v_a.reshape(bsz, t, BRANCH_WIDTH))
    y = final_norm(h, wts[0]["g_final"], tm=min(tm, 512)).reshape(bsz, t, D_MODEL)
    states = (s_b, s_c, s_d) if sample else tuple(jnp.stack(outs[k]) for k in ("hgrn", "gla", "delta"))
    return (y,) + states + tuple(jnp.stack(outs[k]) for k in ("dconv", "fconv", "v"))


def kernel(x_prompt, x_sample, state_hgrn, state_gla, state_delta, state_delta_conv, state_ffn_conv, p_prompt, p_sample, g_mix, w_in, a_ln_g, a_ln_b, a_w_s, a_b_s, b_lb, b_norm_g, c_w_a2, c_b_a, c_norm_g, d_conv_w, d_a_log, d_dt_bias, d_norm_g, w_branch, w_out, g_ffn, w_ffn_up, ffn_conv_w, ffn_conv_b, w_ffn_down, g_ple, w_ple_gate, w_ple_proj, g_final):
    bp, t_p, _ = x_prompt.shape
    bs, t_s, _ = x_sample.shape
    sm = jax.nn.softmax(b_lb.astype(F32), axis=0)
    lbs = jnp.cumsum(sm, axis=0) - sm[0]

    w_in_packed = pack_w_in(w_in)
    shared = [dict(
        g_mix=g_mix[l], w_in_layers=w_in_packed, a_ln_g=a_ln_g[l], a_ln_b=a_ln_b[l],
        b_norm_g=b_norm_g[l], c_w_a2=c_w_a2[l], c_b_a=c_b_a[l], c_norm_g=c_norm_g[l],
        d_conv_w=d_conv_w[l], d_a_log=d_a_log[l], d_dt_bias=d_dt_bias[l], d_norm_g=d_norm_g[l],
        w_branch=w_branch[l].astype(BF16), w_out_layers=w_out, g_ffn=g_ffn[l],
        w_ffn_up_layers=w_ffn_up, ffn_conv_w=ffn_conv_w[l], ffn_conv_b=ffn_conv_b[l],
        w_ffn_down_layers=w_ffn_down, g_ple=g_ple[l], w_ple_gate=w_ple_gate[l].astype(BF16),
        w_ple_proj=w_ple_proj[l].astype(BF16), g_final=g_final) for l in range(DEPTH)]

    def layer_weights(l, sample):
        if sample:
            seqs = A_CHUNK // t_s
            idx = jnp.arange(A_CHUNK) // t_s
            same_seq = idx[:, None] == idx[None, :]
            w_mix = jnp.where(same_seq, jnp.tile(a_w_s[l, :, :t_s, :t_s], (1, seqs, seqs)), 0.0)
            bias_rows = jnp.tile(a_b_s[l, :, :t_s].T, (seqs, 1))
        else:
            w_mix = a_w_s[l]
            bias_rows = a_b_s[l].T
        return dict(shared[l], a_w_mix=w_mix, a_bias_rows=bias_rows)

    dt = x_prompt.dtype
    zeros = lambda *s: jnp.zeros((DEPTH, bp) + s, dt)
    out_p = _trunk(x_prompt, p_prompt, zeros(HEADS, HEAD_DIM, HEAD_DIM), zeros(HEADS, C_KDIM, HEAD_DIM),
                   zeros(HEADS, HEAD_DIM, HEAD_DIM), zeros(D_CONV - 1, D_QKV), zeros(FFN_CONV - 1, FFN_DIM),
                   [layer_weights(l, False) for l in range(DEPTH)], lbs, sample=False)
    out_s = _trunk(x_sample, p_sample, state_hgrn, state_gla, state_delta, state_delta_conv, state_ffn_conv,
                   [layer_weights(l, True) for l in range(DEPTH)], lbs, sample=True)
    y_p, hgrn_p, gla_p, delta_p, dconv_p, fconv_p, _ = out_p
    y_s, hgrn_s, gla_s, delta_s, dconv_s, fconv_s, v_s = out_s
    return (y_p, y_s, hgrn_p, hgrn_s, gla_p, gla_s, delta_p, delta_s,
            dconv_p, dconv_s, fconv_p, fconv_s, v_s)
```

```python
import functools

import jax
import jax.numpy as jnp
from jax import lax
from jax.experimental import pallas as pl
from jax.experimental.pallas import tpu as pltpu

F32 = jnp.float32
BF16 = jnp.bfloat16

D_MODEL = 2048
DEPTH = 2
PLE_DIM = 256
EPS = 1e-6
F_TINY = 1e-30
N_BRANCH = 4
BRANCH_WIDTH = D_MODEL // 4
A_GROUPS = 4
A_CHUNK = 128
HEADS = 4
HEAD_DIM = 128
C_KDIM = 64
C_RANK = 16
C_TAU = 16.0
D_CONV = 4
D_QKV = 3 * BRANCH_WIDTH
FFN_DIM = 5632
FFN_CONV = 3
GLA_CHUNK = 16
LOG2E = 1.4426950408889634

OFF_D_QKV = 0
OFF_A_U = 1536
OFF_A_V = 2048
OFF_B_Q = 2560
OFF_B_F = 3072
OFF_B_I = 3584
OFF_B_G = 4096
OFF_C_V = 4608
OFF_C_R = 5120
OFF_D_Z = 5632
OFF_C_Q = 6144
OFF_C_K = 6400
OFF_C_LR = 6656
OFF_D_BETA = 6784
OFF_D_DECAY = 6912
OFF_GATES = 7168
N_SEQ_COLS = 7168
N_PROJ = OFF_GATES + N_BRANCH * D_MODEL

VMEM_LIMIT = 56 * 1024 * 1024


def _cparams(sem):
    return pltpu.CompilerParams(dimension_semantics=sem, vmem_limit_bytes=VMEM_LIMIT)


def _gelu(x):
    return 0.5 * x * (1.0 + jnp.tanh(0.7978845608028654 * (x + 0.044715 * (x * x * x))))


def _silu(x):
    return x * jax.nn.sigmoid(x)


def _softplus(x):
    return jnp.maximum(x, 0.0) + jnp.log1p(jnp.exp(-jnp.abs(x)))


def _log_sigmoid(x):
    return -_softplus(-x)


def _rms_rows(x):
    return x * lax.rsqrt(jnp.mean(x * x, axis=-1, keepdims=True) + EPS)


def _dot(a, b):
    return jnp.dot(a, b, preferred_element_type=F32)


def _dot_nt(a, b):
    return lax.dot_general(a, b, (((1,), (1,)), ((), ())), preferred_element_type=F32)


def _dot_tn(a, b):
    return lax.dot_general(a, b, (((0,), (0,)), ((), ())), preferred_element_type=F32)


def _cumsum_rows(x):
    n = x.shape[0]
    row = lax.broadcasted_iota(jnp.int32, x.shape, 0)
    sh = 1
    while sh < n:
        x = x + jnp.where(row >= sh, pltpu.roll(x, sh, axis=0), 0.0)
        sh *= 2
    return x


NORM_ROWS = 128


def _norm_to_scratch(x_ref, g_ref, a_scr):
    def slab(n, carry):
        rows = pl.ds(pl.multiple_of(n * NORM_ROWS, NORM_ROWS), NORM_ROWS)
        a_scr[rows, :] = (_rms_rows(x_ref[rows, :]) * g_ref[...]).astype(BF16)
        return carry

    lax.fori_loop(0, x_ref.shape[0] // NORM_ROWS, slab, 0)


def _matmul_residual_body(x_ref, w_ref, r_ref, o_ref, w_scr):
    @pl.when(pl.program_id(1) == 0)
    def _():
        def slab(n, carry):
            rows = pl.ds(pl.multiple_of(n * NORM_ROWS, NORM_ROWS), NORM_ROWS)
            w_scr[rows, :] = w_ref[rows, :].astype(BF16)
            return carry

        lax.fori_loop(0, w_ref.shape[0] // NORM_ROWS, slab, 0)

    o_ref[...] = r_ref[...] + _dot(x_ref[...], w_scr[...])


def matmul_residual(x, w_layers, layer, res, *, tm, tn):
    m, k = x.shape
    n = w_layers.shape[2]
    return pl.pallas_call(
        _matmul_residual_body,
        grid=(n // tn, m // tm),
        in_specs=[pl.BlockSpec((tm, k), lambda j, i: (i, 0)),
                  pl.BlockSpec((None, k, tn), lambda j, i: (layer, 0, j)),
                  pl.BlockSpec((tm, tn), lambda j, i: (i, j))],
        out_specs=pl.BlockSpec((tm, tn), lambda j, i: (i, j)),
        out_shape=jax.ShapeDtypeStruct((m, n), F32),
        scratch_shapes=[pltpu.VMEM((k, tn), BF16)],
        compiler_params=_cparams(("parallel", "arbitrary")),
        name="matmul_residual",
    )(x, w_layers, res)


def _in_proj_body(n_seq_tiles, x_ref, g_ref, w_ref, seq_ref, gate_ref, a_scr):
    j = pl.program_id(1)

    @pl.when(j == 0)
    def _():
        _norm_to_scratch(x_ref, g_ref, a_scr)

    acc = _dot(a_scr[...], w_ref[...])

    @pl.when(j < n_seq_tiles)
    def _():
        seq_ref[...] = acc

    @pl.when(j >= n_seq_tiles)
    def _():
        gate_ref[...] = acc.astype(gate_ref.dtype)


def in_proj(x, g, w_layers, layer, *, tm, tn):
    m, k = x.shape
    n_seq = N_SEQ_COLS // tn
    n_gate = N_BRANCH * D_MODEL // tn
    return pl.pallas_call(
        functools.partial(_in_proj_body, n_seq),
        grid=(m // tm, n_seq + n_gate),
        in_specs=[pl.BlockSpec((tm, k), lambda i, j: (i, 0)),
                  pl.BlockSpec((1, k), lambda i, j: (0, 0)),
                  pl.BlockSpec((None, k, tn), lambda i, j: (layer, 0, j))],
        out_specs=[pl.BlockSpec((tm, tn), lambda i, j: (i, jnp.minimum(j, n_seq - 1))),
                   pl.BlockSpec((tm, tn), lambda i, j: (i, jnp.maximum(j - n_seq, 0)))],
        out_shape=[jax.ShapeDtypeStruct((m, N_SEQ_COLS), F32),
                   jax.ShapeDtypeStruct((m, N_BRANCH * D_MODEL), BF16)],
        scratch_shapes=[pltpu.VMEM((tm, k), BF16)],
        compiler_params=_cparams(("parallel", "arbitrary")),
        name="in_proj",
    )(x, g.reshape(1, k), w_layers)


def _merge_body(oa_ref, ob_ref, oc_ref, od_ref, ga_ref, gb_ref, gc_ref, gd_ref, w_ref, o_ref):
    acc = None
    for b, (o_b, g_b) in enumerate(((oa_ref, ga_ref), (ob_ref, gb_ref), (oc_ref, gc_ref), (od_ref, gd_ref))):
        gate = 0.5 + 0.5 * jnp.tanh(0.5 * g_b[...].astype(F32))
        term = gate * _dot(o_b[...], w_ref[b])
        acc = term if acc is None else acc + term
    o_ref[...] = acc.astype(o_ref.dtype)


def merge_branches(o_branches, gates, w_branch, *, tm, tn):
    m = gates.shape[0]
    gate_specs = [pl.BlockSpec((tm, tn), functools.partial(
        lambda i, j, b: (i, b * (D_MODEL // tn) + j), b=b)) for b in range(N_BRANCH)]
    return pl.pallas_call(
        _merge_body,
        grid=(m // tm, D_MODEL // tn),
        in_specs=[pl.BlockSpec((tm, BRANCH_WIDTH), lambda i, j: (i, 0))] * N_BRANCH + gate_specs
        + [pl.BlockSpec((N_BRANCH, BRANCH_WIDTH, tn), lambda i, j: (0, 0, j))],
        out_specs=pl.BlockSpec((tm, tn), lambda i, j: (i, j)),
        out_shape=jax.ShapeDtypeStruct((m, D_MODEL), BF16),
        compiler_params=_cparams(("parallel", "arbitrary")),
        name="merge_branches",
    )(*o_branches, gates, gates, gates, gates, w_branch)


def _ple_body(final, x_ref, p_ref, g_ref, wg_ref, wp_ref, *refs):
    gout_ref, o_ref = refs if final else (None, refs[0])

    def slab(n, carry):
        rows = pl.ds(pl.multiple_of(n * NORM_ROWS, NORM_ROWS), NORM_ROWS)
        x = x_ref[rows, :]
        a = (_rms_rows(x) * g_ref[...]).astype(BF16)
        gate = 0.5 + 0.5 * jnp.tanh(0.5 * _dot(a, wg_ref[...]))
        h = x + _dot(p_ref[rows, :].astype(BF16), wp_ref[...]) * gate
        o_ref[rows, :] = h if gout_ref is None else _rms_rows(h) * gout_ref[...]
        return carry

    lax.fori_loop(0, x_ref.shape[0] // NORM_ROWS, slab, 0)


def ple_update(x, p, g, w_gate, w_proj, g_out=None, *, tm):
    m, k = x.shape
    full = lambda a: pl.BlockSpec(a.shape, lambda i: (0,) * a.ndim)
    rows = lambda w: pl.BlockSpec((tm, w), lambda i: (i, 0))
    params = [g.reshape(1, k), w_gate, w_proj] + ([] if g_out is None else [g_out.reshape(1, k)])
    return pl.pallas_call(
        functools.partial(_ple_body, g_out is not None),
        grid=(m // tm,),
        in_specs=[rows(k), rows(PLE_DIM)] + [full(a) for a in params],
        out_specs=rows(k),
        out_shape=jax.ShapeDtypeStruct((m, k), F32),
        compiler_params=_cparams(("parallel",)),
        name="ple_update",
    )(x, p, *params)


FFN_SUB = 256


def _cast_rows_to_bf16(src_ref, dst_ref):
    def slab(n, carry):
        rows = pl.ds(pl.multiple_of(n * NORM_ROWS, NORM_ROWS), NORM_ROWS)
        dst_ref[rows, :] = src_ref[rows, :].astype(BF16)
        return carry

    lax.fori_loop(0, src_ref.shape[0] // NORM_ROWS, slab, 0)


def _ffn_up_long_body(tiles_per_seq, a_ref, wg_ref, wu_ref, st_ref, cw_ref, cb_ref, o_ref, st_out_ref,
                      wg_s, wu_s, ext):
    i = pl.program_id(1)
    tm = a_ref.shape[0]

    @pl.when(i == 0)
    def _():
        _cast_rows_to_bf16(wg_ref, wg_s)
        _cast_rows_to_bf16(wu_ref, wu_s)

    @pl.when(i % tiles_per_seq == 0)
    def _():
        ext[6:8, :] = st_ref[0]

    for r in range(0, tm, FFN_SUB):
        a = a_ref[r:r + FFN_SUB, :]
        ext[8 + r:8 + r + FFN_SUB, :] = _dot(a, wg_s[...])
        conv = (ext[6 + r:6 + r + FFN_SUB, :] * cw_ref[0:1, :] + ext[7 + r:7 + r + FFN_SUB, :] * cw_ref[1:2, :]
                + ext[8 + r:8 + r + FFN_SUB, :] * cw_ref[2:3, :])
        o_ref[r:r + FFN_SUB, :] = (_gelu(conv + cb_ref[...]) * _dot(a, wu_s[...])).astype(o_ref.dtype)

    last = ext[6 + tm:8 + tm, :]
    ext[6:8, :] = last
    st_out_ref[0] = last


def _ffn_up_group_body(seg, a_ref, wg_ref, wu_ref, st_ref, cw_ref, cb_ref, o_ref, st_out_ref, wg_s, wu_s, stp, fg_s):
    tm = a_ref.shape[0]
    nseq = tm // seg
    hist = FFN_CONV - 1

    @pl.when(pl.program_id(1) == 0)
    def _():
        _cast_rows_to_bf16(wg_ref, wg_s)
        _cast_rows_to_bf16(wu_ref, wu_s)

    lane_blocks = stp.shape[0]
    stp[...] = jnp.zeros_like(stp)
    for j in range(hist):
        for c in range(lane_blocks):
            stp[c, pl.ds(j, nseq, stride=seg), :] = st_ref[:, j, c * 128:(c + 1) * 128]

    a = a_ref[...]
    fg = _dot(a, wg_s[...])
    for c in range(lane_blocks):
        fg_s[c] = fg[:, c * 128:(c + 1) * 128]
    h = jnp.concatenate([stp[c] for c in range(lane_blocks)], axis=1)
    tpos = lax.broadcasted_iota(jnp.int32, (tm, 1), 0) & (seg - 1)
    lag1 = jnp.where(tpos < 1, pltpu.roll(h, tm - 1, axis=0), pltpu.roll(fg, 1, axis=0))
    lag2 = jnp.where(tpos < 2, h, pltpu.roll(fg, 2, axis=0))
    conv = lag2 * cw_ref[0:1, :] + lag1 * cw_ref[1:2, :] + fg * cw_ref[2:3, :]
    o_ref[...] = (_gelu(conv + cb_ref[...]) * _dot(a, wu_s[...])).astype(o_ref.dtype)
    for j in range(hist):
        for c in range(lane_blocks):
            st_out_ref[:, j, c * 128:(c + 1) * 128] = fg_s[c, pl.ds(seg - hist + j, nseq, stride=seg), :]


def _rms_bf16_body(x_ref, g_ref, o_ref):
    _norm_to_scratch(x_ref, g_ref, o_ref)


def rms_bf16(x, g, *, tm):
    m, k = x.shape
    return pl.pallas_call(
        _rms_bf16_body,
        grid=(m // tm,),
        in_specs=[pl.BlockSpec((tm, k), lambda i: (i, 0)), pl.BlockSpec((1, k), lambda i: (0, 0))],
        out_specs=pl.BlockSpec((tm, k), lambda i: (i, 0)),
        out_shape=jax.ShapeDtypeStruct((m, k), BF16),
        compiler_params=_cparams(("parallel",)),
        name="rms_bf16",
    )(x, g.reshape(1, k))


def ffn_up_gate(x, g, w_up_layers, layer, state, conv_w, conv_b, *, seq_len, tm, tn):
    m, k = x.shape
    bsz = m // seq_len
    ncol = FFN_DIM // tn
    a = rms_bf16(x, g, tm=min(tm, 512))
    common_in = [pl.BlockSpec((tm, k), lambda j, i: (i, 0)),
                 pl.BlockSpec((None, k, tn), lambda j, i: (layer, 0, j)),
                 pl.BlockSpec((None, k, tn), lambda j, i: (layer, 0, ncol + j))]
    conv_in = [pl.BlockSpec((FFN_CONV, tn), lambda j, i: (0, j)), pl.BlockSpec((1, tn), lambda j, i: (0, j))]
    act_spec = pl.BlockSpec((tm, tn), lambda j, i: (i, j))
    act_shape = jax.ShapeDtypeStruct((m, FFN_DIM), BF16)
    w_scr = [pltpu.VMEM((k, tn), BF16)] * 2
    if seq_len >= tm:
        tps = seq_len // tm
        act, tile_last = pl.pallas_call(
            functools.partial(_ffn_up_long_body, tps),
            grid=(ncol, m // tm),
            in_specs=common_in + [pl.BlockSpec((1, FFN_CONV - 1, tn), lambda j, i: (i // tps, 0, j))] + conv_in,
            out_specs=[act_spec, pl.BlockSpec((1, FFN_CONV - 1, tn), lambda j, i: (i, 0, j))],
            out_shape=[act_shape, jax.ShapeDtypeStruct((m // tm, FFN_CONV - 1, FFN_DIM), F32)],
            scratch_shapes=w_scr + [pltpu.VMEM((tm + 8, tn), F32)],
            compiler_params=_cparams(("parallel", "arbitrary")),
            name="ffn_up_gate",
        )(a, w_up_layers, w_up_layers, state, conv_w, conv_b.reshape(1, FFN_DIM))
        return act, tile_last[tps - 1::tps]
    st_spec = pl.BlockSpec((tm // seq_len, FFN_CONV - 1, tn), lambda j, i: (i, 0, j))
    return pl.pallas_call(
        functools.partial(_ffn_up_group_body, seq_len),
        grid=(ncol, m // tm),
        in_specs=common_in + [st_spec] + conv_in,
        out_specs=[act_spec, st_spec],
        out_shape=[act_shape, jax.ShapeDtypeStruct((bsz, FFN_CONV - 1, FFN_DIM), F32)],
        scratch_shapes=w_scr + [pltpu.VMEM((tn // 128, tm, 128), F32)] * 2,
        compiler_params=_cparams(("parallel", "arbitrary")),
        name="ffn_up_gate_grouped",
    )(a, w_up_layers, w_up_layers, state, conv_w, conv_b.reshape(1, FFN_DIM))


def _mixer_a_body(u_ref, v_ref, lng_ref, lnb_ref, w_ref, bias_ref, o_ref, vout_ref):
    u = _gelu(u_ref[...])
    v = _gelu(v_ref[...])
    vc = v - jnp.mean(v, axis=-1, keepdims=True)
    var = jnp.mean(vc * vc, axis=-1, keepdims=True)
    vn = vc * lax.rsqrt(var + EPS) * lng_ref[...] + lnb_ref[...]
    vout_ref[...] = vn
    n = w_ref.shape[1]
    causal = lax.broadcasted_iota(jnp.int32, (n, n), 0) >= lax.broadcasted_iota(jnp.int32, (n, n), 1)
    gd = BRANCH_WIDTH // A_GROUPS
    for g in range(A_GROUPS):
        w = jnp.where(causal, w_ref[g], 0.0)
        mixed = _dot(w, vn[:, g * gd:(g + 1) * gd]) + bias_ref[:, g:g + 1]
        o_ref[:, g * gd:(g + 1) * gd] = (u[:, g * gd:(g + 1) * gd] * mixed).astype(o_ref.dtype)


def mixer_a(proj, ln_g, ln_b, w_mix, bias_rows):
    m = proj.shape[0]
    r = A_CHUNK
    bw = BRANCH_WIDTH
    return pl.pallas_call(
        _mixer_a_body,
        grid=(m // r,),
        in_specs=[pl.BlockSpec((r, bw), lambda i: (i, OFF_A_U // bw)),
                  pl.BlockSpec((r, bw), lambda i: (i, OFF_A_V // bw)),
                  pl.BlockSpec((1, bw), lambda i: (0, 0)),
                  pl.BlockSpec((1, bw), lambda i: (0, 0)),
                  pl.BlockSpec((A_GROUPS, r, r), lambda i: (0, 0, 0)),
                  pl.BlockSpec((r, A_GROUPS), lambda i: (0, 0))],
        out_specs=[pl.BlockSpec((r, bw), lambda i: (i, 0)), pl.BlockSpec((r, bw), lambda i: (i, 0))],
        out_shape=[jax.ShapeDtypeStruct((m, bw), BF16), jax.ShapeDtypeStruct((m, bw), F32)],
        compiler_params=_cparams(("parallel",)),
        name="mixer_a",
    )(proj, proj, ln_g.reshape(1, bw), ln_b.reshape(1, bw), w_mix, bias_rows)


def _gla_head_chunk(q, k, v, g):
    c = q.shape[0]
    g2 = _cumsum_rows(g) * LOG2E
    n_tiles = c // 8
    row8 = lax.broadcasted_iota(jnp.int32, (8, 1), 0)
    q_t = [q[i * 8:(i + 1) * 8] for i in range(n_tiles)]
    g_t = [g2[i * 8:(i + 1) * 8] for i in range(n_tiles)]
    o_t = [jnp.zeros((8, v.shape[1]), F32) for _ in range(n_tiles)]
    for s in range(c):
        for i in range(s // 8, n_tiles):
            a = jnp.sum(q_t[i] * jnp.exp2(g_t[i] - g2[s:s + 1, :]) * k[s:s + 1, :], axis=-1, keepdims=True)
            if i == s // 8:
                a = jnp.where(row8 >= s - 8 * i, a, 0.0)
            o_t[i] = o_t[i] + a * v[s:s + 1, :]
    o = jnp.concatenate(o_t, axis=0) if n_tiles > 1 else o_t[0]
    gl = g2[c - 1:c, :]
    return o, q * jnp.exp2(g2), k * jnp.exp2(gl - g2), jnp.exp2(gl)


def _block_diag(xs):
    z = jnp.zeros_like(xs[0])
    return jnp.concatenate(
        [jnp.concatenate([x if j == i else z for j in range(len(xs))], axis=1) for i, x in enumerate(xs)], axis=0)


def _gla_state_dots(qgs, kds, vs, es, st_ref, bi):
    st = st_ref[bi]
    o_inter = _dot_nt(_block_diag(qgs), st)
    st_ref[bi] = st * jnp.concatenate(es, axis=1) + _dot_tn(jnp.concatenate(vs, axis=0), _block_diag(kds))
    return o_inter


def _gla_state_step(parts, vs, st_ref, bi):
    c = vs[0].shape[0]
    o_inter = _gla_state_dots([p[1] for p in parts], [p[2] for p in parts], vs, [p[3] for p in parts], st_ref, bi)
    return [p[0] + o_inter[h * c:(h + 1) * c] for h, p in enumerate(parts)]


def _gla_pair_tile(q, k, v, g, seg):
    g2 = _segment_cumsum(g, seg) * LOG2E
    tpos = lax.broadcasted_iota(jnp.int32, (8, 1), 0) & (seg - 1)
    o = jnp.sum(q * k, axis=-1, keepdims=True) * v
    for d in range(1, seg):
        a = jnp.sum(q * jnp.exp2(g2 - pltpu.roll(g2, d, axis=0)) * pltpu.roll(k, d, axis=0), axis=-1, keepdims=True)
        o = o + jnp.where(tpos >= d, a, 0.0) * pltpu.roll(v, d, axis=0)
    first = lax.broadcasted_iota(jnp.int32, (8, 1), 0) < seg
    gl_a, gl_b = g2[seg - 1:seg, :], g2[2 * seg - 1:2 * seg, :]
    kd = k * jnp.exp2(jnp.where(first, gl_a, gl_b) - g2)
    return o, q * jnp.exp2(g2), kd, jnp.exp2(gl_a), jnp.exp2(gl_b)


def _lane_broadcast_column(row):
    hi = row.astype(BF16).astype(F32)
    r1 = row - hi
    mid = r1.astype(BF16).astype(F32)
    lo = r1 - mid
    r = lax.broadcasted_iota(jnp.int32, (8, 1), 0)
    pieces = jnp.where(r == 0, hi, jnp.where(r == 1, mid, jnp.where(r == 2, lo, 0.0)))
    return _dot_tn(pieces.astype(BF16), jnp.ones((8, 128), BF16))


def _gla_state_dots_natural(qgs, kds, vs, es, st_ref, bi):
    st = st_ref[bi]
    o_inter = _dot(_block_diag(qgs), st)
    decay = _lane_broadcast_column(jnp.concatenate(es, axis=1))
    st_ref[bi] = decay * st + _dot_tn(_block_diag(kds), jnp.concatenate(vs, axis=0))
    return o_inter


def _gla_pair_step(tiles, vs, st_ref, p, seg):
    first = lax.broadcasted_iota(jnp.int32, (8, 1), 0) < seg
    qgs = [t[1] for t in tiles]
    oi_a = _gla_state_dots_natural(qgs, [jnp.where(first, t[2], 0.0) for t in tiles], vs, [t[3] for t in tiles],
                                   st_ref, 2 * p)
    oi_b = _gla_state_dots_natural(qgs, [jnp.where(first, 0.0, t[2]) for t in tiles], vs, [t[4] for t in tiles],
                                   st_ref, 2 * p + 1)
    return [t[0] + jnp.where(first, oi_a[h * 8:(h + 1) * 8], oi_b[h * 8:(h + 1) * 8]) for h, t in enumerate(tiles)]


def _load_state_natural(s0_ref, st_ref, kdim):
    def per_seq(bi, carry):
        for h in range(HEADS):
            s = s0_ref[bi, h]
            if kdim < HEAD_DIM:
                s = jnp.concatenate([s, jnp.zeros((HEAD_DIM - kdim, HEAD_DIM), F32)], axis=0)
            st_ref[bi, h * HEAD_DIM:(h + 1) * HEAD_DIM, :] = s
        return carry

    lax.fori_loop(0, s0_ref.shape[0], per_seq, 0)


def _store_state_natural(st_ref, sout_ref, kdim):
    def per_seq(bi, carry):
        for h in range(HEADS):
            sout_ref[bi, h] = st_ref[bi, h * HEAD_DIM:h * HEAD_DIM + kdim, :]
        return carry

    lax.fori_loop(0, sout_ref.shape[0], per_seq, 0)


def _load_state_t(s0_ref, st_ref, kdim):
    def per_seq(bi, carry):
        for h in range(HEADS):
            s = s0_ref[bi, h]
            if kdim < HEAD_DIM:
                s = jnp.concatenate([s, jnp.zeros((HEAD_DIM - kdim, HEAD_DIM), F32)], axis=0)
            st_ref[bi, :, h * HEAD_DIM:(h + 1) * HEAD_DIM] = s.T
        return carry

    lax.fori_loop(0, s0_ref.shape[0], per_seq, 0)


def _store_state_t(st_ref, sout_ref, kdim):
    def per_seq(bi, carry):
        for h in range(HEADS):
            sout_ref[bi, h] = st_ref[bi, :, h * HEAD_DIM:(h + 1) * HEAD_DIM].T[:kdim, :]
        return carry

    lax.fori_loop(0, sout_ref.shape[0], per_seq, 0)


GLA_UNROLL = 8


def _gla_chunk_loop(tb, chunk):
    def step(n, carry):
        chunk(pl.ds(pl.multiple_of(n * GLA_CHUNK, GLA_CHUNK), GLA_CHUNK))
        return carry

    lax.fori_loop(0, tb // GLA_CHUNK, step, 0, unroll=GLA_UNROLL)


def _mixer_b_body(q_ref, f_ref, i_ref, og_ref, lb_ref, ng_ref, s0_ref, o_ref, sout_ref, st_ref):
    kd = HEAD_DIM

    @pl.when(pl.program_id(1) == 0)
    def _():
        _load_state_t(s0_ref, st_ref, kd)

    def chunk(rows):
        parts, vs = [], []
        for h in range(HEADS):
            cols = slice(h * kd, (h + 1) * kd)
            lb = lb_ref[:, cols]
            fp = f_ref[0, rows, cols]
            f = lb + (1.0 - lb) * jax.nn.sigmoid(fp)
            g = jnp.log(jnp.maximum(f, F_TINY))
            k = (1.0 - lb) * jax.nn.sigmoid(-fp)
            vs.append(i_ref[0, rows, cols])
            parts.append(_gla_head_chunk(q_ref[0, rows, cols], k, vs[h], g))
        for h, o in enumerate(_gla_state_step(parts, vs, st_ref, 0)):
            cols = slice(h * kd, (h + 1) * kd)
            o = _rms_rows(o) * ng_ref[...] * jax.nn.sigmoid(og_ref[0, rows, cols])
            o_ref[0, rows, cols] = o.astype(o_ref.dtype)

    _gla_chunk_loop(q_ref.shape[1], chunk)

    @pl.when(pl.program_id(1) == pl.num_programs(1) - 1)
    def _():
        _store_state_t(st_ref, sout_ref, kd)


def mixer_b(pseq, lb, norm_g, s0, *, tb):
    bsz, t, _ = pseq.shape
    bw = BRANCH_WIDTH
    field = lambda off: pl.BlockSpec((1, tb, bw), lambda b, s: (b, s, off // bw))
    state = pl.BlockSpec((1, HEADS, HEAD_DIM, HEAD_DIM), lambda b, s: (b, 0, 0, 0))
    return pl.pallas_call(
        _mixer_b_body,
        grid=(bsz, t // tb),
        in_specs=[field(OFF_B_Q), field(OFF_B_F), field(OFF_B_I), field(OFF_B_G),
                  pl.BlockSpec((1, bw), lambda b, s: (0, 0)),
                  pl.BlockSpec((1, HEAD_DIM), lambda b, s: (0, 0)),
                  state],
        out_specs=[pl.BlockSpec((1, tb, bw), lambda b, s: (b, s, 0)), state],
        out_shape=[jax.ShapeDtypeStruct((bsz, t, bw), BF16),
                   jax.ShapeDtypeStruct((bsz, HEADS, HEAD_DIM, HEAD_DIM), F32)],
        scratch_shapes=[pltpu.VMEM((1, HEAD_DIM, HEADS * HEAD_DIM), F32)],
        compiler_params=_cparams(("parallel", "arbitrary")),
        name="mixer_b",
    )(pseq, pseq, pseq, pseq, lb.reshape(1, bw), norm_g.reshape(1, HEAD_DIM), s0)


def _mixer_c_body(q_ref, k_ref, v_ref, r_ref, lr_ref, w2_ref, ba_ref, ng_ref, s0_ref, o_ref, sout_ref, st_ref):
    kd = C_KDIM

    @pl.when(pl.program_id(1) == 0)
    def _():
        _load_state_t(s0_ref, st_ref, kd)

    zpad = jnp.zeros((GLA_CHUNK, HEAD_DIM - kd), F32)

    def chunk(rows):
        gate_in = _dot(lr_ref[0, rows, 0:C_RANK], w2_ref[...]) + ba_ref[...]
        g_all = _log_sigmoid(gate_in) / C_TAU
        parts, vs = [], []
        for h in range(HEADS):
            kcols = slice(h * kd, (h + 1) * kd)
            q = jnp.concatenate([q_ref[0, rows, kcols] * (kd ** -0.5), zpad], axis=1)
            k = jnp.concatenate([k_ref[0, rows, kcols], zpad], axis=1)
            g = jnp.concatenate([g_all[:, kcols], zpad], axis=1)
            vs.append(v_ref[0, rows, h * HEAD_DIM:(h + 1) * HEAD_DIM])
            parts.append(_gla_head_chunk(q, k, vs[h], g))
        for h, o in enumerate(_gla_state_step(parts, vs, st_ref, 0)):
            vcols = slice(h * HEAD_DIM, (h + 1) * HEAD_DIM)
            o = _rms_rows(o) * ng_ref[...] * _silu(r_ref[0, rows, vcols])
            o_ref[0, rows, vcols] = o.astype(o_ref.dtype)

    _gla_chunk_loop(q_ref.shape[1], chunk)

    @pl.when(pl.program_id(1) == pl.num_programs(1) - 1)
    def _():
        _store_state_t(st_ref, sout_ref, kd)


def mixer_c(pseq, w_a2, b_a, norm_g, s0, *, tb):
    bsz, t, _ = pseq.shape
    bw = BRANCH_WIDTH
    kw = HEADS * C_KDIM
    field = lambda off, w: pl.BlockSpec((1, tb, w), lambda b, s: (b, s, off // w))
    state = pl.BlockSpec((1, HEADS, C_KDIM, HEAD_DIM), lambda b, s: (b, 0, 0, 0))
    return pl.pallas_call(
        _mixer_c_body,
        grid=(bsz, t // tb),
        in_specs=[field(OFF_C_Q, kw), field(OFF_C_K, kw), field(OFF_C_V, bw), field(OFF_C_R, bw),
                  field(OFF_C_LR, 128),
                  pl.BlockSpec((C_RANK, kw), lambda b, s: (0, 0)),
                  pl.BlockSpec((1, kw), lambda b, s: (0, 0)),
                  pl.BlockSpec((1, HEAD_DIM), lambda b, s: (0, 0)),
                  state],
        out_specs=[pl.BlockSpec((1, tb, bw), lambda b, s: (b, s, 0)), state],
        out_shape=[jax.ShapeDtypeStruct((bsz, t, bw), BF16),
                   jax.ShapeDtypeStruct((bsz, HEADS, C_KDIM, HEAD_DIM), F32)],
        scratch_shapes=[pltpu.VMEM((1, HEAD_DIM, HEADS * HEAD_DIM), F32)],
        compiler_params=_cparams(("parallel", "arbitrary")),
        name="mixer_c",
    )(pseq, pseq, pseq, pseq, pseq, w_a2, b_a.reshape(1, kw), norm_g.reshape(1, HEAD_DIM), s0)


def _grouped_pairs_loop(n_rows, seg, st_ref, head_inputs, head_outputs):
    def step(it, carry):
        rows = pl.ds(pl.multiple_of(it * 16, 16), 16)
        ins = [head_inputs(rows, h) for h in range(HEADS)]
        halves = []
        for half in range(2):
            sl = slice(half * 8, (half + 1) * 8)
            vs = [x[2][sl] for x in ins]
            tiles = [_gla_pair_tile(x[0][sl], x[1][sl], x[2][sl], x[3][sl], seg) for x in ins]
            halves.append(_gla_pair_step(tiles, vs, st_ref, 2 * it + half, seg))
        for h in range(HEADS):
            head_outputs(rows, h, jnp.concatenate([halves[0][h], halves[1][h]], axis=0))
        return carry

    lax.fori_loop(0, n_rows // 16, step, 0, unroll=2)


def _mixer_b_group_body(seg, q_ref, f_ref, i_ref, og_ref, lb_ref, ng_ref, s0_ref, o_ref, sout_ref, st_ref):
    kd = HEAD_DIM
    _load_state_natural(s0_ref, st_ref, kd)

    def head_inputs(rows, h):
        cols = slice(h * kd, (h + 1) * kd)
        lb = lb_ref[:, cols]
        fp = f_ref[0, rows, cols]
        f = lb + (1.0 - lb) * jax.nn.sigmoid(fp)
        return (q_ref[0, rows, cols], (1.0 - lb) * jax.nn.sigmoid(-fp), i_ref[0, rows, cols],
                jnp.log(jnp.maximum(f, F_TINY)))

    def head_outputs(rows, h, o):
        cols = slice(h * kd, (h + 1) * kd)
        o = _rms_rows(o) * ng_ref[...] * jax.nn.sigmoid(og_ref[0, rows, cols])
        o_ref[0, rows, cols] = o.astype(o_ref.dtype)

    _grouped_pairs_loop(q_ref.shape[1], seg, st_ref, head_inputs, head_outputs)
    _store_state_natural(st_ref, sout_ref, kd)


def _mixer_c_group_body(seg, q_ref, k_ref, v_ref, r_ref, lr_ref, w2_ref, ba_ref, ng_ref, s0_ref, o_ref, sout_ref,
                        st_ref, g_scr):
    kd = C_KDIM
    _load_state_natural(s0_ref, st_ref, kd)
    g_scr[...] = _log_sigmoid(_dot(lr_ref[0, :, 0:C_RANK], w2_ref[...]) + ba_ref[...]) / C_TAU
    zpad = jnp.zeros((16, HEAD_DIM - kd), F32)

    def head_inputs(rows, h):
        kcols = slice(h * kd, (h + 1) * kd)
        pad = lambda x: jnp.concatenate([x, zpad], axis=1)
        return (pad(q_ref[0, rows, kcols] * (kd ** -0.5)), pad(k_ref[0, rows, kcols]),
                v_ref[0, rows, h * HEAD_DIM:(h + 1) * HEAD_DIM], pad(g_scr[rows, kcols]))

    def head_outputs(rows, h, o):
        vcols = slice(h * HEAD_DIM, (h + 1) * HEAD_DIM)
        o = _rms_rows(o) * ng_ref[...] * _silu(r_ref[0, rows, vcols])
        o_ref[0, rows, vcols] = o.astype(o_ref.dtype)

    _grouped_pairs_loop(q_ref.shape[1], seg, st_ref, head_inputs, head_outputs)
    _store_state_natural(st_ref, sout_ref, kd)


def _layer_stacked_state(body, prev_states, state_ref_pos, seqs_per_step):
    n_prev = 0 if prev_states is None else prev_states.shape[0]

    def wrapped(*refs):
        refs = list(refs)
        if n_prev:
            prev_ref = refs.pop(0)
            refs[state_ref_pos][0:n_prev] = prev_ref[...]
        refs[state_ref_pos] = refs[state_ref_pos].at[n_prev]
        return body(*refs)

    def spec(shape_tail, layers):
        return pl.BlockSpec((layers, seqs_per_step) + shape_tail, lambda s: (0, s) + (0,) * len(shape_tail))

    def out_shape(s0):
        return jax.ShapeDtypeStruct((n_prev + 1,) + s0.shape[1:], F32)

    prev_specs = lambda s0: [spec(s0.shape[2:], n_prev)] if n_prev else []
    prev_inputs = [prev_states] if n_prev else []
    return wrapped, prev_specs, prev_inputs, lambda s0: spec(s0.shape[2:], n_prev + 1), out_shape


def _grouped_gla_call(body, prows, fields, params, s0, prev_states, kdim, *, seg, tb, extra_scratch=()):
    m = prows.shape[0]
    bw = BRANCH_WIDTH
    p3 = prows.reshape(1, m, -1)
    layer = 0 if prev_states is None else prev_states.shape[0]
    state = pl.BlockSpec((None, tb // seg, HEADS, kdim, HEAD_DIM), lambda s: (layer, s, 0, 0, 0))
    n_in = len(fields) + len(params) + 1
    wrapped, prev_specs, prev_inputs, state_out_spec, state_out_shape = _layer_stacked_state(
        functools.partial(body, seg), prev_states, n_in + 1, tb // seg)
    o, s_out = pl.pallas_call(
        wrapped,
        grid=(m // tb,),
        in_specs=prev_specs(s0)
        + [pl.BlockSpec((1, tb, w), functools.partial(lambda s, c: (0, s, c), c=off // w)) for off, w in fields]
        + [pl.BlockSpec(x.shape, functools.partial(lambda s, n: (0,) * n, n=x.ndim)) for x in params] + [state],
        out_specs=[pl.BlockSpec((1, tb, bw), lambda s: (0, s, 0)), state_out_spec(s0)],
        out_shape=[jax.ShapeDtypeStruct((1, m, bw), BF16), state_out_shape(s0)],
        scratch_shapes=[pltpu.VMEM((tb // seg, HEADS * HEAD_DIM, HEAD_DIM), F32), *extra_scratch],
        compiler_params=_cparams(("parallel",)),
        name=body.__name__.strip("_"),
    )(*prev_inputs, *([p3] * len(fields)), *params, s0)
    return o.reshape(m, bw), s_out


def mixer_b_grouped(prows, lb, norm_g, s0, prev_states, *, seg, tb):
    bw = BRANCH_WIDTH
    return _grouped_gla_call(
        _mixer_b_group_body, prows, [(OFF_B_Q, bw), (OFF_B_F, bw), (OFF_B_I, bw), (OFF_B_G, bw)],
        [lb.reshape(1, bw), norm_g.reshape(1, HEAD_DIM)], s0, prev_states, HEAD_DIM, seg=seg, tb=tb)


def mixer_c_grouped(prows, w_a2, b_a, norm_g, s0, prev_states, *, seg, tb):
    bw = BRANCH_WIDTH
    kw = HEADS * C_KDIM
    return _grouped_gla_call(
        _mixer_c_group_body, prows, [(OFF_C_Q, kw), (OFF_C_K, kw), (OFF_C_V, bw), (OFF_C_R, bw), (OFF_C_LR, 128)],
        [w_a2, b_a.reshape(1, kw), norm_g.reshape(1, HEAD_DIM)], s0, prev_states, C_KDIM, seg=seg, tb=tb,
        extra_scratch=(pltpu.VMEM((tb, kw), F32),))


def _unit_lower_inverse(a, order):
    c = a.shape[0]
    eye = (lax.broadcasted_iota(jnp.int32, (c, c), 0) == lax.broadcasted_iota(jnp.int32, (c, c), 1)).astype(F32)
    p = eye - a
    pw = a
    n = 2
    while n < order:
        pw = _dot(pw, pw)
        p = p + _dot(p, pw)
        n *= 2
    return p


D_ROWS = 64


def _stack_heads(x):
    return jnp.concatenate([x[:, h * HEAD_DIM:(h + 1) * HEAD_DIM] for h in range(HEADS)], axis=0)


def _stack_head_lanes(x):
    return jnp.concatenate([x[:, h:h + 1] for h in range(HEADS)], axis=0)


def _lane_pad_heads(x):
    return jnp.pad(x.reshape(1, HEADS), ((0, 0), (0, 128 - HEADS)))


def _segment_cumsum(x, seg):
    tpos = lax.broadcasted_iota(jnp.int32, x.shape, 0) & (seg - 1)
    sh = 1
    while sh < seg:
        x = x + jnp.where(tpos >= sh, pltpu.roll(x, sh, axis=0), 0.0)
        sh *= 2
    return x


def _delta_chunk_operands(seg, qkv, beta_all, g_all):
    hd = HEAD_DIM
    bw = BRANCH_WIDTH
    r = HEADS * qkv.shape[0]
    q = _stack_heads(qkv[:, 0:bw])
    k = _stack_heads(qkv[:, bw:2 * bw])
    v = _stack_heads(qkv[:, 2 * bw:3 * bw])
    q = q * lax.rsqrt(jnp.sum(q * q, axis=-1, keepdims=True) + EPS) * (hd ** -0.5)
    k = k * lax.rsqrt(jnp.sum(k * k, axis=-1, keepdims=True) + EPS)
    beta = _stack_head_lanes(beta_all)
    gc = _stack_head_lanes(_segment_cumsum(g_all, seg))
    ri = lax.broadcasted_iota(jnp.int32, (r, r), 0)
    ci = lax.broadcasted_iota(jnp.int32, (r, r), 1)
    shift = seg.bit_length() - 1
    same = (ri >> shift) == (ci >> shift)
    gr = jnp.sum(jnp.where(ri == ci, gc, 0.0), axis=0, keepdims=True)
    decay = jnp.where(same, jnp.where(ri >= ci, jnp.exp(jnp.minimum(gc - gr, 0.0)), 0.0), 0.0)
    a_mat = jnp.where(ri > ci, beta * decay * _dot_nt(k, k), 0.0)
    t_inv = _unit_lower_inverse(a_mat, seg)
    eg = jnp.exp(gc)
    sol = _dot(t_inv, jnp.concatenate([(beta * eg) * k, beta * v], axis=1))
    qk = _dot_nt(q, k) * decay
    is_last = (ci & (seg - 1)) == seg - 1
    gl = jnp.sum(jnp.where(same, jnp.where(is_last, gr, 0.0), 0.0), axis=1, keepdims=True)
    return sol[:, :hd], sol[:, hd:], qk, q * eg, k * jnp.exp(gl - gc), jnp.exp(gl)


def _mixer_d_body(x_ref, beta_ref, dec_ref, z_ref, cst_ref, s0_ref, cw_ref, alog_ref, dtb_ref, ng_ref,
                  o_ref, cst_out_ref, sout_ref, ext, conv, st_ref):
    nb, tb = x_ref.shape[0], x_ref.shape[1]
    c = D_ROWS
    hd = HEAD_DIM
    t = pl.program_id(1)
    nt = pl.num_programs(1)

    @pl.when(t == 0)
    def _():
        ext[:, 5:8, :] = cst_ref[...]
        st_ref[...] = s0_ref[...].reshape(nb * HEADS, hd, hd)

    for bi in range(nb):
        ext[bi, 8:8 + tb, :] = x_ref[bi]
        y = ext[bi, 5:5 + tb, :] * cw_ref[0:1, :]
        for j in range(1, D_CONV):
            y = y + ext[bi, 5 + j:5 + j + tb, :] * cw_ref[j:j + 1, :]
        conv[bi] = _silu(y)

    @pl.when(t == nt - 1)
    def _():
        cst_out_ref[...] = ext[:, 8 + tb - (D_CONV - 1):8 + tb, :]

    ext[:, 5:8, :] = ext[:, 5 + tb:8 + tb, :]

    a_neg = -jnp.exp(alog_ref[...])

    def chunk(n, carry):
        rows = pl.ds(pl.multiple_of(n * c, c), c)
        for bi in range(nb):
            beta_all = jax.nn.sigmoid(beta_ref[bi, rows, :])
            g_all = a_neg * _softplus(dec_ref[bi, rows, :] + dtb_ref[...])
            w, u, qk, qg, kdec, egl = _delta_chunk_operands(c, conv[bi, rows, :], beta_all, g_all)
            deltas, oqs = [], []
            for h in range(HEADS):
                hs = slice(h * c, (h + 1) * c)
                st = st_ref[bi * HEADS + h]
                x = _dot(jnp.concatenate([w[hs], qg[hs]], axis=0), st)
                delta = u[hs] - x[:c]
                deltas.append(delta)
                oqs.append(x[c:])
                st_ref[bi * HEADS + h] = egl[(h + 1) * c - 1:(h + 1) * c, :] * st + _dot_tn(kdec[hs], delta)
            o = jnp.concatenate(oqs, axis=0) + _dot(qk, jnp.concatenate(deltas, axis=0))
            o = _rms_rows(o) * ng_ref[...] * _silu(_stack_heads(z_ref[bi, rows, :]))
            for h in range(HEADS):
                o_ref[bi, rows, h * hd:(h + 1) * hd] = o[h * c:(h + 1) * c].astype(o_ref.dtype)
        return carry

    lax.fori_loop(0, tb // c, chunk, 0, unroll=2)

    @pl.when(t == nt - 1)
    def _():
        sout_ref[...] = st_ref[...].reshape(nb, HEADS, hd, hd)


def _mixer_d_group_body(seg, x_ref, cst_ref, beta_ref, dec_ref, z_ref, s0_ref, cw_ref, alog_ref, dtb_ref, ng_ref,
                        o_ref, cst_out_ref, sout_ref, conv, w_s, u_s, qg_s, kd_s, egl_s, delta_s, oq_s, stp_s, x_s):
    tb = x_ref.shape[1]
    nseq = tb // seg
    c = D_ROWS
    hd = HEAD_DIM
    pairs = c // 8
    hist_rows = D_CONV - 1

    lane_blocks = stp_s.shape[0]
    x = x_ref[0]
    stp_s[...] = jnp.zeros_like(stp_s)
    for cb in range(lane_blocks):
        lanes = slice(cb * 128, (cb + 1) * 128)
        x_s[cb] = x[:, lanes]
        for j in range(hist_rows):
            stp_s[cb, pl.ds(j, nseq, stride=seg), :] = cst_ref[:, j, lanes]
    stp = jnp.concatenate([stp_s[cb] for cb in range(lane_blocks)], axis=1)

    tpos = lax.broadcasted_iota(jnp.int32, (tb, 1), 0) & (seg - 1)
    y = x * cw_ref[3:4, :]
    for d in range(1, D_CONV):
        hist = stp if d == 3 else pltpu.roll(stp, tb - (3 - d), axis=0)
        y = y + jnp.where(tpos < d, hist, pltpu.roll(x, d, axis=0)) * cw_ref[3 - d:4 - d, :]
    conv[...] = _silu(y)
    for cb in range(lane_blocks):
        for j in range(hist_rows):
            cst_out_ref[:, j, cb * 128:(cb + 1) * 128] = x_s[cb, pl.ds(seg - hist_rows + j, nseq, stride=seg), :]

    a_neg = -jnp.exp(alog_ref[...])
    low = lax.broadcasted_iota(jnp.int32, (8, 1), 0) < seg

    for n in range(tb // c):
        rows = slice(n * c, (n + 1) * c)
        beta_all = jax.nn.sigmoid(beta_ref[0, rows, :])
        g_all = a_neg * _softplus(dec_ref[0, rows, :] + dtb_ref[...])
        w, u, qk, qg, kdec, egl = _delta_chunk_operands(seg, conv[rows, :], beta_all, g_all)
        w_s[...] = w
        u_s[...] = u
        qg_s[...] = qg
        kd_s[...] = kdec
        egl_s[...] = jnp.broadcast_to(egl, (HEADS * c, hd))

        def pair(p, carry):
            rp = pl.ds(pl.multiple_of(p * 8, 8), 8)
            h = p // pairs
            ja = n * (c // seg) + (p % pairs) * 2
            sa = s0_ref[ja, h]
            sb = s0_ref[ja + 1, h]
            lhs = jnp.concatenate([w_s[rp, :], qg_s[rp, :]], axis=0)
            xa = _dot(lhs, sa)
            xb = _dot(lhs, sb)
            delta = u_s[rp, :] - jnp.where(low, xa[:8], xb[:8])
            delta_s[rp, :] = delta
            oq_s[rp, :] = jnp.where(low, xa[8:], xb[8:])
            kd = kd_s[rp, :]
            e = egl_s[rp, :]
            sout_ref[ja, h] = e[seg - 1:seg, :] * sa + _dot_tn(jnp.where(low, kd, 0.0), delta)
            sout_ref[ja + 1, h] = e[2 * seg - 1:2 * seg, :] * sb + _dot_tn(jnp.where(low, 0.0, kd), delta)
            return carry

        lax.fori_loop(0, HEADS * pairs, pair, 0, unroll=4)
        o = oq_s[...] + _dot(qk, delta_s[...])
        o = _rms_rows(o) * ng_ref[...] * _silu(_stack_heads(z_ref[0, rows, :]))
        for h in range(HEADS):
            o_ref[0, rows, h * hd:(h + 1) * hd] = o[h * c:(h + 1) * c].astype(o_ref.dtype)


def mixer_d(pseq, conv_state, s0, conv_w, a_log, dt_bias, norm_g, *, tb, nb):
    bsz, t, _ = pseq.shape
    bw = BRANCH_WIDTH
    field = lambda off, w: pl.BlockSpec((nb, tb, w), lambda b, s: (b, s, off // w))
    state = pl.BlockSpec((nb, HEADS, HEAD_DIM, HEAD_DIM), lambda b, s: (b, 0, 0, 0))
    cstate = pl.BlockSpec((nb, D_CONV - 1, D_QKV), lambda b, s: (b, 0, 0))
    return pl.pallas_call(
        _mixer_d_body,
        grid=(bsz // nb, t // tb),
        in_specs=[field(OFF_D_QKV, D_QKV), field(OFF_D_BETA, 128), field(OFF_D_DECAY, 128), field(OFF_D_Z, bw),
                  cstate, state,
                  pl.BlockSpec((D_CONV, D_QKV), lambda b, s: (0, 0)),
                  pl.BlockSpec((1, 128), lambda b, s: (0, 0)),
                  pl.BlockSpec((1, 128), lambda b, s: (0, 0)),
                  pl.BlockSpec((1, HEAD_DIM), lambda b, s: (0, 0))],
        out_specs=[pl.BlockSpec((nb, tb, bw), lambda b, s: (b, s, 0)), cstate, state],
        out_shape=[jax.ShapeDtypeStruct((bsz, t, bw), BF16),
                   jax.ShapeDtypeStruct((bsz, D_CONV - 1, D_QKV), F32),
                   jax.ShapeDtypeStruct((bsz, HEADS, HEAD_DIM, HEAD_DIM), F32)],
        scratch_shapes=[pltpu.VMEM((nb, tb + 8, D_QKV), F32), pltpu.VMEM((nb, tb, D_QKV), F32),
                        pltpu.VMEM((nb * HEADS, HEAD_DIM, HEAD_DIM), F32)],
        compiler_params=_cparams(("parallel", "arbitrary")),
        name="mixer_d",
    )(pseq, pseq, pseq, pseq, conv_state, s0, conv_w, _lane_pad_heads(a_log), _lane_pad_heads(dt_bias),
      norm_g.reshape(1, HEAD_DIM))


def mixer_d_grouped(prows, conv_state, s0, conv_w, a_log, dt_bias, norm_g, prev_states, *, seg, tb):
    m = prows.shape[0]
    bsz = m // seg
    bw = BRANCH_WIDTH
    field = lambda off, w: pl.BlockSpec((1, tb, w), lambda s: (0, s, off // w))
    layer = 0 if prev_states is None else prev_states.shape[0]
    state = pl.BlockSpec((None, tb // seg, HEADS, HEAD_DIM, HEAD_DIM), lambda s: (layer, s, 0, 0, 0))
    cstate = pl.BlockSpec((tb // seg, D_CONV - 1, D_QKV), lambda s: (s, 0, 0))
    rows128 = pltpu.VMEM((HEADS * D_ROWS, HEAD_DIM), F32)
    lane_rows = pltpu.VMEM((D_QKV // 128, tb, 128), F32)
    n_in, state_out_pos = 10, 2
    wrapped, prev_specs, prev_inputs, state_out_spec, state_out_shape = _layer_stacked_state(
        functools.partial(_mixer_d_group_body, seg), prev_states, n_in + state_out_pos, tb // seg)
    o, cst, s_out = pl.pallas_call(
        wrapped,
        grid=(m // tb,),
        in_specs=prev_specs(s0) + [
                  field(OFF_D_QKV, D_QKV), cstate,
                  field(OFF_D_BETA, 128), field(OFF_D_DECAY, 128), field(OFF_D_Z, bw),
                  state,
                  pl.BlockSpec((D_CONV, D_QKV), lambda s: (0, 0)),
                  pl.BlockSpec((1, 128), lambda s: (0, 0)),
                  pl.BlockSpec((1, 128), lambda s: (0, 0)),
                  pl.BlockSpec((1, HEAD_DIM), lambda s: (0, 0))],
        out_specs=[pl.BlockSpec((1, tb, bw), lambda s: (0, s, 0)), cstate, state_out_spec(s0)],
        out_shape=[jax.ShapeDtypeStruct((1, m, bw), BF16),
                   jax.ShapeDtypeStruct((bsz, D_CONV - 1, D_QKV), F32),
                   state_out_shape(s0)],
        scratch_shapes=[pltpu.VMEM((tb, D_QKV), F32)] + [rows128] * 7 + [lane_rows] * 2,
        compiler_params=_cparams(("parallel",)),
        name="mixer_d_grouped",
    )(*prev_inputs, prows.reshape(1, m, -1), conv_state, prows.reshape(1, m, -1), prows.reshape(1, m, -1),
      prows.reshape(1, m, -1), s0, conv_w, _lane_pad_heads(a_log), _lane_pad_heads(dt_bias),
      norm_g.reshape(1, HEAD_DIM))
    return o.reshape(m, bw), cst, s_out


PACK_TILE = 512
PACK_SRC = ([4624, 5136, 5648, 0, 512, 1024, 1536, 2048, 2560, 3584, 4112, 6168, 3072, None]
            + [6680 + PACK_TILE * k for k in range(N_BRANCH * D_MODEL // PACK_TILE)])
PACK_SMALL_TILE = PACK_SRC.index(None)


def _pack_lookup(t, values):
    out = jnp.int32(0)
    for i, v in enumerate(values):
        out = jnp.where(t == i, jnp.int32(v), out)
    return out


PACK_EXTRA = 32


def _pack_w_in_body(a_ref, b_ref, tail_ref, small_ref, o_ref):
    t = pl.program_id(1)
    last = len(PACK_SRC) - 1
    shifts = [0 if s is None else s % PACK_TILE for s in PACK_SRC]
    assert max(shifts) <= PACK_EXTRA and all(s % 8 == 0 for s in shifts)

    def emit(shift, left_ref, right_ref):
        for c in range(a_ref.shape[1] // 128):
            lanes = slice(c * 128, (c + 1) * 128)
            w = left_ref[shift:, lanes]
            if shift:
                w = jnp.concatenate([w, right_ref[0:shift, lanes]], axis=0)
            o_ref[lanes, :] = w.T.astype(BF16)

    for shift in sorted(set(shifts)):
        tiles = [i for i, s in enumerate(shifts) if s == shift and i not in (PACK_SMALL_TILE, last)]
        pl.when(functools.reduce(jnp.logical_or, [t == i for i in tiles]))(
            functools.partial(emit, shift, a_ref, b_ref))
    pl.when(t == last)(functools.partial(emit, shifts[last], a_ref, tail_ref))
    pl.when(t == PACK_SMALL_TILE)(functools.partial(emit, 0, small_ref, None))


def pack_w_in(w_in):
    layers, k, n_src = w_in.shape
    w_t = jnp.swapaxes(w_in, 1, 2)
    bases = [0 if s is None else s - s % PACK_TILE for s in PACK_SRC]
    extra = [(b + PACK_TILE) // PACK_EXTRA for b in bases]
    full_blocks = n_src // PACK_EXTRA
    assert all(e < full_blocks for e in extra[:-1]) and extra[-1] == full_blocks
    extra[-1] = 0
    z = lambda n: jnp.zeros((layers, n, k), w_in.dtype)
    tail = jnp.concatenate([w_t[:, full_blocks * PACK_EXTRA:], z(PACK_EXTRA - n_src % PACK_EXTRA)], axis=1)
    small = jnp.concatenate([w_t[:, 4096:4112], z(112),
                             w_t[:, 6160:6164], z(124),
                             w_t[:, 6164:6168], z(252)], axis=1)
    return pl.pallas_call(
        _pack_w_in_body,
        grid=(layers, len(PACK_SRC)),
        in_specs=[pl.BlockSpec((None, PACK_TILE, k),
                               lambda l, t: (l, _pack_lookup(t, [b // PACK_TILE for b in bases]), 0)),
                  pl.BlockSpec((None, PACK_EXTRA, k), lambda l, t: (l, _pack_lookup(t, extra), 0)),
                  pl.BlockSpec((None, PACK_EXTRA, k), lambda l, t: (l, 0, 0)),
                  pl.BlockSpec((None, PACK_TILE, k), lambda l, t: (l, 0, 0))],
        out_specs=pl.BlockSpec((None, k, PACK_TILE), lambda l, t: (l, 0, t)),
        out_shape=jax.ShapeDtypeStruct((layers, k, N_PROJ), BF16),
        compiler_params=_cparams(("parallel", "arbitrary")),
        name="pack_w_in",
    )(w_t, w_t, tail, small)


def _trunk(x, p, s_hgrn, s_gla, s_delta, c_dconv, c_fconv, wts, lbs, *, sample):
    bsz, t, _ = x.shape
    m = bsz * t
    tm = m if sample else 1024
    h = x.reshape(m, D_MODEL)
    outs = {k: [] for k in ("hgrn", "gla", "delta", "dconv", "fconv", "v")}
    s_b = s_c = s_d = None
    for l in range(DEPTH):
        w = wts[l]
        proj, gates = in_proj(h, w["g_mix"], w["w_in_layers"], l, tm=tm, tn=1024)
        o_a, v_a = mixer_a(proj, w["a_ln_g"], w["a_ln_b"], w["a_w_mix"], w["a_bias_rows"])
        st_b, st_c, st_d = (s_hgrn, s_gla, s_delta) if sample else (s_hgrn[l], s_gla[l], s_delta[l])
        b_args = (lbs[l], w["b_norm_g"], st_b)
        c_args = (w["c_w_a2"], w["c_b_a"], w["c_norm_g"], st_c)
        d_args = (c_dconv[l], st_d, w["d_conv_w"], w["d_a_log"], w["d_dt_bias"], w["d_norm_g"])
        if sample:
            o_b, s_b = mixer_b_grouped(proj, *b_args, s_b, seg=t, tb=D_ROWS)
            o_c, s_c = mixer_c_grouped(proj, *c_args, s_c, seg=t, tb=D_ROWS)
            o_d, nb_d, s_d = mixer_d_grouped(proj, *d_args, s_d, seg=t, tb=D_ROWS)
        else:
            pseq = proj.reshape(bsz, t, N_SEQ_COLS)
            flat = lambda o: o.reshape(m, BRANCH_WIDTH)
            o_b, s_b = mixer_b(pseq, *b_args, tb=256)
            o_c, s_c = mixer_c(pseq, *c_args, tb=256)
            o_d, nb_d, s_d = mixer_d(pseq, *d_args, tb=256, nb=2 if bsz % 2 == 0 else 1)
            o_b, o_c, o_d = flat(o_b), flat(o_c), flat(o_d)
        merged = merge_branches((o_a, o_b, o_c, o_d), gates, w["w_branch"], tm=tm, tn=512)
        h = matmul_residual(merged, w["w_out_layers"], l, h, tm=tm, tn=1024)

        act, nb_f = ffn_up_gate(h, w["g_ffn"], w["w_ffn_up_layers"], l, c_fconv[l], w["ffn_conv_w"],
                                w["ffn_conv_b"], seq_len=t, tm=tm, tn=512)
        h = matmul_residual(act, w["w_ffn_down_layers"], l, h, tm=min(tm, 512), tn=512)

        h = ple_update(h, p[l].reshape(m, PLE_DIM), w["g_ple"], w["w_ple_gate"], w["w_ple_proj"],
                       w["g_final"] if l == DEPTH - 1 else None, tm=min(tm, 512))

        if not sample:
            outs["hgrn"].append(s_b)
            outs["gla"].append(s_c)
            outs["delta"].append(s_d)
        outs["dconv"].append(nb_d)
        outs["fconv"].append(nb_f)
        outs["v"].append(v_a.reshape(bsz, t, BRANCH_WIDTH))
    y = h.reshape(bsz, t, D_MODEL)
    states = (s_b, s_c, s_d) if sample else tuple(jnp.stack(outs[k]) for k in ("hgrn", "gla", "delta"))
    return (y,) + states + tuple(jnp.stack(outs[k]) for k in ("dconv", "fconv", "v"))


def kernel(x_prompt, x_sample, state_hgrn, state_gla, state_delta, state_delta_conv, state_ffn_conv, p_prompt, p_sample, g_mix, w_in, a_ln_g, a_ln_b, a_w_s, a_b_s, b_lb, b_norm_g, c_w_a2, c_b_a, c_norm_g, d_conv_w, d_a_log, d_dt_bias, d_norm_g, w_branch, w_out, g_ffn, w_ffn_up, ffn_conv_w, ffn_conv_b, w_ffn_down, g_ple, w_ple_gate, w_ple_proj, g_final):
    bp, t_p, _ = x_prompt.shape
    bs, t_s, _ = x_sample.shape
    sm = jax.nn.softmax(b_lb.astype(F32), axis=0)
    lbs = jnp.cumsum(sm, axis=0) - sm[0]

    w_in_packed = pack_w_in(w_in)
    shared = [dict(
        g_mix=g_mix[l], w_in_layers=w_in_packed, a_ln_g=a_ln_g[l], a_ln_b=a_ln_b[l],
        b_norm_g=b_norm_g[l], c_w_a2=c_w_a2[l], c_b_a=c_b_a[l], c_norm_g=c_norm_g[l],
        d_conv_w=d_conv_w[l], d_a_log=d_a_log[l], d_dt_bias=d_dt_bias[l], d_norm_g=d_norm_g[l],
        w_branch=w_branch[l].astype(BF16), w_out_layers=w_out, g_ffn=g_ffn[l],
        w_ffn_up_layers=w_ffn_up, ffn_conv_w=ffn_conv_w[l], ffn_conv_b=ffn_conv_b[l],
        w_ffn_down_layers=w_ffn_down, g_ple=g_ple[l], w_ple_gate=w_ple_gate[l].astype(BF16),
        w_ple_proj=w_ple_proj[l].astype(BF16), g_final=g_final) for l in range(DEPTH)]

    def layer_weights(l, sample):
        if sample:
            seqs = A_CHUNK // t_s
            idx = jnp.arange(A_CHUNK) // t_s
            same_seq = idx[:, None] == idx[None, :]
            w_mix = jnp.where(same_seq, jnp.tile(a_w_s[l, :, :t_s, :t_s], (1, seqs, seqs)), 0.0)
            bias_rows = jnp.tile(a_b_s[l, :, :t_s].T, (seqs, 1))
        else:
            w_mix = a_w_s[l]
            bias_rows = a_b_s[l].T
        return dict(shared[l], a_w_mix=w_mix, a_bias_rows=bias_rows)

    dt = x_prompt.dtype
    zeros = lambda *s: jnp.zeros((DEPTH, bp) + s, dt)
    out_p = _trunk(x_prompt, p_prompt, zeros(HEADS, HEAD_DIM, HEAD_DIM), zeros(HEADS, C_KDIM, HEAD_DIM),
                   zeros(HEADS, HEAD_DIM, HEAD_DIM), zeros(D_CONV - 1, D_QKV), zeros(FFN_CONV - 1, FFN_DIM),
                   [layer_weights(l, False) for l in range(DEPTH)], lbs, sample=False)
    out_s = _trunk(x_sample, p_sample, state_hgrn, state_gla, state_delta, state_delta_conv, state_ffn_conv,
                   [layer_weights(l, True) for l in range(DEPTH)], lbs, sample=True)
    y_p, hgrn_p, gla_p, delta_p, dconv_p, fconv_p, _ = out_p
    y_s, hgrn_s, gla_s, delta_s, dconv_s, fconv_s, v_s = out_s
    return (y_p, y_s, hgrn_p, hgrn_s, gla_p, gla_s, delta_p, delta_s,
            dconv_p, dconv_s, fconv_p, fconv_s, v_s)
```

```python
import functools

import jax
import jax.numpy as jnp
from jax import lax
from jax.experimental import pallas as pl
from jax.experimental.pallas import tpu as pltpu

F32 = jnp.float32
BF16 = jnp.bfloat16

D_MODEL = 2048
DEPTH = 2
PLE_DIM = 256
EPS = 1e-6
F_TINY = 1e-30
N_BRANCH = 4
BRANCH_WIDTH = D_MODEL // 4
A_GROUPS = 4
A_CHUNK = 128
HEADS = 4
HEAD_DIM = 128
C_KDIM = 64
C_RANK = 16
C_TAU = 16.0
D_CONV = 4
D_QKV = 3 * BRANCH_WIDTH
FFN_DIM = 5632
FFN_CONV = 3
GLA_CHUNK = 16
LOG2E = 1.4426950408889634

OFF_D_QKV = 0
OFF_A_U = 1536
OFF_A_V = 2048
OFF_B_Q = 2560
OFF_B_F = 3072
OFF_B_I = 3584
OFF_B_G = 4096
OFF_C_V = 4608
OFF_C_R = 5120
OFF_D_Z = 5632
OFF_C_Q = 6144
OFF_C_K = 6400
OFF_C_LR = 6656
OFF_D_BETA = 6784
OFF_D_DECAY = 6912
OFF_GATES = 7168
N_SEQ_COLS = 7168
N_PROJ = OFF_GATES + N_BRANCH * D_MODEL

VMEM_LIMIT = 56 * 1024 * 1024


def _cparams(sem):
    return pltpu.CompilerParams(dimension_semantics=sem, vmem_limit_bytes=VMEM_LIMIT)


def _gelu(x):
    return 0.5 * x * (1.0 + jnp.tanh(0.7978845608028654 * (x + 0.044715 * (x * x * x))))


def _silu(x):
    return x * jax.nn.sigmoid(x)


def _softplus(x):
    return jnp.maximum(x, 0.0) + jnp.log1p(jnp.exp(-jnp.abs(x)))


def _log_sigmoid(x):
    return -_softplus(-x)


def _rms_rows(x):
    return x * lax.rsqrt(jnp.mean(x * x, axis=-1, keepdims=True) + EPS)


def _dot(a, b):
    return jnp.dot(a, b, preferred_element_type=F32)


def _dot_nt(a, b):
    return lax.dot_general(a, b, (((1,), (1,)), ((), ())), preferred_element_type=F32)


def _dot_tn(a, b):
    return lax.dot_general(a, b, (((0,), (0,)), ((), ())), preferred_element_type=F32)


def _cumsum_rows(x):
    n = x.shape[0]
    row = lax.broadcasted_iota(jnp.int32, x.shape, 0)
    sh = 1
    while sh < n:
        x = x + jnp.where(row >= sh, pltpu.roll(x, sh, axis=0), 0.0)
        sh *= 2
    return x


NORM_ROWS = 128


def _norm_to_scratch(x_ref, g_ref, a_scr):
    def slab(n, carry):
        rows = pl.ds(pl.multiple_of(n * NORM_ROWS, NORM_ROWS), NORM_ROWS)
        a_scr[rows, :] = (_rms_rows(x_ref[rows, :]) * g_ref[...]).astype(BF16)
        return carry

    lax.fori_loop(0, x_ref.shape[0] // NORM_ROWS, slab, 0)


def _matmul_residual_body(x_ref, w_ref, r_ref, o_ref, w_scr):
    @pl.when(pl.program_id(1) == 0)
    def _():
        def slab(n, carry):
            rows = pl.ds(pl.multiple_of(n * NORM_ROWS, NORM_ROWS), NORM_ROWS)
            w_scr[rows, :] = w_ref[rows, :].astype(BF16)
            return carry

        lax.fori_loop(0, w_ref.shape[0] // NORM_ROWS, slab, 0)

    o_ref[...] = r_ref[...] + _dot(x_ref[...], w_scr[...])


def matmul_residual(x, w_layers, layer, res, *, tm, tn):
    m, k = x.shape
    n = w_layers.shape[2]
    return pl.pallas_call(
        _matmul_residual_body,
        grid=(n // tn, m // tm),
        in_specs=[pl.BlockSpec((tm, k), lambda j, i: (i, 0)),
                  pl.BlockSpec((None, k, tn), lambda j, i: (layer, 0, j)),
                  pl.BlockSpec((tm, tn), lambda j, i: (i, j))],
        out_specs=pl.BlockSpec((tm, tn), lambda j, i: (i, j)),
        out_shape=jax.ShapeDtypeStruct((m, n), F32),
        scratch_shapes=[pltpu.VMEM((k, tn), BF16)],
        compiler_params=_cparams(("parallel", "arbitrary")),
        name="matmul_residual",
    )(x, w_layers, res)


def _in_proj_body(n_seq_tiles, x_ref, g_ref, w_ref, seq_ref, gate_ref, a_scr):
    j = pl.program_id(1)

    @pl.when(j == 0)
    def _():
        _norm_to_scratch(x_ref, g_ref, a_scr)

    acc = _dot(a_scr[...], w_ref[...])

    @pl.when(j < n_seq_tiles)
    def _():
        seq_ref[...] = acc

    @pl.when(j >= n_seq_tiles)
    def _():
        gate_ref[...] = acc.astype(gate_ref.dtype)


def in_proj(x, g, w_layers, layer, *, tm, tn):
    m, k = x.shape
    n_seq = N_SEQ_COLS // tn
    n_gate = N_BRANCH * D_MODEL // tn
    return pl.pallas_call(
        functools.partial(_in_proj_body, n_seq),
        grid=(m // tm, n_seq + n_gate),
        in_specs=[pl.BlockSpec((tm, k), lambda i, j: (i, 0)),
                  pl.BlockSpec((1, k), lambda i, j: (0, 0)),
                  pl.BlockSpec((None, k, tn), lambda i, j: (layer, 0, j))],
        out_specs=[pl.BlockSpec((tm, tn), lambda i, j: (i, jnp.minimum(j, n_seq - 1))),
                   pl.BlockSpec((tm, tn), lambda i, j: (i, jnp.maximum(j - n_seq, 0)))],
        out_shape=[jax.ShapeDtypeStruct((m, N_SEQ_COLS), F32),
                   jax.ShapeDtypeStruct((m, N_BRANCH * D_MODEL), BF16)],
        scratch_shapes=[pltpu.VMEM((tm, k), BF16)],
        compiler_params=_cparams(("parallel", "arbitrary")),
        name="in_proj",
    )(x, g.reshape(1, k), w_layers)


def _merge_body(oa_ref, ob_ref, oc_ref, od_ref, ga_ref, gb_ref, gc_ref, gd_ref, w_ref, o_ref):
    acc = None
    for b, (o_b, g_b) in enumerate(((oa_ref, ga_ref), (ob_ref, gb_ref), (oc_ref, gc_ref), (od_ref, gd_ref))):
        gate = 0.5 + 0.5 * jnp.tanh(0.5 * g_b[...].astype(F32))
        term = gate * _dot(o_b[...], w_ref[b])
        acc = term if acc is None else acc + term
    o_ref[...] = acc.astype(o_ref.dtype)


def merge_branches(o_branches, gates, w_branch, *, tm, tn):
    m = gates.shape[0]
    gate_specs = [pl.BlockSpec((tm, tn), functools.partial(
        lambda i, j, b: (i, b * (D_MODEL // tn) + j), b=b)) for b in range(N_BRANCH)]
    return pl.pallas_call(
        _merge_body,
        grid=(m // tm, D_MODEL // tn),
        in_specs=[pl.BlockSpec((tm, BRANCH_WIDTH), lambda i, j: (i, 0))] * N_BRANCH + gate_specs
        + [pl.BlockSpec((N_BRANCH, BRANCH_WIDTH, tn), lambda i, j: (0, 0, j))],
        out_specs=pl.BlockSpec((tm, tn), lambda i, j: (i, j)),
        out_shape=jax.ShapeDtypeStruct((m, D_MODEL), BF16),
        compiler_params=_cparams(("parallel", "arbitrary")),
        name="merge_branches",
    )(*o_branches, gates, gates, gates, gates, w_branch)


PLE_ROWS = 256


def _ple_body(final, x_ref, p_ref, g_ref, wg_ref, wp_ref, *refs):
    gout_ref, o_ref = refs if final else (None, refs[0])

    slab_rows = min(PLE_ROWS, x_ref.shape[0])
    assert x_ref.shape[0] % slab_rows == 0

    def slab(n, carry):
        rows = pl.ds(pl.multiple_of(n * slab_rows, slab_rows), slab_rows)
        x = x_ref[rows, :]
        a = (_rms_rows(x) * g_ref[...]).astype(BF16)
        gate = 0.5 + 0.5 * jnp.tanh(0.5 * _dot(a, wg_ref[...]))
        h = x + _dot(p_ref[rows, :].astype(BF16), wp_ref[...]) * gate
        o_ref[rows, :] = h if gout_ref is None else _rms_rows(h) * gout_ref[...]
        return carry

    lax.fori_loop(0, x_ref.shape[0] // slab_rows, slab, 0, unroll=2)


def ple_update(x, p, g, w_gate, w_proj, g_out=None, *, tm):
    m, k = x.shape
    full = lambda a: pl.BlockSpec(a.shape, lambda i: (0,) * a.ndim)
    rows = lambda w: pl.BlockSpec((tm, w), lambda i: (i, 0))
    params = [g.reshape(1, k), w_gate, w_proj] + ([] if g_out is None else [g_out.reshape(1, k)])
    return pl.pallas_call(
        functools.partial(_ple_body, g_out is not None),
        grid=(m // tm,),
        in_specs=[rows(k), rows(PLE_DIM)] + [full(a) for a in params],
        out_specs=rows(k),
        out_shape=jax.ShapeDtypeStruct((m, k), F32),
        compiler_params=_cparams(("parallel",)),
        name="ple_update",
    )(x, p, *params)


FFN_SUB = 512


def _cast_rows_to_bf16(src_ref, dst_ref):
    def slab(n, carry):
        rows = pl.ds(pl.multiple_of(n * NORM_ROWS, NORM_ROWS), NORM_ROWS)
        dst_ref[rows, :] = src_ref[rows, :].astype(BF16)
        return carry

    lax.fori_loop(0, src_ref.shape[0] // NORM_ROWS, slab, 0)


def _ffn_up_long_body(tiles_per_seq, a_ref, wg_ref, wu_ref, st_ref, cw_ref, cb_ref, o_ref, st_out_ref,
                      wg_s, wu_s, ext):
    i = pl.program_id(1)
    tm = a_ref.shape[0]

    @pl.when(i == 0)
    def _():
        _cast_rows_to_bf16(wg_ref, wg_s)
        _cast_rows_to_bf16(wu_ref, wu_s)

    @pl.when(i % tiles_per_seq == 0)
    def _():
        ext[6:8, :] = st_ref[0]

    for r in range(0, tm, FFN_SUB):
        a = a_ref[r:r + FFN_SUB, :]
        ext[8 + r:8 + r + FFN_SUB, :] = _dot(a, wg_s[...])
        conv = (ext[6 + r:6 + r + FFN_SUB, :] * cw_ref[0:1, :] + ext[7 + r:7 + r + FFN_SUB, :] * cw_ref[1:2, :]
                + ext[8 + r:8 + r + FFN_SUB, :] * cw_ref[2:3, :])
        o_ref[r:r + FFN_SUB, :] = (_gelu(conv + cb_ref[...]) * _dot(a, wu_s[...])).astype(o_ref.dtype)

    last = ext[6 + tm:8 + tm, :]
    ext[6:8, :] = last
    st_out_ref[0] = last


def _ffn_up_group_body(seg, a_ref, wg_ref, wu_ref, st_ref, cw_ref, cb_ref, o_ref, st_out_ref, wg_s, wu_s, stp, fg_s):
    tm = a_ref.shape[0]
    nseq = tm // seg
    hist = FFN_CONV - 1

    @pl.when(pl.program_id(1) == 0)
    def _():
        _cast_rows_to_bf16(wg_ref, wg_s)
        _cast_rows_to_bf16(wu_ref, wu_s)

    lane_blocks = stp.shape[0]
    stp[...] = jnp.zeros_like(stp)
    for j in range(hist):
        for c in range(lane_blocks):
            stp[c, pl.ds(j, nseq, stride=seg), :] = st_ref[:, j, c * 128:(c + 1) * 128]

    a = a_ref[...]
    fg = _dot(a, wg_s[...])
    for c in range(lane_blocks):
        fg_s[c] = fg[:, c * 128:(c + 1) * 128]
    h = jnp.concatenate([stp[c] for c in range(lane_blocks)], axis=1)
    tpos = lax.broadcasted_iota(jnp.int32, (tm, 1), 0) & (seg - 1)
    lag1 = jnp.where(tpos < 1, pltpu.roll(h, tm - 1, axis=0), pltpu.roll(fg, 1, axis=0))
    lag2 = jnp.where(tpos < 2, h, pltpu.roll(fg, 2, axis=0))
    conv = lag2 * cw_ref[0:1, :] + lag1 * cw_ref[1:2, :] + fg * cw_ref[2:3, :]
    o_ref[...] = (_gelu(conv + cb_ref[...]) * _dot(a, wu_s[...])).astype(o_ref.dtype)
    for j in range(hist):
        for c in range(lane_blocks):
            st_out_ref[:, j, c * 128:(c + 1) * 128] = fg_s[c, pl.ds(seg - hist + j, nseq, stride=seg), :]


def _rms_bf16_body(x_ref, g_ref, o_ref):
    _norm_to_scratch(x_ref, g_ref, o_ref)


def rms_bf16(x, g, *, tm):
    m, k = x.shape
    return pl.pallas_call(
        _rms_bf16_body,
        grid=(m // tm,),
        in_specs=[pl.BlockSpec((tm, k), lambda i: (i, 0)), pl.BlockSpec((1, k), lambda i: (0, 0))],
        out_specs=pl.BlockSpec((tm, k), lambda i: (i, 0)),
        out_shape=jax.ShapeDtypeStruct((m, k), BF16),
        compiler_params=_cparams(("parallel",)),
        name="rms_bf16",
    )(x, g.reshape(1, k))


def ffn_up_gate(x, g, w_up_layers, layer, state, conv_w, conv_b, *, seq_len, tm, tn):
    m, k = x.shape
    bsz = m // seq_len
    ncol = FFN_DIM // tn
    a = rms_bf16(x, g, tm=min(tm, 512))
    common_in = [pl.BlockSpec((tm, k), lambda j, i: (i, 0)),
                 pl.BlockSpec((None, k, tn), lambda j, i: (layer, 0, j)),
                 pl.BlockSpec((None, k, tn), lambda j, i: (layer, 0, ncol + j))]
    conv_in = [pl.BlockSpec((FFN_CONV, tn), lambda j, i: (0, j)), pl.BlockSpec((1, tn), lambda j, i: (0, j))]
    act_spec = pl.BlockSpec((tm, tn), lambda j, i: (i, j))
    act_shape = jax.ShapeDtypeStruct((m, FFN_DIM), BF16)
    w_scr = [pltpu.VMEM((k, tn), BF16)] * 2
    if seq_len >= tm:
        assert seq_len % tm == 0 and tm % FFN_SUB == 0
        tps = seq_len // tm
        act, tile_last = pl.pallas_call(
            functools.partial(_ffn_up_long_body, tps),
            grid=(ncol, m // tm),
            in_specs=common_in + [pl.BlockSpec((1, FFN_CONV - 1, tn), lambda j, i: (i // tps, 0, j))] + conv_in,
            out_specs=[act_spec, pl.BlockSpec((1, FFN_CONV - 1, tn), lambda j, i: (i, 0, j))],
            out_shape=[act_shape, jax.ShapeDtypeStruct((m // tm, FFN_CONV - 1, FFN_DIM), F32)],
            scratch_shapes=w_scr + [pltpu.VMEM((tm + 8, tn), F32)],
            compiler_params=_cparams(("parallel", "arbitrary")),
            name="ffn_up_gate",
        )(a, w_up_layers, w_up_layers, state, conv_w, conv_b.reshape(1, FFN_DIM))
        return act, tile_last[tps - 1::tps]
    st_spec = pl.BlockSpec((tm // seq_len, FFN_CONV - 1, tn), lambda j, i: (i, 0, j))
    return pl.pallas_call(
        functools.partial(_ffn_up_group_body, seq_len),
        grid=(ncol, m // tm),
        in_specs=common_in + [st_spec] + conv_in,
        out_specs=[act_spec, st_spec],
        out_shape=[act_shape, jax.ShapeDtypeStruct((bsz, FFN_CONV - 1, FFN_DIM), F32)],
        scratch_shapes=w_scr + [pltpu.VMEM((tn // 128, tm, 128), F32)] * 2,
        compiler_params=_cparams(("parallel", "arbitrary")),
        name="ffn_up_gate_grouped",
    )(a, w_up_layers, w_up_layers, state, conv_w, conv_b.reshape(1, FFN_DIM))


def _mixer_a_body(u_ref, v_ref, lng_ref, lnb_ref, w_ref, bias_ref, o_ref, vout_ref):
    u = _gelu(u_ref[...])
    v = _gelu(v_ref[...])
    vc = v - jnp.mean(v, axis=-1, keepdims=True)
    var = jnp.mean(vc * vc, axis=-1, keepdims=True)
    vn = vc * lax.rsqrt(var + EPS) * lng_ref[...] + lnb_ref[...]
    vout_ref[...] = vn
    n = w_ref.shape[1]
    causal = lax.broadcasted_iota(jnp.int32, (n, n), 0) >= lax.broadcasted_iota(jnp.int32, (n, n), 1)
    gd = BRANCH_WIDTH // A_GROUPS
    for g in range(A_GROUPS):
        w = jnp.where(causal, w_ref[g], 0.0)
        mixed = _dot(w, vn[:, g * gd:(g + 1) * gd]) + bias_ref[:, g:g + 1]
        o_ref[:, g * gd:(g + 1) * gd] = (u[:, g * gd:(g + 1) * gd] * mixed).astype(o_ref.dtype)


def mixer_a(proj, ln_g, ln_b, w_mix, bias_rows):
    m = proj.shape[0]
    r = A_CHUNK
    bw = BRANCH_WIDTH
    return pl.pallas_call(
        _mixer_a_body,
        grid=(m // r,),
        in_specs=[pl.BlockSpec((r, bw), lambda i: (i, OFF_A_U // bw)),
                  pl.BlockSpec((r, bw), lambda i: (i, OFF_A_V // bw)),
                  pl.BlockSpec((1, bw), lambda i: (0, 0)),
                  pl.BlockSpec((1, bw), lambda i: (0, 0)),
                  pl.BlockSpec((A_GROUPS, r, r), lambda i: (0, 0, 0)),
                  pl.BlockSpec((r, A_GROUPS), lambda i: (0, 0))],
        out_specs=[pl.BlockSpec((r, bw), lambda i: (i, 0)), pl.BlockSpec((r, bw), lambda i: (i, 0))],
        out_shape=[jax.ShapeDtypeStruct((m, bw), BF16), jax.ShapeDtypeStruct((m, bw), F32)],
        compiler_params=_cparams(("parallel",)),
        name="mixer_a",
    )(proj, proj, ln_g.reshape(1, bw), ln_b.reshape(1, bw), w_mix, bias_rows)


def _gla_head_chunk(q, k, v, g):
    c = q.shape[0]
    g2 = _cumsum_rows(g) * LOG2E
    n_tiles = c // 8
    row8 = lax.broadcasted_iota(jnp.int32, (8, 1), 0)
    q_t = [q[i * 8:(i + 1) * 8] for i in range(n_tiles)]
    g_t = [g2[i * 8:(i + 1) * 8] for i in range(n_tiles)]
    o_t = [jnp.zeros((8, v.shape[1]), F32) for _ in range(n_tiles)]
    for s in range(c):
        for i in range(s // 8, n_tiles):
            a = jnp.sum(q_t[i] * jnp.exp2(g_t[i] - g2[s:s + 1, :]) * k[s:s + 1, :], axis=-1, keepdims=True)
            if i == s // 8:
                a = jnp.where(row8 >= s - 8 * i, a, 0.0)
            o_t[i] = o_t[i] + a * v[s:s + 1, :]
    o = jnp.concatenate(o_t, axis=0) if n_tiles > 1 else o_t[0]
    gl = g2[c - 1:c, :]
    return o, q * jnp.exp2(g2), k * jnp.exp2(gl - g2), jnp.exp2(gl)


def _block_diag(xs):
    z = jnp.zeros_like(xs[0])
    return jnp.concatenate(
        [jnp.concatenate([x if j == i else z for j in range(len(xs))], axis=1) for i, x in enumerate(xs)], axis=0)


def _gla_state_dots(qgs, kds, vs, es, st_ref, bi):
    st = st_ref[bi]
    o_inter = _dot_nt(_block_diag(qgs), st)
    st_ref[bi] = st * jnp.concatenate(es, axis=1) + _dot_tn(jnp.concatenate(vs, axis=0), _block_diag(kds))
    return o_inter


def _gla_state_step(parts, vs, st_ref, bi):
    c = vs[0].shape[0]
    o_inter = _gla_state_dots([p[1] for p in parts], [p[2] for p in parts], vs, [p[3] for p in parts], st_ref, bi)
    return [p[0] + o_inter[h * c:(h + 1) * c] for h, p in enumerate(parts)]


def _gla_pair_tile(q, k, v, g, seg):
    g2 = _segment_cumsum(g, seg) * LOG2E
    tpos = lax.broadcasted_iota(jnp.int32, (8, 1), 0) & (seg - 1)
    o = jnp.sum(q * k, axis=-1, keepdims=True) * v
    for d in range(1, seg):
        a = jnp.sum(q * jnp.exp2(g2 - pltpu.roll(g2, d, axis=0)) * pltpu.roll(k, d, axis=0), axis=-1, keepdims=True)
        o = o + jnp.where(tpos >= d, a, 0.0) * pltpu.roll(v, d, axis=0)
    first = lax.broadcasted_iota(jnp.int32, (8, 1), 0) < seg
    gl_a, gl_b = g2[seg - 1:seg, :], g2[2 * seg - 1:2 * seg, :]
    kd = k * jnp.exp2(jnp.where(first, gl_a, gl_b) - g2)
    return o, q * jnp.exp2(g2), kd, jnp.exp2(gl_a), jnp.exp2(gl_b)


def _lane_broadcast_column(row):
    hi = row.astype(BF16).astype(F32)
    r1 = row - hi
    mid = r1.astype(BF16).astype(F32)
    lo = r1 - mid
    r = lax.broadcasted_iota(jnp.int32, (8, 1), 0)
    pieces = jnp.where(r == 0, hi, jnp.where(r == 1, mid, jnp.where(r == 2, lo, 0.0)))
    return _dot_tn(pieces.astype(BF16), jnp.ones((8, 128), BF16))


def _gla_state_dots_natural(qgs, kds, vs, es, st_ref, bi):
    st = st_ref[bi]
    o_inter = _dot(_block_diag(qgs), st)
    decay = _lane_broadcast_column(jnp.concatenate(es, axis=1))
    st_ref[bi] = decay * st + _dot_tn(_block_diag(kds), jnp.concatenate(vs, axis=0))
    return o_inter


def _gla_pair_step(tiles, vs, st_ref, p, seg):
    first = lax.broadcasted_iota(jnp.int32, (8, 1), 0) < seg
    qgs = [t[1] for t in tiles]
    oi_a = _gla_state_dots_natural(qgs, [jnp.where(first, t[2], 0.0) for t in tiles], vs, [t[3] for t in tiles],
                                   st_ref, 2 * p)
    oi_b = _gla_state_dots_natural(qgs, [jnp.where(first, 0.0, t[2]) for t in tiles], vs, [t[4] for t in tiles],
                                   st_ref, 2 * p + 1)
    return [t[0] + jnp.where(first, oi_a[h * 8:(h + 1) * 8], oi_b[h * 8:(h + 1) * 8]) for h, t in enumerate(tiles)]


def _load_state_natural(s0_ref, st_ref, kdim):
    def per_seq(bi, carry):
        for h in range(HEADS):
            s = s0_ref[bi, h]
            if kdim < HEAD_DIM:
                s = jnp.concatenate([s, jnp.zeros((HEAD_DIM - kdim, HEAD_DIM), F32)], axis=0)
            st_ref[bi, h * HEAD_DIM:(h + 1) * HEAD_DIM, :] = s
        return carry

    lax.fori_loop(0, s0_ref.shape[0], per_seq, 0)


def _store_state_natural(st_ref, sout_ref, kdim):
    def per_seq(bi, carry):
        for h in range(HEADS):
            sout_ref[bi, h] = st_ref[bi, h * HEAD_DIM:h * HEAD_DIM + kdim, :]
        return carry

    lax.fori_loop(0, sout_ref.shape[0], per_seq, 0)


def _load_state_t(s0_ref, st_ref, kdim):
    def per_seq(bi, carry):
        for h in range(HEADS):
            s = s0_ref[bi, h]
            if kdim < HEAD_DIM:
                s = jnp.concatenate([s, jnp.zeros((HEAD_DIM - kdim, HEAD_DIM), F32)], axis=0)
            st_ref[bi, :, h * HEAD_DIM:(h + 1) * HEAD_DIM] = s.T
        return carry

    lax.fori_loop(0, s0_ref.shape[0], per_seq, 0)


def _store_state_t(st_ref, sout_ref, kdim):
    def per_seq(bi, carry):
        for h in range(HEADS):
            sout_ref[bi, h] = st_ref[bi, :, h * HEAD_DIM:(h + 1) * HEAD_DIM].T[:kdim, :]
        return carry

    lax.fori_loop(0, sout_ref.shape[0], per_seq, 0)


GLA_UNROLL = 8


def _gla_chunk_loop(tb, chunk):
    def step(n, carry):
        chunk(pl.ds(pl.multiple_of(n * GLA_CHUNK, GLA_CHUNK), GLA_CHUNK))
        return carry

    lax.fori_loop(0, tb // GLA_CHUNK, step, 0, unroll=GLA_UNROLL)


def _mixer_b_body(q_ref, f_ref, i_ref, og_ref, lb_ref, ng_ref, s0_ref, o_ref, sout_ref, st_ref):
    kd = HEAD_DIM

    @pl.when(pl.program_id(1) == 0)
    def _():
        _load_state_t(s0_ref, st_ref, kd)

    def chunk(rows):
        parts, vs = [], []
        for h in range(HEADS):
            cols = slice(h * kd, (h + 1) * kd)
            lb = lb_ref[:, cols]
            fp = f_ref[0, rows, cols]
            f = lb + (1.0 - lb) * jax.nn.sigmoid(fp)
            g = jnp.log(jnp.maximum(f, F_TINY))
            k = (1.0 - lb) * jax.nn.sigmoid(-fp)
            vs.append(i_ref[0, rows, cols])
            parts.append(_gla_head_chunk(q_ref[0, rows, cols], k, vs[h], g))
        for h, o in enumerate(_gla_state_step(parts, vs, st_ref, 0)):
            cols = slice(h * kd, (h + 1) * kd)
            o = _rms_rows(o) * ng_ref[...] * jax.nn.sigmoid(og_ref[0, rows, cols])
            o_ref[0, rows, cols] = o.astype(o_ref.dtype)

    _gla_chunk_loop(q_ref.shape[1], chunk)

    @pl.when(pl.program_id(1) == pl.num_programs(1) - 1)
    def _():
        _store_state_t(st_ref, sout_ref, kd)


def mixer_b(pseq, lb, norm_g, s0, *, tb):
    bsz, t, _ = pseq.shape
    bw = BRANCH_WIDTH
    field = lambda off: pl.BlockSpec((1, tb, bw), lambda b, s: (b, s, off // bw))
    state = pl.BlockSpec((1, HEADS, HEAD_DIM, HEAD_DIM), lambda b, s: (b, 0, 0, 0))
    return pl.pallas_call(
        _mixer_b_body,
        grid=(bsz, t // tb),
        in_specs=[field(OFF_B_Q), field(OFF_B_F), field(OFF_B_I), field(OFF_B_G),
                  pl.BlockSpec((1, bw), lambda b, s: (0, 0)),
                  pl.BlockSpec((1, HEAD_DIM), lambda b, s: (0, 0)),
                  state],
        out_specs=[pl.BlockSpec((1, tb, bw), lambda b, s: (b, s, 0)), state],
        out_shape=[jax.ShapeDtypeStruct((bsz, t, bw), BF16),
                   jax.ShapeDtypeStruct((bsz, HEADS, HEAD_DIM, HEAD_DIM), F32)],
        scratch_shapes=[pltpu.VMEM((1, HEAD_DIM, HEADS * HEAD_DIM), F32)],
        compiler_params=_cparams(("parallel", "arbitrary")),
        name="mixer_b",
    )(pseq, pseq, pseq, pseq, lb.reshape(1, bw), norm_g.reshape(1, HEAD_DIM), s0)


def _mixer_c_body(q_ref, k_ref, v_ref, r_ref, lr_ref, w2_ref, ba_ref, ng_ref, s0_ref, o_ref, sout_ref, st_ref):
    kd = C_KDIM

    @pl.when(pl.program_id(1) == 0)
    def _():
        _load_state_t(s0_ref, st_ref, kd)

    zpad = jnp.zeros((GLA_CHUNK, HEAD_DIM - kd), F32)

    def chunk(rows):
        gate_in = _dot(lr_ref[0, rows, 0:C_RANK], w2_ref[...]) + ba_ref[...]
        g_all = _log_sigmoid(gate_in) / C_TAU
        parts, vs = [], []
        for h in range(HEADS):
            kcols = slice(h * kd, (h + 1) * kd)
            q = jnp.concatenate([q_ref[0, rows, kcols] * (kd ** -0.5), zpad], axis=1)
            k = jnp.concatenate([k_ref[0, rows, kcols], zpad], axis=1)
            g = jnp.concatenate([g_all[:, kcols], zpad], axis=1)
            vs.append(v_ref[0, rows, h * HEAD_DIM:(h + 1) * HEAD_DIM])
            parts.append(_gla_head_chunk(q, k, vs[h], g))
        for h, o in enumerate(_gla_state_step(parts, vs, st_ref, 0)):
            vcols = slice(h * HEAD_DIM, (h + 1) * HEAD_DIM)
            o = _rms_rows(o) * ng_ref[...] * _silu(r_ref[0, rows, vcols])
            o_ref[0, rows, vcols] = o.astype(o_ref.dtype)

    _gla_chunk_loop(q_ref.shape[1], chunk)

    @pl.when(pl.program_id(1) == pl.num_programs(1) - 1)
    def _():
        _store_state_t(st_ref, sout_ref, kd)


def mixer_c(pseq, w_a2, b_a, norm_g, s0, *, tb):
    bsz, t, _ = pseq.shape
    bw = BRANCH_WIDTH
    kw = HEADS * C_KDIM
    field = lambda off, w: pl.BlockSpec((1, tb, w), lambda b, s: (b, s, off // w))
    state = pl.BlockSpec((1, HEADS, C_KDIM, HEAD_DIM), lambda b, s: (b, 0, 0, 0))
    return pl.pallas_call(
        _mixer_c_body,
        grid=(bsz, t // tb),
        in_specs=[field(OFF_C_Q, kw), field(OFF_C_K, kw), field(OFF_C_V, bw), field(OFF_C_R, bw),
                  field(OFF_C_LR, 128),
                  pl.BlockSpec((C_RANK, kw), lambda b, s: (0, 0)),
                  pl.BlockSpec((1, kw), lambda b, s: (0, 0)),
                  pl.BlockSpec((1, HEAD_DIM), lambda b, s: (0, 0)),
                  state],
        out_specs=[pl.BlockSpec((1, tb, bw), lambda b, s: (b, s, 0)), state],
        out_shape=[jax.ShapeDtypeStruct((bsz, t, bw), BF16),
                   jax.ShapeDtypeStruct((bsz, HEADS, C_KDIM, HEAD_DIM), F32)],
        scratch_shapes=[pltpu.VMEM((1, HEAD_DIM, HEADS * HEAD_DIM), F32)],
        compiler_params=_cparams(("parallel", "arbitrary")),
        name="mixer_c",
    )(pseq, pseq, pseq, pseq, pseq, w_a2, b_a.reshape(1, kw), norm_g.reshape(1, HEAD_DIM), s0)


def _grouped_pairs_loop(n_rows, seg, st_ref, head_inputs, head_outputs):
    def step(it, carry):
        rows = pl.ds(pl.multiple_of(it * 16, 16), 16)
        ins = [head_inputs(rows, h) for h in range(HEADS)]
        halves = []
        for half in range(2):
            sl = slice(half * 8, (half + 1) * 8)
            vs = [x[2][sl] for x in ins]
            tiles = [_gla_pair_tile(x[0][sl], x[1][sl], x[2][sl], x[3][sl], seg) for x in ins]
            halves.append(_gla_pair_step(tiles, vs, st_ref, 2 * it + half, seg))
        for h in range(HEADS):
            head_outputs(rows, h, jnp.concatenate([halves[0][h], halves[1][h]], axis=0))
        return carry

    lax.fori_loop(0, n_rows // 16, step, 0, unroll=2)


def _mixer_b_group_body(seg, q_ref, f_ref, i_ref, og_ref, lb_ref, ng_ref, s0_ref, o_ref, sout_ref, st_ref):
    kd = HEAD_DIM
    _load_state_natural(s0_ref, st_ref, kd)

    def head_inputs(rows, h):
        cols = slice(h * kd, (h + 1) * kd)
        lb = lb_ref[:, cols]
        fp = f_ref[0, rows, cols]
        f = lb + (1.0 - lb) * jax.nn.sigmoid(fp)
        return (q_ref[0, rows, cols], (1.0 - lb) * jax.nn.sigmoid(-fp), i_ref[0, rows, cols],
                jnp.log(jnp.maximum(f, F_TINY)))

    def head_outputs(rows, h, o):
        cols = slice(h * kd, (h + 1) * kd)
        o = _rms_rows(o) * ng_ref[...] * jax.nn.sigmoid(og_ref[0, rows, cols])
        o_ref[0, rows, cols] = o.astype(o_ref.dtype)

    _grouped_pairs_loop(q_ref.shape[1], seg, st_ref, head_inputs, head_outputs)
    _store_state_natural(st_ref, sout_ref, kd)


def _mixer_c_group_body(seg, q_ref, k_ref, v_ref, r_ref, lr_ref, w2_ref, ba_ref, ng_ref, s0_ref, o_ref, sout_ref,
                        st_ref, g_scr):
    kd = C_KDIM
    _load_state_natural(s0_ref, st_ref, kd)
    g_scr[...] = _log_sigmoid(_dot(lr_ref[0, :, 0:C_RANK], w2_ref[...]) + ba_ref[...]) / C_TAU
    zpad = jnp.zeros((16, HEAD_DIM - kd), F32)

    def head_inputs(rows, h):
        kcols = slice(h * kd, (h + 1) * kd)
        pad = lambda x: jnp.concatenate([x, zpad], axis=1)
        return (pad(q_ref[0, rows, kcols] * (kd ** -0.5)), pad(k_ref[0, rows, kcols]),
                v_ref[0, rows, h * HEAD_DIM:(h + 1) * HEAD_DIM], pad(g_scr[rows, kcols]))

    def head_outputs(rows, h, o):
        vcols = slice(h * HEAD_DIM, (h + 1) * HEAD_DIM)
        o = _rms_rows(o) * ng_ref[...] * _silu(r_ref[0, rows, vcols])
        o_ref[0, rows, vcols] = o.astype(o_ref.dtype)

    _grouped_pairs_loop(q_ref.shape[1], seg, st_ref, head_inputs, head_outputs)
    _store_state_natural(st_ref, sout_ref, kd)


def _layer_stacked_state(body, prev_states, state_ref_pos, seqs_per_step):
    n_prev = 0 if prev_states is None else prev_states.shape[0]

    def wrapped(*refs):
        refs = list(refs)
        if n_prev:
            prev_ref = refs.pop(0)
            refs[state_ref_pos][0:n_prev] = prev_ref[...]
        refs[state_ref_pos] = refs[state_ref_pos].at[n_prev]
        return body(*refs)

    def spec(shape_tail, layers):
        return pl.BlockSpec((layers, seqs_per_step) + shape_tail, lambda s: (0, s) + (0,) * len(shape_tail))

    def out_shape(s0):
        return jax.ShapeDtypeStruct((n_prev + 1,) + s0.shape[1:], F32)

    prev_specs = lambda s0: [spec(s0.shape[2:], n_prev)] if n_prev else []
    prev_inputs = [prev_states] if n_prev else []
    return wrapped, prev_specs, prev_inputs, lambda s0: spec(s0.shape[2:], n_prev + 1), out_shape


def _grouped_gla_call(body, prows, fields, params, s0, prev_states, kdim, *, seg, tb, extra_scratch=()):
    m = prows.shape[0]
    bw = BRANCH_WIDTH
    p3 = prows.reshape(1, m, -1)
    layer = 0 if prev_states is None else prev_states.shape[0]
    state = pl.BlockSpec((None, tb // seg, HEADS, kdim, HEAD_DIM), lambda s: (layer, s, 0, 0, 0))
    n_in = len(fields) + len(params) + 1
    wrapped, prev_specs, prev_inputs, state_out_spec, state_out_shape = _layer_stacked_state(
        functools.partial(body, seg), prev_states, n_in + 1, tb // seg)
    o, s_out = pl.pallas_call(
        wrapped,
        grid=(m // tb,),
        in_specs=prev_specs(s0)
        + [pl.BlockSpec((1, tb, w), functools.partial(lambda s, c: (0, s, c), c=off // w)) for off, w in fields]
        + [pl.BlockSpec(x.shape, functools.partial(lambda s, n: (0,) * n, n=x.ndim)) for x in params] + [state],
        out_specs=[pl.BlockSpec((1, tb, bw), lambda s: (0, s, 0)), state_out_spec(s0)],
        out_shape=[jax.ShapeDtypeStruct((1, m, bw), BF16), state_out_shape(s0)],
        scratch_shapes=[pltpu.VMEM((tb // seg, HEADS * HEAD_DIM, HEAD_DIM), F32), *extra_scratch],
        compiler_params=_cparams(("parallel",)),
        name=body.__name__.strip("_"),
    )(*prev_inputs, *([p3] * len(fields)), *params, s0)
    return o.reshape(m, bw), s_out


def mixer_b_grouped(prows, lb, norm_g, s0, prev_states, *, seg, tb):
    bw = BRANCH_WIDTH
    return _grouped_gla_call(
        _mixer_b_group_body, prows, [(OFF_B_Q, bw), (OFF_B_F, bw), (OFF_B_I, bw), (OFF_B_G, bw)],
        [lb.reshape(1, bw), norm_g.reshape(1, HEAD_DIM)], s0, prev_states, HEAD_DIM, seg=seg, tb=tb)


def mixer_c_grouped(prows, w_a2, b_a, norm_g, s0, prev_states, *, seg, tb):
    bw = BRANCH_WIDTH
    kw = HEADS * C_KDIM
    return _grouped_gla_call(
        _mixer_c_group_body, prows, [(OFF_C_Q, kw), (OFF_C_K, kw), (OFF_C_V, bw), (OFF_C_R, bw), (OFF_C_LR, 128)],
        [w_a2, b_a.reshape(1, kw), norm_g.reshape(1, HEAD_DIM)], s0, prev_states, C_KDIM, seg=seg, tb=tb,
        extra_scratch=(pltpu.VMEM((tb, kw), F32),))


def _unit_lower_inverse(a, order):
    c = a.shape[0]
    eye = (lax.broadcasted_iota(jnp.int32, (c, c), 0) == lax.broadcasted_iota(jnp.int32, (c, c), 1)).astype(F32)
    p = eye - a
    pw = a
    n = 2
    while n < order:
        pw = _dot(pw, pw)
        p = p + _dot(p, pw)
        n *= 2
    return p


D_ROWS = 64


def _stack_heads(x):
    return jnp.concatenate([x[:, h * HEAD_DIM:(h + 1) * HEAD_DIM] for h in range(HEADS)], axis=0)


def _stack_head_lanes(x):
    return jnp.concatenate([x[:, h:h + 1] for h in range(HEADS)], axis=0)


def _lane_pad_heads(x):
    return jnp.pad(x.reshape(1, HEADS), ((0, 0), (0, 128 - HEADS)))


def _segment_cumsum(x, seg):
    tpos = lax.broadcasted_iota(jnp.int32, x.shape, 0) & (seg - 1)
    sh = 1
    while sh < seg:
        x = x + jnp.where(tpos >= sh, pltpu.roll(x, sh, axis=0), 0.0)
        sh *= 2
    return x


def _delta_chunk_operands(seg, qkv, beta_all, g_all):
    hd = HEAD_DIM
    bw = BRANCH_WIDTH
    r = HEADS * qkv.shape[0]
    q = _stack_heads(qkv[:, 0:bw])
    k = _stack_heads(qkv[:, bw:2 * bw])
    v = _stack_heads(qkv[:, 2 * bw:3 * bw])
    q = q * lax.rsqrt(jnp.sum(q * q, axis=-1, keepdims=True) + EPS) * (hd ** -0.5)
    k = k * lax.rsqrt(jnp.sum(k * k, axis=-1, keepdims=True) + EPS)
    beta = _stack_head_lanes(beta_all)
    gc = _stack_head_lanes(_segment_cumsum(g_all, seg))
    ri = lax.broadcasted_iota(jnp.int32, (r, r), 0)
    ci = lax.broadcasted_iota(jnp.int32, (r, r), 1)
    shift = seg.bit_length() - 1
    same = (ri >> shift) == (ci >> shift)
    gr = jnp.sum(jnp.where(ri == ci, gc, 0.0), axis=0, keepdims=True)
    decay = jnp.where(same, jnp.where(ri >= ci, jnp.exp(jnp.minimum(gc - gr, 0.0)), 0.0), 0.0)
    a_mat = jnp.where(ri > ci, beta * decay * _dot_nt(k, k), 0.0)
    t_inv = _unit_lower_inverse(a_mat, seg)
    eg = jnp.exp(gc)
    sol = _dot(t_inv, jnp.concatenate([(beta * eg) * k, beta * v], axis=1))
    qk = _dot_nt(q, k) * decay
    is_last = (ci & (seg - 1)) == seg - 1
    gl = jnp.sum(jnp.where(same, jnp.where(is_last, gr, 0.0), 0.0), axis=1, keepdims=True)
    return sol[:, :hd], sol[:, hd:], qk, q * eg, k * jnp.exp(gl - gc), jnp.exp(gl)


def _mixer_d_body(x_ref, beta_ref, dec_ref, z_ref, cst_ref, s0_ref, cw_ref, alog_ref, dtb_ref, ng_ref,
                  o_ref, cst_out_ref, sout_ref, ext, conv, st_ref):
    nb, tb = x_ref.shape[0], x_ref.shape[1]
    c = D_ROWS
    hd = HEAD_DIM
    t = pl.program_id(1)
    nt = pl.num_programs(1)

    @pl.when(t == 0)
    def _():
        ext[:, 5:8, :] = cst_ref[...]
        st_ref[...] = s0_ref[...].reshape(nb * HEADS, hd, hd)

    for bi in range(nb):
        ext[bi, 8:8 + tb, :] = x_ref[bi]
        y = ext[bi, 5:5 + tb, :] * cw_ref[0:1, :]
        for j in range(1, D_CONV):
            y = y + ext[bi, 5 + j:5 + j + tb, :] * cw_ref[j:j + 1, :]
        conv[bi] = _silu(y)

    @pl.when(t == nt - 1)
    def _():
        cst_out_ref[...] = ext[:, 8 + tb - (D_CONV - 1):8 + tb, :]

    ext[:, 5:8, :] = ext[:, 5 + tb:8 + tb, :]

    a_neg = -jnp.exp(alog_ref[...])

    def chunk(n, carry):
        rows = pl.ds(pl.multiple_of(n * c, c), c)
        for bi in range(nb):
            beta_all = jax.nn.sigmoid(beta_ref[bi, rows, :])
            g_all = a_neg * _softplus(dec_ref[bi, rows, :] + dtb_ref[...])
            w, u, qk, qg, kdec, egl = _delta_chunk_operands(c, conv[bi, rows, :], beta_all, g_all)
            deltas, oqs = [], []
            for h in range(HEADS):
                hs = slice(h * c, (h + 1) * c)
                st = st_ref[bi * HEADS + h]
                x = _dot(jnp.concatenate([w[hs], qg[hs]], axis=0), st)
                delta = u[hs] - x[:c]
                deltas.append(delta)
                oqs.append(x[c:])
                st_ref[bi * HEADS + h] = egl[(h + 1) * c - 1:(h + 1) * c, :] * st + _dot_tn(kdec[hs], delta)
            o = jnp.concatenate(oqs, axis=0) + _dot(qk, jnp.concatenate(deltas, axis=0))
            o = _rms_rows(o) * ng_ref[...] * _silu(_stack_heads(z_ref[bi, rows, :]))
            for h in range(HEADS):
                o_ref[bi, rows, h * hd:(h + 1) * hd] = o[h * c:(h + 1) * c].astype(o_ref.dtype)
        return carry

    lax.fori_loop(0, tb // c, chunk, 0, unroll=2)

    @pl.when(t == nt - 1)
    def _():
        sout_ref[...] = st_ref[...].reshape(nb, HEADS, hd, hd)


def _mixer_d_group_body(seg, x_ref, cst_ref, beta_ref, dec_ref, z_ref, s0_ref, cw_ref, alog_ref, dtb_ref, ng_ref,
                        o_ref, cst_out_ref, sout_ref, conv, w_s, u_s, qg_s, kd_s, egl_s, delta_s, oq_s, stp_s, x_s):
    tb = x_ref.shape[1]
    nseq = tb // seg
    c = D_ROWS
    hd = HEAD_DIM
    pairs = c // 8
    hist_rows = D_CONV - 1

    lane_blocks = stp_s.shape[0]
    x = x_ref[0]
    stp_s[...] = jnp.zeros_like(stp_s)
    for cb in range(lane_blocks):
        lanes = slice(cb * 128, (cb + 1) * 128)
        x_s[cb] = x[:, lanes]
        for j in range(hist_rows):
            stp_s[cb, pl.ds(j, nseq, stride=seg), :] = cst_ref[:, j, lanes]
    stp = jnp.concatenate([stp_s[cb] for cb in range(lane_blocks)], axis=1)

    tpos = lax.broadcasted_iota(jnp.int32, (tb, 1), 0) & (seg - 1)
    y = x * cw_ref[3:4, :]
    for d in range(1, D_CONV):
        hist = stp if d == 3 else pltpu.roll(stp, tb - (3 - d), axis=0)
        y = y + jnp.where(tpos < d, hist, pltpu.roll(x, d, axis=0)) * cw_ref[3 - d:4 - d, :]
    conv[...] = _silu(y)
    for cb in range(lane_blocks):
        for j in range(hist_rows):
            cst_out_ref[:, j, cb * 128:(cb + 1) * 128] = x_s[cb, pl.ds(seg - hist_rows + j, nseq, stride=seg), :]

    a_neg = -jnp.exp(alog_ref[...])
    low = lax.broadcasted_iota(jnp.int32, (8, 1), 0) < seg

    for n in range(tb // c):
        rows = slice(n * c, (n + 1) * c)
        beta_all = jax.nn.sigmoid(beta_ref[0, rows, :])
        g_all = a_neg * _softplus(dec_ref[0, rows, :] + dtb_ref[...])
        w, u, qk, qg, kdec, egl = _delta_chunk_operands(seg, conv[rows, :], beta_all, g_all)
        w_s[...] = w
        u_s[...] = u
        qg_s[...] = qg
        kd_s[...] = kdec
        egl_s[...] = jnp.broadcast_to(egl, (HEADS * c, hd))

        def pair(p, carry):
            rp = pl.ds(pl.multiple_of(p * 8, 8), 8)
            h = p // pairs
            ja = n * (c // seg) + (p % pairs) * 2
            sa = s0_ref[ja, h]
            sb = s0_ref[ja + 1, h]
            lhs = jnp.concatenate([w_s[rp, :], qg_s[rp, :]], axis=0)
            xa = _dot(lhs, sa)
            xb = _dot(lhs, sb)
            delta = u_s[rp, :] - jnp.where(low, xa[:8], xb[:8])
            delta_s[rp, :] = delta
            oq_s[rp, :] = jnp.where(low, xa[8:], xb[8:])
            kd = kd_s[rp, :]
            e = egl_s[rp, :]
            sout_ref[ja, h] = e[seg - 1:seg, :] * sa + _dot_tn(jnp.where(low, kd, 0.0), delta)
            sout_ref[ja + 1, h] = e[2 * seg - 1:2 * seg, :] * sb + _dot_tn(jnp.where(low, 0.0, kd), delta)
            return carry

        lax.fori_loop(0, HEADS * pairs, pair, 0, unroll=4)
        o = oq_s[...] + _dot(qk, delta_s[...])
        o = _rms_rows(o) * ng_ref[...] * _silu(_stack_heads(z_ref[0, rows, :]))
        for h in range(HEADS):
            o_ref[0, rows, h * hd:(h + 1) * hd] = o[h * c:(h + 1) * c].astype(o_ref.dtype)


def mixer_d(pseq, conv_state, s0, conv_w, a_log, dt_bias, norm_g, *, tb, nb):
    bsz, t, _ = pseq.shape
    bw = BRANCH_WIDTH
    field = lambda off, w: pl.BlockSpec((nb, tb, w), lambda b, s: (b, s, off // w))
    state = pl.BlockSpec((nb, HEADS, HEAD_DIM, HEAD_DIM), lambda b, s: (b, 0, 0, 0))
    cstate = pl.BlockSpec((nb, D_CONV - 1, D_QKV), lambda b, s: (b, 0, 0))
    return pl.pallas_call(
        _mixer_d_body,
        grid=(bsz // nb, t // tb),
        in_specs=[field(OFF_D_QKV, D_QKV), field(OFF_D_BETA, 128), field(OFF_D_DECAY, 128), field(OFF_D_Z, bw),
                  cstate, state,
                  pl.BlockSpec((D_CONV, D_QKV), lambda b, s: (0, 0)),
                  pl.BlockSpec((1, 128), lambda b, s: (0, 0)),
                  pl.BlockSpec((1, 128), lambda b, s: (0, 0)),
                  pl.BlockSpec((1, HEAD_DIM), lambda b, s: (0, 0))],
        out_specs=[pl.BlockSpec((nb, tb, bw), lambda b, s: (b, s, 0)), cstate, state],
        out_shape=[jax.ShapeDtypeStruct((bsz, t, bw), BF16),
                   jax.ShapeDtypeStruct((bsz, D_CONV - 1, D_QKV), F32),
                   jax.ShapeDtypeStruct((bsz, HEADS, HEAD_DIM, HEAD_DIM), F32)],
        scratch_shapes=[pltpu.VMEM((nb, tb + 8, D_QKV), F32), pltpu.VMEM((nb, tb, D_QKV), F32),
                        pltpu.VMEM((nb * HEADS, HEAD_DIM, HEAD_DIM), F32)],
        compiler_params=_cparams(("parallel", "arbitrary")),
        name="mixer_d",
    )(pseq, pseq, pseq, pseq, conv_state, s0, conv_w, _lane_pad_heads(a_log), _lane_pad_heads(dt_bias),
      norm_g.reshape(1, HEAD_DIM))


def mixer_d_grouped(prows, conv_state, s0, conv_w, a_log, dt_bias, norm_g, prev_states, *, seg, tb):
    m = prows.shape[0]
    bsz = m // seg
    bw = BRANCH_WIDTH
    field = lambda off, w: pl.BlockSpec((1, tb, w), lambda s: (0, s, off // w))
    layer = 0 if prev_states is None else prev_states.shape[0]
    state = pl.BlockSpec((None, tb // seg, HEADS, HEAD_DIM, HEAD_DIM), lambda s: (layer, s, 0, 0, 0))
    cstate = pl.BlockSpec((tb // seg, D_CONV - 1, D_QKV), lambda s: (s, 0, 0))
    rows128 = pltpu.VMEM((HEADS * D_ROWS, HEAD_DIM), F32)
    lane_rows = pltpu.VMEM((D_QKV // 128, tb, 128), F32)
    n_in, state_out_pos = 10, 2
    wrapped, prev_specs, prev_inputs, state_out_spec, state_out_shape = _layer_stacked_state(
        functools.partial(_mixer_d_group_body, seg), prev_states, n_in + state_out_pos, tb // seg)
    o, cst, s_out = pl.pallas_call(
        wrapped,
        grid=(m // tb,),
        in_specs=prev_specs(s0) + [
                  field(OFF_D_QKV, D_QKV), cstate,
                  field(OFF_D_BETA, 128), field(OFF_D_DECAY, 128), field(OFF_D_Z, bw),
                  state,
                  pl.BlockSpec((D_CONV, D_QKV), lambda s: (0, 0)),
                  pl.BlockSpec((1, 128), lambda s: (0, 0)),
                  pl.BlockSpec((1, 128), lambda s: (0, 0)),
                  pl.BlockSpec((1, HEAD_DIM), lambda s: (0, 0))],
        out_specs=[pl.BlockSpec((1, tb, bw), lambda s: (0, s, 0)), cstate, state_out_spec(s0)],
        out_shape=[jax.ShapeDtypeStruct((1, m, bw), BF16),
                   jax.ShapeDtypeStruct((bsz, D_CONV - 1, D_QKV), F32),
                   state_out_shape(s0)],
        scratch_shapes=[pltpu.VMEM((tb, D_QKV), F32)] + [rows128] * 7 + [lane_rows] * 2,
        compiler_params=_cparams(("parallel",)),
        name="mixer_d_grouped",
    )(*prev_inputs, prows.reshape(1, m, -1), conv_state, prows.reshape(1, m, -1), prows.reshape(1, m, -1),
      prows.reshape(1, m, -1), s0, conv_w, _lane_pad_heads(a_log), _lane_pad_heads(dt_bias),
      norm_g.reshape(1, HEAD_DIM))
    return o.reshape(m, bw), cst, s_out


PACK_TILE = 512
PACK_SRC = ([4624, 5136, 5648, 0, 512, 1024, 1536, 2048, 2560, 3584, 4112, 6168, 3072, None]
            + [6680 + PACK_TILE * k for k in range(N_BRANCH * D_MODEL // PACK_TILE)])
PACK_SMALL_TILE = PACK_SRC.index(None)


def _pack_lookup(t, values):
    out = jnp.int32(0)
    for i, v in enumerate(values):
        out = jnp.where(t == i, jnp.int32(v), out)
    return out


PACK_EXTRA = 32


def _pack_w_in_body(a_ref, b_ref, tail_ref, small_ref, o_ref):
    t = pl.program_id(1)
    last = len(PACK_SRC) - 1
    shifts = [0 if s is None else s % PACK_TILE for s in PACK_SRC]
    assert max(shifts) <= PACK_EXTRA and all(s % 8 == 0 for s in shifts)

    def emit(shift, left_ref, right_ref):
        for c in range(a_ref.shape[1] // 128):
            lanes = slice(c * 128, (c + 1) * 128)
            w = left_ref[shift:, lanes]
            if shift:
                w = jnp.concatenate([w, right_ref[0:shift, lanes]], axis=0)
            o_ref[lanes, :] = w.T.astype(BF16)

    for shift in sorted(set(shifts)):
        tiles = [i for i, s in enumerate(shifts) if s == shift and i not in (PACK_SMALL_TILE, last)]
        pl.when(functools.reduce(jnp.logical_or, [t == i for i in tiles]))(
            functools.partial(emit, shift, a_ref, b_ref))
    pl.when(t == last)(functools.partial(emit, shifts[last], a_ref, tail_ref))
    pl.when(t == PACK_SMALL_TILE)(functools.partial(emit, 0, small_ref, None))


def pack_w_in(w_in):
    layers, k, n_src = w_in.shape
    w_t = jnp.swapaxes(w_in, 1, 2)
    bases = [0 if s is None else s - s % PACK_TILE for s in PACK_SRC]
    extra = [(b + PACK_TILE) // PACK_EXTRA for b in bases]
    full_blocks = n_src // PACK_EXTRA
    assert all(e < full_blocks for e in extra[:-1]) and extra[-1] == full_blocks
    extra[-1] = 0
    z = lambda n: jnp.zeros((layers, n, k), w_in.dtype)
    tail = jnp.concatenate([w_t[:, full_blocks * PACK_EXTRA:], z(PACK_EXTRA - n_src % PACK_EXTRA)], axis=1)
    small = jnp.concatenate([w_t[:, 4096:4112], z(112),
                             w_t[:, 6160:6164], z(124),
                             w_t[:, 6164:6168], z(252)], axis=1)
    return pl.pallas_call(
        _pack_w_in_body,
        grid=(layers, len(PACK_SRC)),
        in_specs=[pl.BlockSpec((None, PACK_TILE, k),
                               lambda l, t: (l, _pack_lookup(t, [b // PACK_TILE for b in bases]), 0)),
                  pl.BlockSpec((None, PACK_EXTRA, k), lambda l, t: (l, _pack_lookup(t, extra), 0)),
                  pl.BlockSpec((None, PACK_EXTRA, k), lambda l, t: (l, 0, 0)),
                  pl.BlockSpec((None, PACK_TILE, k), lambda l, t: (l, 0, 0))],
        out_specs=pl.BlockSpec((None, k, PACK_TILE), lambda l, t: (l, 0, t)),
        out_shape=jax.ShapeDtypeStruct((layers, k, N_PROJ), BF16),
        compiler_params=_cparams(("parallel", "arbitrary")),
        name="pack_w_in",
    )(w_t, w_t, tail, small)


def _trunk(x, p, s_hgrn, s_gla, s_delta, c_dconv, c_fconv, wts, lbs, *, sample):
    bsz, t, _ = x.shape
    m = bsz * t
    tm = m if sample else 1024
    h = x.reshape(m, D_MODEL)
    outs = {k: [] for k in ("hgrn", "gla", "delta", "dconv", "fconv", "v")}
    s_b = s_c = s_d = None
    for l in range(DEPTH):
        w = wts[l]
        proj, gates = in_proj(h, w["g_mix"], w["w_in_layers"], l, tm=tm, tn=1024)
        o_a, v_a = mixer_a(proj, w["a_ln_g"], w["a_ln_b"], w["a_w_mix"], w["a_bias_rows"])
        st_b, st_c, st_d = (s_hgrn, s_gla, s_delta) if sample else (s_hgrn[l], s_gla[l], s_delta[l])
        b_args = (lbs[l], w["b_norm_g"], st_b)
        c_args = (w["c_w_a2"], w["c_b_a"], w["c_norm_g"], st_c)
        d_args = (c_dconv[l], st_d, w["d_conv_w"], w["d_a_log"], w["d_dt_bias"], w["d_norm_g"])
        if sample:
            o_b, s_b = mixer_b_grouped(proj, *b_args, s_b, seg=t, tb=D_ROWS)
            o_c, s_c = mixer_c_grouped(proj, *c_args, s_c, seg=t, tb=D_ROWS)
            o_d, nb_d, s_d = mixer_d_grouped(proj, *d_args, s_d, seg=t, tb=D_ROWS)
        else:
            pseq = proj.reshape(bsz, t, N_SEQ_COLS)
            flat = lambda o: o.reshape(m, BRANCH_WIDTH)
            o_b, s_b = mixer_b(pseq, *b_args, tb=256)
            o_c, s_c = mixer_c(pseq, *c_args, tb=256)
            o_d, nb_d, s_d = mixer_d(pseq, *d_args, tb=256, nb=2 if bsz % 2 == 0 else 1)
            o_b, o_c, o_d = flat(o_b), flat(o_c), flat(o_d)
        merged = merge_branches((o_a, o_b, o_c, o_d), gates, w["w_branch"], tm=tm, tn=512)
        h = matmul_residual(merged, w["w_out_layers"], l, h, tm=tm, tn=1024)

        act, nb_f = ffn_up_gate(h, w["g_ffn"], w["w_ffn_up_layers"], l, c_fconv[l], w["ffn_conv_w"],
                                w["ffn_conv_b"], seq_len=t, tm=tm, tn=512)
        h = matmul_residual(act, w["w_ffn_down_layers"], l, h, tm=min(tm, 512), tn=512)

        h = ple_update(h, p[l].reshape(m, PLE_DIM), w["g_ple"], w["w_ple_gate"], w["w_ple_proj"],
                       w["g_final"] if l == DEPTH - 1 else None, tm=min(tm, 512))

        if not sample:
            outs["hgrn"].append(s_b)
            outs["gla"].append(s_c)
            outs["delta"].append(s_d)
        outs["dconv"].append(nb_d)
        outs["fconv"].append(nb_f)
        outs["v"].append(v_a.reshape(bsz, t, BRANCH_WIDTH))
    y = h.reshape(bsz, t, D_MODEL)
    states = (s_b, s_c, s_d) if sample else tuple(jnp.stack(outs[k]) for k in ("hgrn", "gla", "delta"))
    return (y,) + states + tuple(jnp.stack(outs[k]) for k in ("dconv", "fconv", "v"))


def kernel(x_prompt, x_sample, state_hgrn, state_gla, state_delta, state_delta_conv, state_ffn_conv, p_prompt, p_sample, g_mix, w_in, a_ln_g, a_ln_b, a_w_s, a_b_s, b_lb, b_norm_g, c_w_a2, c_b_a, c_norm_g, d_conv_w, d_a_log, d_dt_bias, d_norm_g, w_branch, w_out, g_ffn, w_ffn_up, ffn_conv_w, ffn_conv_b, w_ffn_down, g_ple, w_ple_gate, w_ple_proj, g_final):
    bp, t_p, _ = x_prompt.shape
    bs, t_s, _ = x_sample.shape
    sm = jax.nn.softmax(b_lb.astype(F32), axis=0)
    lbs = jnp.cumsum(sm, axis=0) - sm[0]

    w_in_packed = pack_w_in(w_in)
    shared = [dict(
        g_mix=g_mix[l], w_in_layers=w_in_packed, a_ln_g=a_ln_g[l], a_ln_b=a_ln_b[l],
        b_norm_g=b_norm_g[l], c_w_a2=c_w_a2[l], c_b_a=c_b_a[l], c_norm_g=c_norm_g[l],
        d_conv_w=d_conv_w[l], d_a_log=d_a_log[l], d_dt_bias=d_dt_bias[l], d_norm_g=d_norm_g[l],
        w_branch=w_branch[l].astype(BF16), w_out_layers=w_out, g_ffn=g_ffn[l],
        w_ffn_up_layers=w_ffn_up, ffn_conv_w=ffn_conv_w[l], ffn_conv_b=ffn_conv_b[l],
        w_ffn_down_layers=w_ffn_down, g_ple=g_ple[l], w_ple_gate=w_ple_gate[l].astype(BF16),
        w_ple_proj=w_ple_proj[l].astype(BF16), g_final=g_final) for l in range(DEPTH)]

    def layer_weights(l, sample):
        if sample:
            seqs = A_CHUNK // t_s
            idx = jnp.arange(A_CHUNK) // t_s
            same_seq = idx[:, None] == idx[None, :]
            w_mix = jnp.where(same_seq, jnp.tile(a_w_s[l, :, :t_s, :t_s], (1, seqs, seqs)), 0.0)
            bias_rows = jnp.tile(a_b_s[l, :, :t_s].T, (seqs, 1))
        else:
            w_mix = a_w_s[l]
            bias_rows = a_b_s[l].T
        return dict(shared[l], a_w_mix=w_mix, a_bias_rows=bias_rows)

    dt = x_prompt.dtype
    zeros = lambda *s: jnp.zeros((DEPTH, bp) + s, dt)
    out_p = _trunk(x_prompt, p_prompt, zeros(HEADS, HEAD_DIM, HEAD_DIM), zeros(HEADS, C_KDIM, HEAD_DIM),
                   zeros(HEADS, HEAD_DIM, HEAD_DIM), zeros(D_CONV - 1, D_QKV), zeros(FFN_CONV - 1, FFN_DIM),
                   [layer_weights(l, False) for l in range(DEPTH)], lbs, sample=False)
    out_s = _trunk(x_sample, p_sample, state_hgrn, state_gla, state_delta, state_delta_conv, state_ffn_conv,
                   [layer_weights(l, True) for l in range(DEPTH)], lbs, sample=True)
    y_p, hgrn_p, gla_p, delta_p, dconv_p, fconv_p, _ = out_p
    y_s, hgrn_s, gla_s, delta_s, dconv_s, fconv_s, v_s = out_s
    return (y_p, y_s, hgrn_p, hgrn_s, gla_p, gla_s, delta_p, delta_s,
            dconv_p, dconv_s, fconv_p, fconv_s, v_s)
```

```python
import functools

import jax
import jax.numpy as jnp
from jax import lax
from jax.experimental import pallas as pl
from jax.experimental.pallas import tpu as pltpu

F32 = jnp.float32
BF16 = jnp.bfloat16

D_MODEL = 2048
DEPTH = 2
PLE_DIM = 256
EPS = 1e-6
F_TINY = 1e-30
N_BRANCH = 4
BRANCH_WIDTH = D_MODEL // 4
A_GROUPS = 4
A_CHUNK = 128
HEADS = 4
HEAD_DIM = 128
C_KDIM = 64
C_RANK = 16
C_TAU = 16.0
D_CONV = 4
D_QKV = 3 * BRANCH_WIDTH
FFN_DIM = 5632
FFN_CONV = 3
GLA_CHUNK = 16
LOG2E = 1.4426950408889634

OFF_D_QKV = 0
OFF_A_U = 1536
OFF_A_V = 2048
OFF_B_Q = 2560
OFF_B_F = 3072
OFF_B_I = 3584
OFF_B_G = 4096
OFF_C_V = 4608
OFF_C_R = 5120
OFF_D_Z = 5632
OFF_C_Q = 6144
OFF_C_K = 6400
OFF_C_LR = 6656
OFF_D_BETA = 6784
OFF_D_DECAY = 6912
OFF_GATES = 7168
N_SEQ_COLS = 7168
N_PROJ = OFF_GATES + N_BRANCH * D_MODEL

VMEM_LIMIT = 56 * 1024 * 1024


def _cparams(sem):
    return pltpu.CompilerParams(dimension_semantics=sem, vmem_limit_bytes=VMEM_LIMIT)


def _gelu(x):
    return 0.5 * x * (1.0 + jnp.tanh(0.7978845608028654 * (x + 0.044715 * (x * x * x))))


def _silu(x):
    return x * jax.nn.sigmoid(x)


def _softplus(x):
    return jnp.maximum(x, 0.0) + jnp.log1p(jnp.exp(-jnp.abs(x)))


def _log_sigmoid(x):
    return -_softplus(-x)


def _rms_rows(x):
    return x * lax.rsqrt(jnp.mean(x * x, axis=-1, keepdims=True) + EPS)


def _dot(a, b):
    return jnp.dot(a, b, preferred_element_type=F32)


def _dot_nt(a, b):
    return lax.dot_general(a, b, (((1,), (1,)), ((), ())), preferred_element_type=F32)


def _dot_tn(a, b):
    return lax.dot_general(a, b, (((0,), (0,)), ((), ())), preferred_element_type=F32)


def _cumsum_rows(x):
    n = x.shape[0]
    row = lax.broadcasted_iota(jnp.int32, x.shape, 0)
    sh = 1
    while sh < n:
        x = x + jnp.where(row >= sh, pltpu.roll(x, sh, axis=0), 0.0)
        sh *= 2
    return x


NORM_ROWS = 128


def _norm_to_scratch(x_ref, g_ref, a_scr):
    def slab(n, carry):
        rows = pl.ds(pl.multiple_of(n * NORM_ROWS, NORM_ROWS), NORM_ROWS)
        a_scr[rows, :] = (_rms_rows(x_ref[rows, :]) * g_ref[...]).astype(BF16)
        return carry

    lax.fori_loop(0, x_ref.shape[0] // NORM_ROWS, slab, 0)


def _matmul_residual_body(x_ref, w_ref, r_ref, o_ref, w_scr):
    @pl.when(pl.program_id(1) == 0)
    def _():
        def slab(n, carry):
            rows = pl.ds(pl.multiple_of(n * NORM_ROWS, NORM_ROWS), NORM_ROWS)
            w_scr[rows, :] = w_ref[rows, :].astype(BF16)
            return carry

        lax.fori_loop(0, w_ref.shape[0] // NORM_ROWS, slab, 0)

    o_ref[...] = r_ref[...] + _dot(x_ref[...], w_scr[...])


def matmul_residual(x, w_layers, layer, res, *, tm, tn):
    m, k = x.shape
    n = w_layers.shape[2]
    return pl.pallas_call(
        _matmul_residual_body,
        grid=(n // tn, m // tm),
        in_specs=[pl.BlockSpec((tm, k), lambda j, i: (i, 0)),
                  pl.BlockSpec((None, k, tn), lambda j, i: (layer, 0, j)),
                  pl.BlockSpec((tm, tn), lambda j, i: (i, j))],
        out_specs=pl.BlockSpec((tm, tn), lambda j, i: (i, j)),
        out_shape=jax.ShapeDtypeStruct((m, n), F32),
        scratch_shapes=[pltpu.VMEM((k, tn), BF16)],
        compiler_params=_cparams(("parallel", "arbitrary")),
        name="matmul_residual",
    )(x, w_layers, res)


def _in_proj_body(n_seq_tiles, x_ref, g_ref, w_ref, seq_ref, gate_ref, a_scr):
    j = pl.program_id(1)

    @pl.when(j == 0)
    def _():
        _norm_to_scratch(x_ref, g_ref, a_scr)

    acc = _dot(a_scr[...], w_ref[...])

    @pl.when(j < n_seq_tiles)
    def _():
        seq_ref[...] = acc

    @pl.when(j >= n_seq_tiles)
    def _():
        gate_ref[...] = acc.astype(gate_ref.dtype)


def in_proj(x, g, w_layers, layer, *, tm, tn):
    m, k = x.shape
    n_seq = N_SEQ_COLS // tn
    n_gate = N_BRANCH * D_MODEL // tn
    return pl.pallas_call(
        functools.partial(_in_proj_body, n_seq),
        grid=(m // tm, n_seq + n_gate),
        in_specs=[pl.BlockSpec((tm, k), lambda i, j: (i, 0)),
                  pl.BlockSpec((1, k), lambda i, j: (0, 0)),
                  pl.BlockSpec((None, k, tn), lambda i, j: (layer, 0, j))],
        out_specs=[pl.BlockSpec((tm, tn), lambda i, j: (i, jnp.minimum(j, n_seq - 1))),
                   pl.BlockSpec((tm, tn), lambda i, j: (i, jnp.maximum(j - n_seq, 0)))],
        out_shape=[jax.ShapeDtypeStruct((m, N_SEQ_COLS), F32),
                   jax.ShapeDtypeStruct((m, N_BRANCH * D_MODEL), BF16)],
        scratch_shapes=[pltpu.VMEM((tm, k), BF16)],
        compiler_params=_cparams(("parallel", "arbitrary")),
        name="in_proj",
    )(x, g.reshape(1, k), w_layers)


def _merge_body(oa_ref, ob_ref, oc_ref, od_ref, ga_ref, gb_ref, gc_ref, gd_ref, w_ref, o_ref):
    acc = None
    for b, (o_b, g_b) in enumerate(((oa_ref, ga_ref), (ob_ref, gb_ref), (oc_ref, gc_ref), (od_ref, gd_ref))):
        gate = 0.5 + 0.5 * jnp.tanh(0.5 * g_b[...].astype(F32))
        term = gate * _dot(o_b[...], w_ref[b])
        acc = term if acc is None else acc + term
    o_ref[...] = acc.astype(o_ref.dtype)


def merge_branches(o_branches, gates, w_branch, *, tm, tn):
    m = gates.shape[0]
    gate_specs = [pl.BlockSpec((tm, tn), functools.partial(
        lambda i, j, b: (i, b * (D_MODEL // tn) + j), b=b)) for b in range(N_BRANCH)]
    return pl.pallas_call(
        _merge_body,
        grid=(m // tm, D_MODEL // tn),
        in_specs=[pl.BlockSpec((tm, BRANCH_WIDTH), lambda i, j: (i, 0))] * N_BRANCH + gate_specs
        + [pl.BlockSpec((N_BRANCH, BRANCH_WIDTH, tn), lambda i, j: (0, 0, j))],
        out_specs=pl.BlockSpec((tm, tn), lambda i, j: (i, j)),
        out_shape=jax.ShapeDtypeStruct((m, D_MODEL), BF16),
        compiler_params=_cparams(("parallel", "arbitrary")),
        name="merge_branches",
    )(*o_branches, gates, gates, gates, gates, w_branch)


PLE_ROWS = 256


def _ple_body(final, x_ref, p_ref, g_ref, wg_ref, wp_ref, *refs):
    gout_ref, o_ref = refs if final else (None, refs[0])

    slab_rows = min(PLE_ROWS, x_ref.shape[0])
    assert x_ref.shape[0] % slab_rows == 0

    def slab(n, carry):
        rows = pl.ds(pl.multiple_of(n * slab_rows, slab_rows), slab_rows)
        x = x_ref[rows, :]
        a = (_rms_rows(x) * g_ref[...]).astype(BF16)
        gate = 0.5 + 0.5 * jnp.tanh(0.5 * _dot(a, wg_ref[...]))
        h = x + _dot(p_ref[rows, :].astype(BF16), wp_ref[...]) * gate
        o_ref[rows, :] = h if gout_ref is None else _rms_rows(h) * gout_ref[...]
        return carry

    lax.fori_loop(0, x_ref.shape[0] // slab_rows, slab, 0, unroll=2)


def ple_update(x, p, g, w_gate, w_proj, g_out=None, *, tm):
    m, k = x.shape
    full = lambda a: pl.BlockSpec(a.shape, lambda i: (0,) * a.ndim)
    rows = lambda w: pl.BlockSpec((tm, w), lambda i: (i, 0))
    params = [g.reshape(1, k), w_gate, w_proj] + ([] if g_out is None else [g_out.reshape(1, k)])
    return pl.pallas_call(
        functools.partial(_ple_body, g_out is not None),
        grid=(m // tm,),
        in_specs=[rows(k), rows(PLE_DIM)] + [full(a) for a in params],
        out_specs=rows(k),
        out_shape=jax.ShapeDtypeStruct((m, k), F32),
        compiler_params=_cparams(("parallel",)),
        name="ple_update",
    )(x, p, *params)


FFN_SUB = 512


def _cast_rows_to_bf16(src_ref, dst_ref):
    def slab(n, carry):
        rows = pl.ds(pl.multiple_of(n * NORM_ROWS, NORM_ROWS), NORM_ROWS)
        dst_ref[rows, :] = src_ref[rows, :].astype(BF16)
        return carry

    lax.fori_loop(0, src_ref.shape[0] // NORM_ROWS, slab, 0)


def _ffn_up_long_body(tiles_per_seq, a_ref, wg_ref, wu_ref, st_ref, cw_ref, cb_ref, o_ref, st_out_ref,
                      wg_s, wu_s, ext):
    i = pl.program_id(1)
    tm = a_ref.shape[0]

    @pl.when(i == 0)
    def _():
        _cast_rows_to_bf16(wg_ref, wg_s)
        _cast_rows_to_bf16(wu_ref, wu_s)

    @pl.when(i % tiles_per_seq == 0)
    def _():
        ext[6:8, :] = st_ref[0]

    for r in range(0, tm, FFN_SUB):
        a = a_ref[r:r + FFN_SUB, :]
        ext[8 + r:8 + r + FFN_SUB, :] = _dot(a, wg_s[...])
        conv = (ext[6 + r:6 + r + FFN_SUB, :] * cw_ref[0:1, :] + ext[7 + r:7 + r + FFN_SUB, :] * cw_ref[1:2, :]
                + ext[8 + r:8 + r + FFN_SUB, :] * cw_ref[2:3, :])
        o_ref[r:r + FFN_SUB, :] = (_gelu(conv + cb_ref[...]) * _dot(a, wu_s[...])).astype(o_ref.dtype)

    last = ext[6 + tm:8 + tm, :]
    ext[6:8, :] = last
    st_out_ref[0] = last


def _ffn_up_group_body(seg, a_ref, wg_ref, wu_ref, st_ref, cw_ref, cb_ref, o_ref, st_out_ref, wg_s, wu_s, stp, fg_s):
    tm = a_ref.shape[0]
    nseq = tm // seg
    hist = FFN_CONV - 1

    @pl.when(pl.program_id(1) == 0)
    def _():
        _cast_rows_to_bf16(wg_ref, wg_s)
        _cast_rows_to_bf16(wu_ref, wu_s)

    lane_blocks = stp.shape[0]
    stp[...] = jnp.zeros_like(stp)
    for j in range(hist):
        for c in range(lane_blocks):
            stp[c, pl.ds(j, nseq, stride=seg), :] = st_ref[:, j, c * 128:(c + 1) * 128]

    a = a_ref[...]
    fg = _dot(a, wg_s[...])
    for c in range(lane_blocks):
        fg_s[c] = fg[:, c * 128:(c + 1) * 128]
    h = jnp.concatenate([stp[c] for c in range(lane_blocks)], axis=1)
    tpos = lax.broadcasted_iota(jnp.int32, (tm, 1), 0) & (seg - 1)
    lag1 = jnp.where(tpos < 1, pltpu.roll(h, tm - 1, axis=0), pltpu.roll(fg, 1, axis=0))
    lag2 = jnp.where(tpos < 2, h, pltpu.roll(fg, 2, axis=0))
    conv = lag2 * cw_ref[0:1, :] + lag1 * cw_ref[1:2, :] + fg * cw_ref[2:3, :]
    o_ref[...] = (_gelu(conv + cb_ref[...]) * _dot(a, wu_s[...])).astype(o_ref.dtype)
    for j in range(hist):
        for c in range(lane_blocks):
            st_out_ref[:, j, c * 128:(c + 1) * 128] = fg_s[c, pl.ds(seg - hist + j, nseq, stride=seg), :]


def _rms_bf16_body(x_ref, g_ref, o_ref):
    _norm_to_scratch(x_ref, g_ref, o_ref)


def rms_bf16(x, g, *, tm):
    m, k = x.shape
    return pl.pallas_call(
        _rms_bf16_body,
        grid=(m // tm,),
        in_specs=[pl.BlockSpec((tm, k), lambda i: (i, 0)), pl.BlockSpec((1, k), lambda i: (0, 0))],
        out_specs=pl.BlockSpec((tm, k), lambda i: (i, 0)),
        out_shape=jax.ShapeDtypeStruct((m, k), BF16),
        compiler_params=_cparams(("parallel",)),
        name="rms_bf16",
    )(x, g.reshape(1, k))


def ffn_up_gate(x, g, w_up_layers, layer, state, conv_w, conv_b, *, seq_len, tm, tn):
    m, k = x.shape
    bsz = m // seq_len
    ncol = FFN_DIM // tn
    a = rms_bf16(x, g, tm=min(tm, 512))
    common_in = [pl.BlockSpec((tm, k), lambda j, i: (i, 0)),
                 pl.BlockSpec((None, k, tn), lambda j, i: (layer, 0, j)),
                 pl.BlockSpec((None, k, tn), lambda j, i: (layer, 0, ncol + j))]
    conv_in = [pl.BlockSpec((FFN_CONV, tn), lambda j, i: (0, j)), pl.BlockSpec((1, tn), lambda j, i: (0, j))]
    act_spec = pl.BlockSpec((tm, tn), lambda j, i: (i, j))
    act_shape = jax.ShapeDtypeStruct((m, FFN_DIM), BF16)
    w_scr = [pltpu.VMEM((k, tn), BF16)] * 2
    if seq_len >= tm:
        assert seq_len % tm == 0 and tm % FFN_SUB == 0
        tps = seq_len // tm
        act, tile_last = pl.pallas_call(
            functools.partial(_ffn_up_long_body, tps),
            grid=(ncol, m // tm),
            in_specs=common_in + [pl.BlockSpec((1, FFN_CONV - 1, tn), lambda j, i: (i // tps, 0, j))] + conv_in,
            out_specs=[act_spec, pl.BlockSpec((1, FFN_CONV - 1, tn), lambda j, i: (i, 0, j))],
            out_shape=[act_shape, jax.ShapeDtypeStruct((m // tm, FFN_CONV - 1, FFN_DIM), F32)],
            scratch_shapes=w_scr + [pltpu.VMEM((tm + 8, tn), F32)],
            compiler_params=_cparams(("parallel", "arbitrary")),
            name="ffn_up_gate",
        )(a, w_up_layers, w_up_layers, state, conv_w, conv_b.reshape(1, FFN_DIM))
        return act, tile_last[tps - 1::tps]
    st_spec = pl.BlockSpec((tm // seq_len, FFN_CONV - 1, tn), lambda j, i: (i, 0, j))
    return pl.pallas_call(
        functools.partial(_ffn_up_group_body, seq_len),
        grid=(ncol, m // tm),
        in_specs=common_in + [st_spec] + conv_in,
        out_specs=[act_spec, st_spec],
        out_shape=[act_shape, jax.ShapeDtypeStruct((bsz, FFN_CONV - 1, FFN_DIM), F32)],
        scratch_shapes=w_scr + [pltpu.VMEM((tn // 128, tm, 128), F32)] * 2,
        compiler_params=_cparams(("parallel", "arbitrary")),
        name="ffn_up_gate_grouped",
    )(a, w_up_layers, w_up_layers, state, conv_w, conv_b.reshape(1, FFN_DIM))


def _mixer_a_body(u_ref, v_ref, lng_ref, lnb_ref, w_ref, bias_ref, o_ref, vout_ref):
    u = _gelu(u_ref[...])
    v = _gelu(v_ref[...])
    vc = v - jnp.mean(v, axis=-1, keepdims=True)
    var = jnp.mean(vc * vc, axis=-1, keepdims=True)
    vn = vc * lax.rsqrt(var + EPS) * lng_ref[...] + lnb_ref[...]
    vout_ref[...] = vn
    n = w_ref.shape[1]
    causal = lax.broadcasted_iota(jnp.int32, (n, n), 0) >= lax.broadcasted_iota(jnp.int32, (n, n), 1)
    gd = BRANCH_WIDTH // A_GROUPS
    for g in range(A_GROUPS):
        w = jnp.where(causal, w_ref[g], 0.0)
        mixed = _dot(w, vn[:, g * gd:(g + 1) * gd]) + bias_ref[:, g:g + 1]
        o_ref[:, g * gd:(g + 1) * gd] = (u[:, g * gd:(g + 1) * gd] * mixed).astype(o_ref.dtype)


def mixer_a(proj, ln_g, ln_b, w_mix, bias_rows):
    m = proj.shape[0]
    r = A_CHUNK
    bw = BRANCH_WIDTH
    return pl.pallas_call(
        _mixer_a_body,
        grid=(m // r,),
        in_specs=[pl.BlockSpec((r, bw), lambda i: (i, OFF_A_U // bw)),
                  pl.BlockSpec((r, bw), lambda i: (i, OFF_A_V // bw)),
                  pl.BlockSpec((1, bw), lambda i: (0, 0)),
                  pl.BlockSpec((1, bw), lambda i: (0, 0)),
                  pl.BlockSpec((A_GROUPS, r, r), lambda i: (0, 0, 0)),
                  pl.BlockSpec((r, A_GROUPS), lambda i: (0, 0))],
        out_specs=[pl.BlockSpec((r, bw), lambda i: (i, 0)), pl.BlockSpec((r, bw), lambda i: (i, 0))],
        out_shape=[jax.ShapeDtypeStruct((m, bw), BF16), jax.ShapeDtypeStruct((m, bw), F32)],
        compiler_params=_cparams(("parallel",)),
        name="mixer_a",
    )(proj, proj, ln_g.reshape(1, bw), ln_b.reshape(1, bw), w_mix, bias_rows)


def _gla_head_chunk(q, k, v, g):
    c = q.shape[0]
    g2 = _cumsum_rows(g) * LOG2E
    n_tiles = c // 8
    row8 = lax.broadcasted_iota(jnp.int32, (8, 1), 0)
    q_t = [q[i * 8:(i + 1) * 8] for i in range(n_tiles)]
    g_t = [g2[i * 8:(i + 1) * 8] for i in range(n_tiles)]
    o_t = [jnp.zeros((8, v.shape[1]), F32) for _ in range(n_tiles)]
    for s in range(c):
        for i in range(s // 8, n_tiles):
            a = jnp.sum(q_t[i] * jnp.exp2(g_t[i] - g2[s:s + 1, :]) * k[s:s + 1, :], axis=-1, keepdims=True)
            if i == s // 8:
                a = jnp.where(row8 >= s - 8 * i, a, 0.0)
            o_t[i] = o_t[i] + a * v[s:s + 1, :]
    o = jnp.concatenate(o_t, axis=0) if n_tiles > 1 else o_t[0]
    gl = g2[c - 1:c, :]
    return o, q * jnp.exp2(g2), k * jnp.exp2(gl - g2), jnp.exp2(gl)


def _block_diag(xs):
    z = jnp.zeros_like(xs[0])
    return jnp.concatenate(
        [jnp.concatenate([x if j == i else z for j in range(len(xs))], axis=1) for i, x in enumerate(xs)], axis=0)


def _gla_state_dots(qgs, kds, vs, es, st_ref, bi):
    st = st_ref[bi]
    o_inter = _dot_nt(_block_diag(qgs), st)
    st_ref[bi] = st * jnp.concatenate(es, axis=1) + _dot_tn(jnp.concatenate(vs, axis=0), _block_diag(kds))
    return o_inter


def _gla_state_step(parts, vs, st_ref, bi):
    c = vs[0].shape[0]
    o_inter = _gla_state_dots([p[1] for p in parts], [p[2] for p in parts], vs, [p[3] for p in parts], st_ref, bi)
    return [p[0] + o_inter[h * c:(h + 1) * c] for h, p in enumerate(parts)]


def _gla_pair_tile(q, k, v, g, seg):
    g2 = _segment_cumsum(g, seg) * LOG2E
    tpos = lax.broadcasted_iota(jnp.int32, (8, 1), 0) & (seg - 1)
    o = jnp.sum(q * k, axis=-1, keepdims=True) * v
    for d in range(1, seg):
        a = jnp.sum(q * jnp.exp2(g2 - pltpu.roll(g2, d, axis=0)) * pltpu.roll(k, d, axis=0), axis=-1, keepdims=True)
        o = o + jnp.where(tpos >= d, a, 0.0) * pltpu.roll(v, d, axis=0)
    first = lax.broadcasted_iota(jnp.int32, (8, 1), 0) < seg
    gl_a, gl_b = g2[seg - 1:seg, :], g2[2 * seg - 1:2 * seg, :]
    kd = k * jnp.exp2(jnp.where(first, gl_a, gl_b) - g2)
    return o, q * jnp.exp2(g2), kd, jnp.exp2(gl_a), jnp.exp2(gl_b)


def _lane_broadcast_column(row):
    hi = row.astype(BF16).astype(F32)
    r1 = row - hi
    mid = r1.astype(BF16).astype(F32)
    lo = r1 - mid
    r = lax.broadcasted_iota(jnp.int32, (8, 1), 0)
    pieces = jnp.where(r == 0, hi, jnp.where(r == 1, mid, jnp.where(r == 2, lo, 0.0)))
    return _dot_tn(pieces.astype(BF16), jnp.ones((8, 128), BF16))


def _gla_state_dots_natural(qgs, kds, vs, es, st_ref, bi):
    st = st_ref[bi]
    o_inter = _dot(_block_diag(qgs), st)
    decay = _lane_broadcast_column(jnp.concatenate(es, axis=1))
    st_ref[bi] = decay * st + _dot_tn(_block_diag(kds), jnp.concatenate(vs, axis=0))
    return o_inter


def _gla_pair_step(tiles, vs, st_ref, p, seg):
    first = lax.broadcasted_iota(jnp.int32, (8, 1), 0) < seg
    qgs = [t[1] for t in tiles]
    oi_a = _gla_state_dots_natural(qgs, [jnp.where(first, t[2], 0.0) for t in tiles], vs, [t[3] for t in tiles],
                                   st_ref, 2 * p)
    oi_b = _gla_state_dots_natural(qgs, [jnp.where(first, 0.0, t[2]) for t in tiles], vs, [t[4] for t in tiles],
                                   st_ref, 2 * p + 1)
    return [t[0] + jnp.where(first, oi_a[h * 8:(h + 1) * 8], oi_b[h * 8:(h + 1) * 8]) for h, t in enumerate(tiles)]


def _load_state_natural(s0_ref, st_ref, kdim):
    def per_seq(bi, carry):
        for h in range(HEADS):
            s = s0_ref[bi, h]
            if kdim < HEAD_DIM:
                s = jnp.concatenate([s, jnp.zeros((HEAD_DIM - kdim, HEAD_DIM), F32)], axis=0)
            st_ref[bi, h * HEAD_DIM:(h + 1) * HEAD_DIM, :] = s
        return carry

    lax.fori_loop(0, s0_ref.shape[0], per_seq, 0)


def _store_state_natural(st_ref, sout_ref, kdim):
    def per_seq(bi, carry):
        for h in range(HEADS):
            sout_ref[bi, h] = st_ref[bi, h * HEAD_DIM:h * HEAD_DIM + kdim, :]
        return carry

    lax.fori_loop(0, sout_ref.shape[0], per_seq, 0)


def _load_state_t(s0_ref, st_ref, kdim):
    def per_seq(bi, carry):
        for h in range(HEADS):
            s = s0_ref[bi, h]
            if kdim < HEAD_DIM:
                s = jnp.concatenate([s, jnp.zeros((HEAD_DIM - kdim, HEAD_DIM), F32)], axis=0)
            st_ref[bi, :, h * HEAD_DIM:(h + 1) * HEAD_DIM] = s.T
        return carry

    lax.fori_loop(0, s0_ref.shape[0], per_seq, 0)


def _store_state_t(st_ref, sout_ref, kdim):
    def per_seq(bi, carry):
        for h in range(HEADS):
            sout_ref[bi, h] = st_ref[bi, :, h * HEAD_DIM:(h + 1) * HEAD_DIM].T[:kdim, :]
        return carry

    lax.fori_loop(0, sout_ref.shape[0], per_seq, 0)


GLA_UNROLL = 8


def _gla_chunk_loop(tb, chunk):
    def step(n, carry):
        chunk(pl.ds(pl.multiple_of(n * GLA_CHUNK, GLA_CHUNK), GLA_CHUNK))
        return carry

    lax.fori_loop(0, tb // GLA_CHUNK, step, 0, unroll=GLA_UNROLL)


def _mixer_b_body(q_ref, f_ref, i_ref, og_ref, lb_ref, ng_ref, s0_ref, o_ref, sout_ref, st_ref):
    kd = HEAD_DIM

    @pl.when(pl.program_id(1) == 0)
    def _():
        _load_state_t(s0_ref, st_ref, kd)

    def chunk(rows):
        parts, vs = [], []
        for h in range(HEADS):
            cols = slice(h * kd, (h + 1) * kd)
            lb = lb_ref[:, cols]
            fp = f_ref[0, rows, cols]
            f = lb + (1.0 - lb) * jax.nn.sigmoid(fp)
            g = jnp.log(jnp.maximum(f, F_TINY))
            k = (1.0 - lb) * jax.nn.sigmoid(-fp)
            vs.append(i_ref[0, rows, cols])
            parts.append(_gla_head_chunk(q_ref[0, rows, cols], k, vs[h], g))
        for h, o in enumerate(_gla_state_step(parts, vs, st_ref, 0)):
            cols = slice(h * kd, (h + 1) * kd)
            o = _rms_rows(o) * ng_ref[...] * jax.nn.sigmoid(og_ref[0, rows, cols])
            o_ref[0, rows, cols] = o.astype(o_ref.dtype)

    _gla_chunk_loop(q_ref.shape[1], chunk)

    @pl.when(pl.program_id(1) == pl.num_programs(1) - 1)
    def _():
        _store_state_t(st_ref, sout_ref, kd)


def mixer_b(pseq, lb, norm_g, s0, *, tb):
    bsz, t, _ = pseq.shape
    bw = BRANCH_WIDTH
    field = lambda off: pl.BlockSpec((1, tb, bw), lambda b, s: (b, s, off // bw))
    state = pl.BlockSpec((1, HEADS, HEAD_DIM, HEAD_DIM), lambda b, s: (b, 0, 0, 0))
    return pl.pallas_call(
        _mixer_b_body,
        grid=(bsz, t // tb),
        in_specs=[field(OFF_B_Q), field(OFF_B_F), field(OFF_B_I), field(OFF_B_G),
                  pl.BlockSpec((1, bw), lambda b, s: (0, 0)),
                  pl.BlockSpec((1, HEAD_DIM), lambda b, s: (0, 0)),
                  state],
        out_specs=[pl.BlockSpec((1, tb, bw), lambda b, s: (b, s, 0)), state],
        out_shape=[jax.ShapeDtypeStruct((bsz, t, bw), BF16),
                   jax.ShapeDtypeStruct((bsz, HEADS, HEAD_DIM, HEAD_DIM), F32)],
        scratch_shapes=[pltpu.VMEM((1, HEAD_DIM, HEADS * HEAD_DIM), F32)],
        compiler_params=_cparams(("parallel", "arbitrary")),
        name="mixer_b",
    )(pseq, pseq, pseq, pseq, lb.reshape(1, bw), norm_g.reshape(1, HEAD_DIM), s0)


def _mixer_c_body(q_ref, k_ref, v_ref, r_ref, lr_ref, w2_ref, ba_ref, ng_ref, s0_ref, o_ref, sout_ref, st_ref):
    kd = C_KDIM

    @pl.when(pl.program_id(1) == 0)
    def _():
        _load_state_t(s0_ref, st_ref, kd)

    zpad = jnp.zeros((GLA_CHUNK, HEAD_DIM - kd), F32)

    def chunk(rows):
        gate_in = _dot(lr_ref[0, rows, 0:C_RANK], w2_ref[...]) + ba_ref[...]
        g_all = _log_sigmoid(gate_in) / C_TAU
        parts, vs = [], []
        for h in range(HEADS):
            kcols = slice(h * kd, (h + 1) * kd)
            q = jnp.concatenate([q_ref[0, rows, kcols] * (kd ** -0.5), zpad], axis=1)
            k = jnp.concatenate([k_ref[0, rows, kcols], zpad], axis=1)
            g = jnp.concatenate([g_all[:, kcols], zpad], axis=1)
            vs.append(v_ref[0, rows, h * HEAD_DIM:(h + 1) * HEAD_DIM])
            parts.append(_gla_head_chunk(q, k, vs[h], g))
        for h, o in enumerate(_gla_state_step(parts, vs, st_ref, 0)):
            vcols = slice(h * HEAD_DIM, (h + 1) * HEAD_DIM)
            o = _rms_rows(o) * ng_ref[...] * _silu(r_ref[0, rows, vcols])
            o_ref[0, rows, vcols] = o.astype(o_ref.dtype)

    _gla_chunk_loop(q_ref.shape[1], chunk)

    @pl.when(pl.program_id(1) == pl.num_programs(1) - 1)
    def _():
        _store_state_t(st_ref, sout_ref, kd)


def mixer_c(pseq, w_a2, b_a, norm_g, s0, *, tb):
    bsz, t, _ = pseq.shape
    bw = BRANCH_WIDTH
    kw = HEADS * C_KDIM
    field = lambda off, w: pl.BlockSpec((1, tb, w), lambda b, s: (b, s, off // w))
    state = pl.BlockSpec((1, HEADS, C_KDIM, HEAD_DIM), lambda b, s: (b, 0, 0, 0))
    return pl.pallas_call(
        _mixer_c_body,
        grid=(bsz, t // tb),
        in_specs=[field(OFF_C_Q, kw), field(OFF_C_K, kw), field(OFF_C_V, bw), field(OFF_C_R, bw),
                  field(OFF_C_LR, 128),
                  pl.BlockSpec((C_RANK, kw), lambda b, s: (0, 0)),
                  pl.BlockSpec((1, kw), lambda b, s: (0, 0)),
                  pl.BlockSpec((1, HEAD_DIM), lambda b, s: (0, 0)),
                  state],
        out_specs=[pl.BlockSpec((1, tb, bw), lambda b, s: (b, s, 0)), state],
        out_shape=[jax.ShapeDtypeStruct((bsz, t, bw), BF16),
                   jax.ShapeDtypeStruct((bsz, HEADS, C_KDIM, HEAD_DIM), F32)],
        scratch_shapes=[pltpu.VMEM((1, HEAD_DIM, HEADS * HEAD_DIM), F32)],
        compiler_params=_cparams(("parallel", "arbitrary")),
        name="mixer_c",
    )(pseq, pseq, pseq, pseq, pseq, w_a2, b_a.reshape(1, kw), norm_g.reshape(1, HEAD_DIM), s0)


def _grouped_pairs_loop(n_rows, seg, st_ref, head_inputs, head_outputs):
    def step(it, carry):
        rows = pl.ds(pl.multiple_of(it * 16, 16), 16)
        ins = [head_inputs(rows, h) for h in range(HEADS)]
        halves = []
        for half in range(2):
            sl = slice(half * 8, (half + 1) * 8)
            vs = [x[2][sl] for x in ins]
            tiles = [_gla_pair_tile(x[0][sl], x[1][sl], x[2][sl], x[3][sl], seg) for x in ins]
            halves.append(_gla_pair_step(tiles, vs, st_ref, 2 * it + half, seg))
        for h in range(HEADS):
            head_outputs(rows, h, jnp.concatenate([halves[0][h], halves[1][h]], axis=0))
        return carry

    lax.fori_loop(0, n_rows // 16, step, 0, unroll=2)


def _mixer_b_group_body(seg, q_ref, f_ref, i_ref, og_ref, lb_ref, ng_ref, s0_ref, o_ref, sout_ref, st_ref):
    kd = HEAD_DIM
    _load_state_natural(s0_ref, st_ref, kd)

    def head_inputs(rows, h):
        cols = slice(h * kd, (h + 1) * kd)
        lb = lb_ref[:, cols]
        fp = f_ref[0, rows, cols]
        f = lb + (1.0 - lb) * jax.nn.sigmoid(fp)
        return (q_ref[0, rows, cols], (1.0 - lb) * jax.nn.sigmoid(-fp), i_ref[0, rows, cols],
                jnp.log(jnp.maximum(f, F_TINY)))

    def head_outputs(rows, h, o):
        cols = slice(h * kd, (h + 1) * kd)
        o = _rms_rows(o) * ng_ref[...] * jax.nn.sigmoid(og_ref[0, rows, cols])
        o_ref[0, rows, cols] = o.astype(o_ref.dtype)

    _grouped_pairs_loop(q_ref.shape[1], seg, st_ref, head_inputs, head_outputs)
    _store_state_natural(st_ref, sout_ref, kd)


def _mixer_c_group_body(seg, q_ref, k_ref, v_ref, r_ref, lr_ref, w2_ref, ba_ref, ng_ref, s0_ref, o_ref, sout_ref,
                        st_ref, g_scr):
    kd = C_KDIM
    _load_state_natural(s0_ref, st_ref, kd)
    g_scr[...] = _log_sigmoid(_dot(lr_ref[0, :, 0:C_RANK], w2_ref[...]) + ba_ref[...]) / C_TAU
    zpad = jnp.zeros((16, HEAD_DIM - kd), F32)

    def head_inputs(rows, h):
        kcols = slice(h * kd, (h + 1) * kd)
        pad = lambda x: jnp.concatenate([x, zpad], axis=1)
        return (pad(q_ref[0, rows, kcols] * (kd ** -0.5)), pad(k_ref[0, rows, kcols]),
                v_ref[0, rows, h * HEAD_DIM:(h + 1) * HEAD_DIM], pad(g_scr[rows, kcols]))

    def head_outputs(rows, h, o):
        vcols = slice(h * HEAD_DIM, (h + 1) * HEAD_DIM)
        o = _rms_rows(o) * ng_ref[...] * _silu(r_ref[0, rows, vcols])
        o_ref[0, rows, vcols] = o.astype(o_ref.dtype)

    _grouped_pairs_loop(q_ref.shape[1], seg, st_ref, head_inputs, head_outputs)
    _store_state_natural(st_ref, sout_ref, kd)


def _layer_stacked_state(body, prev_states, state_ref_pos, seqs_per_step):
    n_prev = 0 if prev_states is None else prev_states.shape[0]

    def wrapped(*refs):
        refs = list(refs)
        if n_prev:
            prev_ref = refs.pop(0)
            refs[state_ref_pos][0:n_prev] = prev_ref[...]
        refs[state_ref_pos] = refs[state_ref_pos].at[n_prev]
        return body(*refs)

    def spec(shape_tail, layers):
        return pl.BlockSpec((layers, seqs_per_step) + shape_tail, lambda s: (0, s) + (0,) * len(shape_tail))

    def out_shape(s0):
        return jax.ShapeDtypeStruct((n_prev + 1,) + s0.shape[1:], F32)

    prev_specs = lambda s0: [spec(s0.shape[2:], n_prev)] if n_prev else []
    prev_inputs = [prev_states] if n_prev else []
    return wrapped, prev_specs, prev_inputs, lambda s0: spec(s0.shape[2:], n_prev + 1), out_shape


def _grouped_gla_call(body, prows, fields, params, s0, prev_states, kdim, *, seg, tb, extra_scratch=()):
    m = prows.shape[0]
    bw = BRANCH_WIDTH
    p3 = prows.reshape(1, m, -1)
    layer = 0 if prev_states is None else prev_states.shape[0]
    state = pl.BlockSpec((None, tb // seg, HEADS, kdim, HEAD_DIM), lambda s: (layer, s, 0, 0, 0))
    n_in = len(fields) + len(params) + 1
    wrapped, prev_specs, prev_inputs, state_out_spec, state_out_shape = _layer_stacked_state(
        functools.partial(body, seg), prev_states, n_in + 1, tb // seg)
    o, s_out = pl.pallas_call(
        wrapped,
        grid=(m // tb,),
        in_specs=prev_specs(s0)
        + [pl.BlockSpec((1, tb, w), functools.partial(lambda s, c: (0, s, c), c=off // w)) for off, w in fields]
        + [pl.BlockSpec(x.shape, functools.partial(lambda s, n: (0,) * n, n=x.ndim)) for x in params] + [state],
        out_specs=[pl.BlockSpec((1, tb, bw), lambda s: (0, s, 0)), state_out_spec(s0)],
        out_shape=[jax.ShapeDtypeStruct((1, m, bw), BF16), state_out_shape(s0)],
        scratch_shapes=[pltpu.VMEM((tb // seg, HEADS * HEAD_DIM, HEAD_DIM), F32), *extra_scratch],
        compiler_params=_cparams(("parallel",)),
        name=body.__name__.strip("_"),
    )(*prev_inputs, *([p3] * len(fields)), *params, s0)
    return o.reshape(m, bw), s_out


def mixer_b_grouped(prows, lb, norm_g, s0, prev_states, *, seg, tb):
    bw = BRANCH_WIDTH
    return _grouped_gla_call(
        _mixer_b_group_body, prows, [(OFF_B_Q, bw), (OFF_B_F, bw), (OFF_B_I, bw), (OFF_B_G, bw)],
        [lb.reshape(1, bw), norm_g.reshape(1, HEAD_DIM)], s0, prev_states, HEAD_DIM, seg=seg, tb=tb)


def mixer_c_grouped(prows, w_a2, b_a, norm_g, s0, prev_states, *, seg, tb):
    bw = BRANCH_WIDTH
    kw = HEADS * C_KDIM
    return _grouped_gla_call(
        _mixer_c_group_body, prows, [(OFF_C_Q, kw), (OFF_C_K, kw), (OFF_C_V, bw), (OFF_C_R, bw), (OFF_C_LR, 128)],
        [w_a2, b_a.reshape(1, kw), norm_g.reshape(1, HEAD_DIM)], s0, prev_states, C_KDIM, seg=seg, tb=tb,
        extra_scratch=(pltpu.VMEM((tb, kw), F32),))


def _unit_lower_inverse(a, order):
    c = a.shape[0]
    eye = (lax.broadcasted_iota(jnp.int32, (c, c), 0) == lax.broadcasted_iota(jnp.int32, (c, c), 1)).astype(F32)
    p = eye - a
    pw = a
    n = 2
    while n < order:
        pw = _dot(pw, pw)
        p = p + _dot(p, pw)
        n *= 2
    return p


D_ROWS = 64


def _stack_heads(x):
    return jnp.concatenate([x[:, h * HEAD_DIM:(h + 1) * HEAD_DIM] for h in range(HEADS)], axis=0)


def _stack_head_lanes(x):
    return jnp.concatenate([x[:, h:h + 1] for h in range(HEADS)], axis=0)


def _lane_pad_heads(x):
    return jnp.pad(x.reshape(1, HEADS), ((0, 0), (0, 128 - HEADS)))


def _segment_cumsum(x, seg):
    tpos = lax.broadcasted_iota(jnp.int32, x.shape, 0) & (seg - 1)
    sh = 1
    while sh < seg:
        x = x + jnp.where(tpos >= sh, pltpu.roll(x, sh, axis=0), 0.0)
        sh *= 2
    return x


def _delta_chunk_operands(seg, qkv, beta_all, g_all):
    hd = HEAD_DIM
    bw = BRANCH_WIDTH
    r = HEADS * qkv.shape[0]
    q = _stack_heads(qkv[:, 0:bw])
    k = _stack_heads(qkv[:, bw:2 * bw])
    v = _stack_heads(qkv[:, 2 * bw:3 * bw])
    q = q * lax.rsqrt(jnp.sum(q * q, axis=-1, keepdims=True) + EPS) * (hd ** -0.5)
    k = k * lax.rsqrt(jnp.sum(k * k, axis=-1, keepdims=True) + EPS)
    beta = _stack_head_lanes(beta_all)
    gc = _stack_head_lanes(_segment_cumsum(g_all, seg))
    ri = lax.broadcasted_iota(jnp.int32, (r, r), 0)
    ci = lax.broadcasted_iota(jnp.int32, (r, r), 1)
    shift = seg.bit_length() - 1
    same = (ri >> shift) == (ci >> shift)
    gr = jnp.sum(jnp.where(ri == ci, gc, 0.0), axis=0, keepdims=True)
    decay = jnp.where(same, jnp.where(ri >= ci, jnp.exp(jnp.minimum(gc - gr, 0.0)), 0.0), 0.0)
    a_mat = jnp.where(ri > ci, beta * decay * _dot_nt(k, k), 0.0)
    t_inv = _unit_lower_inverse(a_mat, seg)
    eg = jnp.exp(gc)
    sol = _dot(t_inv, jnp.concatenate([(beta * eg) * k, beta * v], axis=1))
    qk = _dot_nt(q, k) * decay
    is_last = (ci & (seg - 1)) == seg - 1
    gl = jnp.sum(jnp.where(same, jnp.where(is_last, gr, 0.0), 0.0), axis=1, keepdims=True)
    return sol[:, :hd], sol[:, hd:], qk, q * eg, k * jnp.exp(gl - gc), jnp.exp(gl)


def _mixer_d_body(x_ref, beta_ref, dec_ref, z_ref, cst_ref, s0_ref, cw_ref, alog_ref, dtb_ref, ng_ref,
                  o_ref, cst_out_ref, sout_ref, ext, conv, st_ref):
    nb, tb = x_ref.shape[0], x_ref.shape[1]
    c = D_ROWS
    hd = HEAD_DIM
    t = pl.program_id(1)
    nt = pl.num_programs(1)

    @pl.when(t == 0)
    def _():
        ext[:, 5:8, :] = cst_ref[...]
        st_ref[...] = s0_ref[...].reshape(nb * HEADS, hd, hd)

    for bi in range(nb):
        ext[bi, 8:8 + tb, :] = x_ref[bi]
        y = ext[bi, 5:5 + tb, :] * cw_ref[0:1, :]
        for j in range(1, D_CONV):
            y = y + ext[bi, 5 + j:5 + j + tb, :] * cw_ref[j:j + 1, :]
        conv[bi] = _silu(y)

    @pl.when(t == nt - 1)
    def _():
        cst_out_ref[...] = ext[:, 8 + tb - (D_CONV - 1):8 + tb, :]

    ext[:, 5:8, :] = ext[:, 5 + tb:8 + tb, :]

    a_neg = -jnp.exp(alog_ref[...])

    def chunk(n, carry):
        rows = pl.ds(pl.multiple_of(n * c, c), c)
        for bi in range(nb):
            beta_all = jax.nn.sigmoid(beta_ref[bi, rows, :])
            g_all = a_neg * _softplus(dec_ref[bi, rows, :] + dtb_ref[...])
            w, u, qk, qg, kdec, egl = _delta_chunk_operands(c, conv[bi, rows, :], beta_all, g_all)
            deltas, oqs = [], []
            for h in range(HEADS):
                hs = slice(h * c, (h + 1) * c)
                st = st_ref[bi * HEADS + h]
                x = _dot(jnp.concatenate([w[hs], qg[hs]], axis=0), st)
                delta = u[hs] - x[:c]
                deltas.append(delta)
                oqs.append(x[c:])
                st_ref[bi * HEADS + h] = egl[(h + 1) * c - 1:(h + 1) * c, :] * st + _dot_tn(kdec[hs], delta)
            o = jnp.concatenate(oqs, axis=0) + _dot(qk, jnp.concatenate(deltas, axis=0))
            o = _rms_rows(o) * ng_ref[...] * _silu(_stack_heads(z_ref[bi, rows, :]))
            for h in range(HEADS):
                o_ref[bi, rows, h * hd:(h + 1) * hd] = o[h * c:(h + 1) * c].astype(o_ref.dtype)
        return carry

    lax.fori_loop(0, tb // c, chunk, 0, unroll=2)

    @pl.when(t == nt - 1)
    def _():
        sout_ref[...] = st_ref[...].reshape(nb, HEADS, hd, hd)


def _mixer_d_group_body(seg, x_ref, cst_ref, beta_ref, dec_ref, z_ref, s0_ref, cw_ref, alog_ref, dtb_ref, ng_ref,
                        o_ref, cst_out_ref, sout_ref, conv, w_s, u_s, qg_s, kd_s, egl_s, delta_s, oq_s, stp_s, x_s):
    tb = x_ref.shape[1]
    nseq = tb // seg
    c = D_ROWS
    hd = HEAD_DIM
    pairs = c // 8
    hist_rows = D_CONV - 1

    lane_blocks = stp_s.shape[0]
    x = x_ref[0]
    stp_s[...] = jnp.zeros_like(stp_s)
    for cb in range(lane_blocks):
        lanes = slice(cb * 128, (cb + 1) * 128)
        x_s[cb] = x[:, lanes]
        for j in range(hist_rows):
            stp_s[cb, pl.ds(j, nseq, stride=seg), :] = cst_ref[:, j, lanes]
    stp = jnp.concatenate([stp_s[cb] for cb in range(lane_blocks)], axis=1)

    tpos = lax.broadcasted_iota(jnp.int32, (tb, 1), 0) & (seg - 1)
    y = x * cw_ref[3:4, :]
    for d in range(1, D_CONV):
        hist = stp if d == 3 else pltpu.roll(stp, tb - (3 - d), axis=0)
        y = y + jnp.where(tpos < d, hist, pltpu.roll(x, d, axis=0)) * cw_ref[3 - d:4 - d, :]
    conv[...] = _silu(y)
    for cb in range(lane_blocks):
        for j in range(hist_rows):
            cst_out_ref[:, j, cb * 128:(cb + 1) * 128] = x_s[cb, pl.ds(seg - hist_rows + j, nseq, stride=seg), :]

    a_neg = -jnp.exp(alog_ref[...])
    low = lax.broadcasted_iota(jnp.int32, (8, 1), 0) < seg

    for n in range(tb // c):
        rows = slice(n * c, (n + 1) * c)
        beta_all = jax.nn.sigmoid(beta_ref[0, rows, :])
        g_all = a_neg * _softplus(dec_ref[0, rows, :] + dtb_ref[...])
        w, u, qk, qg, kdec, egl = _delta_chunk_operands(seg, conv[rows, :], beta_all, g_all)
        w_s[...] = w
        u_s[...] = u
        qg_s[...] = qg
        kd_s[...] = kdec
        egl_s[...] = jnp.broadcast_to(egl, (HEADS * c, hd))

        def pair(p, carry):
            rp = pl.ds(pl.multiple_of(p * 8, 8), 8)
            h = p // pairs
            ja = n * (c // seg) + (p % pairs) * 2
            sa = s0_ref[ja, h]
            sb = s0_ref[ja + 1, h]
            lhs = jnp.concatenate([w_s[rp, :], qg_s[rp, :]], axis=0)
            xa = _dot(lhs, sa)
            xb = _dot(lhs, sb)
            delta = u_s[rp, :] - jnp.where(low, xa[:8], xb[:8])
            delta_s[rp, :] = delta
            oq_s[rp, :] = jnp.where(low, xa[8:], xb[8:])
            kd = kd_s[rp, :]
            e = egl_s[rp, :]
            sout_ref[ja, h] = e[seg - 1:seg, :] * sa + _dot_tn(jnp.where(low, kd, 0.0), delta)
            sout_ref[ja + 1, h] = e[2 * seg - 1:2 * seg, :] * sb + _dot_tn(jnp.where(low, 0.0, kd), delta)
            return carry

        lax.fori_loop(0, HEADS * pairs, pair, 0, unroll=4)
        o = oq_s[...] + _dot(qk, delta_s[...])
        o = _rms_rows(o) * ng_ref[...] * _silu(_stack_heads(z_ref[0, rows, :]))
        for h in range(HEADS):
            o_ref[0, rows, h * hd:(h + 1) * hd] = o[h * c:(h + 1) * c].astype(o_ref.dtype)


def mixer_d(pseq, conv_state, s0, conv_w, a_log, dt_bias, norm_g, *, tb, nb):
    bsz, t, _ = pseq.shape
    bw = BRANCH_WIDTH
    field = lambda off, w: pl.BlockSpec((nb, tb, w), lambda b, s: (b, s, off // w))
    state = pl.BlockSpec((nb, HEADS, HEAD_DIM, HEAD_DIM), lambda b, s: (b, 0, 0, 0))
    cstate = pl.BlockSpec((nb, D_CONV - 1, D_QKV), lambda b, s: (b, 0, 0))
    return pl.pallas_call(
        _mixer_d_body,
        grid=(bsz // nb, t // tb),
        in_specs=[field(OFF_D_QKV, D_QKV), field(OFF_D_BETA, 128), field(OFF_D_DECAY, 128), field(OFF_D_Z, bw),
                  cstate, state,
                  pl.BlockSpec((D_CONV, D_QKV), lambda b, s: (0, 0)),
                  pl.BlockSpec((1, 128), lambda b, s: (0, 0)),
                  pl.BlockSpec((1, 128), lambda b, s: (0, 0)),
                  pl.BlockSpec((1, HEAD_DIM), lambda b, s: (0, 0))],
        out_specs=[pl.BlockSpec((nb, tb, bw), lambda b, s: (b, s, 0)), cstate, state],
        out_shape=[jax.ShapeDtypeStruct((bsz, t, bw), BF16),
                   jax.ShapeDtypeStruct((bsz, D_CONV - 1, D_QKV), F32),
                   jax.ShapeDtypeStruct((bsz, HEADS, HEAD_DIM, HEAD_DIM), F32)],
        scratch_shapes=[pltpu.VMEM((nb, tb + 8, D_QKV), F32), pltpu.VMEM((nb, tb, D_QKV), F32),
                        pltpu.VMEM((nb * HEADS, HEAD_DIM, HEAD_DIM), F32)],
        compiler_params=_cparams(("parallel", "arbitrary")),
        name="mixer_d",
    )(pseq, pseq, pseq, pseq, conv_state, s0, conv_w, _lane_pad_heads(a_log), _lane_pad_heads(dt_bias),
      norm_g.reshape(1, HEAD_DIM))


def mixer_d_grouped(prows, conv_state, s0, conv_w, a_log, dt_bias, norm_g, prev_states, *, seg, tb):
    m = prows.shape[0]
    bsz = m // seg
    bw = BRANCH_WIDTH
    field = lambda off, w: pl.BlockSpec((1, tb, w), lambda s: (0, s, off // w))
    layer = 0 if prev_states is None else prev_states.shape[0]
    state = pl.BlockSpec((None, tb // seg, HEADS, HEAD_DIM, HEAD_DIM), lambda s: (layer, s, 0, 0, 0))
    cstate = pl.BlockSpec((tb // seg, D_CONV - 1, D_QKV), lambda s: (s, 0, 0))
    rows128 = pltpu.VMEM((HEADS * D_ROWS, HEAD_DIM), F32)
    lane_rows = pltpu.VMEM((D_QKV // 128, tb, 128), F32)
    n_in, state_out_pos = 10, 2
    wrapped, prev_specs, prev_inputs, state_out_spec, state_out_shape = _layer_stacked_state(
        functools.partial(_mixer_d_group_body, seg), prev_states, n_in + state_out_pos, tb // seg)
    o, cst, s_out = pl.pallas_call(
        wrapped,
        grid=(m // tb,),
        in_specs=prev_specs(s0) + [
                  field(OFF_D_QKV, D_QKV), cstate,
                  field(OFF_D_BETA, 128), field(OFF_D_DECAY, 128), field(OFF_D_Z, bw),
                  state,
                  pl.BlockSpec((D_CONV, D_QKV), lambda s: (0, 0)),
                  pl.BlockSpec((1, 128), lambda s: (0, 0)),
                  pl.BlockSpec((1, 128), lambda s: (0, 0)),
                  pl.BlockSpec((1, HEAD_DIM), lambda s: (0, 0))],
        out_specs=[pl.BlockSpec((1, tb, bw), lambda s: (0, s, 0)), cstate, state_out_spec(s0)],
        out_shape=[jax.ShapeDtypeStruct((1, m, bw), BF16),
                   jax.ShapeDtypeStruct((bsz, D_CONV - 1, D_QKV), F32),
                   state_out_shape(s0)],
        scratch_shapes=[pltpu.VMEM((tb, D_QKV), F32)] + [rows128] * 7 + [lane_rows] * 2,
        compiler_params=_cparams(("parallel",)),
        name="mixer_d_grouped",
    )(*prev_inputs, prows.reshape(1, m, -1), conv_state, prows.reshape(1, m, -1), prows.reshape(1, m, -1),
      prows.reshape(1, m, -1), s0, conv_w, _lane_pad_heads(a_log), _lane_pad_heads(dt_bias),
      norm_g.reshape(1, HEAD_DIM))
    return o.reshape(m, bw), cst, s_out


PACK_TILE = 512
PACK_SRC = ([4624, 5136, 5648, 0, 512, 1024, 1536, 2048, 2560, 3584, 4112, 6168, 3072, None]
            + [6680 + PACK_TILE * k for k in range(N_BRANCH * D_MODEL // PACK_TILE)])
PACK_SMALL_TILE = PACK_SRC.index(None)


def _pack_lookup(t, values):
    out = jnp.int32(0)
    for i, v in enumerate(values):
        out = jnp.where(t == i, jnp.int32(v), out)
    return out


PACK_EXTRA = 32


def _pack_w_in_body(a_ref, b_ref, tail_ref, small_ref, o_ref):
    t = pl.program_id(1)
    last = len(PACK_SRC) - 1
    shifts = [0 if s is None else s % PACK_TILE for s in PACK_SRC]
    assert max(shifts) <= PACK_EXTRA and all(s % 8 == 0 for s in shifts)

    def emit(shift, left_ref, right_ref):
        for c in range(a_ref.shape[1] // 128):
            lanes = slice(c * 128, (c + 1) * 128)
            w = left_ref[shift:, lanes]
            if shift:
                w = jnp.concatenate([w, right_ref[0:shift, lanes]], axis=0)
            o_ref[lanes, :] = w.T.astype(BF16)

    for shift in sorted(set(shifts)):
        tiles = [i for i, s in enumerate(shifts) if s == shift and i not in (PACK_SMALL_TILE, last)]
        pl.when(functools.reduce(jnp.logical_or, [t == i for i in tiles]))(
            functools.partial(emit, shift, a_ref, b_ref))
    pl.when(t == last)(functools.partial(emit, shifts[last], a_ref, tail_ref))
    pl.when(t == PACK_SMALL_TILE)(functools.partial(emit, 0, small_ref, None))


def pack_w_in(w_in):
    layers, k, n_src = w_in.shape
    w_t = jnp.swapaxes(w_in, 1, 2)
    bases = [0 if s is None else s - s % PACK_TILE for s in PACK_SRC]
    extra = [(b + PACK_TILE) // PACK_EXTRA for b in bases]
    full_blocks = n_src // PACK_EXTRA
    assert all(e < full_blocks for e in extra[:-1]) and extra[-1] == full_blocks
    extra[-1] = 0
    z = lambda n: jnp.zeros((layers, n, k), w_in.dtype)
    tail = jnp.concatenate([w_t[:, full_blocks * PACK_EXTRA:], z(PACK_EXTRA - n_src % PACK_EXTRA)], axis=1)
    small = jnp.concatenate([w_t[:, 4096:4112], z(112),
                             w_t[:, 6160:6164], z(124),
                             w_t[:, 6164:6168], z(252)], axis=1)
    return pl.pallas_call(
        _pack_w_in_body,
        grid=(layers, len(PACK_SRC)),
        in_specs=[pl.BlockSpec((None, PACK_TILE, k),
                               lambda l, t: (l, _pack_lookup(t, [b // PACK_TILE for b in bases]), 0)),
                  pl.BlockSpec((None, PACK_EXTRA, k), lambda l, t: (l, _pack_lookup(t, extra), 0)),
                  pl.BlockSpec((None, PACK_EXTRA, k), lambda l, t: (l, 0, 0)),
                  pl.BlockSpec((None, PACK_TILE, k), lambda l, t: (l, 0, 0))],
        out_specs=pl.BlockSpec((None, k, PACK_TILE), lambda l, t: (l, 0, t)),
        out_shape=jax.ShapeDtypeStruct((layers, k, N_PROJ), BF16),
        compiler_params=_cparams(("parallel", "arbitrary")),
        name="pack_w_in",
    )(w_t, w_t, tail, small)


def _trunk(x, p, s_hgrn, s_gla, s_delta, c_dconv, c_fconv, wts, lbs, *, sample):
    bsz, t, _ = x.shape
    m = bsz * t
    tm = m if sample else 1024
    h = x.reshape(m, D_MODEL)
    outs = {k: [] for k in ("hgrn", "gla", "delta", "dconv", "fconv", "v")}
    s_b = s_c = s_d = None
    for l in range(DEPTH):
        w = wts[l]
        proj, gates = in_proj(h, w["g_mix"], w["w_in_layers"], l, tm=tm, tn=1024)
        o_a, v_a = mixer_a(proj, w["a_ln_g"], w["a_ln_b"], w["a_w_mix"], w["a_bias_rows"])
        st_b, st_c, st_d = (s_hgrn, s_gla, s_delta) if sample else (s_hgrn[l], s_gla[l], s_delta[l])
        b_args = (lbs[l], w["b_norm_g"], st_b)
        c_args = (w["c_w_a2"], w["c_b_a"], w["c_norm_g"], st_c)
        d_args = (c_dconv[l], st_d, w["d_conv_w"], w["d_a_log"], w["d_dt_bias"], w["d_norm_g"])
        if sample:
            o_b, s_b = mixer_b_grouped(proj, *b_args, s_b, seg=t, tb=D_ROWS)
            o_c, s_c = mixer_c_grouped(proj, *c_args, s_c, seg=t, tb=D_ROWS)
            o_d, nb_d, s_d = mixer_d_grouped(proj, *d_args, s_d, seg=t, tb=D_ROWS)
        else:
            pseq = proj.reshape(bsz, t, N_SEQ_COLS)
            flat = lambda o: o.reshape(m, BRANCH_WIDTH)
            o_b, s_b = mixer_b(pseq, *b_args, tb=256)
            o_c, s_c = mixer_c(pseq, *c_args, tb=256)
            o_d, nb_d, s_d = mixer_d(pseq, *d_args, tb=256, nb=2 if bsz % 2 == 0 else 1)
            o_b, o_c, o_d = flat(o_b), flat(o_c), flat(o_d)
        merged = merge_branches((o_a, o_b, o_c, o_d), gates, w["w_branch"], tm=tm, tn=1024)
        h = matmul_residual(merged, w["w_out_layers"], l, h, tm=tm, tn=1024)

        act, nb_f = ffn_up_gate(h, w["g_ffn"], w["w_ffn_up_layers"], l, c_fconv[l], w["ffn_conv_w"],
                                w["ffn_conv_b"], seq_len=t, tm=tm, tn=512)
        h = matmul_residual(act, w["w_ffn_down_layers"], l, h, tm=min(tm, 512), tn=512)

        h = ple_update(h, p[l].reshape(m, PLE_DIM), w["g_ple"], w["w_ple_gate"], w["w_ple_proj"],
                       w["g_final"] if l == DEPTH - 1 else None, tm=min(tm, 512))

        if not sample:
            outs["hgrn"].append(s_b)
            outs["gla"].append(s_c)
            outs["delta"].append(s_d)
        outs["dconv"].append(nb_d)
        outs["fconv"].append(nb_f)
        outs["v"].append(v_a.reshape(bsz, t, BRANCH_WIDTH))
    y = h.reshape(bsz, t, D_MODEL)
    states = (s_b, s_c, s_d) if sample else tuple(jnp.stack(outs[k]) for k in ("hgrn", "gla", "delta"))
    return (y,) + states + tuple(jnp.stack(outs[k]) for k in ("dconv", "fconv", "v"))


def kernel(x_prompt, x_sample, state_hgrn, state_gla, state_delta, state_delta_conv, state_ffn_conv, p_prompt, p_sample, g_mix, w_in, a_ln_g, a_ln_b, a_w_s, a_b_s, b_lb, b_norm_g, c_w_a2, c_b_a, c_norm_g, d_conv_w, d_a_log, d_dt_bias, d_norm_g, w_branch, w_out, g_ffn, w_ffn_up, ffn_conv_w, ffn_conv_b, w_ffn_down, g_ple, w_ple_gate, w_ple_proj, g_final):
    bp, t_p, _ = x_prompt.shape
    bs, t_s, _ = x_sample.shape
    sm = jax.nn.softmax(b_lb.astype(F32), axis=0)
    lbs = jnp.cumsum(sm, axis=0) - sm[0]

    w_in_packed = pack_w_in(w_in)
    shared = [dict(
        g_mix=g_mix[l], w_in_layers=w_in_packed, a_ln_g=a_ln_g[l], a_ln_b=a_ln_b[l],
        b_norm_g=b_norm_g[l], c_w_a2=c_w_a2[l], c_b_a=c_b_a[l], c_norm_g=c_norm_g[l],
        d_conv_w=d_conv_w[l], d_a_log=d_a_log[l], d_dt_bias=d_dt_bias[l], d_norm_g=d_norm_g[l],
        w_branch=w_branch[l].astype(BF16), w_out_layers=w_out, g_ffn=g_ffn[l],
        w_ffn_up_layers=w_ffn_up, ffn_conv_w=ffn_conv_w[l], ffn_conv_b=ffn_conv_b[l],
        w_ffn_down_layers=w_ffn_down, g_ple=g_ple[l], w_ple_gate=w_ple_gate[l].astype(BF16),
        w_ple_proj=w_ple_proj[l].astype(BF16), g_final=g_final) for l in range(DEPTH)]

    def layer_weights(l, sample):
        if sample:
            seqs = A_CHUNK // t_s
            idx = jnp.arange(A_CHUNK) // t_s
            same_seq = idx[:, None] == idx[None, :]
            w_mix = jnp.where(same_seq, jnp.tile(a_w_s[l, :, :t_s, :t_s], (1, seqs, seqs)), 0.0)
            bias_rows = jnp.tile(a_b_s[l, :, :t_s].T, (seqs, 1))
        else:
            w_mix = a_w_s[l]
            bias_rows = a_b_s[l].T
        return dict(shared[l], a_w_mix=w_mix, a_bias_rows=bias_rows)

    dt = x_prompt.dtype
    zeros = lambda *s: jnp.zeros((DEPTH, bp) + s, dt)
    out_p = _trunk(x_prompt, p_prompt, zeros(HEADS, HEAD_DIM, HEAD_DIM), zeros(HEADS, C_KDIM, HEAD_DIM),
                   zeros(HEADS, HEAD_DIM, HEAD_DIM), zeros(D_CONV - 1, D_QKV), zeros(FFN_CONV - 1, FFN_DIM),
                   [layer_weights(l, False) for l in range(DEPTH)], lbs, sample=False)
    out_s = _trunk(x_sample, p_sample, state_hgrn, state_gla, state_delta, state_delta_conv, state_ffn_conv,
                   [layer_weights(l, True) for l in range(DEPTH)], lbs, sample=True)
    y_p, hgrn_p, gla_p, delta_p, dconv_p, fconv_p, _ = out_p
    y_s, hgrn_s, gla_s, delta_s, dconv_s, fconv_s, v_s = out_s
    return (y_p, y_s, hgrn_p, hgrn_s, gla_p, gla_s, delta_p, delta_s,
            dconv_p, dconv_s, fconv_p, fconv_s, v_s)
```

```python
import functools

import jax
import jax.numpy as jnp
from jax import lax
from jax.experimental import pallas as pl
from jax.experimental.pallas import tpu as pltpu

F32 = jnp.float32
BF16 = jnp.bfloat16

D_MODEL = 2048
DEPTH = 2
PLE_DIM = 256
EPS = 1e-6
F_TINY = 1e-30
N_BRANCH = 4
BRANCH_WIDTH = D_MODEL // 4
A_GROUPS = 4
A_CHUNK = 128
HEADS = 4
HEAD_DIM = 128
C_KDIM = 64
C_RANK = 16
C_TAU = 16.0
D_CONV = 4
D_QKV = 3 * BRANCH_WIDTH
FFN_DIM = 5632
FFN_CONV = 3
GLA_CHUNK = 16
LOG2E = 1.4426950408889634

OFF_D_QKV = 0
OFF_A_U = 1536
OFF_A_V = 2048
OFF_B_Q = 2560
OFF_B_F = 3072
OFF_B_I = 3584
OFF_B_G = 4096
OFF_C_V = 4608
OFF_C_R = 5120
OFF_D_Z = 5632
OFF_C_Q = 6144
OFF_C_K = 6400
OFF_C_LR = 6656
OFF_D_BETA = 6784
OFF_D_DECAY = 6912
OFF_GATES = 7168
N_SEQ_COLS = 7168
N_PROJ = OFF_GATES + N_BRANCH * D_MODEL

VMEM_LIMIT = 56 * 1024 * 1024


def _cparams(sem):
    return pltpu.CompilerParams(dimension_semantics=sem, vmem_limit_bytes=VMEM_LIMIT)


def _gelu(x):
    return 0.5 * x * (1.0 + jnp.tanh(0.7978845608028654 * (x + 0.044715 * (x * x * x))))


def _silu(x):
    return x * jax.nn.sigmoid(x)


def _softplus(x):
    return jnp.maximum(x, 0.0) + jnp.log1p(jnp.exp(-jnp.abs(x)))


def _log_sigmoid(x):
    return -_softplus(-x)


def _rms_rows(x):
    return x * lax.rsqrt(jnp.mean(x * x, axis=-1, keepdims=True) + EPS)


def _dot(a, b):
    return jnp.dot(a, b, preferred_element_type=F32)


def _dot_nt(a, b):
    return lax.dot_general(a, b, (((1,), (1,)), ((), ())), preferred_element_type=F32)


def _dot_tn(a, b):
    return lax.dot_general(a, b, (((0,), (0,)), ((), ())), preferred_element_type=F32)


def _cumsum_rows(x):
    n = x.shape[0]
    row = lax.broadcasted_iota(jnp.int32, x.shape, 0)
    sh = 1
    while sh < n:
        x = x + jnp.where(row >= sh, pltpu.roll(x, sh, axis=0), 0.0)
        sh *= 2
    return x


NORM_ROWS = 128


def _norm_to_scratch(x_ref, g_ref, a_scr):
    def slab(n, carry):
        rows = pl.ds(pl.multiple_of(n * NORM_ROWS, NORM_ROWS), NORM_ROWS)
        a_scr[rows, :] = (_rms_rows(x_ref[rows, :]) * g_ref[...]).astype(BF16)
        return carry

    lax.fori_loop(0, x_ref.shape[0] // NORM_ROWS, slab, 0)


def _matmul_residual_body(x_ref, w_ref, r_ref, o_ref, w_scr):
    @pl.when(pl.program_id(1) == 0)
    def _():
        def slab(n, carry):
            rows = pl.ds(pl.multiple_of(n * NORM_ROWS, NORM_ROWS), NORM_ROWS)
            w_scr[rows, :] = w_ref[rows, :].astype(BF16)
            return carry

        lax.fori_loop(0, w_ref.shape[0] // NORM_ROWS, slab, 0)

    o_ref[...] = r_ref[...] + _dot(x_ref[...], w_scr[...])


def matmul_residual(x, w_layers, layer, res, *, tm, tn):
    m, k = x.shape
    n = w_layers.shape[2]
    return pl.pallas_call(
        _matmul_residual_body,
        grid=(n // tn, m // tm),
        in_specs=[pl.BlockSpec((tm, k), lambda j, i: (i, 0)),
                  pl.BlockSpec((None, k, tn), lambda j, i: (layer, 0, j)),
                  pl.BlockSpec((tm, tn), lambda j, i: (i, j))],
        out_specs=pl.BlockSpec((tm, tn), lambda j, i: (i, j)),
        out_shape=jax.ShapeDtypeStruct((m, n), F32),
        scratch_shapes=[pltpu.VMEM((k, tn), BF16)],
        compiler_params=_cparams(("parallel", "arbitrary")),
        name="matmul_residual",
    )(x, w_layers, res)


def _in_proj_body(n_seq_tiles, x_ref, g_ref, w_ref, seq_ref, gate_ref, a_scr):
    j = pl.program_id(1)

    @pl.when(j == 0)
    def _():
        _norm_to_scratch(x_ref, g_ref, a_scr)

    acc = _dot(a_scr[...], w_ref[...])

    @pl.when(j < n_seq_tiles)
    def _():
        seq_ref[...] = acc

    @pl.when(j >= n_seq_tiles)
    def _():
        gate_ref[...] = acc.astype(gate_ref.dtype)


def in_proj(x, g, w_layers, layer, *, tm, tn):
    m, k = x.shape
    n_seq = N_SEQ_COLS // tn
    n_gate = N_BRANCH * D_MODEL // tn
    return pl.pallas_call(
        functools.partial(_in_proj_body, n_seq),
        grid=(m // tm, n_seq + n_gate),
        in_specs=[pl.BlockSpec((tm, k), lambda i, j: (i, 0)),
                  pl.BlockSpec((1, k), lambda i, j: (0, 0)),
                  pl.BlockSpec((None, k, tn), lambda i, j: (layer, 0, j))],
        out_specs=[pl.BlockSpec((tm, tn), lambda i, j: (i, jnp.minimum(j, n_seq - 1))),
                   pl.BlockSpec((tm, tn), lambda i, j: (i, jnp.maximum(j - n_seq, 0)))],
        out_shape=[jax.ShapeDtypeStruct((m, N_SEQ_COLS), F32),
                   jax.ShapeDtypeStruct((m, N_BRANCH * D_MODEL), BF16)],
        scratch_shapes=[pltpu.VMEM((tm, k), BF16)],
        compiler_params=_cparams(("parallel", "arbitrary")),
        name="in_proj",
    )(x, g.reshape(1, k), w_layers)


def _merge_body(oa_ref, ob_ref, oc_ref, od_ref, ga_ref, gb_ref, gc_ref, gd_ref, w_ref, o_ref):
    acc = None
    for b, (o_b, g_b) in enumerate(((oa_ref, ga_ref), (ob_ref, gb_ref), (oc_ref, gc_ref), (od_ref, gd_ref))):
        gate = 0.5 + 0.5 * jnp.tanh(0.5 * g_b[...].astype(F32))
        term = gate * _dot(o_b[...], w_ref[b])
        acc = term if acc is None else acc + term
    o_ref[...] = acc.astype(o_ref.dtype)


def merge_branches(o_branches, gates, w_branch, *, tm, tn):
    m = gates.shape[0]
    gate_specs = [pl.BlockSpec((tm, tn), functools.partial(
        lambda i, j, b: (i, b * (D_MODEL // tn) + j), b=b)) for b in range(N_BRANCH)]
    return pl.pallas_call(
        _merge_body,
        grid=(m // tm, D_MODEL // tn),
        in_specs=[pl.BlockSpec((tm, BRANCH_WIDTH), lambda i, j: (i, 0))] * N_BRANCH + gate_specs
        + [pl.BlockSpec((N_BRANCH, BRANCH_WIDTH, tn), lambda i, j: (0, 0, j))],
        out_specs=pl.BlockSpec((tm, tn), lambda i, j: (i, j)),
        out_shape=jax.ShapeDtypeStruct((m, D_MODEL), BF16),
        compiler_params=_cparams(("parallel", "arbitrary")),
        name="merge_branches",
    )(*o_branches, gates, gates, gates, gates, w_branch)


PLE_ROWS = 256


def _ple_body(final, x_ref, p_ref, g_ref, wg_ref, wp_ref, *refs):
    gout_ref, o_ref = refs if final else (None, refs[0])

    slab_rows = min(PLE_ROWS, x_ref.shape[0])
    assert x_ref.shape[0] % slab_rows == 0

    def slab(n, carry):
        rows = pl.ds(pl.multiple_of(n * slab_rows, slab_rows), slab_rows)
        x = x_ref[rows, :]
        a = (_rms_rows(x) * g_ref[...]).astype(BF16)
        gate = 0.5 + 0.5 * jnp.tanh(0.5 * _dot(a, wg_ref[...]))
        h = x + _dot(p_ref[rows, :].astype(BF16), wp_ref[...]) * gate
        o_ref[rows, :] = h if gout_ref is None else _rms_rows(h) * gout_ref[...]
        return carry

    lax.fori_loop(0, x_ref.shape[0] // slab_rows, slab, 0, unroll=2)


def ple_update(x, p, g, w_gate, w_proj, g_out=None, *, tm):
    m, k = x.shape
    full = lambda a: pl.BlockSpec(a.shape, lambda i: (0,) * a.ndim)
    rows = lambda w: pl.BlockSpec((tm, w), lambda i: (i, 0))
    params = [g.reshape(1, k), w_gate, w_proj] + ([] if g_out is None else [g_out.reshape(1, k)])
    return pl.pallas_call(
        functools.partial(_ple_body, g_out is not None),
        grid=(m // tm,),
        in_specs=[rows(k), rows(PLE_DIM)] + [full(a) for a in params],
        out_specs=rows(k),
        out_shape=jax.ShapeDtypeStruct((m, k), F32),
        compiler_params=_cparams(("parallel",)),
        name="ple_update",
    )(x, p, *params)


FFN_SUB = 512


def _cast_rows_to_bf16(src_ref, dst_ref):
    def slab(n, carry):
        rows = pl.ds(pl.multiple_of(n * NORM_ROWS, NORM_ROWS), NORM_ROWS)
        dst_ref[rows, :] = src_ref[rows, :].astype(BF16)
        return carry

    lax.fori_loop(0, src_ref.shape[0] // NORM_ROWS, slab, 0)


def _ffn_up_long_body(tiles_per_seq, a_ref, wg_ref, wu_ref, st_ref, cw_ref, cb_ref, o_ref, st_out_ref,
                      wg_s, wu_s, ext):
    i = pl.program_id(1)
    tm = a_ref.shape[0]

    @pl.when(i == 0)
    def _():
        _cast_rows_to_bf16(wg_ref, wg_s)
        _cast_rows_to_bf16(wu_ref, wu_s)

    @pl.when(i % tiles_per_seq == 0)
    def _():
        ext[6:8, :] = st_ref[0]

    for r in range(0, tm, FFN_SUB):
        a = a_ref[r:r + FFN_SUB, :]
        ext[8 + r:8 + r + FFN_SUB, :] = _dot(a, wg_s[...])
        conv = (ext[6 + r:6 + r + FFN_SUB, :] * cw_ref[0:1, :] + ext[7 + r:7 + r + FFN_SUB, :] * cw_ref[1:2, :]
                + ext[8 + r:8 + r + FFN_SUB, :] * cw_ref[2:3, :])
        o_ref[r:r + FFN_SUB, :] = (_gelu(conv + cb_ref[...]) * _dot(a, wu_s[...])).astype(o_ref.dtype)

    last = ext[6 + tm:8 + tm, :]
    ext[6:8, :] = last
    st_out_ref[0] = last


def _ffn_up_group_body(seg, a_ref, wg_ref, wu_ref, st_ref, cw_ref, cb_ref, o_ref, st_out_ref, wg_s, wu_s, stp, fg_s):
    tm = a_ref.shape[0]
    nseq = tm // seg
    hist = FFN_CONV - 1

    @pl.when(pl.program_id(1) == 0)
    def _():
        _cast_rows_to_bf16(wg_ref, wg_s)
        _cast_rows_to_bf16(wu_ref, wu_s)

    lane_blocks = stp.shape[0]
    stp[...] = jnp.zeros_like(stp)
    for j in range(hist):
        for c in range(lane_blocks):
            stp[c, pl.ds(j, nseq, stride=seg), :] = st_ref[:, j, c * 128:(c + 1) * 128]

    a = a_ref[...]
    fg = _dot(a, wg_s[...])
    for c in range(lane_blocks):
        fg_s[c] = fg[:, c * 128:(c + 1) * 128]
    h = jnp.concatenate([stp[c] for c in range(lane_blocks)], axis=1)
    tpos = lax.broadcasted_iota(jnp.int32, (tm, 1), 0) & (seg - 1)
    lag1 = jnp.where(tpos < 1, pltpu.roll(h, tm - 1, axis=0), pltpu.roll(fg, 1, axis=0))
    lag2 = jnp.where(tpos < 2, h, pltpu.roll(fg, 2, axis=0))
    conv = lag2 * cw_ref[0:1, :] + lag1 * cw_ref[1:2, :] + fg * cw_ref[2:3, :]
    o_ref[...] = (_gelu(conv + cb_ref[...]) * _dot(a, wu_s[...])).astype(o_ref.dtype)
    for j in range(hist):
        for c in range(lane_blocks):
            st_out_ref[:, j, c * 128:(c + 1) * 128] = fg_s[c, pl.ds(seg - hist + j, nseq, stride=seg), :]


def _rms_bf16_body(x_ref, g_ref, o_ref):
    _norm_to_scratch(x_ref, g_ref, o_ref)


def rms_bf16(x, g, *, tm):
    m, k = x.shape
    return pl.pallas_call(
        _rms_bf16_body,
        grid=(m // tm,),
        in_specs=[pl.BlockSpec((tm, k), lambda i: (i, 0)), pl.BlockSpec((1, k), lambda i: (0, 0))],
        out_specs=pl.BlockSpec((tm, k), lambda i: (i, 0)),
        out_shape=jax.ShapeDtypeStruct((m, k), BF16),
        compiler_params=_cparams(("parallel",)),
        name="rms_bf16",
    )(x, g.reshape(1, k))


def ffn_up_gate(x, g, w_up_layers, layer, state, conv_w, conv_b, *, seq_len, tm, tn):
    m, k = x.shape
    bsz = m // seq_len
    ncol = FFN_DIM // tn
    a = rms_bf16(x, g, tm=min(tm, 512))
    common_in = [pl.BlockSpec((tm, k), lambda j, i: (i, 0)),
                 pl.BlockSpec((None, k, tn), lambda j, i: (layer, 0, j)),
                 pl.BlockSpec((None, k, tn), lambda j, i: (layer, 0, ncol + j))]
    conv_in = [pl.BlockSpec((FFN_CONV, tn), lambda j, i: (0, j)), pl.BlockSpec((1, tn), lambda j, i: (0, j))]
    act_spec = pl.BlockSpec((tm, tn), lambda j, i: (i, j))
    act_shape = jax.ShapeDtypeStruct((m, FFN_DIM), BF16)
    w_scr = [pltpu.VMEM((k, tn), BF16)] * 2
    if seq_len >= tm:
        assert seq_len % tm == 0 and tm % FFN_SUB == 0
        tps = seq_len // tm
        act, tile_last = pl.pallas_call(
            functools.partial(_ffn_up_long_body, tps),
            grid=(ncol, m // tm),
            in_specs=common_in + [pl.BlockSpec((1, FFN_CONV - 1, tn), lambda j, i: (i // tps, 0, j))] + conv_in,
            out_specs=[act_spec, pl.BlockSpec((1, FFN_CONV - 1, tn), lambda j, i: (i, 0, j))],
            out_shape=[act_shape, jax.ShapeDtypeStruct((m // tm, FFN_CONV - 1, FFN_DIM), F32)],
            scratch_shapes=w_scr + [pltpu.VMEM((tm + 8, tn), F32)],
            compiler_params=_cparams(("parallel", "arbitrary")),
            name="ffn_up_gate",
        )(a, w_up_layers, w_up_layers, state, conv_w, conv_b.reshape(1, FFN_DIM))
        return act, tile_last[tps - 1::tps]
    st_spec = pl.BlockSpec((tm // seq_len, FFN_CONV - 1, tn), lambda j, i: (i, 0, j))
    return pl.pallas_call(
        functools.partial(_ffn_up_group_body, seq_len),
        grid=(ncol, m // tm),
        in_specs=common_in + [st_spec] + conv_in,
        out_specs=[act_spec, st_spec],
        out_shape=[act_shape, jax.ShapeDtypeStruct((bsz, FFN_CONV - 1, FFN_DIM), F32)],
        scratch_shapes=w_scr + [pltpu.VMEM((tn // 128, tm, 128), F32)] * 2,
        compiler_params=_cparams(("parallel", "arbitrary")),
        name="ffn_up_gate_grouped",
    )(a, w_up_layers, w_up_layers, state, conv_w, conv_b.reshape(1, FFN_DIM))


def _mixer_a_body(u_ref, v_ref, lng_ref, lnb_ref, w_ref, bias_ref, o_ref, vout_ref):
    u = _gelu(u_ref[...])
    v = _gelu(v_ref[...])
    vc = v - jnp.mean(v, axis=-1, keepdims=True)
    var = jnp.mean(vc * vc, axis=-1, keepdims=True)
    vn = vc * lax.rsqrt(var + EPS) * lng_ref[...] + lnb_ref[...]
    vout_ref[...] = vn
    n = w_ref.shape[1]
    causal = lax.broadcasted_iota(jnp.int32, (n, n), 0) >= lax.broadcasted_iota(jnp.int32, (n, n), 1)
    gd = BRANCH_WIDTH // A_GROUPS
    for g in range(A_GROUPS):
        w = jnp.where(causal, w_ref[g], 0.0)
        mixed = _dot(w, vn[:, g * gd:(g + 1) * gd]) + bias_ref[:, g:g + 1]
        o_ref[:, g * gd:(g + 1) * gd] = (u[:, g * gd:(g + 1) * gd] * mixed).astype(o_ref.dtype)


def mixer_a(proj, ln_g, ln_b, w_mix, bias_rows):
    m = proj.shape[0]
    r = A_CHUNK
    bw = BRANCH_WIDTH
    return pl.pallas_call(
        _mixer_a_body,
        grid=(m // r,),
        in_specs=[pl.BlockSpec((r, bw), lambda i: (i, OFF_A_U // bw)),
                  pl.BlockSpec((r, bw), lambda i: (i, OFF_A_V // bw)),
                  pl.BlockSpec((1, bw), lambda i: (0, 0)),
                  pl.BlockSpec((1, bw), lambda i: (0, 0)),
                  pl.BlockSpec((A_GROUPS, r, r), lambda i: (0, 0, 0)),
                  pl.BlockSpec((r, A_GROUPS), lambda i: (0, 0))],
        out_specs=[pl.BlockSpec((r, bw), lambda i: (i, 0)), pl.BlockSpec((r, bw), lambda i: (i, 0))],
        out_shape=[jax.ShapeDtypeStruct((m, bw), BF16), jax.ShapeDtypeStruct((m, bw), F32)],
        compiler_params=_cparams(("parallel",)),
        name="mixer_a",
    )(proj, proj, ln_g.reshape(1, bw), ln_b.reshape(1, bw), w_mix, bias_rows)


def _gla_head_chunk(q, k, v, g):
    c = q.shape[0]
    g2 = _cumsum_rows(g) * LOG2E
    n_tiles = c // 8
    row8 = lax.broadcasted_iota(jnp.int32, (8, 1), 0)
    q_t = [q[i * 8:(i + 1) * 8] for i in range(n_tiles)]
    g_t = [g2[i * 8:(i + 1) * 8] for i in range(n_tiles)]
    o_t = [jnp.zeros((8, v.shape[1]), F32) for _ in range(n_tiles)]
    for s in range(c):
        for i in range(s // 8, n_tiles):
            a = jnp.sum(q_t[i] * jnp.exp2(g_t[i] - g2[s:s + 1, :]) * k[s:s + 1, :], axis=-1, keepdims=True)
            if i == s // 8:
                a = jnp.where(row8 >= s - 8 * i, a, 0.0)
            o_t[i] = o_t[i] + a * v[s:s + 1, :]
    o = jnp.concatenate(o_t, axis=0) if n_tiles > 1 else o_t[0]
    gl = g2[c - 1:c, :]
    return o, q * jnp.exp2(g2), k * jnp.exp2(gl - g2), jnp.exp2(gl)


def _block_diag(xs):
    z = jnp.zeros_like(xs[0])
    return jnp.concatenate(
        [jnp.concatenate([x if j == i else z for j in range(len(xs))], axis=1) for i, x in enumerate(xs)], axis=0)


def _gla_state_dots(qgs, kds, vs, es, st_ref, bi):
    st = st_ref[bi]
    o_inter = _dot_nt(_block_diag(qgs), st)
    st_ref[bi] = st * jnp.concatenate(es, axis=1) + _dot_tn(jnp.concatenate(vs, axis=0), _block_diag(kds))
    return o_inter


def _gla_state_step(parts, vs, st_ref, bi):
    c = vs[0].shape[0]
    o_inter = _gla_state_dots([p[1] for p in parts], [p[2] for p in parts], vs, [p[3] for p in parts], st_ref, bi)
    return [p[0] + o_inter[h * c:(h + 1) * c] for h, p in enumerate(parts)]


def _gla_pair_tile(q, k, v, g, seg):
    g2 = _segment_cumsum(g, seg) * LOG2E
    tpos = lax.broadcasted_iota(jnp.int32, (8, 1), 0) & (seg - 1)
    o = jnp.sum(q * k, axis=-1, keepdims=True) * v
    for d in range(1, seg):
        a = jnp.sum(q * jnp.exp2(g2 - pltpu.roll(g2, d, axis=0)) * pltpu.roll(k, d, axis=0), axis=-1, keepdims=True)
        o = o + jnp.where(tpos >= d, a, 0.0) * pltpu.roll(v, d, axis=0)
    first = lax.broadcasted_iota(jnp.int32, (8, 1), 0) < seg
    gl_a, gl_b = g2[seg - 1:seg, :], g2[2 * seg - 1:2 * seg, :]
    kd = k * jnp.exp2(jnp.where(first, gl_a, gl_b) - g2)
    return o, q * jnp.exp2(g2), kd, jnp.exp2(gl_a), jnp.exp2(gl_b)


def _lane_broadcast_column(row):
    hi = row.astype(BF16).astype(F32)
    r1 = row - hi
    mid = r1.astype(BF16).astype(F32)
    lo = r1 - mid
    r = lax.broadcasted_iota(jnp.int32, (8, 1), 0)
    pieces = jnp.where(r == 0, hi, jnp.where(r == 1, mid, jnp.where(r == 2, lo, 0.0)))
    return _dot_tn(pieces.astype(BF16), jnp.ones((8, 128), BF16))


def _gla_state_dots_natural(qgs, kds, vs, es, st_ref, bi):
    st = st_ref[bi]
    o_inter = _dot(_block_diag(qgs), st)
    decay = _lane_broadcast_column(jnp.concatenate(es, axis=1))
    st_ref[bi] = decay * st + _dot_tn(_block_diag(kds), jnp.concatenate(vs, axis=0))
    return o_inter


def _gla_pair_step(tiles, vs, st_ref, p, seg):
    first = lax.broadcasted_iota(jnp.int32, (8, 1), 0) < seg
    qgs = [t[1] for t in tiles]
    oi_a = _gla_state_dots_natural(qgs, [jnp.where(first, t[2], 0.0) for t in tiles], vs, [t[3] for t in tiles],
                                   st_ref, 2 * p)
    oi_b = _gla_state_dots_natural(qgs, [jnp.where(first, 0.0, t[2]) for t in tiles], vs, [t[4] for t in tiles],
                                   st_ref, 2 * p + 1)
    return [t[0] + jnp.where(first, oi_a[h * 8:(h + 1) * 8], oi_b[h * 8:(h + 1) * 8]) for h, t in enumerate(tiles)]


def _load_state_natural(s0_ref, st_ref, kdim):
    def per_seq(bi, carry):
        for h in range(HEADS):
            s = s0_ref[bi, h]
            if kdim < HEAD_DIM:
                s = jnp.concatenate([s, jnp.zeros((HEAD_DIM - kdim, HEAD_DIM), F32)], axis=0)
            st_ref[bi, h * HEAD_DIM:(h + 1) * HEAD_DIM, :] = s
        return carry

    lax.fori_loop(0, s0_ref.shape[0], per_seq, 0)


def _store_state_natural(st_ref, sout_ref, kdim):
    def per_seq(bi, carry):
        for h in range(HEADS):
            sout_ref[bi, h] = st_ref[bi, h * HEAD_DIM:h * HEAD_DIM + kdim, :]
        return carry

    lax.fori_loop(0, sout_ref.shape[0], per_seq, 0)


def _load_state_t(s0_ref, st_ref, kdim):
    def per_seq(bi, carry):
        for h in range(HEADS):
            s = s0_ref[bi, h]
            if kdim < HEAD_DIM:
                s = jnp.concatenate([s, jnp.zeros((HEAD_DIM - kdim, HEAD_DIM), F32)], axis=0)
            st_ref[bi, :, h * HEAD_DIM:(h + 1) * HEAD_DIM] = s.T
        return carry

    lax.fori_loop(0, s0_ref.shape[0], per_seq, 0)


def _store_state_t(st_ref, sout_ref, kdim):
    def per_seq(bi, carry):
        for h in range(HEADS):
            sout_ref[bi, h] = st_ref[bi, :, h * HEAD_DIM:(h + 1) * HEAD_DIM].T[:kdim, :]
        return carry

    lax.fori_loop(0, sout_ref.shape[0], per_seq, 0)


GLA_UNROLL = 8


def _gla_chunk_loop(tb, chunk):
    def step(n, carry):
        chunk(pl.ds(pl.multiple_of(n * GLA_CHUNK, GLA_CHUNK), GLA_CHUNK))
        return carry

    lax.fori_loop(0, tb // GLA_CHUNK, step, 0, unroll=GLA_UNROLL)


def _mixer_b_body(q_ref, f_ref, i_ref, og_ref, lb_ref, ng_ref, s0_ref, o_ref, sout_ref, st_ref):
    kd = HEAD_DIM

    @pl.when(pl.program_id(1) == 0)
    def _():
        _load_state_t(s0_ref, st_ref, kd)

    def chunk(rows):
        parts, vs = [], []
        for h in range(HEADS):
            cols = slice(h * kd, (h + 1) * kd)
            lb = lb_ref[:, cols]
            fp = f_ref[0, rows, cols]
            f = lb + (1.0 - lb) * jax.nn.sigmoid(fp)
            g = jnp.log(jnp.maximum(f, F_TINY))
            k = (1.0 - lb) * jax.nn.sigmoid(-fp)
            vs.append(i_ref[0, rows, cols])
            parts.append(_gla_head_chunk(q_ref[0, rows, cols], k, vs[h], g))
        for h, o in enumerate(_gla_state_step(parts, vs, st_ref, 0)):
            cols = slice(h * kd, (h + 1) * kd)
            o = _rms_rows(o) * ng_ref[...] * jax.nn.sigmoid(og_ref[0, rows, cols])
            o_ref[0, rows, cols] = o.astype(o_ref.dtype)

    _gla_chunk_loop(q_ref.shape[1], chunk)

    @pl.when(pl.program_id(1) == pl.num_programs(1) - 1)
    def _():
        _store_state_t(st_ref, sout_ref, kd)


def mixer_b(pseq, lb, norm_g, s0, *, tb):
    bsz, t, _ = pseq.shape
    bw = BRANCH_WIDTH
    field = lambda off: pl.BlockSpec((1, tb, bw), lambda b, s: (b, s, off // bw))
    state = pl.BlockSpec((1, HEADS, HEAD_DIM, HEAD_DIM), lambda b, s: (b, 0, 0, 0))
    return pl.pallas_call(
        _mixer_b_body,
        grid=(bsz, t // tb),
        in_specs=[field(OFF_B_Q), field(OFF_B_F), field(OFF_B_I), field(OFF_B_G),
                  pl.BlockSpec((1, bw), lambda b, s: (0, 0)),
                  pl.BlockSpec((1, HEAD_DIM), lambda b, s: (0, 0)),
                  state],
        out_specs=[pl.BlockSpec((1, tb, bw), lambda b, s: (b, s, 0)), state],
        out_shape=[jax.ShapeDtypeStruct((bsz, t, bw), BF16),
                   jax.ShapeDtypeStruct((bsz, HEADS, HEAD_DIM, HEAD_DIM), F32)],
        scratch_shapes=[pltpu.VMEM((1, HEAD_DIM, HEADS * HEAD_DIM), F32)],
        compiler_params=_cparams(("parallel", "arbitrary")),
        name="mixer_b",
    )(pseq, pseq, pseq, pseq, lb.reshape(1, bw), norm_g.reshape(1, HEAD_DIM), s0)


def _mixer_c_body(q_ref, k_ref, v_ref, r_ref, lr_ref, w2_ref, ba_ref, ng_ref, s0_ref, o_ref, sout_ref, st_ref):
    kd = C_KDIM

    @pl.when(pl.program_id(1) == 0)
    def _():
        _load_state_t(s0_ref, st_ref, kd)

    zpad = jnp.zeros((GLA_CHUNK, HEAD_DIM - kd), F32)

    def chunk(rows):
        gate_in = _dot(lr_ref[0, rows, 0:C_RANK], w2_ref[...]) + ba_ref[...]
        g_all = _log_sigmoid(gate_in) / C_TAU
        parts, vs = [], []
        for h in range(HEADS):
            kcols = slice(h * kd, (h + 1) * kd)
            q = jnp.concatenate([q_ref[0, rows, kcols] * (kd ** -0.5), zpad], axis=1)
            k = jnp.concatenate([k_ref[0, rows, kcols], zpad], axis=1)
            g = jnp.concatenate([g_all[:, kcols], zpad], axis=1)
            vs.append(v_ref[0, rows, h * HEAD_DIM:(h + 1) * HEAD_DIM])
            parts.append(_gla_head_chunk(q, k, vs[h], g))
        for h, o in enumerate(_gla_state_step(parts, vs, st_ref, 0)):
            vcols = slice(h * HEAD_DIM, (h + 1) * HEAD_DIM)
            o = _rms_rows(o) * ng_ref[...] * _silu(r_ref[0, rows, vcols])
            o_ref[0, rows, vcols] = o.astype(o_ref.dtype)

    _gla_chunk_loop(q_ref.shape[1], chunk)

    @pl.when(pl.program_id(1) == pl.num_programs(1) - 1)
    def _():
        _store_state_t(st_ref, sout_ref, kd)


def mixer_c(pseq, w_a2, b_a, norm_g, s0, *, tb):
    bsz, t, _ = pseq.shape
    bw = BRANCH_WIDTH
    kw = HEADS * C_KDIM
    field = lambda off, w: pl.BlockSpec((1, tb, w), lambda b, s: (b, s, off // w))
    state = pl.BlockSpec((1, HEADS, C_KDIM, HEAD_DIM), lambda b, s: (b, 0, 0, 0))
    return pl.pallas_call(
        _mixer_c_body,
        grid=(bsz, t // tb),
        in_specs=[field(OFF_C_Q, kw), field(OFF_C_K, kw), field(OFF_C_V, bw), field(OFF_C_R, bw),
                  field(OFF_C_LR, 128),
                  pl.BlockSpec((C_RANK, kw), lambda b, s: (0, 0)),
                  pl.BlockSpec((1, kw), lambda b, s: (0, 0)),
                  pl.BlockSpec((1, HEAD_DIM), lambda b, s: (0, 0)),
                  state],
        out_specs=[pl.BlockSpec((1, tb, bw), lambda b, s: (b, s, 0)), state],
        out_shape=[jax.ShapeDtypeStruct((bsz, t, bw), BF16),
                   jax.ShapeDtypeStruct((bsz, HEADS, C_KDIM, HEAD_DIM), F32)],
        scratch_shapes=[pltpu.VMEM((1, HEAD_DIM, HEADS * HEAD_DIM), F32)],
        compiler_params=_cparams(("parallel", "arbitrary")),
        name="mixer_c",
    )(pseq, pseq, pseq, pseq, pseq, w_a2, b_a.reshape(1, kw), norm_g.reshape(1, HEAD_DIM), s0)


def _grouped_pairs_loop(n_rows, seg, st_ref, head_inputs, head_outputs):
    def step(it, carry):
        rows = pl.ds(pl.multiple_of(it * 16, 16), 16)
        ins = [head_inputs(rows, h) for h in range(HEADS)]
        halves = []
        for half in range(2):
            sl = slice(half * 8, (half + 1) * 8)
            vs = [x[2][sl] for x in ins]
            tiles = [_gla_pair_tile(x[0][sl], x[1][sl], x[2][sl], x[3][sl], seg) for x in ins]
            halves.append(_gla_pair_step(tiles, vs, st_ref, 2 * it + half, seg))
        for h in range(HEADS):
            head_outputs(rows, h, jnp.concatenate([halves[0][h], halves[1][h]], axis=0))
        return carry

    lax.fori_loop(0, n_rows // 16, step, 0, unroll=2)


def _mixer_b_group_body(seg, q_ref, f_ref, i_ref, og_ref, lb_ref, ng_ref, s0_ref, o_ref, sout_ref, st_ref):
    kd = HEAD_DIM
    _load_state_natural(s0_ref, st_ref, kd)

    def head_inputs(rows, h):
        cols = slice(h * kd, (h + 1) * kd)
        lb = lb_ref[:, cols]
        fp = f_ref[0, rows, cols]
        f = lb + (1.0 - lb) * jax.nn.sigmoid(fp)
        return (q_ref[0, rows, cols], (1.0 - lb) * jax.nn.sigmoid(-fp), i_ref[0, rows, cols],
                jnp.log(jnp.maximum(f, F_TINY)))

    def head_outputs(rows, h, o):
        cols = slice(h * kd, (h + 1) * kd)
        o = _rms_rows(o) * ng_ref[...] * jax.nn.sigmoid(og_ref[0, rows, cols])
        o_ref[0, rows, cols] = o.astype(o_ref.dtype)

    _grouped_pairs_loop(q_ref.shape[1], seg, st_ref, head_inputs, head_outputs)
    _store_state_natural(st_ref, sout_ref, kd)


def _mixer_c_group_body(seg, q_ref, k_ref, v_ref, r_ref, lr_ref, w2_ref, ba_ref, ng_ref, s0_ref, o_ref, sout_ref,
                        st_ref, g_scr):
    kd = C_KDIM
    _load_state_natural(s0_ref, st_ref, kd)
    g_scr[...] = _log_sigmoid(_dot(lr_ref[0, :, 0:C_RANK], w2_ref[...]) + ba_ref[...]) / C_TAU
    zpad = jnp.zeros((16, HEAD_DIM - kd), F32)

    def head_inputs(rows, h):
        kcols = slice(h * kd, (h + 1) * kd)
        pad = lambda x: jnp.concatenate([x, zpad], axis=1)
        return (pad(q_ref[0, rows, kcols] * (kd ** -0.5)), pad(k_ref[0, rows, kcols]),
                v_ref[0, rows, h * HEAD_DIM:(h + 1) * HEAD_DIM], pad(g_scr[rows, kcols]))

    def head_outputs(rows, h, o):
        vcols = slice(h * HEAD_DIM, (h + 1) * HEAD_DIM)
        o = _rms_rows(o) * ng_ref[...] * _silu(r_ref[0, rows, vcols])
        o_ref[0, rows, vcols] = o.astype(o_ref.dtype)

    _grouped_pairs_loop(q_ref.shape[1], seg, st_ref, head_inputs, head_outputs)
    _store_state_natural(st_ref, sout_ref, kd)


def _layer_stacked_state(body, prev_states, state_ref_pos, seqs_per_step):
    n_prev = 0 if prev_states is None else prev_states.shape[0]

    def wrapped(*refs):
        refs = list(refs)
        if n_prev:
            prev_ref = refs.pop(0)
            refs[state_ref_pos][0:n_prev] = prev_ref[...]
        refs[state_ref_pos] = refs[state_ref_pos].at[n_prev]
        return body(*refs)

    def spec(shape_tail, layers):
        return pl.BlockSpec((layers, seqs_per_step) + shape_tail, lambda s: (0, s) + (0,) * len(shape_tail))

    def out_shape(s0):
        return jax.ShapeDtypeStruct((n_prev + 1,) + s0.shape[1:], F32)

    prev_specs = lambda s0: [spec(s0.shape[2:], n_prev)] if n_prev else []
    prev_inputs = [prev_states] if n_prev else []
    return wrapped, prev_specs, prev_inputs, lambda s0: spec(s0.shape[2:], n_prev + 1), out_shape


def _grouped_gla_call(body, prows, fields, params, s0, prev_states, kdim, *, seg, tb, extra_scratch=()):
    m = prows.shape[0]
    bw = BRANCH_WIDTH
    p3 = prows.reshape(1, m, -1)
    layer = 0 if prev_states is None else prev_states.shape[0]
    state = pl.BlockSpec((None, tb // seg, HEADS, kdim, HEAD_DIM), lambda s: (layer, s, 0, 0, 0))
    n_in = len(fields) + len(params) + 1
    wrapped, prev_specs, prev_inputs, state_out_spec, state_out_shape = _layer_stacked_state(
        functools.partial(body, seg), prev_states, n_in + 1, tb // seg)
    o, s_out = pl.pallas_call(
        wrapped,
        grid=(m // tb,),
        in_specs=prev_specs(s0)
        + [pl.BlockSpec((1, tb, w), functools.partial(lambda s, c: (0, s, c), c=off // w)) for off, w in fields]
        + [pl.BlockSpec(x.shape, functools.partial(lambda s, n: (0,) * n, n=x.ndim)) for x in params] + [state],
        out_specs=[pl.BlockSpec((1, tb, bw), lambda s: (0, s, 0)), state_out_spec(s0)],
        out_shape=[jax.ShapeDtypeStruct((1, m, bw), BF16), state_out_shape(s0)],
        scratch_shapes=[pltpu.VMEM((tb // seg, HEADS * HEAD_DIM, HEAD_DIM), F32), *extra_scratch],
        compiler_params=_cparams(("parallel",)),
        name=body.__name__.strip("_"),
    )(*prev_inputs, *([p3] * len(fields)), *params, s0)
    return o.reshape(m, bw), s_out


def mixer_b_grouped(prows, lb, norm_g, s0, prev_states, *, seg, tb):
    bw = BRANCH_WIDTH
    return _grouped_gla_call(
        _mixer_b_group_body, prows, [(OFF_B_Q, bw), (OFF_B_F, bw), (OFF_B_I, bw), (OFF_B_G, bw)],
        [lb.reshape(1, bw), norm_g.reshape(1, HEAD_DIM)], s0, prev_states, HEAD_DIM, seg=seg, tb=tb)


def mixer_c_grouped(prows, w_a2, b_a, norm_g, s0, prev_states, *, seg, tb):
    bw = BRANCH_WIDTH
    kw = HEADS * C_KDIM
    return _grouped_gla_call(
        _mixer_c_group_body, prows, [(OFF_C_Q, kw), (OFF_C_K, kw), (OFF_C_V, bw), (OFF_C_R, bw), (OFF_C_LR, 128)],
        [w_a2, b_a.reshape(1, kw), norm_g.reshape(1, HEAD_DIM)], s0, prev_states, C_KDIM, seg=seg, tb=tb,
        extra_scratch=(pltpu.VMEM((tb, kw), F32),))


def _unit_lower_inverse(a, order):
    c = a.shape[0]
    eye = (lax.broadcasted_iota(jnp.int32, (c, c), 0) == lax.broadcasted_iota(jnp.int32, (c, c), 1)).astype(F32)
    p = eye - a
    pw = a
    n = 2
    pw_b = pw.astype(BF16)
    while n < order:
        pw_b = _dot(pw_b, pw_b).astype(BF16)
        p = p + _dot(p.astype(BF16), pw_b)
        n *= 2
    return p


D_ROWS = 64


def _stack_heads(x):
    return jnp.concatenate([x[:, h * HEAD_DIM:(h + 1) * HEAD_DIM] for h in range(HEADS)], axis=0)


def _stack_head_lanes(x):
    return jnp.concatenate([x[:, h:h + 1] for h in range(HEADS)], axis=0)


def _lane_pad_heads(x):
    return jnp.pad(x.reshape(1, HEADS), ((0, 0), (0, 128 - HEADS)))


def _segment_cumsum(x, seg):
    tpos = lax.broadcasted_iota(jnp.int32, x.shape, 0) & (seg - 1)
    sh = 1
    while sh < seg:
        x = x + jnp.where(tpos >= sh, pltpu.roll(x, sh, axis=0), 0.0)
        sh *= 2
    return x


def _delta_chunk_operands(seg, qkv, beta_all, g_all):
    hd = HEAD_DIM
    bw = BRANCH_WIDTH
    r = HEADS * qkv.shape[0]
    q = _stack_heads(qkv[:, 0:bw])
    k = _stack_heads(qkv[:, bw:2 * bw])
    v = _stack_heads(qkv[:, 2 * bw:3 * bw])
    q = q * lax.rsqrt(jnp.sum(q * q, axis=-1, keepdims=True) + EPS) * (hd ** -0.5)
    k = k * lax.rsqrt(jnp.sum(k * k, axis=-1, keepdims=True) + EPS)
    beta = _stack_head_lanes(beta_all)
    gc = _stack_head_lanes(_segment_cumsum(g_all, seg))
    ri = lax.broadcasted_iota(jnp.int32, (r, r), 0)
    ci = lax.broadcasted_iota(jnp.int32, (r, r), 1)
    shift = seg.bit_length() - 1
    same = (ri >> shift) == (ci >> shift)
    gr = jnp.sum(jnp.where(ri == ci, gc, 0.0), axis=0, keepdims=True)
    decay = jnp.where(same, jnp.where(ri >= ci, jnp.exp(jnp.minimum(gc - gr, 0.0)), 0.0), 0.0)
    k_b = k.astype(BF16)
    a_mat = jnp.where(ri > ci, beta * decay * _dot_nt(k_b, k_b), 0.0)
    t_inv = _unit_lower_inverse(a_mat, seg)
    eg = jnp.exp(gc)
    sol = _dot(t_inv.astype(BF16), jnp.concatenate([(beta * eg) * k, beta * v], axis=1).astype(BF16))
    qk = _dot_nt(q.astype(BF16), k_b) * decay
    is_last = (ci & (seg - 1)) == seg - 1
    gl = jnp.sum(jnp.where(same, jnp.where(is_last, gr, 0.0), 0.0), axis=1, keepdims=True)
    return sol[:, :hd], sol[:, hd:], qk, q * eg, k * jnp.exp(gl - gc), jnp.exp(gl)


def _mixer_d_body(x_ref, beta_ref, dec_ref, z_ref, cst_ref, s0_ref, cw_ref, alog_ref, dtb_ref, ng_ref,
                  o_ref, cst_out_ref, sout_ref, ext, conv, st_ref):
    nb, tb = x_ref.shape[0], x_ref.shape[1]
    c = D_ROWS
    hd = HEAD_DIM
    t = pl.program_id(1)
    nt = pl.num_programs(1)

    @pl.when(t == 0)
    def _():
        ext[:, 5:8, :] = cst_ref[...]
        st_ref[...] = s0_ref[...].reshape(nb * HEADS, hd, hd)

    for bi in range(nb):
        ext[bi, 8:8 + tb, :] = x_ref[bi]
        y = ext[bi, 5:5 + tb, :] * cw_ref[0:1, :]
        for j in range(1, D_CONV):
            y = y + ext[bi, 5 + j:5 + j + tb, :] * cw_ref[j:j + 1, :]
        conv[bi] = _silu(y)

    @pl.when(t == nt - 1)
    def _():
        cst_out_ref[...] = ext[:, 8 + tb - (D_CONV - 1):8 + tb, :]

    ext[:, 5:8, :] = ext[:, 5 + tb:8 + tb, :]

    a_neg = -jnp.exp(alog_ref[...])

    def chunk(n, carry):
        rows = pl.ds(pl.multiple_of(n * c, c), c)
        for bi in range(nb):
            beta_all = jax.nn.sigmoid(beta_ref[bi, rows, :])
            g_all = a_neg * _softplus(dec_ref[bi, rows, :] + dtb_ref[...])
            w, u, qk, qg, kdec, egl = _delta_chunk_operands(c, conv[bi, rows, :], beta_all, g_all)
            deltas, oqs = [], []
            for h in range(HEADS):
                hs = slice(h * c, (h + 1) * c)
                st = st_ref[bi * HEADS + h]
                x = _dot(jnp.concatenate([w[hs], qg[hs]], axis=0).astype(BF16), st.astype(BF16))
                delta = u[hs] - x[:c]
                deltas.append(delta)
                oqs.append(x[c:])
                st_ref[bi * HEADS + h] = (egl[(h + 1) * c - 1:(h + 1) * c, :] * st
                                          + _dot_tn(kdec[hs].astype(BF16), delta.astype(BF16)))
            o = jnp.concatenate(oqs, axis=0) + _dot(qk.astype(BF16), jnp.concatenate(deltas, axis=0).astype(BF16))
            o = _rms_rows(o) * ng_ref[...] * _silu(_stack_heads(z_ref[bi, rows, :]))
            for h in range(HEADS):
                o_ref[bi, rows, h * hd:(h + 1) * hd] = o[h * c:(h + 1) * c].astype(o_ref.dtype)
        return carry

    lax.fori_loop(0, tb // c, chunk, 0, unroll=2)

    @pl.when(t == nt - 1)
    def _():
        sout_ref[...] = st_ref[...].reshape(nb, HEADS, hd, hd)


def _mixer_d_group_body(seg, x_ref, cst_ref, beta_ref, dec_ref, z_ref, s0_ref, cw_ref, alog_ref, dtb_ref, ng_ref,
                        o_ref, cst_out_ref, sout_ref, conv, w_s, u_s, qg_s, kd_s, egl_s, delta_s, oq_s, stp_s, x_s):
    tb = x_ref.shape[1]
    nseq = tb // seg
    c = D_ROWS
    hd = HEAD_DIM
    pairs = c // 8
    hist_rows = D_CONV - 1

    lane_blocks = stp_s.shape[0]
    x = x_ref[0]
    stp_s[...] = jnp.zeros_like(stp_s)
    for cb in range(lane_blocks):
        lanes = slice(cb * 128, (cb + 1) * 128)
        x_s[cb] = x[:, lanes]
        for j in range(hist_rows):
            stp_s[cb, pl.ds(j, nseq, stride=seg), :] = cst_ref[:, j, lanes]
    stp = jnp.concatenate([stp_s[cb] for cb in range(lane_blocks)], axis=1)

    tpos = lax.broadcasted_iota(jnp.int32, (tb, 1), 0) & (seg - 1)
    y = x * cw_ref[3:4, :]
    for d in range(1, D_CONV):
        hist = stp if d == 3 else pltpu.roll(stp, tb - (3 - d), axis=0)
        y = y + jnp.where(tpos < d, hist, pltpu.roll(x, d, axis=0)) * cw_ref[3 - d:4 - d, :]
    conv[...] = _silu(y)
    for cb in range(lane_blocks):
        for j in range(hist_rows):
            cst_out_ref[:, j, cb * 128:(cb + 1) * 128] = x_s[cb, pl.ds(seg - hist_rows + j, nseq, stride=seg), :]

    a_neg = -jnp.exp(alog_ref[...])
    low = lax.broadcasted_iota(jnp.int32, (8, 1), 0) < seg

    for n in range(tb // c):
        rows = slice(n * c, (n + 1) * c)
        beta_all = jax.nn.sigmoid(beta_ref[0, rows, :])
        g_all = a_neg * _softplus(dec_ref[0, rows, :] + dtb_ref[...])
        w, u, qk, qg, kdec, egl = _delta_chunk_operands(seg, conv[rows, :], beta_all, g_all)
        w_s[...] = w
        u_s[...] = u
        qg_s[...] = qg
        kd_s[...] = kdec
        egl_s[...] = jnp.broadcast_to(egl, (HEADS * c, hd))

        def pair(p, carry):
            rp = pl.ds(pl.multiple_of(p * 8, 8), 8)
            h = p // pairs
            ja = n * (c // seg) + (p % pairs) * 2
            sa = s0_ref[ja, h]
            sb = s0_ref[ja + 1, h]
            lhs = jnp.concatenate([w_s[rp, :], qg_s[rp, :]], axis=0)
            xa = _dot(lhs, sa)
            xb = _dot(lhs, sb)
            delta = u_s[rp, :] - jnp.where(low, xa[:8], xb[:8])
            delta_s[rp, :] = delta
            oq_s[rp, :] = jnp.where(low, xa[8:], xb[8:])
            kd = kd_s[rp, :]
            e = egl_s[rp, :]
            sout_ref[ja, h] = e[seg - 1:seg, :] * sa + _dot_tn(jnp.where(low, kd, 0.0), delta)
            sout_ref[ja + 1, h] = e[2 * seg - 1:2 * seg, :] * sb + _dot_tn(jnp.where(low, 0.0, kd), delta)
            return carry

        lax.fori_loop(0, HEADS * pairs, pair, 0, unroll=4)
        o = oq_s[...] + _dot(qk, delta_s[...])
        o = _rms_rows(o) * ng_ref[...] * _silu(_stack_heads(z_ref[0, rows, :]))
        for h in range(HEADS):
            o_ref[0, rows, h * hd:(h + 1) * hd] = o[h * c:(h + 1) * c].astype(o_ref.dtype)


def mixer_d(pseq, conv_state, s0, conv_w, a_log, dt_bias, norm_g, *, tb, nb):
    bsz, t, _ = pseq.shape
    bw = BRANCH_WIDTH
    field = lambda off, w: pl.BlockSpec((nb, tb, w), lambda b, s: (b, s, off // w))
    state = pl.BlockSpec((nb, HEADS, HEAD_DIM, HEAD_DIM), lambda b, s: (b, 0, 0, 0))
    cstate = pl.BlockSpec((nb, D_CONV - 1, D_QKV), lambda b, s: (b, 0, 0))
    return pl.pallas_call(
        _mixer_d_body,
        grid=(bsz // nb, t // tb),
        in_specs=[field(OFF_D_QKV, D_QKV), field(OFF_D_BETA, 128), field(OFF_D_DECAY, 128), field(OFF_D_Z, bw),
                  cstate, state,
                  pl.BlockSpec((D_CONV, D_QKV), lambda b, s: (0, 0)),
                  pl.BlockSpec((1, 128), lambda b, s: (0, 0)),
                  pl.BlockSpec((1, 128), lambda b, s: (0, 0)),
                  pl.BlockSpec((1, HEAD_DIM), lambda b, s: (0, 0))],
        out_specs=[pl.BlockSpec((nb, tb, bw), lambda b, s: (b, s, 0)), cstate, state],
        out_shape=[jax.ShapeDtypeStruct((bsz, t, bw), BF16),
                   jax.ShapeDtypeStruct((bsz, D_CONV - 1, D_QKV), F32),
                   jax.ShapeDtypeStruct((bsz, HEADS, HEAD_DIM, HEAD_DIM), F32)],
        scratch_shapes=[pltpu.VMEM((nb, tb + 8, D_QKV), F32), pltpu.VMEM((nb, tb, D_QKV), F32),
                        pltpu.VMEM((nb * HEADS, HEAD_DIM, HEAD_DIM), F32)],
        compiler_params=_cparams(("parallel", "arbitrary")),
        name="mixer_d",
    )(pseq, pseq, pseq, pseq, conv_state, s0, conv_w, _lane_pad_heads(a_log), _lane_pad_heads(dt_bias),
      norm_g.reshape(1, HEAD_DIM))


def mixer_d_grouped(prows, conv_state, s0, conv_w, a_log, dt_bias, norm_g, prev_states, *, seg, tb):
    m = prows.shape[0]
    bsz = m // seg
    bw = BRANCH_WIDTH
    field = lambda off, w: pl.BlockSpec((1, tb, w), lambda s: (0, s, off // w))
    layer = 0 if prev_states is None else prev_states.shape[0]
    state = pl.BlockSpec((None, tb // seg, HEADS, HEAD_DIM, HEAD_DIM), lambda s: (layer, s, 0, 0, 0))
    cstate = pl.BlockSpec((tb // seg, D_CONV - 1, D_QKV), lambda s: (s, 0, 0))
    rows128 = pltpu.VMEM((HEADS * D_ROWS, HEAD_DIM), F32)
    lane_rows = pltpu.VMEM((D_QKV // 128, tb, 128), F32)
    n_in, state_out_pos = 10, 2
    wrapped, prev_specs, prev_inputs, state_out_spec, state_out_shape = _layer_stacked_state(
        functools.partial(_mixer_d_group_body, seg), prev_states, n_in + state_out_pos, tb // seg)
    o, cst, s_out = pl.pallas_call(
        wrapped,
        grid=(m // tb,),
        in_specs=prev_specs(s0) + [
                  field(OFF_D_QKV, D_QKV), cstate,
                  field(OFF_D_BETA, 128), field(OFF_D_DECAY, 128), field(OFF_D_Z, bw),
                  state,
                  pl.BlockSpec((D_CONV, D_QKV), lambda s: (0, 0)),
                  pl.BlockSpec((1, 128), lambda s: (0, 0)),
                  pl.BlockSpec((1, 128), lambda s: (0, 0)),
                  pl.BlockSpec((1, HEAD_DIM), lambda s: (0, 0))],
        out_specs=[pl.BlockSpec((1, tb, bw), lambda s: (0, s, 0)), cstate, state_out_spec(s0)],
        out_shape=[jax.ShapeDtypeStruct((1, m, bw), BF16),
                   jax.ShapeDtypeStruct((bsz, D_CONV - 1, D_QKV), F32),
                   state_out_shape(s0)],
        scratch_shapes=[pltpu.VMEM((tb, D_QKV), F32)] + [rows128] * 7 + [lane_rows] * 2,
        compiler_params=_cparams(("parallel",)),
        name="mixer_d_grouped",
    )(*prev_inputs, prows.reshape(1, m, -1), conv_state, prows.reshape(1, m, -1), prows.reshape(1, m, -1),
      prows.reshape(1, m, -1), s0, conv_w, _lane_pad_heads(a_log), _lane_pad_heads(dt_bias),
      norm_g.reshape(1, HEAD_DIM))
    return o.reshape(m, bw), cst, s_out


PACK_TILE = 512
PACK_SRC = ([4624, 5136, 5648, 0, 512, 1024, 1536, 2048, 2560, 3584, 4112, 6168, 3072, None]
            + [6680 + PACK_TILE * k for k in range(N_BRANCH * D_MODEL // PACK_TILE)])
PACK_SMALL_TILE = PACK_SRC.index(None)


def _pack_lookup(t, values):
    out = jnp.int32(0)
    for i, v in enumerate(values):
        out = jnp.where(t == i, jnp.int32(v), out)
    return out


PACK_EXTRA = 32


def _pack_w_in_body(a_ref, b_ref, tail_ref, small_ref, o_ref):
    t = pl.program_id(1)
    last = len(PACK_SRC) - 1
    shifts = [0 if s is None else s % PACK_TILE for s in PACK_SRC]
    assert max(shifts) <= PACK_EXTRA and all(s % 8 == 0 for s in shifts)

    def emit(shift, left_ref, right_ref):
        for c in range(a_ref.shape[1] // 128):
            lanes = slice(c * 128, (c + 1) * 128)
            w = left_ref[shift:, lanes]
            if shift:
                w = jnp.concatenate([w, right_ref[0:shift, lanes]], axis=0)
            o_ref[lanes, :] = w.T.astype(BF16)

    for shift in sorted(set(shifts)):
        tiles = [i for i, s in enumerate(shifts) if s == shift and i not in (PACK_SMALL_TILE, last)]
        pl.when(functools.reduce(jnp.logical_or, [t == i for i in tiles]))(
            functools.partial(emit, shift, a_ref, b_ref))
    pl.when(t == last)(functools.partial(emit, shifts[last], a_ref, tail_ref))
    pl.when(t == PACK_SMALL_TILE)(functools.partial(emit, 0, small_ref, None))


def pack_w_in(w_in):
    layers, k, n_src = w_in.shape
    w_t = jnp.swapaxes(w_in, 1, 2)
    bases = [0 if s is None else s - s % PACK_TILE for s in PACK_SRC]
    extra = [(b + PACK_TILE) // PACK_EXTRA for b in bases]
    full_blocks = n_src // PACK_EXTRA
    assert all(e < full_blocks for e in extra[:-1]) and extra[-1] == full_blocks
    extra[-1] = 0
    z = lambda n: jnp.zeros((layers, n, k), w_in.dtype)
    tail = jnp.concatenate([w_t[:, full_blocks * PACK_EXTRA:], z(PACK_EXTRA - n_src % PACK_EXTRA)], axis=1)
    small = jnp.concatenate([w_t[:, 4096:4112], z(112),
                             w_t[:, 6160:6164], z(124),
                             w_t[:, 6164:6168], z(252)], axis=1)
    return pl.pallas_call(
        _pack_w_in_body,
        grid=(layers, len(PACK_SRC)),
        in_specs=[pl.BlockSpec((None, PACK_TILE, k),
                               lambda l, t: (l, _pack_lookup(t, [b // PACK_TILE for b in bases]), 0)),
                  pl.BlockSpec((None, PACK_EXTRA, k), lambda l, t: (l, _pack_lookup(t, extra), 0)),
                  pl.BlockSpec((None, PACK_EXTRA, k), lambda l, t: (l, 0, 0)),
                  pl.BlockSpec((None, PACK_TILE, k), lambda l, t: (l, 0, 0))],
        out_specs=pl.BlockSpec((None, k, PACK_TILE), lambda l, t: (l, 0, t)),
        out_shape=jax.ShapeDtypeStruct((layers, k, N_PROJ), BF16),
        compiler_params=_cparams(("parallel", "arbitrary")),
        name="pack_w_in",
    )(w_t, w_t, tail, small)


def _trunk(x, p, s_hgrn, s_gla, s_delta, c_dconv, c_fconv, wts, lbs, *, sample):
    bsz, t, _ = x.shape
    m = bsz * t
    tm = m if sample else 1024
    h = x.reshape(m, D_MODEL)
    outs = {k: [] for k in ("hgrn", "gla", "delta", "dconv", "fconv", "v")}
    s_b = s_c = s_d = None
    for l in range(DEPTH):
        w = wts[l]
        proj, gates = in_proj(h, w["g_mix"], w["w_in_layers"], l, tm=tm, tn=1024)
        o_a, v_a = mixer_a(proj, w["a_ln_g"], w["a_ln_b"], w["a_w_mix"], w["a_bias_rows"])
        st_b, st_c, st_d = (s_hgrn, s_gla, s_delta) if sample else (s_hgrn[l], s_gla[l], s_delta[l])
        b_args = (lbs[l], w["b_norm_g"], st_b)
        c_args = (w["c_w_a2"], w["c_b_a"], w["c_norm_g"], st_c)
        d_args = (c_dconv[l], st_d, w["d_conv_w"], w["d_a_log"], w["d_dt_bias"], w["d_norm_g"])
        if sample:
            o_b, s_b = mixer_b_grouped(proj, *b_args, s_b, seg=t, tb=D_ROWS)
            o_c, s_c = mixer_c_grouped(proj, *c_args, s_c, seg=t, tb=D_ROWS)
            o_d, nb_d, s_d = mixer_d_grouped(proj, *d_args, s_d, seg=t, tb=D_ROWS)
        else:
            pseq = proj.reshape(bsz, t, N_SEQ_COLS)
            flat = lambda o: o.reshape(m, BRANCH_WIDTH)
            o_b, s_b = mixer_b(pseq, *b_args, tb=256)
            o_c, s_c = mixer_c(pseq, *c_args, tb=256)
            o_d, nb_d, s_d = mixer_d(pseq, *d_args, tb=256, nb=2 if bsz % 2 == 0 else 1)
            o_b, o_c, o_d = flat(o_b), flat(o_c), flat(o_d)
        merged = merge_branches((o_a, o_b, o_c, o_d), gates, w["w_branch"], tm=tm, tn=1024)
        h = matmul_residual(merged, w["w_out_layers"], l, h, tm=tm, tn=1024)

        act, nb_f = ffn_up_gate(h, w["g_ffn"], w["w_ffn_up_layers"], l, c_fconv[l], w["ffn_conv_w"],
                                w["ffn_conv_b"], seq_len=t, tm=tm, tn=512)
        h = matmul_residual(act, w["w_ffn_down_layers"], l, h, tm=min(tm, 512), tn=512)

        h = ple_update(h, p[l].reshape(m, PLE_DIM), w["g_ple"], w["w_ple_gate"], w["w_ple_proj"],
                       w["g_final"] if l == DEPTH - 1 else None, tm=min(tm, 512))

        if not sample:
            outs["hgrn"].append(s_b)
            outs["gla"].append(s_c)
            outs["delta"].append(s_d)
        outs["dconv"].append(nb_d)
        outs["fconv"].append(nb_f)
        outs["v"].append(v_a.reshape(bsz, t, BRANCH_WIDTH))
    y = h.reshape(bsz, t, D_MODEL)
    states = (s_b, s_c, s_d) if sample else tuple(jnp.stack(outs[k]) for k in ("hgrn", "gla", "delta"))
    return (y,) + states + tuple(jnp.stack(outs[k]) for k in ("dconv", "fconv", "v"))


def kernel(x_prompt, x_sample, state_hgrn, state_gla, state_delta, state_delta_conv, state_ffn_conv, p_prompt, p_sample, g_mix, w_in, a_ln_g, a_ln_b, a_w_s, a_b_s, b_lb, b_norm_g, c_w_a2, c_b_a, c_norm_g, d_conv_w, d_a_log, d_dt_bias, d_norm_g, w_branch, w_out, g_ffn, w_ffn_up, ffn_conv_w, ffn_conv_b, w_ffn_down, g_ple, w_ple_gate, w_ple_proj, g_final):
    bp, t_p, _ = x_prompt.shape
    bs, t_s, _ = x_sample.shape
    sm = jax.nn.softmax(b_lb.astype(F32), axis=0)
    lbs = jnp.cumsum(sm, axis=0) - sm[0]

    w_in_packed = pack_w_in(w_in)
    shared = [dict(
        g_mix=g_mix[l], w_in_layers=w_in_packed, a_ln_g=a_ln_g[l], a_ln_b=a_ln_b[l],
        b_norm_g=b_norm_g[l], c_w_a2=c_w_a2[l], c_b_a=c_b_a[l], c_norm_g=c_norm_g[l],
        d_conv_w=d_conv_w[l], d_a_log=d_a_log[l], d_dt_bias=d_dt_bias[l], d_norm_g=d_norm_g[l],
        w_branch=w_branch[l].astype(BF16), w_out_layers=w_out, g_ffn=g_ffn[l],
        w_ffn_up_layers=w_ffn_up, ffn_conv_w=ffn_conv_w[l], ffn_conv_b=ffn_conv_b[l],
        w_ffn_down_layers=w_ffn_down, g_ple=g_ple[l], w_ple_gate=w_ple_gate[l].astype(BF16),
        w_ple_proj=w_ple_proj[l].astype(BF16), g_final=g_final) for l in range(DEPTH)]

    def layer_weights(l, sample):
        if sample:
            seqs = A_CHUNK // t_s
            idx = jnp.arange(A_CHUNK) // t_s
            same_seq = idx[:, None] == idx[None, :]
            w_mix = jnp.where(same_seq, jnp.tile(a_w_s[l, :, :t_s, :t_s], (1, seqs, seqs)), 0.0)
            bias_rows = jnp.tile(a_b_s[l, :, :t_s].T, (seqs, 1))
        else:
            w_mix = a_w_s[l]
            bias_rows = a_b_s[l].T
        return dict(shared[l], a_w_mix=w_mix, a_bias_rows=bias_rows)

    dt = x_prompt.dtype
    zeros = lambda *s: jnp.zeros((DEPTH, bp) + s, dt)
    out_p = _trunk(x_prompt, p_prompt, zeros(HEADS, HEAD_DIM, HEAD_DIM), zeros(HEADS, C_KDIM, HEAD_DIM),
                   zeros(HEADS, HEAD_DIM, HEAD_DIM), zeros(D_CONV - 1, D_QKV), zeros(FFN_CONV - 1, FFN_DIM),
                   [layer_weights(l, False) for l in range(DEPTH)], lbs, sample=False)
    out_s = _trunk(x_sample, p_sample, state_hgrn, state_gla, state_delta, state_delta_conv, state_ffn_conv,
                   [layer_weights(l, True) for l in range(DEPTH)], lbs, sample=True)
    y_p, hgrn_p, gla_p, delta_p, dconv_p, fconv_p, _ = out_p
    y_s, hgrn_s, gla_s, delta_s, dconv_s, fconv_s, v_s = out_s
    return (y_p, y_s, hgrn_p, hgrn_s, gla_p, gla_s, delta_p, delta_s,
            dconv_p, dconv_s, fconv_p, fconv_s, v_s)
```
